```python
import jax, jax.numpy as jnp
from jax import lax
import numpy as np

D_MODEL = 1024
BATCH = 8
SEQ = 2048
DEPTH = 2

GRID_W = 64
CTX_LEN = 256
D_MIX = D_MODEL
MLA_HEADS = 8
MLA_NOPE = 64
MLA_ROPE = 32
MLA_V = 64
MLA_QK = MLA_NOPE + MLA_ROPE
MLA_WIDTH = MLA_HEADS * MLA_V
Q_LORA = 256
KV_LORA = 128
Q_BLOCK = 128
RET_HEADS = 4
RET_DK = 64
RET_DV = 64
RET_WIDTH = RET_HEADS * RET_DV
RET_CHUNK = 128
CONV_WIDTH = D_MIX - MLA_WIDTH - RET_WIDTH
CONV_K = 31

ROPE_BASE = 10000.0
EPS = 1e-5
ALPHA = (2 * DEPTH) ** 0.25
BETA = (8 * DEPTH) ** -0.25

IN_SIZES = (Q_LORA, KV_LORA, MLA_ROPE, MLA_WIDTH,
            RET_HEADS * RET_DK, RET_HEADS * RET_DK, RET_WIDTH, RET_WIDTH,
            2 * CONV_WIDTH, CONV_WIDTH)
IN_COLS = sum(IN_SIZES)

kernel_name = 'hybrid_mla_retention_conformer_dit'


def _split_cols(p):
    offs, s = [], 0
    for n in IN_SIZES[:-1]:
        s += n
        offs.append(s)
    return jnp.split(p, offs, axis=-1)


def _standardize(x):
    xf = x.astype(jnp.float32)
    mu = jnp.mean(xf, axis=-1, keepdims=True)
    var = jnp.mean(jnp.square(xf - mu), axis=-1, keepdims=True)
    return ((xf - mu) * lax.rsqrt(var + EPS)).astype(x.dtype)


def _layer_norm(x, g, b):
    xf = x.astype(jnp.float32)
    mu = jnp.mean(xf, axis=-1, keepdims=True)
    var = jnp.mean(jnp.square(xf - mu), axis=-1, keepdims=True)
    y = (xf - mu) * lax.rsqrt(var + EPS) * g.astype(jnp.float32) + b.astype(jnp.float32)
    return y.astype(x.dtype)


def _rms_norm(x, g):
    xf = x.astype(jnp.float32)
    y = xf * lax.rsqrt(jnp.mean(jnp.square(xf), axis=-1, keepdims=True) + EPS) * g.astype(jnp.float32)
    return y.astype(x.dtype)


def _rotate(x, pos):
    d2 = x.shape[-1]
    inv = ROPE_BASE ** (-jnp.arange(0, d2, 2, dtype=jnp.float32) / d2)
    ang = pos.astype(jnp.float32)[:, None] * inv[None, :]
    cos = jnp.cos(ang)[:, None, :].astype(x.dtype)
    sin = jnp.sin(ang)[:, None, :].astype(x.dtype)
    x1, x2 = jnp.split(x, 2, axis=-1)
    return jnp.concatenate([x1 * cos - x2 * sin, x1 * sin + x2 * cos], axis=-1)


def _axial_rope(x, row, col):
    half = x.shape[-1] // 2
    return jnp.concatenate([_rotate(x[..., :half], row), _rotate(x[..., half:], col)], axis=-1)


def _block_attention(q, k, v):
    B, L, H, dqk = q.shape
    nb = L // Q_BLOCK
    scale = dqk ** -0.5
    qb = q.reshape(B, nb, Q_BLOCK, H, dqk).transpose(1, 0, 2, 3, 4)

    def one_block(qi):
        s = jnp.einsum('bqhd,bkhd->bhqk', qi, k).astype(jnp.float32) * scale
        p = jax.nn.softmax(s, axis=-1).astype(v.dtype)
        return jnp.einsum('bhqk,bkhe->bqhe', p, v)

    o = lax.map(one_block, qb)
    return o.transpose(1, 0, 2, 3, 4).reshape(B, L, H * v.shape[-1])


def _mla_q(p_q, g_q, w_uq, row, col):
    B, L, _ = p_q.shape
    q = (_rms_norm(p_q, g_q) @ w_uq).reshape(B, L, MLA_HEADS, MLA_QK)
    q_nope, q_rope = q[..., :MLA_NOPE], q[..., MLA_NOPE:]
    if row is not None:
        q_rope = _axial_rope(q_rope, row, col)
    return jnp.concatenate([q_nope, q_rope], axis=-1)


def _mla_kv(p_kv, p_kr, g_kv, w_ukv, row, col):
    B, L, _ = p_kv.shape
    kv = (_rms_norm(p_kv, g_kv) @ w_ukv).reshape(B, L, MLA_HEADS, MLA_NOPE + MLA_V)
    k_nope, v = kv[..., :MLA_NOPE], kv[..., MLA_NOPE:]
    k_rope = p_kr[:, :, None, :]
    if row is not None:
        k_rope = _axial_rope(k_rope, row, col)
    k = jnp.concatenate([k_nope, jnp.broadcast_to(k_rope, (B, L, MLA_HEADS, MLA_ROPE))], axis=-1)
    return k, v


def _ret_qkv(p_q, p_k, p_v, row, col):
    B, L, _ = p_q.shape
    q = p_q.reshape(B, L, RET_HEADS, RET_DK)
    k = p_k.reshape(B, L, RET_HEADS, RET_DK) * (RET_DK ** -0.5)
    v = p_v.reshape(B, L, RET_HEADS, RET_DV)
    if row is not None:
        q = _axial_rope(q, row, col)
        k = _axial_rope(k, row, col)
    to_bhld = lambda t: t.transpose(0, 2, 1, 3).astype(jnp.float32)
    return to_bhld(q), to_bhld(k), to_bhld(v)


def _retention_chunks(q, k, v, log_gamma, s0, strict):
    B, H, L, dk = q.shape
    dv = v.shape[-1]
    n = L // RET_CHUNK
    idx = jnp.arange(RET_CHUNK, dtype=jnp.float32)
    diff = idx[:, None] - idx[None, :]
    lg = log_gamma[:, None, None]
    mask = (diff > 0) if strict else (diff >= 0)
    dmat = jnp.where(mask, jnp.exp(lg * jnp.maximum(diff, 0.0)), 0.0)
    q_dec = jnp.exp(lg * (idx[None, :, None] + 1.0))
    k_dec = jnp.exp(lg * (RET_CHUNK - 1.0 - idx)[None, :, None])
    c_dec = jnp.exp(log_gamma * RET_CHUNK)[:, None, None]

    def chunked(t):
        return t.reshape(B, H, n, RET_CHUNK, t.shape[-1]).transpose(2, 0, 1, 3, 4)

    def step(state, qkv):
        qi, ki, vi = qkv
        inner = jnp.einsum('bhij,bhje->bhie', jnp.einsum('bhid,bhjd->bhij', qi, ki) * dmat, vi)
        cross = jnp.einsum('bhid,bhde->bhie', qi * q_dec, state)
        state = state * c_dec + jnp.einsum('bhjd,bhje->bhde', ki * k_dec, vi)
        return state, inner + cross

    s_fin, o = lax.scan(step, s0, (chunked(q), chunked(k), chunked(v)))
    return o.transpose(1, 2, 0, 3, 4).reshape(B, H, L, dv), s_fin


def _ret_out(o, gate, g, b):
    B, H, L, dv = o.shape
    of = o.transpose(0, 2, 1, 3)
    mu = jnp.mean(of, axis=-1, keepdims=True)
    var = jnp.mean(jnp.square(of - mu), axis=-1, keepdims=True)
    y = ((of - mu) * lax.rsqrt(var + EPS)).reshape(B, L, H * dv)
    y = y * g.astype(jnp.float32) + b.astype(jnp.float32)
    return y.astype(gate.dtype) * jax.nn.silu(gate)


def _conformer_conv(p_glu, p_gate, dw, dw_b, ln_g, ln_b, pw, pw_b):
    a, g = jnp.split(p_glu, 2, axis=-1)
    u = a * jax.nn.sigmoid(g)
    y = lax.conv_general_dilated(u, dw[:, None, :], window_strides=(1,),
                                 padding=[(CONV_K // 2, CONV_K // 2)],
                                 dimension_numbers=('NWC', 'WIO', 'NWC'),
                                 feature_group_count=CONV_WIDTH) + dw_b
    y = jax.nn.silu(_layer_norm(y, ln_g, ln_b))
    y = y @ pw + pw_b
    return y * jax.nn.silu(p_gate)


def _layer(x, hc, c, c_ctx, w_mod, b_mod, w_in, g_q, w_uq, g_kv, w_ukv, dec_f, dec_b, gn_g, gn_b,
           dw, dw_b, cln_g, cln_b, pw, pw_b, w_out, ln_g, ln_b, row, col, need_ctx):
    B = x.shape[0]
    shift, scale, gate = jnp.split(jax.nn.silu(c) @ w_mod + b_mod, 3, axis=-1)
    shift_c, scale_c, gate_c = jnp.split(jax.nn.silu(c_ctx) @ w_mod + b_mod, 3, axis=-1)
    u = _standardize(x) * (1 + scale[:, None]) + shift[:, None]
    uc = _standardize(hc) * (1 + scale_c) + shift_c
    (pq, pkv, pkr, pg_mla, prq, prk, prv, pg_ret, pglu, pg_conv) = _split_cols(u @ w_in)
    (cq, ckv, ckr, cg_mla, crq, crk, crv, cg_ret, cglu, cg_conv) = _split_cols(uc @ w_in)

    kc, vc = _mla_kv(ckv, ckr, g_kv, w_ukv, None, None)
    k, v = _mla_kv(pkv, pkr, g_kv, w_ukv, row, col)
    q = _mla_q(pq, g_q, w_uq, row, col)
    o_mla = _block_attention(q, jnp.concatenate([kc, k], axis=1),
                             jnp.concatenate([vc, v], axis=1)) * jax.nn.silu(pg_mla)

    qc_r, kc_r, vc_r = _ret_qkv(crq, crk, crv, None, None)
    q_r, k_r, v_r = _ret_qkv(prq, prk, prv, row, col)
    lg_f = jax.nn.log_sigmoid(dec_f.astype(jnp.float32))
    lg_b = jax.nn.log_sigmoid(dec_b.astype(jnp.float32))
    zero = jnp.zeros((B, RET_HEADS, RET_DK, RET_DV), jnp.float32)
    fl = lambda t: jnp.flip(t, axis=2)
    oc_f, sc_f = _retention_chunks(qc_r, kc_r, vc_r, lg_f, zero, False)
    oc_b, sc_b = _retention_chunks(fl(qc_r), fl(kc_r), fl(vc_r), lg_b, zero, True)
    o_f, _ = _retention_chunks(q_r, k_r, v_r, lg_f, sc_f, False)
    o_b, _ = _retention_chunks(fl(q_r), fl(k_r), fl(v_r), lg_b, sc_b, True)
    o_ret = _ret_out(o_f + fl(o_b), pg_ret, gn_g, gn_b)

    o_conv = _conformer_conv(pglu, pg_conv, dw, dw_b, cln_g, cln_b, pw, pw_b)

    y = jnp.concatenate([o_mla, o_ret, o_conv], axis=-1) @ w_out
    x_new = _layer_norm(ALPHA * x + gate[:, None] * y, ln_g, ln_b)
    if not need_ctx:
        return x_new, None

    qc = _mla_q(cq, g_q, w_uq, None, None)
    oc_mla = _block_attention(qc, kc, vc) * jax.nn.silu(cg_mla)
    oc_ret = _ret_out(oc_f + fl(oc_b), cg_ret, gn_g, gn_b)
    oc_conv = _conformer_conv(cglu, cg_conv, dw, dw_b, cln_g, cln_b, pw, pw_b)
    yc = jnp.concatenate([oc_mla, oc_ret, oc_conv], axis=-1) @ w_out
    hc_new = _layer_norm(ALPHA * hc + gate_c * yc, ln_g, ln_b)
    return x_new, hc_new


def setup_inputs(seed: int = 0) -> dict:
    key = jax.random.key(seed)
    ks = jax.random.split(key, 24)
    f32 = jnp.float32
    nrm = lambda k, shape, s: jax.random.normal(k, shape, f32) * s
    gam = 1.0 - 2.0 ** (-5.0 - jnp.arange(RET_HEADS, dtype=f32))
    logit = jnp.log(gam) - jnp.log1p(-gam)
    return {
        'x': nrm(ks[0], (BATCH, SEQ, D_MODEL), 1.0),
        'c': nrm(ks[1], (BATCH, D_MODEL), 1.0),
        'ctx': nrm(ks[2], (BATCH, CTX_LEN, D_MODEL), 1.0),
        'c_ctx': nrm(ks[3], (D_MODEL,), 1.0),
        'w_mod': nrm(ks[4], (DEPTH, D_MODEL, 3 * D_MODEL), D_MODEL ** -0.5),
        'b_mod': nrm(ks[5], (DEPTH, 3 * D_MODEL), 0.02),
        'w_in': nrm(ks[6], (DEPTH, D_MODEL, IN_COLS), D_MODEL ** -0.5),
        'mla_q_norm': 1.0 + nrm(ks[7], (DEPTH, Q_LORA), 0.02),
        'w_uq': nrm(ks[8], (DEPTH, Q_LORA, MLA_HEADS * MLA_QK), Q_LORA ** -0.5),
        'mla_kv_norm': 1.0 + nrm(ks[9], (DEPTH, KV_LORA), 0.02),
        'w_ukv': nrm(ks[10], (DEPTH, KV_LORA, MLA_HEADS * (MLA_NOPE + MLA_V)), KV_LORA ** -0.5),
        'ret_decay_fwd': logit[None, :] + nrm(ks[11], (DEPTH, RET_HEADS), 0.1),
        'ret_decay_bwd': logit[None, :] + nrm(ks[12], (DEPTH, RET_HEADS), 0.1),
        'ret_gn_g': 1.0 + nrm(ks[13], (DEPTH, RET_WIDTH), 0.02),
        'ret_gn_b': nrm(ks[14], (DEPTH, RET_WIDTH), 0.02),
        'conv_dw': nrm(ks[15], (DEPTH, CONV_K, CONV_WIDTH), CONV_K ** -0.5),
        'conv_dw_b': nrm(ks[16], (DEPTH, CONV_WIDTH), 0.02),
        'conv_ln_g': 1.0 + nrm(ks[17], (DEPTH, CONV_WIDTH), 0.02),
        'conv_ln_b': nrm(ks[18], (DEPTH, CONV_WIDTH), 0.02),
        'conv_pw': nrm(ks[19], (DEPTH, CONV_WIDTH, CONV_WIDTH), CONV_WIDTH ** -0.5),
        'conv_pw_b': nrm(ks[20], (DEPTH, CONV_WIDTH), 0.02),
        'w_out': nrm(ks[21], (DEPTH, D_MIX, D_MODEL), BETA * D_MIX ** -0.5),
        'ln_g': 1.0 + nrm(ks[22], (DEPTH, D_MODEL), 0.02),
        'ln_b': nrm(ks[23], (DEPTH, D_MODEL), 0.02),
    }


def reference(x, c, ctx, c_ctx, w_mod, b_mod, w_in, mla_q_norm, w_uq, mla_kv_norm, w_ukv,
              ret_decay_fwd, ret_decay_bwd, ret_gn_g, ret_gn_b, conv_dw, conv_dw_b, conv_ln_g, conv_ln_b,
              conv_pw, conv_pw_b, w_out, ln_g, ln_b):
    L = x.shape[1]
    ROWS = L // GRID_W
    row = jnp.broadcast_to(jnp.arange(ROWS, dtype=jnp.int32)[:, None], (ROWS, GRID_W)).reshape(-1)
    col = jnp.broadcast_to(jnp.arange(GRID_W, dtype=jnp.int32)[None, :], (ROWS, GRID_W)).reshape(-1)
    hc = ctx
    for l in range(DEPTH):
        x, hc = _layer(x, hc, c, c_ctx, w_mod[l], b_mod[l], w_in[l], mla_q_norm[l], w_uq[l],
                       mla_kv_norm[l], w_ukv[l], ret_decay_fwd[l], ret_decay_bwd[l], ret_gn_g[l], ret_gn_b[l],
                       conv_dw[l], conv_dw_b[l], conv_ln_g[l], conv_ln_b[l], conv_pw[l], conv_pw_b[l],
                       w_out[l], ln_g[l], ln_b[l], row, col, l < DEPTH - 1)
    return x
```

```python
import functools
import math

import jax
import jax.numpy as jnp
from jax import lax
from jax.experimental import pallas as pl
from jax.experimental.pallas import tpu as pltpu

GRID_W = 64
MLA_HEADS = 8
MLA_NOPE = 64
MLA_ROPE = 32
MLA_V = 64
MLA_QK = MLA_NOPE + MLA_ROPE
MLA_WIDTH = MLA_HEADS * MLA_V
Q_LORA = 256
KV_LORA = 128
RET_HEADS = 4
RET_DK = 64
RET_DV = 64
RET_WIDTH = RET_HEADS * RET_DV
CONV_WIDTH = 256
CONV_K = 31
ROPE_BASE = 10000.0
EPS = 1e-5

LANES = 128
HEAD_SLOT = LANES
VMEM_LIMIT_BYTES = 56 * 1024 * 1024
LOG2E = 1.4426950408889634

F32 = jnp.float32
BF16 = jnp.bfloat16

_NT = (((1,), (1,)), ((), ()))


def _dot(a, b):
    return jnp.dot(a, b, preferred_element_type=F32)


def _dot_nt(a, b):
    return lax.dot_general(a, b, _NT, preferred_element_type=F32)


def _params(semantics):
    return pltpu.CompilerParams(dimension_semantics=semantics, vmem_limit_bytes=VMEM_LIMIT_BYTES)


def _rot_cols(w, unit):
    k, n = w.shape
    q = unit // 4
    w5 = w.reshape(k, n // unit, 2, 2, q)
    return jnp.stack([-w5[:, :, :, 1, :], w5[:, :, :, 0, :]], axis=3).reshape(k, n)


def _rope_tables(length, unit, reps):
    d2 = unit // 2
    t = jnp.arange(length, dtype=jnp.int32)
    inv = ROPE_BASE ** (-jnp.arange(0, d2, 2, dtype=F32) / d2)

    def half(pos):
        ang = pos.astype(F32)[:, None] * inv[None, :]
        return jnp.concatenate([jnp.cos(ang)] * 2, axis=-1), jnp.concatenate([jnp.sin(ang)] * 2, axis=-1)

    cr, sr = half(t // GRID_W)
    cc, sc = half(t % GRID_W)
    cos = jnp.concatenate([cr, cc], axis=-1)
    sin = jnp.concatenate([sr, sc], axis=-1)
    return jnp.tile(cos, (1, reps)), jnp.tile(sin, (1, reps))


def _rope_table_block(length, rotate):
    if rotate:
        cos_r, sin_r = _rope_tables(length, RET_DK, RET_HEADS)
        cos_m, sin_m = _rope_tables(length, MLA_ROPE, MLA_HEADS)
    else:
        cos_r = jnp.ones((length, RET_WIDTH), F32)
        sin_r = jnp.zeros((length, RET_WIDTH), F32)
        cos_m = jnp.ones((length, MLA_HEADS * MLA_ROPE), F32)
        sin_m = jnp.zeros((length, MLA_HEADS * MLA_ROPE), F32)
    kslot = jnp.concatenate([cos_m[:, :MLA_ROPE], sin_m[:, :MLA_ROPE],
                             jnp.zeros((length, LANES - 2 * MLA_ROPE), F32)], axis=-1)
    return jnp.concatenate([cos_r, sin_r, cos_m, sin_m, kslot], axis=-1)


def _prep_layer_weights(w_in, w_uq, w_ukv):
    d = w_in.shape[0]
    sizes = (Q_LORA, KV_LORA, MLA_ROPE, MLA_WIDTH, RET_HEADS * RET_DK, RET_HEADS * RET_DK, RET_WIDTH,
             RET_WIDTH, 2 * CONV_WIDTH, CONV_WIDTH)
    offs = [0]
    for s in sizes:
        offs.append(offs[-1] + s)
    seg = [w_in[:, offs[i]:offs[i + 1]] for i in range(len(sizes))]
    wq, wkv, wkr, wg_mla, wrq, wrk, wrv, wg_ret, wglu, wg_conv = seg
    kslot = jnp.concatenate([wkr, _rot_cols(wkr, MLA_ROPE), jnp.zeros((d, LANES - 2 * MLA_ROPE), w_in.dtype)], axis=1)
    w_row = jnp.concatenate([wq, wkv, kslot, wrq, _rot_cols(wrq, RET_DK), wrk, _rot_cols(wrk, RET_DK),
                             wg_mla, wg_ret, wg_conv, wglu], axis=1).astype(BF16)
    w_rvt = wrv.T.astype(BF16)

    uq = w_uq.reshape(Q_LORA, MLA_HEADS, MLA_QK)
    uq_nope = jnp.pad(uq[:, :, :MLA_NOPE], ((0, 0), (0, 0), (0, HEAD_SLOT - MLA_NOPE))).reshape(Q_LORA, -1)
    uq_rope = uq[:, :, MLA_NOPE:].reshape(Q_LORA, MLA_HEADS * MLA_ROPE)
    w_q2 = jnp.concatenate([uq_rope, _rot_cols(uq_rope, MLA_ROPE)], axis=1).astype(BF16)
    place = jnp.zeros((MLA_HEADS, MLA_ROPE, MLA_HEADS, HEAD_SLOT), w_in.dtype)
    hh = jnp.arange(MLA_HEADS)
    dd = jnp.arange(MLA_ROPE)
    place = place.at[hh[:, None], dd[None, :], hh[:, None], MLA_NOPE + dd[None, :]].set(1.0)
    w_qcomb = jnp.concatenate([uq_nope, place.reshape(MLA_HEADS * MLA_ROPE, -1)], axis=0).astype(BF16)

    ukv = w_ukv.reshape(KV_LORA, MLA_HEADS, MLA_NOPE + MLA_V)
    uk = jnp.pad(ukv[:, :, :MLA_NOPE], ((0, 0), (0, 0), (0, HEAD_SLOT - MLA_NOPE))).reshape(KV_LORA, -1)
    kplace = jnp.zeros((LANES, MLA_HEADS, HEAD_SLOT), w_in.dtype)
    kplace = kplace.at[dd[:, None], hh[None, :], MLA_NOPE + dd[:, None]].set(1.0)
    kplace = kplace.at[MLA_ROPE + dd[:, None], hh[None, :], MLA_NOPE + dd[:, None]].set(1.0)
    w_kcomb = jnp.concatenate([uk, kplace.reshape(LANES, -1)], axis=0).astype(BF16)
    w_uvt = ukv[:, :, MLA_NOPE:].reshape(KV_LORA, MLA_WIDTH).T.astype(BF16)
    return w_row, w_rvt, w_q2, w_qcomb, w_kcomb, w_uvt


def _mod_kernel(c_ref, w_ref, b_ref, o_ref):
    c = c_ref[...]
    a = (c * jax.nn.sigmoid(c)).astype(BF16)
    o_ref[0] = _dot(a, w_ref[0].astype(BF16)) + b_ref[0]


def _modulation(cc, w_mod, b_mod):
    depth, d, d3 = w_mod.shape
    nblk = d3 // d
    return pl.pallas_call(
        _mod_kernel,
        grid=(depth, nblk),
        in_specs=[pl.BlockSpec(cc.shape, lambda l, j: (0, 0)),
                  pl.BlockSpec((1, d, d), lambda l, j: (l, 0, j)),
                  pl.BlockSpec((1, 1, d), lambda l, j: (l, 0, j))],
        out_specs=pl.BlockSpec((1, cc.shape[0], d), lambda l, j: (l, 0, j)),
        out_shape=jax.ShapeDtypeStruct((depth, cc.shape[0], d3), F32),
        compiler_params=_params(("arbitrary", "arbitrary")),
        name="mod",
    )(cc, w_mod, b_mod.reshape(depth, 1, d3))


_C_Q, _C_KV, _C_KS, _C_RQ, _C_RQR, _C_RK, _C_RKR, _C_G, _C_GLU, _C_END = (
    0, 256, 384, 512, 768, 1024, 1280, 1536, 2560, 3072)
_T_CR, _T_SR, _T_CM, _T_SM, _T_KS, _T_END = 0, 256, 512, 768, 1024, 1152


def _rms(x, g):
    return x * lax.rsqrt(jnp.mean(x * x, axis=-1, keepdims=True) + EPS) * g


def _proj_kernel(x_ref, mod_ref, tab_ref, w_ref, wrvt_ref, gq_ref, gkv_ref, wq2_ref, wqc_ref, wkc_ref, wuvt_ref,
                 q_ref, k_ref, vt_ref, rq_ref, rk_ref, rvt_ref, g_ref, u_ref):
    x = x_ref[0]
    mu = jnp.mean(x, axis=-1, keepdims=True)
    xc = x - mu
    var = jnp.mean(xc * xc, axis=-1, keepdims=True)
    shift = mod_ref[0, 0:1, :]
    scale = mod_ref[0, 1:2, :]
    u = (xc * lax.rsqrt(var + EPS) * (1.0 + scale) + shift).astype(BF16)

    def seg(lo, hi):
        return _dot(u, w_ref[:, lo:hi])

    nq = _rms(seg(_C_Q, _C_KV), gq_ref[...]).astype(BF16)
    qc = _dot(nq, wq2_ref[...])
    nr = MLA_HEADS * MLA_ROPE
    q_rope = qc[:, :nr] * tab_ref[:, _T_CM:_T_SM] + qc[:, nr:] * tab_ref[:, _T_SM:_T_KS]
    q = _dot(jnp.concatenate([nq, q_rope.astype(BF16)], axis=1), wqc_ref[...])
    q_ref[0] = (q * (MLA_QK ** -0.5 * LOG2E)).astype(BF16)

    nkv = _rms(seg(_C_KV, _C_KS), gkv_ref[...]).astype(BF16)
    ks = seg(_C_KS, _C_RQ) * tab_ref[:, _T_KS:_T_END]
    k_ref[0] = _dot(jnp.concatenate([nkv, ks.astype(BF16)], axis=1), wkc_ref[...]).astype(BF16)
    vt_ref[0] = _dot_nt(wuvt_ref[...], nkv).astype(BF16)

    cos_r = tab_ref[:, _T_CR:_T_SR]
    sin_r = tab_ref[:, _T_SR:_T_CM]
    rq_ref[0] = seg(_C_RQ, _C_RQR) * cos_r + seg(_C_RQR, _C_RK) * sin_r
    rk_ref[0] = (seg(_C_RK, _C_RKR) * cos_r + seg(_C_RKR, _C_G) * sin_r) * (RET_DK ** -0.5)
    rvt_ref[0] = _dot_nt(wrvt_ref[...], u).astype(BF16)

    g = seg(_C_G, _C_GLU)
    g_ref[0] = g * jax.nn.sigmoid(g)
    glu = seg(_C_GLU, _C_END)
    u_ref[0] = glu[:, :CONV_WIDTH] * jax.nn.sigmoid(glu[:, CONV_WIDTH:])


def _proj(x, mod, tab, weights, gq, gkv, tm):
    b, t, d = x.shape
    w_row, w_rvt, w_q2, w_qcomb, w_kcomb, w_uvt = weights
    const = lambda a: pl.BlockSpec(a.shape, lambda i, j: (0,) * a.ndim)
    row = lambda n: pl.BlockSpec((1, tm, n), lambda i, j: (j, i, 0))
    col = lambda n: pl.BlockSpec((1, n, tm), lambda i, j: (j, 0, i))
    hs = MLA_HEADS * HEAD_SLOT
    out_shape = (jax.ShapeDtypeStruct((b, t, hs), BF16),
                 jax.ShapeDtypeStruct((b, t, hs), BF16),
                 jax.ShapeDtypeStruct((b, MLA_WIDTH, t), BF16),
                 jax.ShapeDtypeStruct((b, t, RET_WIDTH), F32),
                 jax.ShapeDtypeStruct((b, t, RET_WIDTH), F32),
                 jax.ShapeDtypeStruct((b, RET_WIDTH, t), BF16),
                 jax.ShapeDtypeStruct((b, t, d), F32),
                 jax.ShapeDtypeStruct((b, t, CONV_WIDTH), F32))
    return pl.pallas_call(
        _proj_kernel,
        grid=(t // tm, b),
        in_specs=[row(d),
                  pl.BlockSpec((1, 3, d), lambda i, j: (j, 0, 0)),
                  pl.BlockSpec((tm, _T_END), lambda i, j: (i, 0)),
                  const(w_row), const(w_rvt), const(gq), const(gkv), const(w_q2), const(w_qcomb), const(w_kcomb),
                  const(w_uvt)],
        out_specs=(row(hs), row(hs), col(MLA_WIDTH), row(RET_WIDTH), row(RET_WIDTH), col(RET_WIDTH), row(d),
                   row(CONV_WIDTH)),
        out_shape=out_shape,
        compiler_params=_params(("arbitrary", "arbitrary")),
        name="proj",
    )(x, mod, tab, w_row, w_rvt, gq, gkv, w_q2, w_qcomb, w_kcomb, w_uvt)


def _attn_kernel(*refs, n_seg, tq):
    q_ref = refs[0]
    k_refs = refs[1:1 + n_seg]
    vt_refs = refs[1 + n_seg:1 + 2 * n_seg]
    o_ref = refs[1 + 2 * n_seg]
    n_q = q_ref.shape[1] // tq

    def body(i, carry):
        r0 = pl.multiple_of(i * tq, tq)
        outs = []
        for j in range(2):
            lanes = slice(j * HEAD_SLOT, (j + 1) * HEAD_SLOT)
            qh = q_ref[0, pl.ds(r0, tq), lanes]
            s = [_dot_nt(k_ref[0, :, lanes], qh) for k_ref in k_refs]
            m = functools.reduce(jnp.maximum, [jnp.max(x, axis=0, keepdims=True) for x in s])
            p = [jnp.exp2(x - m) for x in s]
            l = functools.reduce(jnp.add, [jnp.sum(x, axis=0, keepdims=True) for x in p])
            rows = slice(j * MLA_V, (j + 1) * MLA_V)
            acc = functools.reduce(jnp.add, [_dot(vt_ref[0, rows, :], x.astype(BF16))
                                             for vt_ref, x in zip(vt_refs, p)])
            outs.append(acc * (1.0 / l))
        o_ref[0, pl.ds(r0, tq), :] = jnp.concatenate(outs, axis=0).T
        return carry

    lax.fori_loop(0, n_q, body, 0)


def _attention(q, ks, vts, tq):
    b, t, _ = q.shape
    n_seg = len(ks)
    pair = 2 * HEAD_SLOT
    in_specs = [pl.BlockSpec((1, t, pair), lambda i, h: (i, 0, h))]
    in_specs += [pl.BlockSpec((1, k.shape[1], pair), lambda i, h: (i, 0, h)) for k in ks]
    in_specs += [pl.BlockSpec((1, 2 * MLA_V, v.shape[2]), lambda i, h: (i, h, 0)) for v in vts]
    return pl.pallas_call(
        functools.partial(_attn_kernel, n_seg=n_seg, tq=tq),
        grid=(b, MLA_HEADS // 2),
        in_specs=in_specs,
        out_specs=pl.BlockSpec((1, t, 2 * MLA_V), lambda i, h: (i, 0, h)),
        out_shape=jax.ShapeDtypeStruct((b, t, MLA_WIDTH), F32),
        compiler_params=_params(("arbitrary", "arbitrary")),
        name="attention",
    )(q, *ks, *vts)


def _log_sigmoid(x):
    return jnp.minimum(x, 0.0) - jnp.log1p(jnp.exp(-jnp.abs(x)))


def _ret_kernel(dec_ref, q_ref, k_ref, vt_ref, s0_ref, gng_ref, gnb_ref, o_ref, sfin_ref, sb_scr, *, chunk):
    c_len = chunk
    t = q_ref.shape[1]
    n_c = t // c_len
    hp = pl.program_id(1)
    r = lax.broadcasted_iota(jnp.int32, (c_len, LANES), 0).astype(F32)
    lane = lax.broadcasted_iota(jnp.int32, (c_len, LANES), 1)
    km = lax.broadcasted_iota(jnp.int32, (c_len, c_len), 0)
    qn = lax.broadcasted_iota(jnp.int32, (c_len, c_len), 1)

    heads = []
    for j in range(2):
        h = hp * 2 + j
        lg_f = _log_sigmoid(jnp.full((c_len, LANES), dec_ref[0, h], F32))
        lg_b = _log_sigmoid(jnp.full((c_len, LANES), dec_ref[1, h], F32))
        own = ((lane >= j * RET_DK) & (lane < (j + 1) * RET_DK)).astype(F32)
        diff = (qn - km).astype(F32)
        lg_f2 = _log_sigmoid(jnp.full((c_len, c_len), dec_ref[0, h], F32))
        lg_b2 = _log_sigmoid(jnp.full((c_len, c_len), dec_ref[1, h], F32))
        heads.append(dict(
            own=own,
            kdf=jnp.exp(lg_f * (c_len - 1.0 - r)) * own,
            kdb=jnp.exp(lg_b * r) * own,
            qdf=jnp.exp(lg_f * (r + 1.0)),
            qdb=jnp.exp(lg_b * (c_len - r)),
            cf=jnp.exp(_log_sigmoid(jnp.full((RET_DV, LANES), dec_ref[0, h], F32)) * c_len),
            cb=jnp.exp(_log_sigmoid(jnp.full((RET_DV, LANES), dec_ref[1, h], F32)) * c_len),
            dt=jnp.exp(jnp.where(diff >= 0, lg_f2 * diff, -lg_b2 * diff)),
            rows=slice(j * RET_DV, (j + 1) * RET_DV),
        ))

    def back(i, sb):
        c = n_c - 1 - i
        c0 = pl.multiple_of(c * c_len, c_len)
        kc = k_ref[0, pl.ds(c0, c_len), :]
        new = []
        for j, hd in enumerate(heads):
            sb_scr[j, c] = sb[j]
            vt = vt_ref[0, hd["rows"], pl.ds(c0, c_len)]
            new.append(_dot(vt, (kc * hd["kdb"]).astype(BF16)) + hd["cb"] * sb[j])
        return tuple(new)

    sb_fin = lax.fori_loop(0, n_c, back, (s0_ref[0, 0, 1], s0_ref[0, 0, 3]))

    def fwd(c, sf):
        c0 = pl.multiple_of(c * c_len, c_len)
        qc = q_ref[0, pl.ds(c0, c_len), :]
        kc = k_ref[0, pl.ds(c0, c_len), :]
        qb = qc.astype(BF16)
        new, ys = [], []
        for j, hd in enumerate(heads):
            vt = vt_ref[0, hd["rows"], pl.ds(c0, c_len)]
            st = _dot_nt((kc * hd["own"]).astype(BF16), qb) * hd["dt"]
            o = _dot(vt, st.astype(BF16))
            o += _dot_nt(sf[j].astype(BF16), (qc * hd["qdf"]).astype(BF16))
            o += _dot_nt(sb_scr[j, c].astype(BF16), (qc * hd["qdb"]).astype(BF16))
            new.append(hd["cf"] * sf[j] + _dot(vt, (kc * hd["kdf"]).astype(BF16)))
            mu = jnp.mean(o, axis=0, keepdims=True)
            oc = o - mu
            var = jnp.mean(oc * oc, axis=0, keepdims=True)
            ys.append(oc * lax.rsqrt(var + EPS))
        y = jnp.concatenate(ys, axis=0).T
        o_ref[0, pl.ds(c0, c_len), :] = y * gng_ref[...] + gnb_ref[...]
        return tuple(new)

    sf_fin = lax.fori_loop(0, n_c, fwd, (s0_ref[0, 0, 0], s0_ref[0, 0, 2]))
    sfin_ref[0, 0, 0] = sf_fin[0]
    sfin_ref[0, 0, 1] = sb_fin[0]
    sfin_ref[0, 0, 2] = sf_fin[1]
    sfin_ref[0, 0, 3] = sb_fin[1]


def _retention(dec, rq, rk, rvt, s0, gn_g, gn_b, chunk):
    b, t, _ = rq.shape
    n_pair = RET_HEADS // 2
    pair = 2 * RET_DK
    return pl.pallas_call(
        functools.partial(_ret_kernel, chunk=chunk),
        grid=(b, n_pair),
        in_specs=[pl.BlockSpec(memory_space=pltpu.SMEM),
                  pl.BlockSpec((1, t, pair), lambda i, h: (i, 0, h)),
                  pl.BlockSpec((1, t, pair), lambda i, h: (i, 0, h)),
                  pl.BlockSpec((1, 2 * RET_DV, t), lambda i, h: (i, h, 0)),
                  pl.BlockSpec((1, 1, 4, RET_DV, pair), lambda i, h: (i, h, 0, 0, 0)),
                  pl.BlockSpec((1, pair), lambda i, h: (0, h)),
                  pl.BlockSpec((1, pair), lambda i, h: (0, h))],
        out_specs=(pl.BlockSpec((1, t, pair), lambda i, h: (i, 0, h)),
                   pl.BlockSpec((1, 1, 4, RET_DV, pair), lambda i, h: (i, h, 0, 0, 0))),
        out_shape=(jax.ShapeDtypeStruct((b, t, RET_WIDTH), F32),
                   jax.ShapeDtypeStruct((b, n_pair, 4, RET_DV, pair), F32)),
        scratch_shapes=[pltpu.VMEM((2, t // chunk, RET_DV, pair), F32)],
        compiler_params=_params(("arbitrary", "arbitrary")),
        name="retention",
    )(dec, rq, rk, rvt, s0, gn_g, gn_b)


_HALO = 16


def _final_kernel(x_ref, mod_ref, om_ref, or_ref, g_ref, u_ref, dw_ref, dwb_ref, clg_ref, clb_ref, pw_ref, pwb_ref,
                  wo_ref, lng_ref, lnb_ref, o_ref, upad, *, alpha):
    tm = x_ref.shape[1]
    t = u_ref.shape[1]
    i = pl.program_id(1)
    t0 = pl.multiple_of(i * tm, tm)

    upad[_HALO:_HALO + tm, :] = u_ref[0, pl.ds(t0, tm), :]
    lo = pl.multiple_of(jnp.maximum(t0 - _HALO, 0), _HALO)
    hi = pl.multiple_of(jnp.minimum(t0 + tm, t - _HALO), _HALO)
    upad[0:_HALO, :] = jnp.where(t0 > 0, u_ref[0, pl.ds(lo, _HALO), :], 0.0)
    upad[_HALO + tm:, :] = jnp.where(t0 + tm < t, u_ref[0, pl.ds(hi, _HALO), :], 0.0)
    base = _HALO - CONV_K // 2
    acc = jnp.zeros((tm, CONV_WIDTH), F32) + dwb_ref[...]
    for j in range(CONV_K):
        acc = acc + upad[base + j:base + j + tm, :] * dw_ref[j:j + 1, :]
    mu = jnp.mean(acc, axis=-1, keepdims=True)
    ac = acc - mu
    var = jnp.mean(ac * ac, axis=-1, keepdims=True)
    yc = ac * lax.rsqrt(var + EPS) * clg_ref[...] + clb_ref[...]
    yc = (yc * jax.nn.sigmoid(yc)).astype(BF16)
    o_conv = _dot(yc, pw_ref[...]) + pwb_ref[...]

    n1, n2 = MLA_WIDTH, MLA_WIDTH + RET_WIDTH
    y = _dot((om_ref[0] * g_ref[0, :, :n1]).astype(BF16), wo_ref[:n1, :])
    y += _dot((or_ref[0] * g_ref[0, :, n1:n2]).astype(BF16), wo_ref[n1:n2, :])
    y += _dot((o_conv * g_ref[0, :, n2:]).astype(BF16), wo_ref[n2:, :])

    v = alpha * x_ref[0] + mod_ref[0, 2:3, :] * y
    mu = jnp.mean(v, axis=-1, keepdims=True)
    vc = v - mu
    var = jnp.mean(vc * vc, axis=-1, keepdims=True)
    o_ref[0] = vc * lax.rsqrt(var + EPS) * lng_ref[...] + lnb_ref[...]


def _final(x, mod, o_mla, o_ret, gates, u, dw, dw_b, cln_g, cln_b, pw, pw_b, w_out, ln_g, ln_b, tm, alpha):
    b, t, d = x.shape
    const = lambda a: pl.BlockSpec(a.shape, lambda i, j: (0,) * a.ndim)
    row = lambda n: pl.BlockSpec((1, tm, n), lambda i, j: (i, j, 0))
    return pl.pallas_call(
        functools.partial(_final_kernel, alpha=alpha),
        grid=(b, t // tm),
        in_specs=[row(d),
                  pl.BlockSpec((1, 3, d), lambda i, j: (i, 0, 0)),
                  row(MLA_WIDTH), row(RET_WIDTH), row(d),
                  pl.BlockSpec((1, t, CONV_WIDTH), lambda i, j: (i, 0, 0)),
                  const(dw), const(dw_b), const(cln_g), const(cln_b), const(pw), const(pw_b), const(w_out),
                  const(ln_g), const(ln_b)],
        out_specs=row(d),
        out_shape=jax.ShapeDtypeStruct((b, t, d), F32),
        scratch_shapes=[pltpu.VMEM((tm + 2 * _HALO, CONV_WIDTH), F32)],
        compiler_params=_params(("arbitrary", "arbitrary")),
        name="final",
    )(x, mod, o_mla, o_ret, gates, u, dw, dw_b, cln_g, cln_b, pw, pw_b, w_out, ln_g, ln_b)


def _tile(n, pref):
    return pref if n % pref == 0 else n


def kernel(x, c, ctx, c_ctx, w_mod, b_mod, w_in, mla_q_norm, w_uq, mla_kv_norm, w_ukv, ret_decay_fwd, ret_decay_bwd,
           ret_gn_g, ret_gn_b, conv_dw, conv_dw_b, conv_ln_g, conv_ln_b, conv_pw, conv_pw_b, w_out, ln_g, ln_b):
    depth = w_mod.shape[0]
    b, t, d = x.shape
    t_ctx = ctx.shape[1]
    alpha = (2 * depth) ** 0.25
    n_mod_rows = 16
    cc = jnp.concatenate([c, c_ctx[None, :], jnp.zeros((n_mod_rows - b - 1, d), c.dtype)], axis=0)
    mod_all = _modulation(cc, w_mod, b_mod)
    tab_x = _rope_table_block(t, True)
    tab_c = _rope_table_block(t_ctx, False)
    s_zero = jnp.zeros((b, RET_HEADS // 2, 4, RET_DV, 2 * RET_DK), F32)
    row2 = lambda a: a.reshape(1, -1)

    hc = ctx
    for l in range(depth):
        need_ctx = l < depth - 1
        mod_x = mod_all[l, :b].reshape(b, 3, d)
        mod_c = jnp.broadcast_to(mod_all[l, b].reshape(1, 3, d), (b, 3, d))
        weights = _prep_layer_weights(w_in[l], w_uq[l], w_ukv[l])
        gq, gkv = row2(mla_q_norm[l]), row2(mla_kv_norm[l])
        dec = jnp.stack([ret_decay_fwd[l], ret_decay_bwd[l]]).astype(F32)
        gn_g, gn_b = row2(ret_gn_g[l]), row2(ret_gn_b[l])
        dw = jnp.concatenate([conv_dw[l], jnp.zeros((1, CONV_WIDTH), F32)], axis=0)
        tail = (dw, row2(conv_dw_b[l]), row2(conv_ln_g[l]), row2(conv_ln_b[l]), conv_pw[l].astype(BF16),
                row2(conv_pw_b[l]), w_out[l].astype(BF16), row2(ln_g[l]), row2(ln_b[l]))

        cq, ck, cvt, crq, crk, crvt, cg, cu = _proj(hc, mod_c, tab_c, weights, gq, gkv, _tile(t_ctx, 256))
        xq, xk, xvt, xrq, xrk, xrvt, xg, xu = _proj(x, mod_x, tab_x, weights, gq, gkv, _tile(t, 512))

        oc_ret, s_ctx = _retention(dec, crq, crk, crvt, s_zero, gn_g, gn_b, _tile(t_ctx, 256))
        o_ret, _ = _retention(dec, xrq, xrk, xrvt, s_ctx, gn_g, gn_b, _tile(t, 256))
        o_mla = _attention(xq, [ck, xk], [cvt, xvt], _tile(t, 256))
        x_new = _final(x, mod_x, o_mla, o_ret, xg, xu, *tail, tm=_tile(t, 512), alpha=alpha)
        if need_ctx:
            oc_mla = _attention(cq, [ck], [cvt], _tile(t_ctx, 256))
            hc = _final(hc, mod_c, oc_mla, oc_ret, cg, cu, *tail, tm=_tile(t_ctx, 256), alpha=alpha)
        x = x_new
    return x
```

```python
import functools
import math

import jax
import jax.numpy as jnp
import numpy as np
from jax import lax
from jax.experimental import pallas as pl
from jax.experimental.pallas import tpu as pltpu

GRID_W = 64
MLA_HEADS = 8
MLA_NOPE = 64
MLA_ROPE = 32
MLA_V = 64
MLA_QK = MLA_NOPE + MLA_ROPE
MLA_WIDTH = MLA_HEADS * MLA_V
Q_LORA = 256
KV_LORA = 128
RET_HEADS = 4
RET_DK = 64
RET_DV = 64
RET_WIDTH = RET_HEADS * RET_DV
CONV_WIDTH = 256
CONV_K = 31
ROPE_BASE = 10000.0
EPS = 1e-5

LANES = 128
HEAD_SLOT = LANES
VMEM_LIMIT_BYTES = 56 * 1024 * 1024
LOG2E = 1.4426950408889634

F32 = jnp.float32
BF16 = jnp.bfloat16

_NT = (((1,), (1,)), ((), ()))


def _dot(a, b):
    return jnp.dot(a, b, preferred_element_type=F32)


def _dot_nt(a, b):
    return lax.dot_general(a, b, _NT, preferred_element_type=F32)


def _params(semantics):
    return pltpu.CompilerParams(dimension_semantics=semantics, vmem_limit_bytes=VMEM_LIMIT_BYTES)


def _rot_cols(w, unit):
    k, n = w.shape
    q = unit // 4
    w5 = w.reshape(k, n // unit, 2, 2, q)
    return jnp.stack([-w5[:, :, :, 1, :], w5[:, :, :, 0, :]], axis=3).reshape(k, n)


def _rope_tables(length, unit, reps):
    d2 = unit // 2
    t = np.arange(length, dtype=np.int32)
    inv = np.float32(ROPE_BASE) ** (-np.arange(0, d2, 2, dtype=np.float32) / np.float32(d2))

    def half(pos):
        ang = pos.astype(np.float32)[:, None] * inv[None, :]
        return np.concatenate([np.cos(ang)] * 2, axis=-1), np.concatenate([np.sin(ang)] * 2, axis=-1)

    cr, sr = half(t // GRID_W)
    cc, sc = half(t % GRID_W)
    cos = np.concatenate([cr, cc], axis=-1).astype(np.float32)
    sin = np.concatenate([sr, sc], axis=-1).astype(np.float32)
    return np.tile(cos, (1, reps)), np.tile(sin, (1, reps))


def _rope_table_block(length, rotate):
    if rotate:
        cos_r, sin_r = _rope_tables(length, RET_DK, RET_HEADS)
        cos_m, sin_m = _rope_tables(length, MLA_ROPE, MLA_HEADS)
    else:
        cos_r = np.ones((length, RET_WIDTH), np.float32)
        sin_r = np.zeros((length, RET_WIDTH), np.float32)
        cos_m = np.ones((length, MLA_HEADS * MLA_ROPE), np.float32)
        sin_m = np.zeros((length, MLA_HEADS * MLA_ROPE), np.float32)
    kslot = np.concatenate([cos_m[:, :MLA_ROPE], sin_m[:, :MLA_ROPE],
                            np.zeros((length, LANES - 2 * MLA_ROPE), np.float32)], axis=-1)
    return jnp.asarray(np.concatenate([cos_r, sin_r, cos_m, sin_m, kslot], axis=-1))


def _placement_matrices():
    q_place = np.zeros((MLA_HEADS, MLA_ROPE, MLA_HEADS, HEAD_SLOT), np.float32)
    k_place = np.zeros((LANES, MLA_HEADS, HEAD_SLOT), np.float32)
    for d in range(MLA_ROPE):
        for h in range(MLA_HEADS):
            q_place[h, d, h, MLA_NOPE + d] = 1.0
            k_place[d, h, MLA_NOPE + d] = 1.0
            k_place[MLA_ROPE + d, h, MLA_NOPE + d] = 1.0
    return (q_place.reshape(MLA_HEADS * MLA_ROPE, MLA_HEADS * HEAD_SLOT),
            k_place.reshape(LANES, MLA_HEADS * HEAD_SLOT))


def _prep_layer_weights(w_in, w_uq, w_ukv):
    d = w_in.shape[0]
    sizes = (Q_LORA, KV_LORA, MLA_ROPE, MLA_WIDTH, RET_HEADS * RET_DK, RET_HEADS * RET_DK, RET_WIDTH,
             RET_WIDTH, 2 * CONV_WIDTH, CONV_WIDTH)
    offs = [0]
    for s in sizes:
        offs.append(offs[-1] + s)
    seg = [w_in[:, offs[i]:offs[i + 1]] for i in range(len(sizes))]
    wq, wkv, wkr, wg_mla, wrq, wrk, wrv, wg_ret, wglu, wg_conv = seg
    kslot = jnp.concatenate([wkr, _rot_cols(wkr, MLA_ROPE), jnp.zeros((d, LANES - 2 * MLA_ROPE), w_in.dtype)], axis=1)
    w_row = jnp.concatenate([wq, wkv, kslot, wrq, _rot_cols(wrq, RET_DK), wrk, _rot_cols(wrk, RET_DK),
                             wg_mla, wg_ret, wg_conv, wglu], axis=1).astype(BF16)
    w_rvt = wrv.T.astype(BF16)

    uq = w_uq.reshape(Q_LORA, MLA_HEADS, MLA_QK)
    uq_nope = jnp.pad(uq[:, :, :MLA_NOPE], ((0, 0), (0, 0), (0, HEAD_SLOT - MLA_NOPE))).reshape(Q_LORA, -1)
    uq_rope = uq[:, :, MLA_NOPE:].reshape(Q_LORA, MLA_HEADS * MLA_ROPE)
    w_q2 = jnp.concatenate([uq_rope, _rot_cols(uq_rope, MLA_ROPE)], axis=1).astype(BF16)
    q_place, k_place = _placement_matrices()
    w_qcomb = jnp.concatenate([uq_nope.astype(BF16), jnp.asarray(q_place, BF16)], axis=0)

    ukv = w_ukv.reshape(KV_LORA, MLA_HEADS, MLA_NOPE + MLA_V)
    uk = jnp.pad(ukv[:, :, :MLA_NOPE], ((0, 0), (0, 0), (0, HEAD_SLOT - MLA_NOPE))).reshape(KV_LORA, -1)
    w_kcomb = jnp.concatenate([uk.astype(BF16), jnp.asarray(k_place, BF16)], axis=0)
    w_uvt = ukv[:, :, MLA_NOPE:].reshape(KV_LORA, MLA_WIDTH).T.astype(BF16)
    return w_row, w_rvt, w_q2, w_qcomb, w_kcomb, w_uvt


def _mod_kernel(c_ref, w_ref, b_ref, o_ref):
    c = c_ref[...]
    a = (c * jax.nn.sigmoid(c)).astype(BF16)
    o_ref[0] = _dot(a, w_ref[0].astype(BF16)) + b_ref[0]


def _modulation(cc, w_mod, b_mod):
    depth, d, d3 = w_mod.shape
    nblk = d3 // d
    return pl.pallas_call(
        _mod_kernel,
        grid=(depth, nblk),
        in_specs=[pl.BlockSpec(cc.shape, lambda l, j: (0, 0)),
                  pl.BlockSpec((1, d, d), lambda l, j: (l, 0, j)),
                  pl.BlockSpec((1, 1, d), lambda l, j: (l, 0, j))],
        out_specs=pl.BlockSpec((1, cc.shape[0], d), lambda l, j: (l, 0, j)),
        out_shape=jax.ShapeDtypeStruct((depth, cc.shape[0], d3), F32),
        compiler_params=_params(("arbitrary", "arbitrary")),
        name="mod",
    )(cc, w_mod, b_mod.reshape(depth, 1, d3))


_C_Q, _C_KV, _C_KS, _C_RQ, _C_RQR, _C_RK, _C_RKR, _C_G, _C_GLU, _C_END = (
    0, 256, 384, 512, 768, 1024, 1280, 1536, 2560, 3072)
_T_CR, _T_SR, _T_CM, _T_SM, _T_KS, _T_END = 0, 256, 512, 768, 1024, 1152


def _rms(x, g):
    return x * lax.rsqrt(jnp.mean(x * x, axis=-1, keepdims=True) + EPS) * g


def _proj_kernel(x_ref, mod_ref, tab_ref, w_ref, wrvt_ref, gq_ref, gkv_ref, wq2_ref, wqc_ref, wkc_ref, wuvt_ref,
                 q_ref, k_ref, vt_ref, rq_ref, rk_ref, rvt_ref, g_ref, u_ref):
    x = x_ref[0]
    mu = jnp.mean(x, axis=-1, keepdims=True)
    xc = x - mu
    var = jnp.mean(xc * xc, axis=-1, keepdims=True)
    shift = mod_ref[0, 0:1, :]
    scale = mod_ref[0, 1:2, :]
    u = (xc * lax.rsqrt(var + EPS) * (1.0 + scale) + shift).astype(BF16)

    def seg(lo, hi):
        return _dot(u, w_ref[:, lo:hi])

    nq = _rms(seg(_C_Q, _C_KV), gq_ref[...]).astype(BF16)
    qc = _dot(nq, wq2_ref[...])
    nr = MLA_HEADS * MLA_ROPE
    q_rope = qc[:, :nr] * tab_ref[:, _T_CM:_T_SM] + qc[:, nr:] * tab_ref[:, _T_SM:_T_KS]
    q = _dot(jnp.concatenate([nq, q_rope.astype(BF16)], axis=1), wqc_ref[...])
    q_ref[0] = (q * (MLA_QK ** -0.5 * LOG2E)).astype(BF16)

    nkv = _rms(seg(_C_KV, _C_KS), gkv_ref[...]).astype(BF16)
    ks = seg(_C_KS, _C_RQ) * tab_ref[:, _T_KS:_T_END]
    k_ref[0] = _dot(jnp.concatenate([nkv, ks.astype(BF16)], axis=1), wkc_ref[...]).astype(BF16)
    vt_ref[0] = _dot_nt(wuvt_ref[...], nkv).astype(BF16)

    cos_r = tab_ref[:, _T_CR:_T_SR]
    sin_r = tab_ref[:, _T_SR:_T_CM]
    rq_ref[0] = seg(_C_RQ, _C_RQR) * cos_r + seg(_C_RQR, _C_RK) * sin_r
    rk_ref[0] = (seg(_C_RK, _C_RKR) * cos_r + seg(_C_RKR, _C_G) * sin_r) * (RET_DK ** -0.5)
    rvt_ref[0] = _dot_nt(wrvt_ref[...], u).astype(BF16)

    g = seg(_C_G, _C_GLU)
    g_ref[0] = g * jax.nn.sigmoid(g)
    glu = seg(_C_GLU, _C_END)
    u_ref[0] = glu[:, :CONV_WIDTH] * jax.nn.sigmoid(glu[:, CONV_WIDTH:])


def _proj(x, mod, tab, weights, gq, gkv, tm):
    b, t, d = x.shape
    w_row, w_rvt, w_q2, w_qcomb, w_kcomb, w_uvt = weights
    const = lambda a: pl.BlockSpec(a.shape, lambda i, j: (0,) * a.ndim)
    row = lambda n: pl.BlockSpec((1, tm, n), lambda i, j: (j, i, 0))
    col = lambda n: pl.BlockSpec((1, n, tm), lambda i, j: (j, 0, i))
    hs = MLA_HEADS * HEAD_SLOT
    out_shape = (jax.ShapeDtypeStruct((b, t, hs), BF16),
                 jax.ShapeDtypeStruct((b, t, hs), BF16),
                 jax.ShapeDtypeStruct((b, MLA_WIDTH, t), BF16),
                 jax.ShapeDtypeStruct((b, t, RET_WIDTH), F32),
                 jax.ShapeDtypeStruct((b, t, RET_WIDTH), F32),
                 jax.ShapeDtypeStruct((b, RET_WIDTH, t), BF16),
                 jax.ShapeDtypeStruct((b, t, d), F32),
                 jax.ShapeDtypeStruct((b, t, CONV_WIDTH), F32))
    return pl.pallas_call(
        _proj_kernel,
        grid=(t // tm, b),
        in_specs=[row(d),
                  pl.BlockSpec((1, 3, d), lambda i, j: (j, 0, 0)),
                  pl.BlockSpec((tm, _T_END), lambda i, j: (i, 0)),
                  const(w_row), const(w_rvt), const(gq), const(gkv), const(w_q2), const(w_qcomb), const(w_kcomb),
                  const(w_uvt)],
        out_specs=(row(hs), row(hs), col(MLA_WIDTH), row(RET_WIDTH), row(RET_WIDTH), col(RET_WIDTH), row(d),
                   row(CONV_WIDTH)),
        out_shape=out_shape,
        compiler_params=_params(("arbitrary", "arbitrary")),
        name="proj",
    )(x, mod, tab, w_row, w_rvt, gq, gkv, w_q2, w_qcomb, w_kcomb, w_uvt)


def _attn_kernel(*refs, n_seg, tq):
    q_ref = refs[0]
    k_refs = refs[1:1 + n_seg]
    vt_refs = refs[1 + n_seg:1 + 2 * n_seg]
    o_ref = refs[1 + 2 * n_seg]
    s_bufs = refs[2 + 2 * n_seg:4 + 2 * n_seg]
    m_bufs = refs[4 + 2 * n_seg:6 + 2 * n_seg]
    n_q = q_ref.shape[1] // tq
    lanes = [slice(j * HEAD_SLOT, (j + 1) * HEAD_SLOT) for j in range(2)]
    rows = [slice(j * MLA_V, (j + 1) * MLA_V) for j in range(2)]

    def scores(c, s_ref, m_ref):
        r0 = pl.multiple_of(c * tq, tq)
        for j in range(2):
            qh = q_ref[0, pl.ds(r0, tq), lanes[j]]
            off, ms = 0, []
            for k_ref in k_refs:
                ts = k_ref.shape[1]
                s = _dot_nt(k_ref[0, :, lanes[j]], qh)
                s_ref[j, off:off + ts, :] = s
                ms.append(jnp.max(s, axis=0, keepdims=True))
                off += ts
            m_ref[j] = functools.reduce(jnp.maximum, ms)

    def outputs(c, s_ref, m_ref):
        r0 = pl.multiple_of(c * tq, tq)
        outs = []
        for j in range(2):
            m = m_ref[j]
            off, acc, l = 0, None, None
            for vt_ref in vt_refs:
                ts = vt_ref.shape[2]
                p = jnp.exp2(s_ref[j, off:off + ts, :] - m)
                ls = jnp.sum(p, axis=0, keepdims=True)
                pv = _dot(vt_ref[0, rows[j], :], p.astype(BF16))
                l = ls if l is None else l + ls
                acc = pv if acc is None else acc + pv
                off += ts
            outs.append(acc * (1.0 / l))
        o_ref[0, pl.ds(r0, tq), :] = jnp.concatenate(outs, axis=0).T

    scores(0, s_bufs[0], m_bufs[0])
    if n_q == 1:
        outputs(0, s_bufs[0], m_bufs[0])
        return
    assert n_q % 2 == 0

    def body(i, carry):
        c = 2 * i
        scores(c + 1, s_bufs[1], m_bufs[1])
        outputs(c, s_bufs[0], m_bufs[0])
        scores(jnp.minimum(c + 2, n_q - 1), s_bufs[0], m_bufs[0])
        outputs(c + 1, s_bufs[1], m_bufs[1])
        return carry

    lax.fori_loop(0, n_q // 2, body, 0)


def _attention(q, ks, vts, tq):
    b, t, _ = q.shape
    n_seg = len(ks)
    pair = 2 * HEAD_SLOT
    n_keys = sum(k.shape[1] for k in ks)
    in_specs = [pl.BlockSpec((1, t, pair), lambda i, h: (i, 0, h))]
    in_specs += [pl.BlockSpec((1, k.shape[1], pair), lambda i, h: (i, 0, h)) for k in ks]
    in_specs += [pl.BlockSpec((1, 2 * MLA_V, v.shape[2]), lambda i, h: (i, h, 0)) for v in vts]
    score_buf = pltpu.VMEM((2, n_keys, tq), F32)
    max_buf = pltpu.VMEM((2, 1, tq), F32)
    return pl.pallas_call(
        functools.partial(_attn_kernel, n_seg=n_seg, tq=tq),
        grid=(b, MLA_HEADS // 2),
        in_specs=in_specs,
        out_specs=pl.BlockSpec((1, t, 2 * MLA_V), lambda i, h: (i, 0, h)),
        out_shape=jax.ShapeDtypeStruct((b, t, MLA_WIDTH), F32),
        scratch_shapes=[score_buf, score_buf, max_buf, max_buf],
        compiler_params=_params(("arbitrary", "arbitrary")),
        name="attention",
    )(q, *ks, *vts)


def _log_sigmoid(x):
    return jnp.minimum(x, 0.0) - jnp.log1p(jnp.exp(-jnp.abs(x)))


def _ret_kernel(dec_ref, q_ref, k_ref, vt_ref, s0_ref, gng_ref, gnb_ref, o_ref, sfin_ref, sb_scr, *, chunk):
    c_len = chunk
    t = q_ref.shape[1]
    n_c = t // c_len
    hp = pl.program_id(1)
    r = lax.broadcasted_iota(jnp.int32, (c_len, LANES), 0).astype(F32)
    lane = lax.broadcasted_iota(jnp.int32, (c_len, LANES), 1)
    km = lax.broadcasted_iota(jnp.int32, (c_len, c_len), 0)
    qn = lax.broadcasted_iota(jnp.int32, (c_len, c_len), 1)

    heads = []
    for j in range(2):
        h = hp * 2 + j
        lg_f = _log_sigmoid(jnp.full((c_len, LANES), dec_ref[0, h], F32))
        lg_b = _log_sigmoid(jnp.full((c_len, LANES), dec_ref[1, h], F32))
        own = ((lane >= j * RET_DK) & (lane < (j + 1) * RET_DK)).astype(F32)
        diff = (qn - km).astype(F32)
        lg_f2 = _log_sigmoid(jnp.full((c_len, c_len), dec_ref[0, h], F32))
        lg_b2 = _log_sigmoid(jnp.full((c_len, c_len), dec_ref[1, h], F32))
        heads.append(dict(
            own=own,
            kdf=jnp.exp(lg_f * (c_len - 1.0 - r)) * own,
            kdb=jnp.exp(lg_b * r) * own,
            qdf=jnp.exp(lg_f * (r + 1.0)),
            qdb=jnp.exp(lg_b * (c_len - r)),
            cf=jnp.exp(_log_sigmoid(jnp.full((RET_DV, LANES), dec_ref[0, h], F32)) * c_len),
            cb=jnp.exp(_log_sigmoid(jnp.full((RET_DV, LANES), dec_ref[1, h], F32)) * c_len),
            dt=jnp.exp(jnp.where(diff >= 0, lg_f2 * diff, -lg_b2 * diff)),
            rows=slice(j * RET_DV, (j + 1) * RET_DV),
        ))

    def back(i, sb):
        c = n_c - 1 - i
        c0 = pl.multiple_of(c * c_len, c_len)
        kc = k_ref[0, pl.ds(c0, c_len), :]
        new = []
        for j, hd in enumerate(heads):
            sb_scr[j, c] = sb[j]
            vt = vt_ref[0, hd["rows"], pl.ds(c0, c_len)]
            new.append(_dot(vt, (kc * hd["kdb"]).astype(BF16)) + hd["cb"] * sb[j])
        return tuple(new)

    sb_fin = lax.fori_loop(0, n_c, back, (s0_ref[0, 0, 1], s0_ref[0, 0, 3]))

    def fwd(c, sf):
        c0 = pl.multiple_of(c * c_len, c_len)
        qc = q_ref[0, pl.ds(c0, c_len), :]
        kc = k_ref[0, pl.ds(c0, c_len), :]
        qb = qc.astype(BF16)
        new, ys = [], []
        for j, hd in enumerate(heads):
            vt = vt_ref[0, hd["rows"], pl.ds(c0, c_len)]
            st = _dot_nt((kc * hd["own"]).astype(BF16), qb) * hd["dt"]
            o = _dot(vt, st.astype(BF16))
            o += _dot_nt(sf[j].astype(BF16), (qc * hd["qdf"]).astype(BF16))
            o += _dot_nt(sb_scr[j, c].astype(BF16), (qc * hd["qdb"]).astype(BF16))
            new.append(hd["cf"] * sf[j] + _dot(vt, (kc * hd["kdf"]).astype(BF16)))
            mu = jnp.mean(o, axis=0, keepdims=True)
            oc = o - mu
            var = jnp.mean(oc * oc, axis=0, keepdims=True)
            ys.append(oc * lax.rsqrt(var + EPS))
        y = jnp.concatenate(ys, axis=0).T
        o_ref[0, pl.ds(c0, c_len), :] = y * gng_ref[...] + gnb_ref[...]
        return tuple(new)

    sf_fin = lax.fori_loop(0, n_c, fwd, (s0_ref[0, 0, 0], s0_ref[0, 0, 2]))
    sfin_ref[0, 0, 0] = sf_fin[0]
    sfin_ref[0, 0, 1] = sb_fin[0]
    sfin_ref[0, 0, 2] = sf_fin[1]
    sfin_ref[0, 0, 3] = sb_fin[1]


def _retention(dec, rq, rk, rvt, s0, gn_g, gn_b, chunk):
    b, t, _ = rq.shape
    n_pair = RET_HEADS // 2
    pair = 2 * RET_DK
    return pl.pallas_call(
        functools.partial(_ret_kernel, chunk=chunk),
        grid=(b, n_pair),
        in_specs=[pl.BlockSpec(memory_space=pltpu.SMEM),
                  pl.BlockSpec((1, t, pair), lambda i, h: (i, 0, h)),
                  pl.BlockSpec((1, t, pair), lambda i, h: (i, 0, h)),
                  pl.BlockSpec((1, 2 * RET_DV, t), lambda i, h: (i, h, 0)),
                  pl.BlockSpec((1, 1, 4, RET_DV, pair), lambda i, h: (i, h, 0, 0, 0)),
                  pl.BlockSpec((1, pair), lambda i, h: (0, h)),
                  pl.BlockSpec((1, pair), lambda i, h: (0, h))],
        out_specs=(pl.BlockSpec((1, t, pair), lambda i, h: (i, 0, h)),
                   pl.BlockSpec((1, 1, 4, RET_DV, pair), lambda i, h: (i, h, 0, 0, 0))),
        out_shape=(jax.ShapeDtypeStruct((b, t, RET_WIDTH), F32),
                   jax.ShapeDtypeStruct((b, n_pair, 4, RET_DV, pair), F32)),
        scratch_shapes=[pltpu.VMEM((2, t // chunk, RET_DV, pair), F32)],
        compiler_params=_params(("arbitrary", "arbitrary")),
        name="retention",
    )(dec, rq, rk, rvt, s0, gn_g, gn_b)


_HALO = 16
_CONV_ROWS = 64


def _final_kernel(x_ref, mod_ref, om_ref, or_ref, g_ref, u_ref, dw_ref, dwb_ref, clg_ref, clb_ref, pw_ref, pwb_ref,
                  wo_ref, lng_ref, lnb_ref, o_ref, upad, ushift, *, alpha):
    tm = x_ref.shape[1]
    t = u_ref.shape[1]
    i = pl.program_id(1)
    t0 = pl.multiple_of(i * tm, tm)

    upad[_HALO:_HALO + tm, :] = u_ref[0, pl.ds(t0, tm), :]
    lo = pl.multiple_of(jnp.maximum(t0 - _HALO, 0), _HALO)
    hi = pl.multiple_of(jnp.minimum(t0 + tm, t - _HALO), _HALO)
    upad[0:_HALO, :] = jnp.where(t0 > 0, u_ref[0, pl.ds(lo, _HALO), :], 0.0)
    upad[_HALO + tm:, :] = jnp.where(t0 + tm < t, u_ref[0, pl.ds(hi, _HALO), :], 0.0)
    span = tm + 2 * _HALO - 8
    for r in range(1, 8):
        ushift[r - 1] = upad[r:r + span, :]
    base = _HALO - CONV_K // 2
    blocks = []
    for rb in range(0, tm, _CONV_ROWS):
        acc = jnp.zeros((_CONV_ROWS, CONV_WIDTH), F32) + dwb_ref[...]
        for j in range(CONV_K):
            a, r = divmod(base + j, 8)
            src = upad if r == 0 else ushift.at[r - 1]
            acc = acc + src[8 * a + rb:8 * a + rb + _CONV_ROWS, :] * dw_ref[j:j + 1, :]
        mu = jnp.mean(acc, axis=-1, keepdims=True)
        ac = acc - mu
        var = jnp.mean(ac * ac, axis=-1, keepdims=True)
        yc = ac * lax.rsqrt(var + EPS) * clg_ref[...] + clb_ref[...]
        blocks.append((yc * jax.nn.sigmoid(yc)).astype(BF16))
    o_conv = _dot(jnp.concatenate(blocks, axis=0), pw_ref[...]) + pwb_ref[...]

    n1, n2 = MLA_WIDTH, MLA_WIDTH + RET_WIDTH
    y = _dot((om_ref[0] * g_ref[0, :, :n1]).astype(BF16), wo_ref[:n1, :])
    y += _dot((or_ref[0] * g_ref[0, :, n1:n2]).astype(BF16), wo_ref[n1:n2, :])
    y += _dot((o_conv * g_ref[0, :, n2:]).astype(BF16), wo_ref[n2:, :])

    v = alpha * x_ref[0] + mod_ref[0, 2:3, :] * y
    mu = jnp.mean(v, axis=-1, keepdims=True)
    vc = v - mu
    var = jnp.mean(vc * vc, axis=-1, keepdims=True)
    o_ref[0] = vc * lax.rsqrt(var + EPS) * lng_ref[...] + lnb_ref[...]


def _final(x, mod, o_mla, o_ret, gates, u, dw, dw_b, cln_g, cln_b, pw, pw_b, w_out, ln_g, ln_b, tm, alpha):
    b, t, d = x.shape
    const = lambda a: pl.BlockSpec(a.shape, lambda i, j: (0,) * a.ndim)
    row = lambda n: pl.BlockSpec((1, tm, n), lambda i, j: (i, j, 0))
    return pl.pallas_call(
        functools.partial(_final_kernel, alpha=alpha),
        grid=(b, t // tm),
        in_specs=[row(d),
                  pl.BlockSpec((1, 3, d), lambda i, j: (i, 0, 0)),
                  row(MLA_WIDTH), row(RET_WIDTH), row(d),
                  pl.BlockSpec((1, t, CONV_WIDTH), lambda i, j: (i, 0, 0)),
                  const(dw), const(dw_b), const(cln_g), const(cln_b), const(pw), const(pw_b), const(w_out),
                  const(ln_g), const(ln_b)],
        out_specs=row(d),
        out_shape=jax.ShapeDtypeStruct((b, t, d), F32),
        scratch_shapes=[pltpu.VMEM((tm + 2 * _HALO, CONV_WIDTH), F32),
                        pltpu.VMEM((7, tm + 2 * _HALO - 8, CONV_WIDTH), F32)],
        compiler_params=_params(("arbitrary", "arbitrary")),
        name="final",
    )(x, mod, o_mla, o_ret, gates, u, dw, dw_b, cln_g, cln_b, pw, pw_b, w_out, ln_g, ln_b)


def _tile(n, pref):
    return pref if n % pref == 0 else n


def kernel(x, c, ctx, c_ctx, w_mod, b_mod, w_in, mla_q_norm, w_uq, mla_kv_norm, w_ukv, ret_decay_fwd, ret_decay_bwd,
           ret_gn_g, ret_gn_b, conv_dw, conv_dw_b, conv_ln_g, conv_ln_b, conv_pw, conv_pw_b, w_out, ln_g, ln_b):
    depth = w_mod.shape[0]
    b, t, d = x.shape
    t_ctx = ctx.shape[1]
    alpha = (2 * depth) ** 0.25
    n_mod_rows = 16
    cc = jnp.concatenate([c, c_ctx[None, :], jnp.zeros((n_mod_rows - b - 1, d), c.dtype)], axis=0)
    mod_all = _modulation(cc, w_mod, b_mod)
    tab_x = _rope_table_block(t, True)
    tab_c = _rope_table_block(t_ctx, False)
    s_zero = jnp.zeros((b, RET_HEADS // 2, 4, RET_DV, 2 * RET_DK), F32)
    row2 = lambda a: a.reshape(1, -1)

    hc = ctx
    for l in range(depth):
        need_ctx = l < depth - 1
        mod_x = mod_all[l, :b].reshape(b, 3, d)
        mod_c = jnp.broadcast_to(mod_all[l, b].reshape(1, 3, d), (b, 3, d))
        weights = _prep_layer_weights(w_in[l], w_uq[l], w_ukv[l])
        gq, gkv = row2(mla_q_norm[l]), row2(mla_kv_norm[l])
        dec = jnp.stack([ret_decay_fwd[l], ret_decay_bwd[l]]).astype(F32)
        gn_g, gn_b = row2(ret_gn_g[l]), row2(ret_gn_b[l])
        dw = jnp.concatenate([conv_dw[l], jnp.zeros((1, CONV_WIDTH), F32)], axis=0)
        tail = (dw, row2(conv_dw_b[l]), row2(conv_ln_g[l]), row2(conv_ln_b[l]), conv_pw[l].astype(BF16),
                row2(conv_pw_b[l]), w_out[l].astype(BF16), row2(ln_g[l]), row2(ln_b[l]))

        cq, ck, cvt, crq, crk, crvt, cg, cu = _proj(hc, mod_c, tab_c, weights, gq, gkv, _tile(t_ctx, 256))
        xq, xk, xvt, xrq, xrk, xrvt, xg, xu = _proj(x, mod_x, tab_x, weights, gq, gkv, _tile(t, 512))

        oc_ret, s_ctx = _retention(dec, crq, crk, crvt, s_zero, gn_g, gn_b, _tile(t_ctx, 256))
        o_ret, _ = _retention(dec, xrq, xrk, xrvt, s_ctx, gn_g, gn_b, _tile(t, 256))
        o_mla = _attention(xq, [ck, xk], [cvt, xvt], _tile(t, 256))
        x_new = _final(x, mod_x, o_mla, o_ret, xg, xu, *tail, tm=_tile(t, 512), alpha=alpha)
        if need_ctx:
            oc_mla = _attention(cq, [ck], [cvt], _tile(t_ctx, 256))
            hc = _final(hc, mod_c, oc_mla, oc_ret, cg, cu, *tail, tm=_tile(t_ctx, 256), alpha=alpha)
        x = x_new
    return x
```

```python
import functools
import math

import jax
import jax.numpy as jnp
import numpy as np
from jax import lax
from jax.experimental import pallas as pl
from jax.experimental.pallas import tpu as pltpu

GRID_W = 64
MLA_HEADS = 8
MLA_NOPE = 64
MLA_ROPE = 32
MLA_V = 64
MLA_QK = MLA_NOPE + MLA_ROPE
MLA_WIDTH = MLA_HEADS * MLA_V
Q_LORA = 256
KV_LORA = 128
RET_HEADS = 4
RET_DK = 64
RET_DV = 64
RET_WIDTH = RET_HEADS * RET_DV
CONV_WIDTH = 256
CONV_K = 31
ROPE_BASE = 10000.0
EPS = 1e-5

LANES = 128
HEAD_SLOT = LANES
VMEM_LIMIT_BYTES = 56 * 1024 * 1024
LOG2E = 1.4426950408889634

F32 = jnp.float32
BF16 = jnp.bfloat16

_NT = (((1,), (1,)), ((), ()))


def _dot(a, b):
    return jnp.dot(a, b, preferred_element_type=F32)


def _dot_nt(a, b):
    return lax.dot_general(a, b, _NT, preferred_element_type=F32)


def _params(semantics):
    return pltpu.CompilerParams(dimension_semantics=semantics, vmem_limit_bytes=VMEM_LIMIT_BYTES)


def _rot_cols(w, unit):
    k, n = w.shape
    q = unit // 4
    w5 = w.reshape(k, n // unit, 2, 2, q)
    return jnp.stack([-w5[:, :, :, 1, :], w5[:, :, :, 0, :]], axis=3).reshape(k, n)


def _rope_tables(length, unit, reps):
    d2 = unit // 2
    t = np.arange(length, dtype=np.int32)
    inv = np.float32(ROPE_BASE) ** (-np.arange(0, d2, 2, dtype=np.float32) / np.float32(d2))

    def half(pos):
        ang = pos.astype(np.float32)[:, None] * inv[None, :]
        return np.concatenate([np.cos(ang)] * 2, axis=-1), np.concatenate([np.sin(ang)] * 2, axis=-1)

    cr, sr = half(t // GRID_W)
    cc, sc = half(t % GRID_W)
    cos = np.concatenate([cr, cc], axis=-1).astype(np.float32)
    sin = np.concatenate([sr, sc], axis=-1).astype(np.float32)
    return np.tile(cos, (1, reps)), np.tile(sin, (1, reps))


def _rope_table_block(length, rotate):
    if rotate:
        cos_r, sin_r = _rope_tables(length, RET_DK, RET_HEADS)
        cos_m, sin_m = _rope_tables(length, MLA_ROPE, MLA_HEADS)
    else:
        cos_r = np.ones((length, RET_WIDTH), np.float32)
        sin_r = np.zeros((length, RET_WIDTH), np.float32)
        cos_m = np.ones((length, MLA_HEADS * MLA_ROPE), np.float32)
        sin_m = np.zeros((length, MLA_HEADS * MLA_ROPE), np.float32)
    kslot = np.concatenate([cos_m[:, :MLA_ROPE], sin_m[:, :MLA_ROPE],
                            np.zeros((length, LANES - 2 * MLA_ROPE), np.float32)], axis=-1)
    return jnp.asarray(np.concatenate([cos_r, sin_r, cos_m, sin_m, kslot], axis=-1))


def _placement_matrices():
    q_place = np.zeros((MLA_HEADS, MLA_ROPE, MLA_HEADS, HEAD_SLOT), np.float32)
    k_place = np.zeros((LANES, MLA_HEADS, HEAD_SLOT), np.float32)
    for d in range(MLA_ROPE):
        for h in range(MLA_HEADS):
            q_place[h, d, h, MLA_NOPE + d] = 1.0
            k_place[d, h, MLA_NOPE + d] = 1.0
            k_place[MLA_ROPE + d, h, MLA_NOPE + d] = 1.0
    return (q_place.reshape(MLA_HEADS * MLA_ROPE, MLA_HEADS * HEAD_SLOT),
            k_place.reshape(LANES, MLA_HEADS * HEAD_SLOT))


def _prep_layer_weights(w_in, w_uq, w_ukv):
    d = w_in.shape[0]
    sizes = (Q_LORA, KV_LORA, MLA_ROPE, MLA_WIDTH, RET_HEADS * RET_DK, RET_HEADS * RET_DK, RET_WIDTH,
             RET_WIDTH, 2 * CONV_WIDTH, CONV_WIDTH)
    offs = [0]
    for s in sizes:
        offs.append(offs[-1] + s)
    seg = [w_in[:, offs[i]:offs[i + 1]] for i in range(len(sizes))]
    wq, wkv, wkr, wg_mla, wrq, wrk, wrv, wg_ret, wglu, wg_conv = seg
    kslot = jnp.concatenate([wkr, _rot_cols(wkr, MLA_ROPE), jnp.zeros((d, LANES - 2 * MLA_ROPE), w_in.dtype)], axis=1)
    w_row = jnp.concatenate([wq, wkv, kslot, wrq, _rot_cols(wrq, RET_DK), wrk, _rot_cols(wrk, RET_DK),
                             wg_mla, wg_ret, wg_conv, wglu], axis=1).astype(BF16)
    w_rvt = wrv.T.astype(BF16)

    uq = w_uq.reshape(Q_LORA, MLA_HEADS, MLA_QK)
    uq_nope = jnp.pad(uq[:, :, :MLA_NOPE], ((0, 0), (0, 0), (0, HEAD_SLOT - MLA_NOPE))).reshape(Q_LORA, -1)
    uq_rope = uq[:, :, MLA_NOPE:].reshape(Q_LORA, MLA_HEADS * MLA_ROPE)
    w_q2 = jnp.concatenate([uq_rope, _rot_cols(uq_rope, MLA_ROPE)], axis=1).astype(BF16)
    q_place, k_place = _placement_matrices()
    w_qcomb = jnp.concatenate([uq_nope.astype(BF16), jnp.asarray(q_place, BF16)], axis=0)

    ukv = w_ukv.reshape(KV_LORA, MLA_HEADS, MLA_NOPE + MLA_V)
    uk = jnp.pad(ukv[:, :, :MLA_NOPE], ((0, 0), (0, 0), (0, HEAD_SLOT - MLA_NOPE))).reshape(KV_LORA, -1)
    w_kcomb = jnp.concatenate([uk.astype(BF16), jnp.asarray(k_place, BF16)], axis=0)
    w_uvt = ukv[:, :, MLA_NOPE:].reshape(KV_LORA, MLA_WIDTH).T.astype(BF16)
    return w_row, w_rvt, w_q2, w_qcomb, w_kcomb, w_uvt


def _mod_kernel(c_ref, w_ref, b_ref, o_ref):
    c = c_ref[...]
    a = (c * jax.nn.sigmoid(c)).astype(BF16)
    o_ref[0] = _dot(a, w_ref[0].astype(BF16)) + b_ref[0]


def _modulation(cc, w_mod, b_mod):
    depth, d, d3 = w_mod.shape
    nblk = d3 // d
    return pl.pallas_call(
        _mod_kernel,
        grid=(depth, nblk),
        in_specs=[pl.BlockSpec(cc.shape, lambda l, j: (0, 0)),
                  pl.BlockSpec((1, d, d), lambda l, j: (l, 0, j)),
                  pl.BlockSpec((1, 1, d), lambda l, j: (l, 0, j))],
        out_specs=pl.BlockSpec((1, cc.shape[0], d), lambda l, j: (l, 0, j)),
        out_shape=jax.ShapeDtypeStruct((depth, cc.shape[0], d3), F32),
        compiler_params=_params(("arbitrary", "arbitrary")),
        name="mod",
    )(cc, w_mod, b_mod.reshape(depth, 1, d3))


_C_Q, _C_KV, _C_KS, _C_RQ, _C_RQR, _C_RK, _C_RKR, _C_G, _C_GLU, _C_END = (
    0, 256, 384, 512, 768, 1024, 1280, 1536, 2560, 3072)
_T_CR, _T_SR, _T_CM, _T_SM, _T_KS, _T_END = 0, 256, 512, 768, 1024, 1152


_PROJ_SUBTILES = 1


def _rms(x, g):
    return x * lax.rsqrt(jnp.mean(x * x, axis=-1, keepdims=True) + EPS) * g


def _proj_kernel(x_ref, mod_ref, tab_ref, w_ref, wrvt_ref, gq_ref, gkv_ref, wq2_ref, wqc_ref, wkc_ref, wuvt_ref,
                 q_ref, k_ref, vt_ref, rq_ref, rk_ref, rvt_ref, g_ref, u_ref):
    tm = x_ref.shape[1]
    n_sub = _PROJ_SUBTILES if tm % (_PROJ_SUBTILES * LANES) == 0 else 1
    sub = tm // n_sub
    rows = [slice(s * sub, (s + 1) * sub) for s in range(n_sub)]
    shift = mod_ref[0, 0:1, :]
    scale = mod_ref[0, 1:2, :]
    us = []
    for rs in rows:
        x = x_ref[0, rs, :]
        mu = jnp.mean(x, axis=-1, keepdims=True)
        xc = x - mu
        var = jnp.mean(xc * xc, axis=-1, keepdims=True)
        us.append((xc * lax.rsqrt(var + EPS) * (1.0 + scale) + shift).astype(BF16))

    def seg(u, lo, hi):
        return _dot(u, w_ref[:, lo:hi])

    def tab(rs, lo, hi):
        return tab_ref[rs, lo:hi]

    nr = MLA_HEADS * MLA_ROPE
    for rs, u in zip(rows, us):
        nq = _rms(seg(u, _C_Q, _C_KV), gq_ref[...]).astype(BF16)
        qc = _dot(nq, wq2_ref[...])
        q_rope = qc[:, :nr] * tab(rs, _T_CM, _T_SM) + qc[:, nr:] * tab(rs, _T_SM, _T_KS)
        q = _dot(jnp.concatenate([nq, q_rope.astype(BF16)], axis=1), wqc_ref[...])
        q_ref[0, rs, :] = (q * (MLA_QK ** -0.5 * LOG2E)).astype(BF16)

    for rs, u in zip(rows, us):
        nkv = _rms(seg(u, _C_KV, _C_KS), gkv_ref[...]).astype(BF16)
        ks = seg(u, _C_KS, _C_RQ) * tab(rs, _T_KS, _T_END)
        k_ref[0, rs, :] = _dot(jnp.concatenate([nkv, ks.astype(BF16)], axis=1), wkc_ref[...]).astype(BF16)
        vt_ref[0, :, rs] = _dot_nt(wuvt_ref[...], nkv).astype(BF16)

    for rs, u in zip(rows, us):
        cos_r = tab(rs, _T_CR, _T_SR)
        sin_r = tab(rs, _T_SR, _T_CM)
        rq_ref[0, rs, :] = seg(u, _C_RQ, _C_RQR) * cos_r + seg(u, _C_RQR, _C_RK) * sin_r
        rk_ref[0, rs, :] = (seg(u, _C_RK, _C_RKR) * cos_r + seg(u, _C_RKR, _C_G) * sin_r) * (RET_DK ** -0.5)
        rvt_ref[0, :, rs] = _dot_nt(wrvt_ref[...], u).astype(BF16)

    for rs, u in zip(rows, us):
        g = seg(u, _C_G, _C_GLU)
        g_ref[0, rs, :] = g * jax.nn.sigmoid(g)
    for rs, u in zip(rows, us):
        glu = seg(u, _C_GLU, _C_END)
        u_ref[0, rs, :] = glu[:, :CONV_WIDTH] * jax.nn.sigmoid(glu[:, CONV_WIDTH:])


def _proj(x, mod, tab, weights, gq, gkv, tm):
    b, t, d = x.shape
    w_row, w_rvt, w_q2, w_qcomb, w_kcomb, w_uvt = weights
    const = lambda a: pl.BlockSpec(a.shape, lambda i, j: (0,) * a.ndim)
    row = lambda n: pl.BlockSpec((1, tm, n), lambda i, j: (j, i, 0))
    col = lambda n: pl.BlockSpec((1, n, tm), lambda i, j: (j, 0, i))
    hs = MLA_HEADS * HEAD_SLOT
    out_shape = (jax.ShapeDtypeStruct((b, t, hs), BF16),
                 jax.ShapeDtypeStruct((b, t, hs), BF16),
                 jax.ShapeDtypeStruct((b, MLA_WIDTH, t), BF16),
                 jax.ShapeDtypeStruct((b, t, RET_WIDTH), F32),
                 jax.ShapeDtypeStruct((b, t, RET_WIDTH), F32),
                 jax.ShapeDtypeStruct((b, RET_WIDTH, t), BF16),
                 jax.ShapeDtypeStruct((b, t, d), F32),
                 jax.ShapeDtypeStruct((b, t, CONV_WIDTH), F32))
    return pl.pallas_call(
        _proj_kernel,
        grid=(t // tm, b),
        in_specs=[row(d),
                  pl.BlockSpec((1, 3, d), lambda i, j: (j, 0, 0)),
                  pl.BlockSpec((tm, _T_END), lambda i, j: (i, 0)),
                  const(w_row), const(w_rvt), const(gq), const(gkv), const(w_q2), const(w_qcomb), const(w_kcomb),
                  const(w_uvt)],
        out_specs=(row(hs), row(hs), col(MLA_WIDTH), row(RET_WIDTH), row(RET_WIDTH), col(RET_WIDTH), row(d),
                   row(CONV_WIDTH)),
        out_shape=out_shape,
        compiler_params=_params(("arbitrary", "arbitrary")),
        name="proj",
    )(x, mod, tab, w_row, w_rvt, gq, gkv, w_q2, w_qcomb, w_kcomb, w_uvt)


_ATTN_KEY_BLOCK = 512


def _attn_kernel(*refs, n_seg, tq):
    q_ref = refs[0]
    k_refs = refs[1:1 + n_seg]
    vt_refs = refs[1 + n_seg:1 + 2 * n_seg]
    o_ref = refs[1 + 2 * n_seg]
    s_bufs = refs[2 + 2 * n_seg:4 + 2 * n_seg]
    m_bufs = refs[4 + 2 * n_seg:6 + 2 * n_seg]
    n_q = q_ref.shape[1] // tq
    lanes = [slice(j * HEAD_SLOT, (j + 1) * HEAD_SLOT) for j in range(2)]
    rows = [slice(j * MLA_V, (j + 1) * MLA_V) for j in range(2)]

    blocks, off = [], 0
    for si in range(n_seg):
        ts = k_refs[si].shape[1]
        kb = min(_ATTN_KEY_BLOCK, ts)
        blocks += [(si, b0, kb, off + b0) for b0 in range(0, ts, kb)]
        off += ts

    def phase(nxt, cur):
        if nxt is not None:
            c_n, s_n, m_n = nxt
            rn = pl.multiple_of(c_n * tq, tq)
            qn = [q_ref[0, pl.ds(rn, tq), lanes[j]] for j in range(2)]
            mx = [None, None]
        if cur is not None:
            c_c, s_c, m_c = cur
            rc = pl.multiple_of(c_c * tq, tq)
            mc = [m_c[j] for j in range(2)]
            acc, l = [None, None], [None, None]
        for si, b0, kb, o0 in blocks:
            for j in range(2):
                if nxt is not None:
                    s = _dot_nt(k_refs[si][0, b0:b0 + kb, lanes[j]], qn[j])
                    s_n[j, o0:o0 + kb, :] = s
                    mb = jnp.max(s, axis=0, keepdims=True)
                    mx[j] = mb if mx[j] is None else jnp.maximum(mx[j], mb)
                if cur is not None:
                    p = jnp.exp2(s_c[j, o0:o0 + kb, :] - mc[j])
                    ls = jnp.sum(p, axis=0, keepdims=True)
                    pv = _dot(vt_refs[si][0, rows[j], b0:b0 + kb], p.astype(BF16))
                    l[j] = ls if l[j] is None else l[j] + ls
                    acc[j] = pv if acc[j] is None else acc[j] + pv
        if nxt is not None:
            for j in range(2):
                m_n[j] = mx[j]
        if cur is not None:
            outs = [acc[j] * (1.0 / l[j]) for j in range(2)]
            o_ref[0, pl.ds(rc, tq), :] = jnp.concatenate(outs, axis=0).T

    buf = [(s_bufs[0], m_bufs[0]), (s_bufs[1], m_bufs[1])]
    phase((0, *buf[0]), None)
    if n_q == 1:
        phase(None, (0, *buf[0]))
        return
    assert n_q % 2 == 0

    def body(i, carry):
        c = 2 * i
        phase((c + 1, *buf[1]), (c, *buf[0]))
        phase((jnp.minimum(c + 2, n_q - 1), *buf[0]), (c + 1, *buf[1]))
        return carry

    lax.fori_loop(0, n_q // 2, body, 0)


def _attention(q, ks, vts, tq):
    b, t, _ = q.shape
    n_seg = len(ks)
    pair = 2 * HEAD_SLOT
    n_keys = sum(k.shape[1] for k in ks)
    in_specs = [pl.BlockSpec((1, t, pair), lambda i, h: (i, 0, h))]
    in_specs += [pl.BlockSpec((1, k.shape[1], pair), lambda i, h: (i, 0, h)) for k in ks]
    in_specs += [pl.BlockSpec((1, 2 * MLA_V, v.shape[2]), lambda i, h: (i, h, 0)) for v in vts]
    score_buf = pltpu.VMEM((2, n_keys, tq), F32)
    max_buf = pltpu.VMEM((2, 1, tq), F32)
    return pl.pallas_call(
        functools.partial(_attn_kernel, n_seg=n_seg, tq=tq),
        grid=(b, MLA_HEADS // 2),
        in_specs=in_specs,
        out_specs=pl.BlockSpec((1, t, 2 * MLA_V), lambda i, h: (i, 0, h)),
        out_shape=jax.ShapeDtypeStruct((b, t, MLA_WIDTH), F32),
        scratch_shapes=[score_buf, score_buf, max_buf, max_buf],
        compiler_params=_params(("arbitrary", "arbitrary")),
        name="attention",
    )(q, *ks, *vts)


_RET_GROUP = 8


def _log_sigmoid(x):
    return jnp.minimum(x, 0.0) - jnp.log1p(jnp.exp(-jnp.abs(x)))


def _ret_kernel(dec_ref, q_ref, k_ref, vt_ref, s0_ref, gng_ref, gnb_ref, o_ref, sfin_ref, sf_scr, sb_scr, *, chunk,
                group):
    c_len = chunk
    t = q_ref.shape[1]
    n_c = t // c_len
    hp = pl.program_id(1)
    r = lax.broadcasted_iota(jnp.int32, (c_len, LANES), 0).astype(F32)
    lane = lax.broadcasted_iota(jnp.int32, (c_len, LANES), 1)
    km = lax.broadcasted_iota(jnp.int32, (c_len, c_len), 0)
    qn = lax.broadcasted_iota(jnp.int32, (c_len, c_len), 1)
    diff = (qn - km).astype(F32)

    heads = []
    for j in range(2):
        h = hp * 2 + j
        lg_f = _log_sigmoid(jnp.full((c_len, LANES), dec_ref[0, h], F32))
        lg_b = _log_sigmoid(jnp.full((c_len, LANES), dec_ref[1, h], F32))
        own = ((lane >= j * RET_DK) & (lane < (j + 1) * RET_DK)).astype(F32)
        lg_f2 = _log_sigmoid(jnp.full((c_len, c_len), dec_ref[0, h], F32))
        lg_b2 = _log_sigmoid(jnp.full((c_len, c_len), dec_ref[1, h], F32))
        heads.append(dict(
            own=own,
            kdf=jnp.exp(lg_f * (c_len - 1.0 - r)) * own,
            kdb=jnp.exp(lg_b * r) * own,
            qd=jnp.concatenate([jnp.exp(lg_f * (r + 1.0)), jnp.exp(lg_b * (c_len - r))], axis=1),
            cf=jnp.exp(_log_sigmoid(jnp.full((RET_DV, LANES), dec_ref[0, h], F32)) * c_len),
            cb=jnp.exp(_log_sigmoid(jnp.full((RET_DV, LANES), dec_ref[1, h], F32)) * c_len),
            dt=jnp.exp(jnp.where(diff >= 0, lg_f2 * diff, -lg_b2 * diff)),
            rows=slice(j * RET_DV, (j + 1) * RET_DV),
        ))

    def local(i, carry):
        for g in range(group):
            c = i * group + g
            c0 = pl.multiple_of(c * c_len, c_len)
            kc = k_ref[0, pl.ds(c0, c_len), :]
            for j, hd in enumerate(heads):
                vt = vt_ref[0, hd["rows"], pl.ds(c0, c_len)]
                sf_scr[j, c] = _dot(vt, (kc * hd["kdf"]).astype(BF16))
                sb_scr[j, c] = _dot(vt, (kc * hd["kdb"]).astype(BF16))
        return carry

    lax.fori_loop(0, n_c // group, local, 0)

    for j, hd in enumerate(heads):
        sf = s0_ref[0, 0, 2 * j]
        for c in range(n_c):
            a = sf_scr[j, c]
            sf_scr[j, c] = sf
            sf = hd["cf"] * sf + a
        sfin_ref[0, 0, 2 * j] = sf
        sb = s0_ref[0, 0, 2 * j + 1]
        for c in reversed(range(n_c)):
            a = sb_scr[j, c]
            sb_scr[j, c] = sb
            sb = hd["cb"] * sb + a
        sfin_ref[0, 0, 2 * j + 1] = sb

    def outputs(i, carry):
        items = []
        for g in range(group):
            c = i * group + g
            c0 = pl.multiple_of(c * c_len, c_len)
            qc = q_ref[0, pl.ds(c0, c_len), :]
            kc = k_ref[0, pl.ds(c0, c_len), :]
            items.append(dict(c=c, c0=c0, kc=kc, qb=qc.astype(BF16), qq=jnp.concatenate([qc, qc], axis=1)))
        work = [(it, j, hd) for it in items for j, hd in enumerate(heads)]
        st = [_dot_nt((it["kc"] * hd["own"]).astype(BF16), it["qb"]) for it, j, hd in work]
        cross = [_dot_nt(jnp.concatenate([sf_scr[j, it["c"]], sb_scr[j, it["c"]]], axis=1).astype(BF16),
                         (it["qq"] * hd["qd"]).astype(BF16)) for it, j, hd in work]
        o = [_dot(vt_ref[0, hd["rows"], pl.ds(it["c0"], c_len)], (s * hd["dt"]).astype(BF16)) + x
             for (it, j, hd), s, x in zip(work, st, cross)]
        ys = []
        for v in o:
            mu = jnp.mean(v, axis=0, keepdims=True)
            vc = v - mu
            var = jnp.mean(vc * vc, axis=0, keepdims=True)
            ys.append(vc * lax.rsqrt(var + EPS))
        for g, it in enumerate(items):
            y = jnp.concatenate(ys[2 * g:2 * g + 2], axis=0).T
            o_ref[0, pl.ds(it["c0"], c_len), :] = y * gng_ref[...] + gnb_ref[...]
        return carry

    lax.fori_loop(0, n_c // group, outputs, 0)


def _retention(dec, rq, rk, rvt, s0, gn_g, gn_b, chunk):
    b, t, _ = rq.shape
    n_pair = RET_HEADS // 2
    pair = 2 * RET_DK
    n_c = t // chunk
    group = math.gcd(n_c, _RET_GROUP)
    state_buf = pltpu.VMEM((2, n_c, RET_DV, pair), F32)
    return pl.pallas_call(
        functools.partial(_ret_kernel, chunk=chunk, group=group),
        grid=(b, n_pair),
        in_specs=[pl.BlockSpec(memory_space=pltpu.SMEM),
                  pl.BlockSpec((1, t, pair), lambda i, h: (i, 0, h)),
                  pl.BlockSpec((1, t, pair), lambda i, h: (i, 0, h)),
                  pl.BlockSpec((1, 2 * RET_DV, t), lambda i, h: (i, h, 0)),
                  pl.BlockSpec((1, 1, 4, RET_DV, pair), lambda i, h: (i, h, 0, 0, 0)),
                  pl.BlockSpec((1, pair), lambda i, h: (0, h)),
                  pl.BlockSpec((1, pair), lambda i, h: (0, h))],
        out_specs=(pl.BlockSpec((1, t, pair), lambda i, h: (i, 0, h)),
                   pl.BlockSpec((1, 1, 4, RET_DV, pair), lambda i, h: (i, h, 0, 0, 0))),
        out_shape=(jax.ShapeDtypeStruct((b, t, RET_WIDTH), F32),
                   jax.ShapeDtypeStruct((b, n_pair, 4, RET_DV, pair), F32)),
        scratch_shapes=[state_buf, state_buf],
        compiler_params=_params(("arbitrary", "arbitrary")),
        name="retention",
    )(dec, rq, rk, rvt, s0, gn_g, gn_b)


_HALO = 16
_CONV_ROWS = 64


def _final_kernel(x_ref, mod_ref, om_ref, or_ref, g_ref, u_ref, dw_ref, dwb_ref, clg_ref, clb_ref, pw_ref, pwb_ref,
                  wo_ref, lng_ref, lnb_ref, o_ref, upad, ushift, *, alpha):
    tm = x_ref.shape[1]
    t = u_ref.shape[1]
    i = pl.program_id(1)
    t0 = pl.multiple_of(i * tm, tm)

    upad[_HALO:_HALO + tm, :] = u_ref[0, pl.ds(t0, tm), :]
    lo = pl.multiple_of(jnp.maximum(t0 - _HALO, 0), _HALO)
    hi = pl.multiple_of(jnp.minimum(t0 + tm, t - _HALO), _HALO)
    upad[0:_HALO, :] = jnp.where(t0 > 0, u_ref[0, pl.ds(lo, _HALO), :], 0.0)
    upad[_HALO + tm:, :] = jnp.where(t0 + tm < t, u_ref[0, pl.ds(hi, _HALO), :], 0.0)
    span = tm + 2 * _HALO - 8
    for r in range(1, 8):
        ushift[r - 1] = upad[r:r + span, :]
    base = _HALO - CONV_K // 2
    blocks = []
    for rb in range(0, tm, _CONV_ROWS):
        acc = jnp.zeros((_CONV_ROWS, CONV_WIDTH), F32) + dwb_ref[...]
        for j in range(CONV_K):
            a, r = divmod(base + j, 8)
            src = upad if r == 0 else ushift.at[r - 1]
            acc = acc + src[8 * a + rb:8 * a + rb + _CONV_ROWS, :] * dw_ref[j:j + 1, :]
        mu = jnp.mean(acc, axis=-1, keepdims=True)
        ac = acc - mu
        var = jnp.mean(ac * ac, axis=-1, keepdims=True)
        yc = ac * lax.rsqrt(var + EPS) * clg_ref[...] + clb_ref[...]
        blocks.append((yc * jax.nn.sigmoid(yc)).astype(BF16))
    o_conv = _dot(jnp.concatenate(blocks, axis=0), pw_ref[...]) + pwb_ref[...]

    n1, n2 = MLA_WIDTH, MLA_WIDTH + RET_WIDTH
    y = _dot((om_ref[0] * g_ref[0, :, :n1]).astype(BF16), wo_ref[:n1, :])
    y += _dot((or_ref[0] * g_ref[0, :, n1:n2]).astype(BF16), wo_ref[n1:n2, :])
    y += _dot((o_conv * g_ref[0, :, n2:]).astype(BF16), wo_ref[n2:, :])

    v = alpha * x_ref[0] + mod_ref[0, 2:3, :] * y
    mu = jnp.mean(v, axis=-1, keepdims=True)
    vc = v - mu
    var = jnp.mean(vc * vc, axis=-1, keepdims=True)
    o_ref[0] = vc * lax.rsqrt(var + EPS) * lng_ref[...] + lnb_ref[...]


def _final(x, mod, o_mla, o_ret, gates, u, dw, dw_b, cln_g, cln_b, pw, pw_b, w_out, ln_g, ln_b, tm, alpha):
    b, t, d = x.shape
    const = lambda a: pl.BlockSpec(a.shape, lambda i, j: (0,) * a.ndim)
    row = lambda n: pl.BlockSpec((1, tm, n), lambda i, j: (i, j, 0))
    return pl.pallas_call(
        functools.partial(_final_kernel, alpha=alpha),
        grid=(b, t // tm),
        in_specs=[row(d),
                  pl.BlockSpec((1, 3, d), lambda i, j: (i, 0, 0)),
                  row(MLA_WIDTH), row(RET_WIDTH), row(d),
                  pl.BlockSpec((1, t, CONV_WIDTH), lambda i, j: (i, 0, 0)),
                  const(dw), const(dw_b), const(cln_g), const(cln_b), const(pw), const(pw_b), const(w_out),
                  const(ln_g), const(ln_b)],
        out_specs=row(d),
        out_shape=jax.ShapeDtypeStruct((b, t, d), F32),
        scratch_shapes=[pltpu.VMEM((tm + 2 * _HALO, CONV_WIDTH), F32),
                        pltpu.VMEM((7, tm + 2 * _HALO - 8, CONV_WIDTH), F32)],
        compiler_params=_params(("arbitrary", "arbitrary")),
        name="final",
    )(x, mod, o_mla, o_ret, gates, u, dw, dw_b, cln_g, cln_b, pw, pw_b, w_out, ln_g, ln_b)


def _tile(n, pref):
    return pref if n % pref == 0 else n


def kernel(x, c, ctx, c_ctx, w_mod, b_mod, w_in, mla_q_norm, w_uq, mla_kv_norm, w_ukv, ret_decay_fwd, ret_decay_bwd,
           ret_gn_g, ret_gn_b, conv_dw, conv_dw_b, conv_ln_g, conv_ln_b, conv_pw, conv_pw_b, w_out, ln_g, ln_b):
    depth = w_mod.shape[0]
    b, t, d = x.shape
    t_ctx = ctx.shape[1]
    alpha = (2 * depth) ** 0.25
    n_mod_rows = 16
    cc = jnp.concatenate([c, c_ctx[None, :], jnp.zeros((n_mod_rows - b - 1, d), c.dtype)], axis=0)
    mod_all = _modulation(cc, w_mod, b_mod)
    tab_x = _rope_table_block(t, True)
    tab_c = _rope_table_block(t_ctx, False)
    s_zero = jnp.zeros((b, RET_HEADS // 2, 4, RET_DV, 2 * RET_DK), F32)
    row2 = lambda a: a.reshape(1, -1)

    hc = ctx
    for l in range(depth):
        need_ctx = l < depth - 1
        mod_x = mod_all[l, :b].reshape(b, 3, d)
        mod_c = jnp.broadcast_to(mod_all[l, b].reshape(1, 3, d), (b, 3, d))
        weights = _prep_layer_weights(w_in[l], w_uq[l], w_ukv[l])
        gq, gkv = row2(mla_q_norm[l]), row2(mla_kv_norm[l])
        dec = jnp.stack([ret_decay_fwd[l], ret_decay_bwd[l]]).astype(F32)
        gn_g, gn_b = row2(ret_gn_g[l]), row2(ret_gn_b[l])
        dw = jnp.concatenate([conv_dw[l], jnp.zeros((1, CONV_WIDTH), F32)], axis=0)
        tail = (dw, row2(conv_dw_b[l]), row2(conv_ln_g[l]), row2(conv_ln_b[l]), conv_pw[l].astype(BF16),
                row2(conv_pw_b[l]), w_out[l].astype(BF16), row2(ln_g[l]), row2(ln_b[l]))

        cq, ck, cvt, crq, crk, crvt, cg, cu = _proj(hc, mod_c, tab_c, weights, gq, gkv, _tile(t_ctx, 256))
        xq, xk, xvt, xrq, xrk, xrvt, xg, xu = _proj(x, mod_x, tab_x, weights, gq, gkv, _tile(t, 512))

        oc_ret, s_ctx = _retention(dec, crq, crk, crvt, s_zero, gn_g, gn_b, _tile(t_ctx, 256))
        o_ret, _ = _retention(dec, xrq, xrk, xrvt, s_ctx, gn_g, gn_b, _tile(t, 256))
        o_mla = _attention(xq, [ck, xk], [cvt, xvt], _tile(t, 256))
        x_new = _final(x, mod_x, o_mla, o_ret, xg, xu, *tail, tm=_tile(t, 512), alpha=alpha)
        if need_ctx:
            oc_mla = _attention(cq, [ck], [cvt], _tile(t_ctx, 256))
            hc = _final(hc, mod_c, oc_mla, oc_ret, cg, cu, *tail, tm=_tile(t_ctx, 256), alpha=alpha)
        x = x_new
    return x
```

```python
import functools
import math

import jax
import jax.numpy as jnp
import numpy as np
from jax import lax
from jax.experimental import pallas as pl
from jax.experimental.pallas import tpu as pltpu

GRID_W = 64
MLA_HEADS = 8
MLA_NOPE = 64
MLA_ROPE = 32
MLA_V = 64
MLA_QK = MLA_NOPE + MLA_ROPE
MLA_WIDTH = MLA_HEADS * MLA_V
Q_LORA = 256
KV_LORA = 128
RET_HEADS = 4
RET_DK = 64
RET_DV = 64
RET_WIDTH = RET_HEADS * RET_DV
CONV_WIDTH = 256
CONV_K = 31
ROPE_BASE = 10000.0
EPS = 1e-5

LANES = 128
HEAD_SLOT = LANES
VMEM_LIMIT_BYTES = 56 * 1024 * 1024
LOG2E = 1.4426950408889634

F32 = jnp.float32
BF16 = jnp.bfloat16

_NT = (((1,), (1,)), ((), ()))


def _dot(a, b):
    return jnp.dot(a, b, preferred_element_type=F32)


def _dot_nt(a, b):
    return lax.dot_general(a, b, _NT, preferred_element_type=F32)


def _params(semantics):
    return pltpu.CompilerParams(dimension_semantics=semantics, vmem_limit_bytes=VMEM_LIMIT_BYTES)


def _rot_cols(w, unit):
    q = unit // 4
    w5 = w.reshape(w.shape[:-1] + (w.shape[-1] // unit, 2, 2, q))
    return jnp.stack([-w5[..., 1, :], w5[..., 0, :]], axis=-2).reshape(w.shape)


def _rope_tables(length, unit, reps):
    d2 = unit // 2
    t = np.arange(length, dtype=np.int32)
    inv = np.float32(ROPE_BASE) ** (-np.arange(0, d2, 2, dtype=np.float32) / np.float32(d2))

    def half(pos):
        ang = pos.astype(np.float32)[:, None] * inv[None, :]
        return np.concatenate([np.cos(ang)] * 2, axis=-1), np.concatenate([np.sin(ang)] * 2, axis=-1)

    cr, sr = half(t // GRID_W)
    cc, sc = half(t % GRID_W)
    cos = np.concatenate([cr, cc], axis=-1).astype(np.float32)
    sin = np.concatenate([sr, sc], axis=-1).astype(np.float32)
    return np.tile(cos, (1, reps)), np.tile(sin, (1, reps))


def _rope_table_block(length, rotate):
    if rotate:
        cos_r, sin_r = _rope_tables(length, RET_DK, RET_HEADS)
        cos_m, sin_m = _rope_tables(length, MLA_ROPE, MLA_HEADS)
    else:
        cos_r = np.ones((length, RET_WIDTH), np.float32)
        sin_r = np.zeros((length, RET_WIDTH), np.float32)
        cos_m = np.ones((length, MLA_HEADS * MLA_ROPE), np.float32)
        sin_m = np.zeros((length, MLA_HEADS * MLA_ROPE), np.float32)
    kslot = np.concatenate([cos_m[:, :MLA_ROPE], sin_m[:, :MLA_ROPE],
                            np.zeros((length, LANES - 2 * MLA_ROPE), np.float32)], axis=-1)
    return jnp.asarray(np.concatenate([cos_r, sin_r, cos_m, sin_m, kslot], axis=-1))


def _placement_matrices():
    q_place = np.zeros((MLA_HEADS, MLA_ROPE, MLA_HEADS, HEAD_SLOT), np.float32)
    k_place = np.zeros((LANES, MLA_HEADS, HEAD_SLOT), np.float32)
    for d in range(MLA_ROPE):
        for h in range(MLA_HEADS):
            q_place[h, d, h, MLA_NOPE + d] = 1.0
            k_place[d, h, MLA_NOPE + d] = 1.0
            k_place[MLA_ROPE + d, h, MLA_NOPE + d] = 1.0
    return (q_place.reshape(MLA_HEADS * MLA_ROPE, MLA_HEADS * HEAD_SLOT),
            k_place.reshape(LANES, MLA_HEADS * HEAD_SLOT))


def _prep_weights(w_in, w_uq, w_ukv):
    depth, d, _ = w_in.shape
    sizes = (Q_LORA, KV_LORA, MLA_ROPE, MLA_WIDTH, RET_HEADS * RET_DK, RET_HEADS * RET_DK, RET_WIDTH,
             RET_WIDTH, 2 * CONV_WIDTH, CONV_WIDTH)
    offs = [0]
    for s in sizes:
        offs.append(offs[-1] + s)
    seg = [w_in[..., offs[i]:offs[i + 1]] for i in range(len(sizes))]
    wq, wkv, wkr, wg_mla, wrq, wrk, wrv, wg_ret, wglu, wg_conv = seg
    kslot = jnp.concatenate([wkr, _rot_cols(wkr, MLA_ROPE), jnp.zeros((depth, d, LANES - 2 * MLA_ROPE), w_in.dtype)],
                            axis=-1)
    w_row = jnp.concatenate([wq, wkv, kslot, wrq, _rot_cols(wrq, RET_DK), wrk, _rot_cols(wrk, RET_DK),
                             wg_mla, wg_ret, wg_conv, wglu], axis=-1).astype(BF16)
    w_rvt = jnp.swapaxes(wrv, 1, 2).astype(BF16)

    uq = w_uq.reshape(depth, Q_LORA, MLA_HEADS, MLA_QK)
    pad = ((0, 0), (0, 0), (0, 0), (0, HEAD_SLOT - MLA_NOPE))
    uq_nope = jnp.pad(uq[..., :MLA_NOPE], pad).reshape(depth, Q_LORA, -1)
    uq_rope = uq[..., MLA_NOPE:].reshape(depth, Q_LORA, MLA_HEADS * MLA_ROPE)
    w_q2 = jnp.concatenate([uq_rope, _rot_cols(uq_rope, MLA_ROPE)], axis=-1).astype(BF16)
    q_place, k_place = (jnp.broadcast_to(jnp.asarray(m, BF16), (depth,) + m.shape) for m in _placement_matrices())
    w_qcomb = jnp.concatenate([uq_nope.astype(BF16), q_place], axis=1)

    ukv = w_ukv.reshape(depth, KV_LORA, MLA_HEADS, MLA_NOPE + MLA_V)
    uk = jnp.pad(ukv[..., :MLA_NOPE], pad).reshape(depth, KV_LORA, -1)
    w_kcomb = jnp.concatenate([uk.astype(BF16), k_place], axis=1)
    w_uvt = jnp.swapaxes(ukv[..., MLA_NOPE:].reshape(depth, KV_LORA, MLA_WIDTH), 1, 2).astype(BF16)
    return w_row, w_rvt, w_q2, w_qcomb, w_kcomb, w_uvt


def _mod_kernel(c_ref, w_ref, b_ref, o_ref):
    c = c_ref[...]
    a = (c * jax.nn.sigmoid(c)).astype(BF16)
    o_ref[0] = _dot(a, w_ref[0].astype(BF16)) + b_ref[0]


def _modulation(cc, w_mod, b_mod):
    depth, d, d3 = w_mod.shape
    nblk = d3 // d
    return pl.pallas_call(
        _mod_kernel,
        grid=(depth, nblk),
        in_specs=[pl.BlockSpec(cc.shape, lambda l, j: (0, 0)),
                  pl.BlockSpec((1, d, d), lambda l, j: (l, 0, j)),
                  pl.BlockSpec((1, 1, d), lambda l, j: (l, 0, j))],
        out_specs=pl.BlockSpec((1, cc.shape[0], d), lambda l, j: (l, 0, j)),
        out_shape=jax.ShapeDtypeStruct((depth, cc.shape[0], d3), F32),
        compiler_params=_params(("arbitrary", "arbitrary")),
        name="mod",
    )(cc, w_mod, b_mod.reshape(depth, 1, d3))


_C_Q, _C_KV, _C_KS, _C_RQ, _C_RQR, _C_RK, _C_RKR, _C_G, _C_GLU, _C_END = (
    0, 256, 384, 512, 768, 1024, 1280, 1536, 2560, 3072)
_T_CR, _T_SR, _T_CM, _T_SM, _T_KS, _T_END = 0, 256, 512, 768, 1024, 1152


_HALO = 16
_CONV_ROWS = 64


def _rms(x, g):
    return x * lax.rsqrt(jnp.mean(x * x, axis=-1, keepdims=True) + EPS) * g


def _proj_kernel(x_ref, xp_ref, xn_ref, mod_ref, tab_ref, w_ref, wrvt_ref, gq_ref, gkv_ref, wq2_ref, wqc_ref,
                 wkc_ref, wuvt_ref, dw_ref, dwb_ref, clg_ref, clb_ref, pw_ref, pwb_ref,
                 q_ref, k_ref, vt_ref, rq_ref, rk_ref, rvt_ref, g_ref, oc_ref, upad, ushift):
    tm = x_ref.shape[1]
    shift = mod_ref[0, 0:1, :]
    scale = mod_ref[0, 1:2, :]

    def modulate(x):
        mu = jnp.mean(x, axis=-1, keepdims=True)
        xc = x - mu
        var = jnp.mean(xc * xc, axis=-1, keepdims=True)
        return (xc * lax.rsqrt(var + EPS) * (1.0 + scale) + shift).astype(BF16)

    u = modulate(x_ref[0])

    def seg(lo, hi):
        return _dot(u, w_ref[:, lo:hi])

    i = pl.program_id(0)
    u_ext = jnp.concatenate([modulate(xp_ref[0]), u, modulate(xn_ref[0])], axis=0)
    glu = _dot(u_ext, w_ref[:, _C_GLU:_C_END])
    uc = glu[:, :CONV_WIDTH] * jax.nn.sigmoid(glu[:, CONV_WIDTH:])
    row = lax.broadcasted_iota(jnp.int32, uc.shape, 0)
    inside = ((row >= _HALO) | (i > 0)) & ((row < _HALO + tm) | (i < pl.num_programs(0) - 1))
    upad[...] = jnp.where(inside, uc, 0.0)
    span = tm + 2 * _HALO - 8
    for r in range(1, 8):
        ushift[r - 1] = upad[r:r + span, :]
    base = _HALO - CONV_K // 2
    conv_blocks = []

    def conv_rows(n_blocks):
        for _ in range(n_blocks):
            rb = len(conv_blocks) * _CONV_ROWS
            acc = jnp.zeros((_CONV_ROWS, CONV_WIDTH), F32) + dwb_ref[...]
            for j in range(CONV_K):
                a, r = divmod(base + j, 8)
                src = upad if r == 0 else ushift.at[r - 1]
                acc = acc + src[8 * a + rb:8 * a + rb + _CONV_ROWS, :] * dw_ref[j:j + 1, :]
            mu = jnp.mean(acc, axis=-1, keepdims=True)
            ac = acc - mu
            var = jnp.mean(ac * ac, axis=-1, keepdims=True)
            yc = ac * lax.rsqrt(var + EPS) * clg_ref[...] + clb_ref[...]
            conv_blocks.append((yc * jax.nn.sigmoid(yc)).astype(BF16))

    n_conv = tm // _CONV_ROWS
    per_stage = -(-n_conv // 4)

    nq = _rms(seg(_C_Q, _C_KV), gq_ref[...]).astype(BF16)
    qc = _dot(nq, wq2_ref[...])
    nr = MLA_HEADS * MLA_ROPE
    q_rope = qc[:, :nr] * tab_ref[:, _T_CM:_T_SM] + qc[:, nr:] * tab_ref[:, _T_SM:_T_KS]
    q = _dot(jnp.concatenate([nq, q_rope.astype(BF16)], axis=1), wqc_ref[...])
    q_ref[0] = (q * (MLA_QK ** -0.5 * LOG2E)).astype(BF16)
    conv_rows(min(per_stage, n_conv - len(conv_blocks)))

    nkv = _rms(seg(_C_KV, _C_KS), gkv_ref[...]).astype(BF16)
    ks = seg(_C_KS, _C_RQ) * tab_ref[:, _T_KS:_T_END]
    k_ref[0] = _dot(jnp.concatenate([nkv, ks.astype(BF16)], axis=1), wkc_ref[...]).astype(BF16)
    vt_ref[0] = _dot_nt(wuvt_ref[...], nkv).astype(BF16)
    conv_rows(min(per_stage, n_conv - len(conv_blocks)))

    cos_r = tab_ref[:, _T_CR:_T_SR]
    sin_r = tab_ref[:, _T_SR:_T_CM]
    rq_ref[0] = seg(_C_RQ, _C_RQR) * cos_r + seg(_C_RQR, _C_RK) * sin_r
    rk_ref[0] = (seg(_C_RK, _C_RKR) * cos_r + seg(_C_RKR, _C_G) * sin_r) * (RET_DK ** -0.5)
    rvt_ref[0] = _dot_nt(wrvt_ref[...], u).astype(BF16)
    conv_rows(min(per_stage, n_conv - len(conv_blocks)))

    g = seg(_C_G, _C_GLU)
    g_ref[0] = g * jax.nn.sigmoid(g)
    conv_rows(n_conv - len(conv_blocks))

    oc_ref[0] = _dot(jnp.concatenate(conv_blocks, axis=0), pw_ref[...]) + pwb_ref[...]


def _layer_spec(a, layer):
    return pl.BlockSpec((None,) + a.shape[1:], lambda *_: (layer,) + (0,) * (a.ndim - 1))


def _mod_spec(mod_all, layer, shared_row, batch_axis):
    def index(*grid):
        return (layer, grid[batch_axis] if shared_row is None else shared_row, 0, 0)
    return pl.BlockSpec((None, 1) + mod_all.shape[2:], index)


def _proj(x, mod_all, mod_row, tab, weights, gq, gkv, conv, layer, tm):
    b, t, d = x.shape
    w_row, w_rvt, w_q2, w_qcomb, w_kcomb, w_uvt = weights
    const = lambda a: _layer_spec(a, layer)
    row = lambda n: pl.BlockSpec((1, tm, n), lambda i, j: (j, i, 0))
    col = lambda n: pl.BlockSpec((1, n, tm), lambda i, j: (j, 0, i))
    per_tile = tm // _HALO
    last = t // _HALO - 1
    halo_prev = pl.BlockSpec((1, _HALO, d), lambda i, j: (j, jnp.maximum(i * per_tile - 1, 0), 0))
    halo_next = pl.BlockSpec((1, _HALO, d), lambda i, j: (j, jnp.minimum((i + 1) * per_tile, last), 0))
    hs = MLA_HEADS * HEAD_SLOT
    out_shape = (jax.ShapeDtypeStruct((b, t, hs), BF16),
                 jax.ShapeDtypeStruct((b, t, hs), BF16),
                 jax.ShapeDtypeStruct((b, MLA_WIDTH, t), BF16),
                 jax.ShapeDtypeStruct((b, t, RET_WIDTH), F32),
                 jax.ShapeDtypeStruct((b, t, RET_WIDTH), F32),
                 jax.ShapeDtypeStruct((b, RET_WIDTH, t), BF16),
                 jax.ShapeDtypeStruct((b, t, d), F32),
                 jax.ShapeDtypeStruct((b, t, CONV_WIDTH), F32))
    return pl.pallas_call(
        _proj_kernel,
        grid=(t // tm, b),
        in_specs=[row(d), halo_prev, halo_next,
                  _mod_spec(mod_all, layer, mod_row, 1),
                  pl.BlockSpec((tm, _T_END), lambda i, j: (i, 0)),
                  const(w_row), const(w_rvt), const(gq), const(gkv), const(w_q2), const(w_qcomb), const(w_kcomb),
                  const(w_uvt)] + [const(a) for a in conv],
        out_specs=(row(hs), row(hs), col(MLA_WIDTH), row(RET_WIDTH), row(RET_WIDTH), col(RET_WIDTH), row(d),
                   row(CONV_WIDTH)),
        out_shape=out_shape,
        scratch_shapes=[pltpu.VMEM((tm + 2 * _HALO, CONV_WIDTH), F32),
                        pltpu.VMEM((7, tm + 2 * _HALO - 8, CONV_WIDTH), F32)],
        compiler_params=_params(("arbitrary", "arbitrary")),
        name="proj",
    )(x, x, x, mod_all, tab, w_row, w_rvt, gq, gkv, w_q2, w_qcomb, w_kcomb, w_uvt, *conv)


_ATTN_KEY_BLOCK = 512


def _attn_kernel(*refs, n_seg, tq):
    q_ref = refs[0]
    k_refs = refs[1:1 + n_seg]
    vt_refs = refs[1 + n_seg:1 + 2 * n_seg]
    o_ref = refs[1 + 2 * n_seg]
    s_bufs = refs[2 + 2 * n_seg:4 + 2 * n_seg]
    m_bufs = refs[4 + 2 * n_seg:6 + 2 * n_seg]
    n_q = q_ref.shape[1] // tq
    lanes = [slice(j * HEAD_SLOT, (j + 1) * HEAD_SLOT) for j in range(2)]
    rows = [slice(j * MLA_V, (j + 1) * MLA_V) for j in range(2)]

    blocks, off = [], 0
    for si in range(n_seg):
        ts = k_refs[si].shape[1]
        kb = min(_ATTN_KEY_BLOCK, ts)
        blocks += [(si, b0, kb, off + b0) for b0 in range(0, ts, kb)]
        off += ts

    def phase(nxt, cur):
        if nxt is not None:
            c_n, s_n, m_n = nxt
            rn = pl.multiple_of(c_n * tq, tq)
            qn = [q_ref[0, pl.ds(rn, tq), lanes[j]] for j in range(2)]
            mx = [None, None]
        if cur is not None:
            c_c, s_c, m_c = cur
            rc = pl.multiple_of(c_c * tq, tq)
            mc = [m_c[j] for j in range(2)]
            acc, l = [None, None], [None, None]
        for si, b0, kb, o0 in blocks:
            for j in range(2):
                if nxt is not None:
                    s = _dot_nt(k_refs[si][0, b0:b0 + kb, lanes[j]], qn[j])
                    s_n[j, o0:o0 + kb, :] = s
                    mb = jnp.max(s, axis=0, keepdims=True)
                    mx[j] = mb if mx[j] is None else jnp.maximum(mx[j], mb)
                if cur is not None:
                    p = jnp.exp2(s_c[j, o0:o0 + kb, :] - mc[j])
                    ls = jnp.sum(p, axis=0, keepdims=True)
                    pv = _dot(vt_refs[si][0, rows[j], b0:b0 + kb], p.astype(BF16))
                    l[j] = ls if l[j] is None else l[j] + ls
                    acc[j] = pv if acc[j] is None else acc[j] + pv
        if nxt is not None:
            for j in range(2):
                m_n[j] = mx[j]
        if cur is not None:
            outs = [acc[j] * (1.0 / l[j]) for j in range(2)]
            o_ref[0, pl.ds(rc, tq), :] = jnp.concatenate(outs, axis=0).T

    buf = [(s_bufs[0], m_bufs[0]), (s_bufs[1], m_bufs[1])]
    phase((0, *buf[0]), None)
    if n_q == 1:
        phase(None, (0, *buf[0]))
        return
    assert n_q % 2 == 0

    def body(i, carry):
        c = 2 * i
        phase((c + 1, *buf[1]), (c, *buf[0]))
        phase((jnp.minimum(c + 2, n_q - 1), *buf[0]), (c + 1, *buf[1]))
        return carry

    lax.fori_loop(0, n_q // 2, body, 0)


def _attention(q, ks, vts, tq):
    b, t, _ = q.shape
    n_seg = len(ks)
    pair = 2 * HEAD_SLOT
    n_keys = sum(k.shape[1] for k in ks)
    in_specs = [pl.BlockSpec((1, t, pair), lambda i, h: (i, 0, h))]
    in_specs += [pl.BlockSpec((1, k.shape[1], pair), lambda i, h: (i, 0, h)) for k in ks]
    in_specs += [pl.BlockSpec((1, 2 * MLA_V, v.shape[2]), lambda i, h: (i, h, 0)) for v in vts]
    score_buf = pltpu.VMEM((2, n_keys, tq), F32)
    max_buf = pltpu.VMEM((2, 1, tq), F32)
    return pl.pallas_call(
        functools.partial(_attn_kernel, n_seg=n_seg, tq=tq),
        grid=(b, MLA_HEADS // 2),
        in_specs=in_specs,
        out_specs=pl.BlockSpec((1, t, 2 * MLA_V), lambda i, h: (i, 0, h)),
        out_shape=jax.ShapeDtypeStruct((b, t, MLA_WIDTH), F32),
        scratch_shapes=[score_buf, score_buf, max_buf, max_buf],
        compiler_params=_params(("arbitrary", "arbitrary")),
        name="attention",
    )(q, *ks, *vts)


_RET_GROUP = 8


def _log_sigmoid(x):
    return jnp.minimum(x, 0.0) - jnp.log1p(jnp.exp(-jnp.abs(x)))


def _ret_kernel(dec_ref, q_ref, k_ref, vt_ref, s0_ref, gng_ref, gnb_ref, o_ref, sfin_ref, sf_scr, sb_scr, *, chunk,
                group, layer):
    c_len = chunk
    t = q_ref.shape[1]
    n_c = t // c_len
    hp = pl.program_id(1)
    r = lax.broadcasted_iota(jnp.int32, (c_len, LANES), 0).astype(F32)
    lane = lax.broadcasted_iota(jnp.int32, (c_len, LANES), 1)
    km = lax.broadcasted_iota(jnp.int32, (c_len, c_len), 0)
    qn = lax.broadcasted_iota(jnp.int32, (c_len, c_len), 1)
    diff = (qn - km).astype(F32)

    heads = []
    for j in range(2):
        h = hp * 2 + j
        lg_f = _log_sigmoid(jnp.full((c_len, LANES), dec_ref[layer, 0, h], F32))
        lg_b = _log_sigmoid(jnp.full((c_len, LANES), dec_ref[layer, 1, h], F32))
        own = ((lane >= j * RET_DK) & (lane < (j + 1) * RET_DK)).astype(F32)
        lg_f2 = _log_sigmoid(jnp.full((c_len, c_len), dec_ref[layer, 0, h], F32))
        lg_b2 = _log_sigmoid(jnp.full((c_len, c_len), dec_ref[layer, 1, h], F32))
        heads.append(dict(
            own=own,
            kdf=jnp.exp(lg_f * (c_len - 1.0 - r)) * own,
            kdb=jnp.exp(lg_b * r) * own,
            qd=jnp.concatenate([jnp.exp(lg_f * (r + 1.0)), jnp.exp(lg_b * (c_len - r))], axis=1),
            cf=jnp.exp(_log_sigmoid(jnp.full((RET_DV, LANES), dec_ref[layer, 0, h], F32)) * c_len),
            cb=jnp.exp(_log_sigmoid(jnp.full((RET_DV, LANES), dec_ref[layer, 1, h], F32)) * c_len),
            dt=jnp.exp(jnp.where(diff >= 0, lg_f2 * diff, -lg_b2 * diff)),
            rows=slice(j * RET_DV, (j + 1) * RET_DV),
        ))

    def local(i, carry):
        for g in range(group):
            c = i * group + g
            c0 = pl.multiple_of(c * c_len, c_len)
            kc = k_ref[0, pl.ds(c0, c_len), :]
            for j, hd in enumerate(heads):
                vt = vt_ref[0, hd["rows"], pl.ds(c0, c_len)]
                sf_scr[j, c] = _dot(vt, (kc * hd["kdf"]).astype(BF16))
                sb_scr[j, c] = _dot(vt, (kc * hd["kdb"]).astype(BF16))
        return carry

    lax.fori_loop(0, n_c // group, local, 0)

    for j, hd in enumerate(heads):
        sf = s0_ref[0, 0, 2 * j]
        for c in range(n_c):
            a = sf_scr[j, c]
            sf_scr[j, c] = sf
            sf = hd["cf"] * sf + a
        sfin_ref[0, 0, 2 * j] = sf
        sb = s0_ref[0, 0, 2 * j + 1]
        for c in reversed(range(n_c)):
            a = sb_scr[j, c]
            sb_scr[j, c] = sb
            sb = hd["cb"] * sb + a
        sfin_ref[0, 0, 2 * j + 1] = sb

    def outputs(i, carry):
        items = []
        for g in range(group):
            c = i * group + g
            c0 = pl.multiple_of(c * c_len, c_len)
            qc = q_ref[0, pl.ds(c0, c_len), :]
            kc = k_ref[0, pl.ds(c0, c_len), :]
            items.append(dict(c=c, c0=c0, kc=kc, qb=qc.astype(BF16), qq=jnp.concatenate([qc, qc], axis=1)))
        work = [(it, j, hd) for it in items for j, hd in enumerate(heads)]
        st = [_dot_nt((it["kc"] * hd["own"]).astype(BF16), it["qb"]) for it, j, hd in work]
        cross = [_dot_nt(jnp.concatenate([sf_scr[j, it["c"]], sb_scr[j, it["c"]]], axis=1).astype(BF16),
                         (it["qq"] * hd["qd"]).astype(BF16)) for it, j, hd in work]
        o = [_dot(vt_ref[0, hd["rows"], pl.ds(it["c0"], c_len)], (s * hd["dt"]).astype(BF16)) + x
             for (it, j, hd), s, x in zip(work, st, cross)]
        ys = []
        for v in o:
            mu = jnp.mean(v, axis=0, keepdims=True)
            vc = v - mu
            var = jnp.mean(vc * vc, axis=0, keepdims=True)
            ys.append(vc * lax.rsqrt(var + EPS))
        for g, it in enumerate(items):
            y = jnp.concatenate(ys[2 * g:2 * g + 2], axis=0).T
            o_ref[0, pl.ds(it["c0"], c_len), :] = y * gng_ref[...] + gnb_ref[...]
        return carry

    lax.fori_loop(0, n_c // group, outputs, 0)


def _retention(dec, rq, rk, rvt, s0, gn_g, gn_b, layer, chunk):
    b, t, _ = rq.shape
    n_pair = RET_HEADS // 2
    pair = 2 * RET_DK
    n_c = t // chunk
    group = math.gcd(n_c, _RET_GROUP)
    state_buf = pltpu.VMEM((2, n_c, RET_DV, pair), F32)
    return pl.pallas_call(
        functools.partial(_ret_kernel, chunk=chunk, group=group, layer=layer),
        grid=(b, n_pair),
        in_specs=[pl.BlockSpec(memory_space=pltpu.SMEM),
                  pl.BlockSpec((1, t, pair), lambda i, h: (i, 0, h)),
                  pl.BlockSpec((1, t, pair), lambda i, h: (i, 0, h)),
                  pl.BlockSpec((1, 2 * RET_DV, t), lambda i, h: (i, h, 0)),
                  pl.BlockSpec((1, 1, 4, RET_DV, pair), lambda i, h: (i, h, 0, 0, 0)),
                  pl.BlockSpec((None, 1, pair), lambda i, h: (layer, 0, h)),
                  pl.BlockSpec((None, 1, pair), lambda i, h: (layer, 0, h))],
        out_specs=(pl.BlockSpec((1, t, pair), lambda i, h: (i, 0, h)),
                   pl.BlockSpec((1, 1, 4, RET_DV, pair), lambda i, h: (i, h, 0, 0, 0))),
        out_shape=(jax.ShapeDtypeStruct((b, t, RET_WIDTH), F32),
                   jax.ShapeDtypeStruct((b, n_pair, 4, RET_DV, pair), F32)),
        scratch_shapes=[state_buf, state_buf],
        compiler_params=_params(("arbitrary", "arbitrary")),
        name="retention",
    )(dec, rq, rk, rvt, s0, gn_g, gn_b)


def _final_kernel(x_ref, mod_ref, om_ref, or_ref, oc_ref, g_ref, wo_ref, lng_ref, lnb_ref, o_ref, *, alpha):
    n1, n2 = MLA_WIDTH, MLA_WIDTH + RET_WIDTH
    y = _dot((om_ref[0] * g_ref[0, :, :n1]).astype(BF16), wo_ref[:n1, :])
    y += _dot((or_ref[0] * g_ref[0, :, n1:n2]).astype(BF16), wo_ref[n1:n2, :])
    y += _dot((oc_ref[0] * g_ref[0, :, n2:]).astype(BF16), wo_ref[n2:, :])

    v = alpha * x_ref[0] + mod_ref[0, 2:3, :] * y
    mu = jnp.mean(v, axis=-1, keepdims=True)
    vc = v - mu
    var = jnp.mean(vc * vc, axis=-1, keepdims=True)
    o_ref[0] = vc * lax.rsqrt(var + EPS) * lng_ref[...] + lnb_ref[...]


def _final(x, mod_all, mod_row, o_mla, o_ret, o_conv, gates, w_out, ln_g, ln_b, layer, tm, alpha):
    b, t, d = x.shape
    const = lambda a: _layer_spec(a, layer)
    row = lambda n: pl.BlockSpec((1, tm, n), lambda i, j: (i, j, 0))
    return pl.pallas_call(
        functools.partial(_final_kernel, alpha=alpha),
        grid=(b, t // tm),
        in_specs=[row(d),
                  _mod_spec(mod_all, layer, mod_row, 0),
                  row(MLA_WIDTH), row(RET_WIDTH), row(CONV_WIDTH), row(d),
                  const(w_out), const(ln_g), const(ln_b)],
        out_specs=row(d),
        out_shape=jax.ShapeDtypeStruct((b, t, d), F32),
        compiler_params=_params(("arbitrary", "arbitrary")),
        name="final",
    )(x, mod_all, o_mla, o_ret, o_conv, gates, w_out, ln_g, ln_b)


def _tile(n, pref):
    return pref if n % pref == 0 else n


def kernel(x, c, ctx, c_ctx, w_mod, b_mod, w_in, mla_q_norm, w_uq, mla_kv_norm, w_ukv, ret_decay_fwd, ret_decay_bwd,
           ret_gn_g, ret_gn_b, conv_dw, conv_dw_b, conv_ln_g, conv_ln_b, conv_pw, conv_pw_b, w_out, ln_g, ln_b):
    depth = w_mod.shape[0]
    b, t, d = x.shape
    t_ctx = ctx.shape[1]
    alpha = (2 * depth) ** 0.25
    n_mod_rows = 16
    cc = jnp.concatenate([c, c_ctx[None, :], jnp.zeros((n_mod_rows - b - 1, d), c.dtype)], axis=0)
    mod_all = _modulation(cc, w_mod, b_mod).reshape(depth, n_mod_rows, 3, d)
    tab_x = _rope_table_block(t, True)
    tab_c = _rope_table_block(t_ctx, False)
    s_zero = jnp.zeros((b, RET_HEADS // 2, 4, RET_DV, 2 * RET_DK), F32)

    row3 = lambda a: a.reshape(depth, 1, -1)
    weights = _prep_weights(w_in, w_uq, w_ukv)
    gq, gkv = row3(mla_q_norm), row3(mla_kv_norm)
    dec = jnp.stack([ret_decay_fwd, ret_decay_bwd], axis=1).astype(F32)
    gn_g, gn_b = row3(ret_gn_g), row3(ret_gn_b)
    dw = jnp.pad(conv_dw, ((0, 0), (0, 1), (0, 0)))
    conv = (dw, row3(conv_dw_b), row3(conv_ln_g), row3(conv_ln_b), conv_pw.astype(BF16), row3(conv_pw_b))
    tail = (w_out.astype(BF16), row3(ln_g), row3(ln_b))
    tm_x, tm_c = _tile(t, 512), _tile(t_ctx, 256)
    tq_x, tq_c = _tile(t, 256), _tile(t_ctx, 256)

    hc = ctx
    for l in range(depth):
        need_ctx = l < depth - 1
        cq, ck, cvt, crq, crk, crvt, cg, cconv = _proj(hc, mod_all, b, tab_c, weights, gq, gkv, conv, l, tm_c)
        xq, xk, xvt, xrq, xrk, xrvt, xg, xconv = _proj(x, mod_all, None, tab_x, weights, gq, gkv, conv, l, tm_x)

        oc_ret, s_ctx = _retention(dec, crq, crk, crvt, s_zero, gn_g, gn_b, l, tq_c)
        o_ret, _ = _retention(dec, xrq, xrk, xrvt, s_ctx, gn_g, gn_b, l, tq_x)
        o_mla = _attention(xq, [ck, xk], [cvt, xvt], tq_x)
        x_new = _final(x, mod_all, None, o_mla, o_ret, xconv, xg, *tail, layer=l, tm=tm_x, alpha=alpha)
        if need_ctx:
            oc_mla = _attention(cq, [ck], [cvt], tq_c)
            hc = _final(hc, mod_all, b, oc_mla, oc_ret, cconv, cg, *tail, layer=l, tm=tm_c, alpha=alpha)
        x = x_new
    return x
```

```python
import functools
import math

import jax
import jax.numpy as jnp
import numpy as np
from jax import lax
from jax.experimental import pallas as pl
from jax.experimental.pallas import tpu as pltpu

GRID_W = 64
MLA_HEADS = 8
MLA_NOPE = 64
MLA_ROPE = 32
MLA_V = 64
MLA_QK = MLA_NOPE + MLA_ROPE
MLA_WIDTH = MLA_HEADS * MLA_V
Q_LORA = 256
KV_LORA = 128
RET_HEADS = 4
RET_DK = 64
RET_DV = 64
RET_WIDTH = RET_HEADS * RET_DV
CONV_WIDTH = 256
CONV_K = 31
ROPE_BASE = 10000.0
EPS = 1e-5

LANES = 128
HEAD_SLOT = LANES
VMEM_LIMIT_BYTES = 56 * 1024 * 1024
LOG2E = 1.4426950408889634

F32 = jnp.float32
BF16 = jnp.bfloat16

_NT = (((1,), (1,)), ((), ()))


def _dot(a, b):
    return jnp.dot(a, b, preferred_element_type=F32)


def _dot_nt(a, b):
    return lax.dot_general(a, b, _NT, preferred_element_type=F32)


def _params(semantics):
    return pltpu.CompilerParams(dimension_semantics=semantics, vmem_limit_bytes=VMEM_LIMIT_BYTES)


def _rot_cols(w, unit):
    q = unit // 4
    w5 = w.reshape(w.shape[:-1] + (w.shape[-1] // unit, 2, 2, q))
    return jnp.stack([-w5[..., 1, :], w5[..., 0, :]], axis=-2).reshape(w.shape)


def _rope_tables(length, unit, reps):
    d2 = unit // 2
    t = np.arange(length, dtype=np.int32)
    inv = np.float32(ROPE_BASE) ** (-np.arange(0, d2, 2, dtype=np.float32) / np.float32(d2))

    def half(pos):
        ang = pos.astype(np.float32)[:, None] * inv[None, :]
        return np.concatenate([np.cos(ang)] * 2, axis=-1), np.concatenate([np.sin(ang)] * 2, axis=-1)

    cr, sr = half(t // GRID_W)
    cc, sc = half(t % GRID_W)
    cos = np.concatenate([cr, cc], axis=-1).astype(np.float32)
    sin = np.concatenate([sr, sc], axis=-1).astype(np.float32)
    return np.tile(cos, (1, reps)), np.tile(sin, (1, reps))


def _rope_table_block(length, rotate):
    if rotate:
        cos_r, sin_r = _rope_tables(length, RET_DK, RET_HEADS)
        cos_m, sin_m = _rope_tables(length, MLA_ROPE, MLA_HEADS)
    else:
        cos_r = np.ones((length, RET_WIDTH), np.float32)
        sin_r = np.zeros((length, RET_WIDTH), np.float32)
        cos_m = np.ones((length, MLA_HEADS * MLA_ROPE), np.float32)
        sin_m = np.zeros((length, MLA_HEADS * MLA_ROPE), np.float32)
    kslot = np.concatenate([cos_m[:, :MLA_ROPE], sin_m[:, :MLA_ROPE],
                            np.zeros((length, LANES - 2 * MLA_ROPE), np.float32)], axis=-1)
    return jnp.asarray(np.concatenate([cos_r, sin_r, cos_m, sin_m, kslot], axis=-1))


def _placement_matrices():
    q_place = np.zeros((MLA_HEADS, MLA_ROPE, MLA_HEADS, HEAD_SLOT), np.float32)
    k_place = np.zeros((LANES, MLA_HEADS, HEAD_SLOT), np.float32)
    for d in range(MLA_ROPE):
        for h in range(MLA_HEADS):
            q_place[h, d, h, MLA_NOPE + d] = 1.0
            k_place[d, h, MLA_NOPE + d] = 1.0
            k_place[MLA_ROPE + d, h, MLA_NOPE + d] = 1.0
    return (q_place.reshape(MLA_HEADS * MLA_ROPE, MLA_HEADS * HEAD_SLOT),
            k_place.reshape(LANES, MLA_HEADS * HEAD_SLOT))


def _prep_weights(w_in, w_uq, w_ukv):
    depth, d, _ = w_in.shape
    sizes = (Q_LORA, KV_LORA, MLA_ROPE, MLA_WIDTH, RET_HEADS * RET_DK, RET_HEADS * RET_DK, RET_WIDTH,
             RET_WIDTH, 2 * CONV_WIDTH, CONV_WIDTH)
    offs = [0]
    for s in sizes:
        offs.append(offs[-1] + s)
    seg = [w_in[..., offs[i]:offs[i + 1]] for i in range(len(sizes))]
    wq, wkv, wkr, wg_mla, wrq, wrk, wrv, wg_ret, wglu, wg_conv = seg
    kslot = jnp.concatenate([wkr, _rot_cols(wkr, MLA_ROPE), jnp.zeros((depth, d, LANES - 2 * MLA_ROPE), w_in.dtype)],
                            axis=-1)
    w_row = jnp.concatenate([wq, wkv, kslot, wrq, _rot_cols(wrq, RET_DK), wrk, _rot_cols(wrk, RET_DK),
                             wg_mla, wg_ret, wg_conv, wglu], axis=-1).astype(BF16)
    w_rvt = jnp.swapaxes(wrv, 1, 2).astype(BF16)

    uq = w_uq.reshape(depth, Q_LORA, MLA_HEADS, MLA_QK)
    pad = ((0, 0), (0, 0), (0, 0), (0, HEAD_SLOT - MLA_NOPE))
    uq_nope = jnp.pad(uq[..., :MLA_NOPE], pad).reshape(depth, Q_LORA, -1)
    uq_rope = uq[..., MLA_NOPE:].reshape(depth, Q_LORA, MLA_HEADS * MLA_ROPE)
    w_q2 = jnp.concatenate([uq_rope, _rot_cols(uq_rope, MLA_ROPE)], axis=-1).astype(BF16)
    q_place, k_place = (jnp.broadcast_to(jnp.asarray(m, BF16), (depth,) + m.shape) for m in _placement_matrices())
    w_qcomb = jnp.concatenate([uq_nope.astype(BF16), q_place], axis=1)

    ukv = w_ukv.reshape(depth, KV_LORA, MLA_HEADS, MLA_NOPE + MLA_V)
    uk = jnp.pad(ukv[..., :MLA_NOPE], pad).reshape(depth, KV_LORA, -1)
    w_kcomb = jnp.concatenate([uk.astype(BF16), k_place], axis=1)
    w_uvt = jnp.swapaxes(ukv[..., MLA_NOPE:].reshape(depth, KV_LORA, MLA_WIDTH), 1, 2).astype(BF16)
    return w_row, w_rvt, w_q2, w_qcomb, w_kcomb, w_uvt


def _mod_kernel(c_ref, w_ref, b_ref, o_ref):
    c = c_ref[...]
    a = (c * jax.nn.sigmoid(c)).astype(BF16)
    o_ref[0] = _dot(a, w_ref[0].astype(BF16)) + b_ref[0]


def _modulation(cc, w_mod, b_mod):
    depth, d, d3 = w_mod.shape
    nblk = d3 // d
    return pl.pallas_call(
        _mod_kernel,
        grid=(depth, nblk),
        in_specs=[pl.BlockSpec(cc.shape, lambda l, j: (0, 0)),
                  pl.BlockSpec((1, d, d), lambda l, j: (l, 0, j)),
                  pl.BlockSpec((1, 1, d), lambda l, j: (l, 0, j))],
        out_specs=pl.BlockSpec((1, cc.shape[0], d), lambda l, j: (l, 0, j)),
        out_shape=jax.ShapeDtypeStruct((depth, cc.shape[0], d3), F32),
        compiler_params=_params(("arbitrary", "arbitrary")),
        name="mod",
    )(cc, w_mod, b_mod.reshape(depth, 1, d3))


_C_Q, _C_KV, _C_KS, _C_RQ, _C_RQR, _C_RK, _C_RKR, _C_G, _C_GLU, _C_END = (
    0, 256, 384, 512, 768, 1024, 1280, 1536, 2560, 3072)
_T_CR, _T_SR, _T_CM, _T_SM, _T_KS, _T_END = 0, 256, 512, 768, 1024, 1152


_HALO = 16
_CONV_ROWS = 64


def _rms(x, g):
    return x * lax.rsqrt(jnp.mean(x * x, axis=-1, keepdims=True) + EPS) * g


def _proj_kernel(x_ref, xp_ref, xn_ref, mod_ref, tab_ref, w_ref, wrvt_ref, gq_ref, gkv_ref, wq2_ref, wqc_ref,
                 wkc_ref, wuvt_ref, dw_ref, dwb_ref, clg_ref, clb_ref, pw_ref, pwb_ref, *out_refs, keys_only):
    if keys_only:
        k_ref, vt_ref, rk_ref, rvt_ref = out_refs
    else:
        q_ref, k_ref, vt_ref, rq_ref, rk_ref, rvt_ref, g_ref, oc_ref, upad, ushift = out_refs
    tm = x_ref.shape[1]
    shift = mod_ref[0, 0:1, :]
    scale = mod_ref[0, 1:2, :]
    pair = 2 * HEAD_SLOT

    def modulate(x):
        mu = jnp.mean(x, axis=-1, keepdims=True)
        xc = x - mu
        var = jnp.mean(xc * xc, axis=-1, keepdims=True)
        return (xc * lax.rsqrt(var + EPS) * (1.0 + scale) + shift).astype(BF16)

    u = modulate(x_ref[0])

    def seg(lo, hi):
        return _dot(u, w_ref[:, lo:hi])

    def store_pairs(ref, val):
        for p in range(MLA_HEADS // 2):
            ref[0, p] = val[:, p * pair:(p + 1) * pair]

    def keys():
        nkv = _rms(seg(_C_KV, _C_KS), gkv_ref[...]).astype(BF16)
        ks = seg(_C_KS, _C_RQ) * tab_ref[:, _T_KS:_T_END]
        store_pairs(k_ref, _dot(jnp.concatenate([nkv, ks.astype(BF16)], axis=1), wkc_ref[...]).astype(BF16))
        vt_ref[0] = _dot_nt(wuvt_ref[...], nkv).astype(BF16)

    cos_r = tab_ref[:, _T_CR:_T_SR]
    sin_r = tab_ref[:, _T_SR:_T_CM]

    def ret_keys():
        rk_ref[0] = (seg(_C_RK, _C_RKR) * cos_r + seg(_C_RKR, _C_G) * sin_r) * (RET_DK ** -0.5)
        rvt_ref[0] = _dot_nt(wrvt_ref[...], u).astype(BF16)

    if keys_only:
        keys()
        ret_keys()
        return

    i = pl.program_id(0)
    u_ext = jnp.concatenate([modulate(xp_ref[0]), u, modulate(xn_ref[0])], axis=0)
    glu = _dot(u_ext, w_ref[:, _C_GLU:_C_END])
    uc = glu[:, :CONV_WIDTH] * jax.nn.sigmoid(glu[:, CONV_WIDTH:])
    row = lax.broadcasted_iota(jnp.int32, uc.shape, 0)
    inside = ((row >= _HALO) | (i > 0)) & ((row < _HALO + tm) | (i < pl.num_programs(0) - 1))
    upad[...] = jnp.where(inside, uc, 0.0)
    span = tm + 2 * _HALO - 8
    for r in range(1, 8):
        ushift[r - 1] = upad[r:r + span, :]
    base = _HALO - CONV_K // 2
    conv_blocks = []

    def conv_rows(n_blocks):
        for _ in range(n_blocks):
            rb = len(conv_blocks) * _CONV_ROWS
            acc = jnp.zeros((_CONV_ROWS, CONV_WIDTH), F32) + dwb_ref[...]
            for j in range(CONV_K):
                a, r = divmod(base + j, 8)
                src = upad if r == 0 else ushift.at[r - 1]
                acc = acc + src[8 * a + rb:8 * a + rb + _CONV_ROWS, :] * dw_ref[j:j + 1, :]
            mu = jnp.mean(acc, axis=-1, keepdims=True)
            ac = acc - mu
            var = jnp.mean(ac * ac, axis=-1, keepdims=True)
            yc = ac * lax.rsqrt(var + EPS) * clg_ref[...] + clb_ref[...]
            conv_blocks.append((yc * jax.nn.sigmoid(yc)).astype(BF16))

    n_conv = tm // _CONV_ROWS
    per_stage = -(-n_conv // 4)

    nq = _rms(seg(_C_Q, _C_KV), gq_ref[...]).astype(BF16)
    qc = _dot(nq, wq2_ref[...])
    nr = MLA_HEADS * MLA_ROPE
    q_rope = qc[:, :nr] * tab_ref[:, _T_CM:_T_SM] + qc[:, nr:] * tab_ref[:, _T_SM:_T_KS]
    q = _dot(jnp.concatenate([nq, q_rope.astype(BF16)], axis=1), wqc_ref[...])
    store_pairs(q_ref, (q * (MLA_QK ** -0.5 * LOG2E)).astype(BF16))
    conv_rows(min(per_stage, n_conv - len(conv_blocks)))

    keys()
    conv_rows(min(per_stage, n_conv - len(conv_blocks)))

    rq_ref[0] = seg(_C_RQ, _C_RQR) * cos_r + seg(_C_RQR, _C_RK) * sin_r
    ret_keys()
    conv_rows(min(per_stage, n_conv - len(conv_blocks)))

    g = seg(_C_G, _C_GLU)
    g_ref[0] = (g * jax.nn.sigmoid(g)).astype(g_ref.dtype)
    conv_rows(n_conv - len(conv_blocks))

    oc_ref[0] = (_dot(jnp.concatenate(conv_blocks, axis=0), pw_ref[...]) + pwb_ref[...]).astype(oc_ref.dtype)


def _layer_spec(a, layer):
    return pl.BlockSpec((None,) + a.shape[1:], lambda *_: (layer,) + (0,) * (a.ndim - 1))


def _mod_spec(mod_all, layer, shared_row, batch_axis):
    def index(*grid):
        return (layer, grid[batch_axis] if shared_row is None else shared_row, 0, 0)
    return pl.BlockSpec((None, 1) + mod_all.shape[2:], index)


def _proj(x, mod_all, mod_row, tab, weights, gq, gkv, conv, layer, tm, keys_only=False):
    b, t, d = x.shape
    w_row, w_rvt, w_q2, w_qcomb, w_kcomb, w_uvt = weights
    const = lambda a: _layer_spec(a, layer)
    row = lambda n: pl.BlockSpec((1, tm, n), lambda i, j: (j, i, 0))
    col = lambda n: pl.BlockSpec((1, n, tm), lambda i, j: (j, 0, i))
    n_pair, pair = MLA_HEADS // 2, 2 * HEAD_SLOT
    pairs = pl.BlockSpec((1, n_pair, tm, pair), lambda i, j: (j, 0, i, 0))
    per_tile = tm // _HALO
    last = t // _HALO - 1
    halo_prev = pl.BlockSpec((1, _HALO, d), lambda i, j: (j, jnp.maximum(i * per_tile - 1, 0), 0))
    halo_next = pl.BlockSpec((1, _HALO, d), lambda i, j: (j, jnp.minimum((i + 1) * per_tile, last), 0))
    sds = jax.ShapeDtypeStruct
    outs = dict(q=(sds((b, n_pair, t, pair), BF16), pairs),
                k=(sds((b, n_pair, t, pair), BF16), pairs),
                vt=(sds((b, MLA_WIDTH, t), BF16), col(MLA_WIDTH)),
                rq=(sds((b, t, RET_WIDTH), F32), row(RET_WIDTH)),
                rk=(sds((b, t, RET_WIDTH), F32), row(RET_WIDTH)),
                rvt=(sds((b, RET_WIDTH, t), BF16), col(RET_WIDTH)),
                gates=(sds((b, t, d), BF16), row(d)),
                conv=(sds((b, t, CONV_WIDTH), BF16), row(CONV_WIDTH)))
    names = ("k", "vt", "rk", "rvt") if keys_only else tuple(outs)
    scratch = [] if keys_only else [pltpu.VMEM((tm + 2 * _HALO, CONV_WIDTH), F32),
                                    pltpu.VMEM((7, tm + 2 * _HALO - 8, CONV_WIDTH), F32)]
    res = pl.pallas_call(
        functools.partial(_proj_kernel, keys_only=keys_only),
        grid=(t // tm, b),
        in_specs=[row(d), halo_prev, halo_next,
                  _mod_spec(mod_all, layer, mod_row, 1),
                  pl.BlockSpec((tm, _T_END), lambda i, j: (i, 0)),
                  const(w_row), const(w_rvt), const(gq), const(gkv), const(w_q2), const(w_qcomb), const(w_kcomb),
                  const(w_uvt)] + [const(a) for a in conv],
        out_specs=tuple(outs[n][1] for n in names),
        out_shape=tuple(outs[n][0] for n in names),
        scratch_shapes=scratch,
        compiler_params=_params(("arbitrary", "arbitrary")),
        name="proj",
    )(x, x, x, mod_all, tab, w_row, w_rvt, gq, gkv, w_q2, w_qcomb, w_kcomb, w_uvt, *conv)
    res = dict(zip(names, res))
    res["vt"] = res["vt"].reshape(b, n_pair, 2 * MLA_V, t)
    return res


_ATTN_KEY_BLOCK = 512


def _attn_kernel(*refs, n_seg, tq):
    q_ref = refs[0]
    k_refs = refs[1:1 + n_seg]
    vt_refs = refs[1 + n_seg:1 + 2 * n_seg]
    o_ref = refs[1 + 2 * n_seg]
    s_bufs = refs[2 + 2 * n_seg:4 + 2 * n_seg]
    m_bufs = refs[4 + 2 * n_seg:6 + 2 * n_seg]
    n_pair = q_ref.shape[1]
    n_q = q_ref.shape[2] // tq
    n_items = n_pair * n_q
    lanes = [slice(j * HEAD_SLOT, (j + 1) * HEAD_SLOT) for j in range(2)]
    rows = [slice(j * MLA_V, (j + 1) * MLA_V) for j in range(2)]

    blocks, off = [], 0
    for si in range(n_seg):
        ts = k_refs[si].shape[2]
        kb = min(_ATTN_KEY_BLOCK, ts)
        blocks += [(si, b0, kb, off + b0) for b0 in range(0, ts, kb)]
        off += ts

    def locate(item):
        if isinstance(item, int):
            return item // n_q, (item % n_q) * tq
        pair = lax.div(item, jnp.int32(n_q))
        return pair, pl.multiple_of((item - pair * n_q) * tq, tq)

    def phase(nxt, cur):
        if nxt is not None:
            item_n, s_n, m_n = nxt
            pn, rn = locate(item_n)
            qn = [q_ref[0, pn, pl.ds(rn, tq), lanes[j]] for j in range(2)]
            mx = [None, None]
        if cur is not None:
            item_c, s_c, m_c = cur
            pc, rc = locate(item_c)
            mc = [m_c[j] for j in range(2)]
            acc, l = [None, None], [None, None]
        for si, b0, kb, o0 in blocks:
            for j in range(2):
                if nxt is not None:
                    s = _dot_nt(k_refs[si][0, pn, b0:b0 + kb, lanes[j]], qn[j])
                    s_n[j, o0:o0 + kb, :] = s
                    mb = jnp.max(s, axis=0, keepdims=True)
                    mx[j] = mb if mx[j] is None else jnp.maximum(mx[j], mb)
                if cur is not None:
                    p = jnp.exp2(s_c[j, o0:o0 + kb, :] - mc[j])
                    ls = jnp.sum(p, axis=0, keepdims=True)
                    pv = _dot(vt_refs[si][0, pc, rows[j], b0:b0 + kb], p.astype(BF16))
                    l[j] = ls if l[j] is None else l[j] + ls
                    acc[j] = pv if acc[j] is None else acc[j] + pv
        if nxt is not None:
            for j in range(2):
                m_n[j] = mx[j]
        if cur is not None:
            outs = [acc[j] * (1.0 / l[j]) for j in range(2)]
            o_ref[0, pc, pl.ds(rc, tq), :] = jnp.concatenate(outs, axis=0).T.astype(o_ref.dtype)

    buf = [(s_bufs[0], m_bufs[0]), (s_bufs[1], m_bufs[1])]
    phase((0, *buf[0]), None)
    assert n_items % 2 == 0

    def body(i, carry):
        n = 2 * i
        phase((n + 1, *buf[1]), (n, *buf[0]))
        phase((jnp.minimum(n + 2, n_items - 1), *buf[0]), (n + 1, *buf[1]))
        return carry

    lax.fori_loop(0, n_items // 2, body, 0)


def _attention(q, ks, vts, tq):
    b, n_pair, t, pair = q.shape
    n_seg = len(ks)
    n_keys = sum(k.shape[2] for k in ks)
    whole = lambda a: pl.BlockSpec((1,) + a.shape[1:], lambda i: (i, 0, 0, 0))
    score_buf = pltpu.VMEM((2, n_keys, tq), F32)
    max_buf = pltpu.VMEM((2, 1, tq), F32)
    return pl.pallas_call(
        functools.partial(_attn_kernel, n_seg=n_seg, tq=tq),
        grid=(b,),
        in_specs=[whole(q)] + [whole(k) for k in ks] + [whole(v) for v in vts],
        out_specs=pl.BlockSpec((1, n_pair, t, 2 * MLA_V), lambda i: (i, 0, 0, 0)),
        out_shape=jax.ShapeDtypeStruct((b, n_pair, t, 2 * MLA_V), BF16),
        scratch_shapes=[score_buf, score_buf, max_buf, max_buf],
        compiler_params=_params(("arbitrary",)),
        name="attention",
    )(q, *ks, *vts)


_RET_GROUP = 8


def _log_sigmoid(x):
    return jnp.minimum(x, 0.0) - jnp.log1p(jnp.exp(-jnp.abs(x)))


def _ret_kernel(dec_ref, *refs, chunk, group, layer, states_only):
    if states_only:
        k_ref, vt_ref, s0_ref, sfin_ref, sf_scr, sb_scr = refs
    else:
        q_ref, k_ref, vt_ref, s0_ref, gng_ref, gnb_ref, o_ref, sfin_ref, sf_scr, sb_scr = refs
    c_len = chunk
    t = k_ref.shape[1]
    n_c = t // c_len
    hp = pl.program_id(1)
    r = lax.broadcasted_iota(jnp.int32, (c_len, LANES), 0).astype(F32)
    lane = lax.broadcasted_iota(jnp.int32, (c_len, LANES), 1)
    km = lax.broadcasted_iota(jnp.int32, (c_len, c_len), 0)
    qn = lax.broadcasted_iota(jnp.int32, (c_len, c_len), 1)
    diff = (qn - km).astype(F32)

    heads = []
    for j in range(2):
        h = hp * 2 + j
        lg_f = _log_sigmoid(jnp.full((c_len, LANES), dec_ref[layer, 0, h], F32))
        lg_b = _log_sigmoid(jnp.full((c_len, LANES), dec_ref[layer, 1, h], F32))
        own = ((lane >= j * RET_DK) & (lane < (j + 1) * RET_DK)).astype(F32)
        lg_f2 = _log_sigmoid(jnp.full((c_len, c_len), dec_ref[layer, 0, h], F32))
        lg_b2 = _log_sigmoid(jnp.full((c_len, c_len), dec_ref[layer, 1, h], F32))
        heads.append(dict(
            own=own,
            kdf=jnp.exp(lg_f * (c_len - 1.0 - r)) * own,
            kdb=jnp.exp(lg_b * r) * own,
            qd=jnp.concatenate([jnp.exp(lg_f * (r + 1.0)), jnp.exp(lg_b * (c_len - r))], axis=1),
            cf=jnp.exp(_log_sigmoid(jnp.full((RET_DV, LANES), dec_ref[layer, 0, h], F32)) * c_len),
            cb=jnp.exp(_log_sigmoid(jnp.full((RET_DV, LANES), dec_ref[layer, 1, h], F32)) * c_len),
            dt=jnp.exp(jnp.where(diff >= 0, lg_f2 * diff, -lg_b2 * diff)),
            rows=slice(j * RET_DV, (j + 1) * RET_DV),
        ))

    def local(i, carry):
        for g in range(group):
            c = i * group + g
            c0 = pl.multiple_of(c * c_len, c_len)
            kc = k_ref[0, pl.ds(c0, c_len), :]
            for j, hd in enumerate(heads):
                vt = vt_ref[0, hd["rows"], pl.ds(c0, c_len)]
                sf_scr[j, c] = _dot(vt, (kc * hd["kdf"]).astype(BF16))
                sb_scr[j, c] = _dot(vt, (kc * hd["kdb"]).astype(BF16))
        return carry

    lax.fori_loop(0, n_c // group, local, 0)

    for j, hd in enumerate(heads):
        sf = s0_ref[0, 0, 2 * j]
        for c in range(n_c):
            a = sf_scr[j, c]
            sf_scr[j, c] = sf
            sf = hd["cf"] * sf + a
        sfin_ref[0, 0, 2 * j] = sf
        sb = s0_ref[0, 0, 2 * j + 1]
        for c in reversed(range(n_c)):
            a = sb_scr[j, c]
            sb_scr[j, c] = sb
            sb = hd["cb"] * sb + a
        sfin_ref[0, 0, 2 * j + 1] = sb

    if states_only:
        return

    def outputs(i, carry):
        items = []
        for g in range(group):
            c = i * group + g
            c0 = pl.multiple_of(c * c_len, c_len)
            qc = q_ref[0, pl.ds(c0, c_len), :]
            kc = k_ref[0, pl.ds(c0, c_len), :]
            items.append(dict(c=c, c0=c0, kc=kc, qb=qc.astype(BF16), qq=jnp.concatenate([qc, qc], axis=1)))
        work = [(it, j, hd) for it in items for j, hd in enumerate(heads)]
        st = [_dot_nt((it["kc"] * hd["own"]).astype(BF16), it["qb"]) for it, j, hd in work]
        cross = [_dot_nt(jnp.concatenate([sf_scr[j, it["c"]], sb_scr[j, it["c"]]], axis=1).astype(BF16),
                         (it["qq"] * hd["qd"]).astype(BF16)) for it, j, hd in work]
        o = [_dot(vt_ref[0, hd["rows"], pl.ds(it["c0"], c_len)], (s * hd["dt"]).astype(BF16)) + x
             for (it, j, hd), s, x in zip(work, st, cross)]
        ys = []
        for v in o:
            mu = jnp.mean(v, axis=0, keepdims=True)
            vc = v - mu
            var = jnp.mean(vc * vc, axis=0, keepdims=True)
            ys.append(vc * lax.rsqrt(var + EPS))
        for g, it in enumerate(items):
            y = jnp.concatenate(ys[2 * g:2 * g + 2], axis=0).T
            o_ref[0, pl.ds(it["c0"], c_len), :] = y * gng_ref[...] + gnb_ref[...]
        return carry

    lax.fori_loop(0, n_c // group, outputs, 0)


def _retention(dec, rq, rk, rvt, s0, gn_g, gn_b, layer, chunk):
    b, t, _ = rk.shape
    states_only = rq is None
    n_pair = RET_HEADS // 2
    pair = 2 * RET_DK
    n_c = t // chunk
    group = math.gcd(n_c, _RET_GROUP)
    state_buf = pltpu.VMEM((2, n_c, RET_DV, pair), F32)
    seq = pl.BlockSpec((1, t, pair), lambda i, h: (i, 0, h))
    state = pl.BlockSpec((1, 1, 4, RET_DV, pair), lambda i, h: (i, h, 0, 0, 0))
    affine = pl.BlockSpec((None, 1, pair), lambda i, h: (layer, 0, h))
    state_shape = jax.ShapeDtypeStruct((b, n_pair, 4, RET_DV, pair), F32)
    smem = pl.BlockSpec(memory_space=pltpu.SMEM)
    vt_spec = pl.BlockSpec((1, 2 * RET_DV, t), lambda i, h: (i, h, 0))
    if states_only:
        in_specs, args = [smem, seq, vt_spec, state], (dec, rk, rvt, s0)
        out_specs, out_shape = state, state_shape
    else:
        in_specs, args = [smem, seq, seq, vt_spec, state, affine, affine], (dec, rq, rk, rvt, s0, gn_g, gn_b)
        out_specs, out_shape = (seq, state), (jax.ShapeDtypeStruct((b, t, RET_WIDTH), F32), state_shape)
    res = pl.pallas_call(
        functools.partial(_ret_kernel, chunk=chunk, group=group, layer=layer, states_only=states_only),
        grid=(b, n_pair),
        in_specs=in_specs,
        out_specs=out_specs,
        out_shape=out_shape,
        scratch_shapes=[state_buf, state_buf],
        compiler_params=_params(("arbitrary", "arbitrary")),
        name="retention",
    )(*args)
    return (None, res) if states_only else res


def _final_kernel(x_ref, mod_ref, om_ref, or_ref, oc_ref, g_ref, wo_ref, lng_ref, lnb_ref, o_ref, *, alpha):
    n1, n2 = MLA_WIDTH, MLA_WIDTH + RET_WIDTH
    o_mla = jnp.concatenate([om_ref[0, p] for p in range(om_ref.shape[1])], axis=1)
    y = _dot((o_mla * g_ref[0, :, :n1]).astype(BF16), wo_ref[:n1, :])
    y += _dot((or_ref[0] * g_ref[0, :, n1:n2]).astype(BF16), wo_ref[n1:n2, :])
    y += _dot((oc_ref[0] * g_ref[0, :, n2:]).astype(BF16), wo_ref[n2:, :])

    v = alpha * x_ref[0] + mod_ref[0, 2:3, :] * y
    mu = jnp.mean(v, axis=-1, keepdims=True)
    vc = v - mu
    var = jnp.mean(vc * vc, axis=-1, keepdims=True)
    o_ref[0] = vc * lax.rsqrt(var + EPS) * lng_ref[...] + lnb_ref[...]


def _final(x, mod_all, mod_row, o_mla, o_ret, o_conv, gates, w_out, ln_g, ln_b, layer, tm, alpha):
    b, t, d = x.shape
    const = lambda a: _layer_spec(a, layer)
    row = lambda n: pl.BlockSpec((1, tm, n), lambda i, j: (i, j, 0))
    return pl.pallas_call(
        functools.partial(_final_kernel, alpha=alpha),
        grid=(b, t // tm),
        in_specs=[row(d),
                  _mod_spec(mod_all, layer, mod_row, 0),
                  pl.BlockSpec((1, o_mla.shape[1], tm, o_mla.shape[3]), lambda i, j: (i, 0, j, 0)),
                  row(RET_WIDTH), row(CONV_WIDTH), row(d),
                  const(w_out), const(ln_g), const(ln_b)],
        out_specs=row(d),
        out_shape=jax.ShapeDtypeStruct((b, t, d), F32),
        compiler_params=_params(("arbitrary", "arbitrary")),
        name="final",
    )(x, mod_all, o_mla, o_ret, o_conv, gates, w_out, ln_g, ln_b)


def _tile(n, pref):
    return pref if n % pref == 0 else n


def kernel(x, c, ctx, c_ctx, w_mod, b_mod, w_in, mla_q_norm, w_uq, mla_kv_norm, w_ukv, ret_decay_fwd, ret_decay_bwd,
           ret_gn_g, ret_gn_b, conv_dw, conv_dw_b, conv_ln_g, conv_ln_b, conv_pw, conv_pw_b, w_out, ln_g, ln_b):
    depth = w_mod.shape[0]
    b, t, d = x.shape
    t_ctx = ctx.shape[1]
    alpha = (2 * depth) ** 0.25
    n_mod_rows = 16
    cc = jnp.concatenate([c, c_ctx[None, :], jnp.zeros((n_mod_rows - b - 1, d), c.dtype)], axis=0)
    mod_all = _modulation(cc, w_mod, b_mod).reshape(depth, n_mod_rows, 3, d)
    tab_x = _rope_table_block(t, True)
    tab_c = _rope_table_block(t_ctx, False)
    s_zero = jnp.zeros((b, RET_HEADS // 2, 4, RET_DV, 2 * RET_DK), F32)

    row3 = lambda a: a.reshape(depth, 1, -1)
    weights = _prep_weights(w_in, w_uq, w_ukv)
    gq, gkv = row3(mla_q_norm), row3(mla_kv_norm)
    dec = jnp.stack([ret_decay_fwd, ret_decay_bwd], axis=1).astype(F32)
    gn_g, gn_b = row3(ret_gn_g), row3(ret_gn_b)
    dw = jnp.pad(conv_dw, ((0, 0), (0, 1), (0, 0)))
    conv = (dw, row3(conv_dw_b), row3(conv_ln_g), row3(conv_ln_b), conv_pw.astype(BF16), row3(conv_pw_b))
    tail = (w_out.astype(BF16), row3(ln_g), row3(ln_b))
    tm_x, tm_c = _tile(t, 512), _tile(t_ctx, 256)
    tq_x, tq_c = _tile(t, 256), _tile(t_ctx, 256)

    hc = ctx
    for l in range(depth):
        need_ctx = l < depth - 1
        pc = _proj(hc, mod_all, b, tab_c, weights, gq, gkv, conv, l, tm_c, keys_only=not need_ctx)
        px = _proj(x, mod_all, None, tab_x, weights, gq, gkv, conv, l, tm_x)

        oc_ret, s_ctx = _retention(dec, pc.get("rq"), pc["rk"], pc["rvt"], s_zero, gn_g, gn_b, l, tq_c)
        o_ret, _ = _retention(dec, px["rq"], px["rk"], px["rvt"], s_ctx, gn_g, gn_b, l, tq_x)
        o_mla = _attention(px["q"], [pc["k"], px["k"]], [pc["vt"], px["vt"]], tq_x)
        x_new = _final(x, mod_all, None, o_mla, o_ret, px["conv"], px["gates"], *tail, layer=l, tm=tm_x, alpha=alpha)
        if need_ctx:
            oc_mla = _attention(pc["q"], [pc["k"]], [pc["vt"]], tq_c)
            hc = _final(hc, mod_all, b, oc_mla, oc_ret, pc["conv"], pc["gates"], *tail, layer=l, tm=tm_c,
                        alpha=alpha)
        x = x_new
    return x
```

```python
import functools
import math

import jax
import jax.numpy as jnp
import numpy as np
from jax import lax
from jax.experimental import pallas as pl
from jax.experimental.pallas import tpu as pltpu

GRID_W = 64
MLA_HEADS = 8
MLA_NOPE = 64
MLA_ROPE = 32
MLA_V = 64
MLA_QK = MLA_NOPE + MLA_ROPE
MLA_WIDTH = MLA_HEADS * MLA_V
Q_LORA = 256
KV_LORA = 128
RET_HEADS = 4
RET_DK = 64
RET_DV = 64
RET_WIDTH = RET_HEADS * RET_DV
CONV_WIDTH = 256
CONV_K = 31
ROPE_BASE = 10000.0
EPS = 1e-5

LANES = 128
HEAD_SLOT = LANES
VMEM_LIMIT_BYTES = 56 * 1024 * 1024
LOG2E = 1.4426950408889634

F32 = jnp.float32
BF16 = jnp.bfloat16

_NT = (((1,), (1,)), ((), ()))


def _dot(a, b):
    return jnp.dot(a, b, preferred_element_type=F32)


def _dot_nt(a, b):
    return lax.dot_general(a, b, _NT, preferred_element_type=F32)


def _params(semantics):
    return pltpu.CompilerParams(dimension_semantics=semantics, vmem_limit_bytes=VMEM_LIMIT_BYTES)


def _rot_cols(w, unit):
    xp = np if isinstance(w, np.ndarray) else jnp
    q = unit // 4
    w5 = w.reshape(w.shape[:-1] + (w.shape[-1] // unit, 2, 2, q))
    return xp.stack([-w5[..., 1, :], w5[..., 0, :]], axis=-2).reshape(w.shape)


def _rope_tables(length, unit, reps):
    d2 = unit // 2
    t = np.arange(length, dtype=np.int32)
    inv = np.float32(ROPE_BASE) ** (-np.arange(0, d2, 2, dtype=np.float32) / np.float32(d2))

    def half(pos):
        ang = pos.astype(np.float32)[:, None] * inv[None, :]
        return np.concatenate([np.cos(ang)] * 2, axis=-1), np.concatenate([np.sin(ang)] * 2, axis=-1)

    cr, sr = half(t // GRID_W)
    cc, sc = half(t % GRID_W)
    cos = np.concatenate([cr, cc], axis=-1).astype(np.float32)
    sin = np.concatenate([sr, sc], axis=-1).astype(np.float32)
    return np.tile(cos, (1, reps)), np.tile(sin, (1, reps))


def _rope_table_block(length, rotate):
    if rotate:
        cos_r, sin_r = _rope_tables(length, RET_DK, RET_HEADS)
        cos_m, sin_m = _rope_tables(length, MLA_ROPE, MLA_HEADS)
    else:
        cos_r = np.ones((length, RET_WIDTH), np.float32)
        sin_r = np.zeros((length, RET_WIDTH), np.float32)
        cos_m = np.ones((length, MLA_HEADS * MLA_ROPE), np.float32)
        sin_m = np.zeros((length, MLA_HEADS * MLA_ROPE), np.float32)
    kslot = np.concatenate([cos_m[:, :MLA_ROPE], sin_m[:, :MLA_ROPE],
                            np.zeros((length, LANES - 2 * MLA_ROPE), np.float32)], axis=-1)
    return jnp.asarray(np.concatenate([cos_r, sin_r, cos_m, sin_m, kslot], axis=-1))


def _placement_matrices():
    q_place = np.zeros((MLA_HEADS, MLA_ROPE, MLA_HEADS, HEAD_SLOT), np.float32)
    k_place = np.zeros((LANES, MLA_HEADS, HEAD_SLOT), np.float32)
    for d in range(MLA_ROPE):
        for h in range(MLA_HEADS):
            q_place[h, d, h, MLA_NOPE + d] = 1.0
            k_place[d, h, MLA_NOPE + d] = 1.0
            k_place[MLA_ROPE + d, h, MLA_NOPE + d] = 1.0
    return (q_place.reshape(MLA_HEADS * MLA_ROPE, MLA_HEADS * HEAD_SLOT),
            k_place.reshape(LANES, MLA_HEADS * HEAD_SLOT))


def _prep_weights(w_in, w_uq, w_ukv):
    depth, d, _ = w_in.shape
    sizes = (Q_LORA, KV_LORA, MLA_ROPE, MLA_WIDTH, RET_HEADS * RET_DK, RET_HEADS * RET_DK, RET_WIDTH,
             RET_WIDTH, 2 * CONV_WIDTH, CONV_WIDTH)
    offs = [0]
    for s in sizes:
        offs.append(offs[-1] + s)
    seg = [w_in[..., offs[i]:offs[i + 1]] for i in range(len(sizes))]
    wq, wkv, wkr, wg_mla, wrq, wrk, wrv, wg_ret, wglu, wg_conv = seg
    kslot = jnp.concatenate([wkr, _rot_cols(wkr, MLA_ROPE), jnp.zeros((depth, d, LANES - 2 * MLA_ROPE), w_in.dtype)],
                            axis=-1)
    w_row = jnp.concatenate([wq, wkv, kslot, wrq, wrk, wg_mla, wg_ret, wg_conv, wglu],
                            axis=-1).astype(BF16)
    w_rvt = jnp.swapaxes(wrv, 1, 2).astype(BF16)

    uq = w_uq.reshape(depth, Q_LORA, MLA_HEADS, MLA_QK)
    pad = ((0, 0), (0, 0), (0, 0), (0, HEAD_SLOT - MLA_NOPE))
    uq_nope = jnp.pad(uq[..., :MLA_NOPE], pad).reshape(depth, Q_LORA, -1)
    uq_rope = uq[..., MLA_NOPE:].reshape(depth, Q_LORA, MLA_HEADS * MLA_ROPE)
    w_q2 = jnp.concatenate([uq_rope, _rot_cols(uq_rope, MLA_ROPE)], axis=-1).astype(BF16)
    q_place, k_place = (jnp.broadcast_to(jnp.asarray(m, BF16), (depth,) + m.shape) for m in _placement_matrices())
    w_qcomb = jnp.concatenate([uq_nope.astype(BF16), q_place], axis=1)

    ukv = w_ukv.reshape(depth, KV_LORA, MLA_HEADS, MLA_NOPE + MLA_V)
    uk = jnp.pad(ukv[..., :MLA_NOPE], pad).reshape(depth, KV_LORA, -1)
    w_kcomb = jnp.concatenate([uk.astype(BF16), k_place], axis=1)
    w_uvt = jnp.swapaxes(ukv[..., MLA_NOPE:].reshape(depth, KV_LORA, MLA_WIDTH), 1, 2).astype(BF16)
    return w_row, w_rvt, w_q2, w_qcomb, w_kcomb, w_uvt


def _mod_kernel(c_ref, w_ref, b_ref, o_ref):
    c = c_ref[...]
    a = (c * jax.nn.sigmoid(c)).astype(BF16)
    o_ref[0] = _dot(a, w_ref[0].astype(BF16)) + b_ref[0]


def _modulation(cc, w_mod, b_mod):
    depth, d, d3 = w_mod.shape
    nblk = d3 // d
    return pl.pallas_call(
        _mod_kernel,
        grid=(depth, nblk),
        in_specs=[pl.BlockSpec(cc.shape, lambda l, j: (0, 0)),
                  pl.BlockSpec((1, d, d), lambda l, j: (l, 0, j)),
                  pl.BlockSpec((1, 1, d), lambda l, j: (l, 0, j))],
        out_specs=pl.BlockSpec((1, cc.shape[0], d), lambda l, j: (l, 0, j)),
        out_shape=jax.ShapeDtypeStruct((depth, cc.shape[0], d3), F32),
        compiler_params=_params(("arbitrary", "arbitrary")),
        name="mod",
    )(cc, w_mod, b_mod.reshape(depth, 1, d3))


_C_Q, _C_KV, _C_KS, _C_RQ, _C_RK, _C_G, _C_GLU, _C_END = 0, 256, 384, 512, 768, 1024, 2048, 2560
_T_CR, _T_SR, _T_CM, _T_SM, _T_KS, _T_END = 0, 256, 512, 768, 1024, 1152


_HALO = 16
_CONV_ROWS = 64


def _rms(x, g):
    return x * lax.rsqrt(jnp.mean(x * x, axis=-1, keepdims=True) + EPS) * g


def _proj_kernel(x_ref, xp_ref, xn_ref, mod_ref, tab_ref, rot_ref, w_ref, wrvt_ref, gq_ref, gkv_ref, wq2_ref, wqc_ref,
                 wkc_ref, wuvt_ref, dw_ref, dwb_ref, clg_ref, clb_ref, pw_ref, pwb_ref, *out_refs, keys_only):
    if keys_only:
        k_ref, vt_ref, rk_ref, rvt_ref = out_refs
    else:
        q_ref, k_ref, vt_ref, rq_ref, rk_ref, rvt_ref, g_ref, oc_ref, upad, ushift = out_refs
    tm = x_ref.shape[1]
    shift = mod_ref[0, 0:1, :]
    scale = mod_ref[0, 1:2, :]
    pair = 2 * HEAD_SLOT

    def modulate(x):
        mu = jnp.mean(x, axis=-1, keepdims=True)
        xc = x - mu
        var = jnp.mean(xc * xc, axis=-1, keepdims=True)
        return (xc * lax.rsqrt(var + EPS) * (1.0 + scale) + shift).astype(BF16)

    u = modulate(x_ref[0])

    def seg(lo, hi):
        return _dot(u, w_ref[:, lo:hi])

    def store_pairs(ref, val):
        for p in range(MLA_HEADS // 2):
            ref[0, p] = val[:, p * pair:(p + 1) * pair]

    def keys():
        nkv = _rms(seg(_C_KV, _C_KS), gkv_ref[...]).astype(BF16)
        ks = seg(_C_KS, _C_RQ) * tab_ref[:, _T_KS:_T_END]
        store_pairs(k_ref, _dot(jnp.concatenate([nkv, ks.astype(BF16)], axis=1), wkc_ref[...]).astype(BF16))
        vt_ref[0] = _dot_nt(wuvt_ref[...], nkv).astype(BF16)

    def ret_rope(raw):
        rot = _dot(raw.astype(BF16), rot_ref[...])
        return raw * tab_ref[:, _T_CR:_T_SR] + rot * tab_ref[:, _T_SR:_T_CM]

    def ret_keys():
        rk_ref[0] = ret_rope(seg(_C_RK, _C_G)) * (RET_DK ** -0.5)
        rvt_ref[0] = _dot_nt(wrvt_ref[...], u).astype(BF16)

    if keys_only:
        keys()
        ret_keys()
        return

    i = pl.program_id(0)
    u_ext = jnp.concatenate([modulate(xp_ref[0]), u, modulate(xn_ref[0])], axis=0)
    glu = _dot(u_ext, w_ref[:, _C_GLU:_C_END])
    uc = glu[:, :CONV_WIDTH] * jax.nn.sigmoid(glu[:, CONV_WIDTH:])
    row = lax.broadcasted_iota(jnp.int32, uc.shape, 0)
    inside = ((row >= _HALO) | (i > 0)) & ((row < _HALO + tm) | (i < pl.num_programs(0) - 1))
    upad[...] = jnp.where(inside, uc, 0.0)
    span = tm + 2 * _HALO - 8
    for r in range(1, 8):
        ushift[r - 1] = upad[r:r + span, :]
    base = _HALO - CONV_K // 2
    conv_blocks = []

    def conv_rows(n_blocks):
        for _ in range(n_blocks):
            rb = len(conv_blocks) * _CONV_ROWS
            acc = jnp.zeros((_CONV_ROWS, CONV_WIDTH), F32) + dwb_ref[...]
            for j in range(CONV_K):
                a, r = divmod(base + j, 8)
                src = upad if r == 0 else ushift.at[r - 1]
                acc = acc + src[8 * a + rb:8 * a + rb + _CONV_ROWS, :] * dw_ref[j:j + 1, :]
            mu = jnp.mean(acc, axis=-1, keepdims=True)
            ac = acc - mu
            var = jnp.mean(ac * ac, axis=-1, keepdims=True)
            yc = ac * lax.rsqrt(var + EPS) * clg_ref[...] + clb_ref[...]
            conv_blocks.append((yc * jax.nn.sigmoid(yc)).astype(BF16))

    n_conv = tm // _CONV_ROWS
    per_stage = -(-n_conv // 4)

    nq = _rms(seg(_C_Q, _C_KV), gq_ref[...]).astype(BF16)
    qc = _dot(nq, wq2_ref[...])
    nr = MLA_HEADS * MLA_ROPE
    q_rope = qc[:, :nr] * tab_ref[:, _T_CM:_T_SM] + qc[:, nr:] * tab_ref[:, _T_SM:_T_KS]
    q = _dot(jnp.concatenate([nq, q_rope.astype(BF16)], axis=1), wqc_ref[...])
    store_pairs(q_ref, (q * (MLA_QK ** -0.5 * LOG2E)).astype(BF16))
    conv_rows(min(per_stage, n_conv - len(conv_blocks)))

    keys()
    conv_rows(min(per_stage, n_conv - len(conv_blocks)))

    rq_ref[0] = ret_rope(seg(_C_RQ, _C_RK))
    ret_keys()
    conv_rows(min(per_stage, n_conv - len(conv_blocks)))

    g = seg(_C_G, _C_GLU)
    g_ref[0] = (g * jax.nn.sigmoid(g)).astype(g_ref.dtype)
    conv_rows(n_conv - len(conv_blocks))

    oc_ref[0] = (_dot(jnp.concatenate(conv_blocks, axis=0), pw_ref[...]) + pwb_ref[...]).astype(oc_ref.dtype)


def _layer_spec(a, layer):
    return pl.BlockSpec((None,) + a.shape[1:], lambda *_: (layer,) + (0,) * (a.ndim - 1),
                        pipeline_mode=pl.Buffered(1))


def _mod_spec(mod_all, layer, shared_row, batch_axis):
    def index(*grid):
        return (layer, grid[batch_axis] if shared_row is None else shared_row, 0, 0)
    return pl.BlockSpec((None, 1) + mod_all.shape[2:], index)


def _proj(x, mod_all, mod_row, tab, weights, gq, gkv, conv, layer, tm, keys_only=False):
    b, t, d = x.shape
    w_row, w_rvt, w_q2, w_qcomb, w_kcomb, w_uvt = weights
    const = lambda a: _layer_spec(a, layer)
    row = lambda n: pl.BlockSpec((1, tm, n), lambda i, j: (j, i, 0))
    col = lambda n: pl.BlockSpec((1, n, tm), lambda i, j: (j, 0, i))
    n_pair, pair = MLA_HEADS // 2, 2 * HEAD_SLOT
    pairs = pl.BlockSpec((1, n_pair, tm, pair), lambda i, j: (j, 0, i, 0))
    per_tile = tm // _HALO
    last = t // _HALO - 1
    halo_prev = pl.BlockSpec((1, _HALO, d), lambda i, j: (j, jnp.maximum(i * per_tile - 1, 0), 0))
    halo_next = pl.BlockSpec((1, _HALO, d), lambda i, j: (j, jnp.minimum((i + 1) * per_tile, last), 0))
    rot = jnp.asarray(_rot_cols(np.eye(RET_WIDTH, dtype=np.float32), RET_DK), BF16)
    sds = jax.ShapeDtypeStruct
    outs = dict(q=(sds((b, n_pair, t, pair), BF16), pairs),
                k=(sds((b, n_pair, t, pair), BF16), pairs),
                vt=(sds((b, MLA_WIDTH, t), BF16), col(MLA_WIDTH)),
                rq=(sds((b, t, RET_WIDTH), F32), row(RET_WIDTH)),
                rk=(sds((b, t, RET_WIDTH), F32), row(RET_WIDTH)),
                rvt=(sds((b, RET_WIDTH, t), BF16), col(RET_WIDTH)),
                gates=(sds((b, t, d), BF16), row(d)),
                conv=(sds((b, t, CONV_WIDTH), BF16), row(CONV_WIDTH)))
    names = ("k", "vt", "rk", "rvt") if keys_only else tuple(outs)
    scratch = [] if keys_only else [pltpu.VMEM((tm + 2 * _HALO, CONV_WIDTH), F32),
                                    pltpu.VMEM((7, tm + 2 * _HALO - 8, CONV_WIDTH), F32)]
    res = pl.pallas_call(
        functools.partial(_proj_kernel, keys_only=keys_only),
        grid=(t // tm, b),
        in_specs=[row(d), halo_prev, halo_next,
                  _mod_spec(mod_all, layer, mod_row, 1),
                  pl.BlockSpec((tm, _T_END), lambda i, j: (i, 0)),
                  pl.BlockSpec(rot.shape, lambda i, j: (0, 0)),
                  const(w_row), const(w_rvt), const(gq), const(gkv), const(w_q2), const(w_qcomb), const(w_kcomb),
                  const(w_uvt)] + [const(a) for a in conv],
        out_specs=tuple(outs[n][1] for n in names),
        out_shape=tuple(outs[n][0] for n in names),
        scratch_shapes=scratch,
        compiler_params=_params(("arbitrary", "arbitrary")),
        name="proj",
    )(x, x, x, mod_all, tab, rot, w_row, w_rvt, gq, gkv, w_q2, w_qcomb, w_kcomb, w_uvt, *conv)
    res = dict(zip(names, res))
    res["vt"] = res["vt"].reshape(b, n_pair, 2 * MLA_V, t)
    return res


_ATTN_KEY_BLOCK = 512


def _attn_kernel(*refs, n_seg, tq):
    q_ref = refs[0]
    k_refs = refs[1:1 + n_seg]
    vt_refs = refs[1 + n_seg:1 + 2 * n_seg]
    o_ref = refs[1 + 2 * n_seg]
    s_bufs = refs[2 + 2 * n_seg:4 + 2 * n_seg]
    m_bufs = refs[4 + 2 * n_seg:6 + 2 * n_seg]
    n_pair = q_ref.shape[1]
    n_q = q_ref.shape[2] // tq
    n_items = n_pair * n_q
    lanes = [slice(j * HEAD_SLOT, (j + 1) * HEAD_SLOT) for j in range(2)]
    rows = [slice(j * MLA_V, (j + 1) * MLA_V) for j in range(2)]

    blocks, off = [], 0
    for si in range(n_seg):
        ts = k_refs[si].shape[2]
        kb = min(_ATTN_KEY_BLOCK, ts)
        blocks += [(si, b0, kb, off + b0) for b0 in range(0, ts, kb)]
        off += ts

    def locate(item):
        if isinstance(item, int):
            return item // n_q, (item % n_q) * tq
        pair = lax.div(item, jnp.int32(n_q))
        return pair, pl.multiple_of((item - pair * n_q) * tq, tq)

    def phase(nxt, cur):
        if nxt is not None:
            item_n, s_n, m_n = nxt
            pn, rn = locate(item_n)
            qn = [q_ref[0, pn, pl.ds(rn, tq), lanes[j]] for j in range(2)]
            mx = [None, None]
        if cur is not None:
            item_c, s_c, m_c = cur
            pc, rc = locate(item_c)
            mc = [m_c[j] for j in range(2)]
            acc, l = [None, None], [None, None]
        for si, b0, kb, o0 in blocks:
            for j in range(2):
                if nxt is not None:
                    s = _dot_nt(k_refs[si][0, pn, b0:b0 + kb, lanes[j]], qn[j])
                    s_n[j, o0:o0 + kb, :] = s
                    mb = jnp.max(s, axis=0, keepdims=True)
                    mx[j] = mb if mx[j] is None else jnp.maximum(mx[j], mb)
                if cur is not None:
                    p = jnp.exp2(s_c[j, o0:o0 + kb, :] - mc[j])
                    ls = jnp.sum(p, axis=0, keepdims=True)
                    pv = _dot(vt_refs[si][0, pc, rows[j], b0:b0 + kb], p.astype(BF16))
                    l[j] = ls if l[j] is None else l[j] + ls
                    acc[j] = pv if acc[j] is None else acc[j] + pv
        if nxt is not None:
            for j in range(2):
                m_n[j] = mx[j]
        if cur is not None:
            outs = [acc[j] * (1.0 / l[j]) for j in range(2)]
            o_ref[0, pc, pl.ds(rc, tq), :] = jnp.concatenate(outs, axis=0).T.astype(o_ref.dtype)

    buf = [(s_bufs[0], m_bufs[0]), (s_bufs[1], m_bufs[1])]
    phase((0, *buf[0]), None)
    assert n_items % 2 == 0

    def body(i, carry):
        n = 2 * i
        phase((n + 1, *buf[1]), (n, *buf[0]))
        phase((jnp.minimum(n + 2, n_items - 1), *buf[0]), (n + 1, *buf[1]))
        return carry

    lax.fori_loop(0, n_items // 2, body, 0)


def _attention(q, ks, vts, tq):
    b, n_pair, t, pair = q.shape
    n_seg = len(ks)
    n_keys = sum(k.shape[2] for k in ks)
    whole = lambda a: pl.BlockSpec((1,) + a.shape[1:], lambda i: (i, 0, 0, 0))
    score_buf = pltpu.VMEM((2, n_keys, tq), F32)
    max_buf = pltpu.VMEM((2, 1, tq), F32)
    return pl.pallas_call(
        functools.partial(_attn_kernel, n_seg=n_seg, tq=tq),
        grid=(b,),
        in_specs=[whole(q)] + [whole(k) for k in ks] + [whole(v) for v in vts],
        out_specs=pl.BlockSpec((1, n_pair, t, 2 * MLA_V), lambda i: (i, 0, 0, 0)),
        out_shape=jax.ShapeDtypeStruct((b, n_pair, t, 2 * MLA_V), BF16),
        scratch_shapes=[score_buf, score_buf, max_buf, max_buf],
        compiler_params=_params(("arbitrary",)),
        name="attention",
    )(q, *ks, *vts)


_RET_GROUP = 8


def _log_sigmoid(x):
    return jnp.minimum(x, 0.0) - jnp.log1p(jnp.exp(-jnp.abs(x)))


def _ret_kernel(dec_ref, *refs, chunk, group, layer, states_only):
    if states_only:
        k_ref, vt_ref, s0_ref, sfin_ref, sf_scr, sb_scr, kd_scr, qd_scr, dt_scr, cc_scr = refs
    else:
        (q_ref, k_ref, vt_ref, s0_ref, gng_ref, gnb_ref, o_ref, sfin_ref, sf_scr, sb_scr, kd_scr, qd_scr, dt_scr,
         cc_scr) = refs
    c_len = chunk
    t = k_ref.shape[1]
    n_c = t // c_len
    hp = pl.program_id(0)

    @pl.when(pl.program_id(1) == 0)
    def _():
        r = lax.broadcasted_iota(jnp.int32, (c_len, LANES), 0).astype(F32)
        lane = lax.broadcasted_iota(jnp.int32, (c_len, LANES), 1)
        km = lax.broadcasted_iota(jnp.int32, (c_len, c_len), 0)
        qn = lax.broadcasted_iota(jnp.int32, (c_len, c_len), 1)
        diff = (qn - km).astype(F32)
        for j in range(2):
            h = hp * 2 + j
            lg_f = _log_sigmoid(jnp.full((c_len, LANES), dec_ref[layer, 0, h], F32))
            lg_b = _log_sigmoid(jnp.full((c_len, LANES), dec_ref[layer, 1, h], F32))
            own = ((lane >= j * RET_DK) & (lane < (j + 1) * RET_DK)).astype(F32)
            lg_f2 = _log_sigmoid(jnp.full((c_len, c_len), dec_ref[layer, 0, h], F32))
            lg_b2 = _log_sigmoid(jnp.full((c_len, c_len), dec_ref[layer, 1, h], F32))
            kd_scr[j] = jnp.concatenate([jnp.exp(lg_f * (c_len - 1.0 - r)) * own, jnp.exp(lg_b * r) * own, own], axis=1)
            qd_scr[j] = jnp.concatenate([jnp.exp(lg_f * (r + 1.0)), jnp.exp(lg_b * (c_len - r))], axis=1)
            dt_scr[j] = jnp.exp(jnp.where(diff >= 0, lg_f2 * diff, -lg_b2 * diff))
            cc_scr[j] = jnp.concatenate(
                [jnp.exp(_log_sigmoid(jnp.full((RET_DV, LANES), dec_ref[layer, 0, h], F32)) * c_len),
                 jnp.exp(_log_sigmoid(jnp.full((RET_DV, LANES), dec_ref[layer, 1, h], F32)) * c_len)], axis=1)

    heads = []
    for j in range(2):
        heads.append(dict(
            kdf=kd_scr[j, :, 0:LANES],
            kdb=kd_scr[j, :, LANES:2 * LANES],
            own=kd_scr[j, :, 2 * LANES:3 * LANES],
            qd=qd_scr[j],
            dt=dt_scr[j],
            cf=cc_scr[j, :, 0:LANES],
            cb=cc_scr[j, :, LANES:2 * LANES],
            rows=slice(j * RET_DV, (j + 1) * RET_DV),
        ))

    def local(i, carry):
        for g in range(group):
            c = i * group + g
            c0 = pl.multiple_of(c * c_len, c_len)
            kc = k_ref[0, pl.ds(c0, c_len), :]
            for j, hd in enumerate(heads):
                vt = vt_ref[0, hd["rows"], pl.ds(c0, c_len)]
                sf_scr[j, c] = _dot(vt, (kc * hd["kdf"]).astype(BF16))
                sb_scr[j, c] = _dot(vt, (kc * hd["kdb"]).astype(BF16))
        return carry

    lax.fori_loop(0, n_c // group, local, 0)

    for j, hd in enumerate(heads):
        sf = s0_ref[0, 0, 2 * j]
        for c in range(n_c):
            a = sf_scr[j, c]
            sf_scr[j, c] = sf
            sf = hd["cf"] * sf + a
        sfin_ref[0, 0, 2 * j] = sf
        sb = s0_ref[0, 0, 2 * j + 1]
        for c in reversed(range(n_c)):
            a = sb_scr[j, c]
            sb_scr[j, c] = sb
            sb = hd["cb"] * sb + a
        sfin_ref[0, 0, 2 * j + 1] = sb

    if states_only:
        return

    def outputs(i, carry):
        items = []
        for g in range(group):
            c = i * group + g
            c0 = pl.multiple_of(c * c_len, c_len)
            qc = q_ref[0, pl.ds(c0, c_len), :]
            kc = k_ref[0, pl.ds(c0, c_len), :]
            items.append(dict(c=c, c0=c0, kc=kc, qb=qc.astype(BF16), qq=jnp.concatenate([qc, qc], axis=1)))
        work = [(it, j, hd) for it in items for j, hd in enumerate(heads)]
        st = [_dot_nt((it["kc"] * hd["own"]).astype(BF16), it["qb"]) for it, j, hd in work]
        cross = [_dot_nt(jnp.concatenate([sf_scr[j, it["c"]], sb_scr[j, it["c"]]], axis=1).astype(BF16),
                         (it["qq"] * hd["qd"]).astype(BF16)) for it, j, hd in work]
        o = [_dot(vt_ref[0, hd["rows"], pl.ds(it["c0"], c_len)], (s * hd["dt"]).astype(BF16)) + x
             for (it, j, hd), s, x in zip(work, st, cross)]
        ys = []
        for v in o:
            mu = jnp.mean(v, axis=0, keepdims=True)
            vc = v - mu
            var = jnp.mean(vc * vc, axis=0, keepdims=True)
            ys.append(vc * lax.rsqrt(var + EPS))
        for g, it in enumerate(items):
            y = jnp.concatenate(ys[2 * g:2 * g + 2], axis=0).T
            o_ref[0, pl.ds(it["c0"], c_len), :] = y * gng_ref[...] + gnb_ref[...]
        return carry

    lax.fori_loop(0, n_c // group, outputs, 0)


def _retention(dec, rq, rk, rvt, s0, gn_g, gn_b, layer, chunk):
    b, t, _ = rk.shape
    states_only = rq is None
    n_pair = RET_HEADS // 2
    pair = 2 * RET_DK
    n_c = t // chunk
    group = math.gcd(n_c, _RET_GROUP)
    state_buf = pltpu.VMEM((2, n_c, RET_DV, pair), F32)
    tables = [pltpu.VMEM((2, chunk, 3 * LANES), F32), pltpu.VMEM((2, chunk, 2 * LANES), F32),
              pltpu.VMEM((2, chunk, chunk), F32), pltpu.VMEM((2, RET_DV, 2 * LANES), F32)]
    seq = pl.BlockSpec((1, t, pair), lambda h, i: (i, 0, h))
    state = pl.BlockSpec((1, 1, 4, RET_DV, pair), lambda h, i: (i, h, 0, 0, 0))
    affine = pl.BlockSpec((None, 1, pair), lambda h, i: (layer, 0, h))
    state_shape = jax.ShapeDtypeStruct((b, n_pair, 4, RET_DV, pair), F32)
    smem = pl.BlockSpec(memory_space=pltpu.SMEM)
    vt_spec = pl.BlockSpec((1, 2 * RET_DV, t), lambda h, i: (i, h, 0))
    if states_only:
        in_specs, args = [smem, seq, vt_spec, state], (dec, rk, rvt, s0)
        out_specs, out_shape = state, state_shape
    else:
        in_specs, args = [smem, seq, seq, vt_spec, state, affine, affine], (dec, rq, rk, rvt, s0, gn_g, gn_b)
        out_specs, out_shape = (seq, state), (jax.ShapeDtypeStruct((b, t, RET_WIDTH), F32), state_shape)
    res = pl.pallas_call(
        functools.partial(_ret_kernel, chunk=chunk, group=group, layer=layer, states_only=states_only),
        grid=(n_pair, b),
        in_specs=in_specs,
        out_specs=out_specs,
        out_shape=out_shape,
        scratch_shapes=[state_buf, state_buf] + tables,
        compiler_params=_params(("arbitrary", "arbitrary")),
        name="retention",
    )(*args)
    return (None, res) if states_only else res


def _final_kernel(x_ref, mod_ref, om_ref, or_ref, oc_ref, g_ref, wo_ref, lng_ref, lnb_ref, o_ref, *, alpha):
    n1, n2 = MLA_WIDTH, MLA_WIDTH + RET_WIDTH
    o_mla = jnp.concatenate([om_ref[0, p] for p in range(om_ref.shape[1])], axis=1)
    y = _dot((o_mla * g_ref[0, :, :n1]).astype(BF16), wo_ref[:n1, :])
    y += _dot((or_ref[0] * g_ref[0, :, n1:n2]).astype(BF16), wo_ref[n1:n2, :])
    y += _dot((oc_ref[0] * g_ref[0, :, n2:]).astype(BF16), wo_ref[n2:, :])

    v = alpha * x_ref[0] + mod_ref[0, 2:3, :] * y
    mu = jnp.mean(v, axis=-1, keepdims=True)
    vc = v - mu
    var = jnp.mean(vc * vc, axis=-1, keepdims=True)
    o_ref[0] = vc * lax.rsqrt(var + EPS) * lng_ref[...] + lnb_ref[...]


def _final(x, mod_all, mod_row, o_mla, o_ret, o_conv, gates, w_out, ln_g, ln_b, layer, tm, alpha):
    b, t, d = x.shape
    const = lambda a: _layer_spec(a, layer)
    row = lambda n: pl.BlockSpec((1, tm, n), lambda i, j: (i, j, 0))
    return pl.pallas_call(
        functools.partial(_final_kernel, alpha=alpha),
        grid=(b, t // tm),
        in_specs=[row(d),
                  _mod_spec(mod_all, layer, mod_row, 0),
                  pl.BlockSpec((1, o_mla.shape[1], tm, o_mla.shape[3]), lambda i, j: (i, 0, j, 0)),
                  row(RET_WIDTH), row(CONV_WIDTH), row(d),
                  const(w_out), const(ln_g), const(ln_b)],
        out_specs=row(d),
        out_shape=jax.ShapeDtypeStruct((b, t, d), F32),
        compiler_params=_params(("arbitrary", "arbitrary")),
        name="final",
    )(x, mod_all, o_mla, o_ret, o_conv, gates, w_out, ln_g, ln_b)


def _tile(n, pref):
    return pref if n % pref == 0 else n


def kernel(x, c, ctx, c_ctx, w_mod, b_mod, w_in, mla_q_norm, w_uq, mla_kv_norm, w_ukv, ret_decay_fwd, ret_decay_bwd,
           ret_gn_g, ret_gn_b, conv_dw, conv_dw_b, conv_ln_g, conv_ln_b, conv_pw, conv_pw_b, w_out, ln_g, ln_b):
    depth = w_mod.shape[0]
    b, t, d = x.shape
    t_ctx = ctx.shape[1]
    alpha = (2 * depth) ** 0.25
    n_mod_rows = 16
    cc = jnp.concatenate([c, c_ctx[None, :], jnp.zeros((n_mod_rows - b - 1, d), c.dtype)], axis=0)
    mod_all = _modulation(cc, w_mod, b_mod).reshape(depth, n_mod_rows, 3, d)
    tab_x = _rope_table_block(t, True)
    tab_c = _rope_table_block(t_ctx, False)
    s_zero = jnp.zeros((b, RET_HEADS // 2, 4, RET_DV, 2 * RET_DK), F32)

    row3 = lambda a: a.reshape(depth, 1, -1)
    weights = _prep_weights(w_in, w_uq, w_ukv)
    gq, gkv = row3(mla_q_norm), row3(mla_kv_norm)
    dec = jnp.stack([ret_decay_fwd, ret_decay_bwd], axis=1).astype(F32)
    gn_g, gn_b = row3(ret_gn_g), row3(ret_gn_b)
    dw = jnp.pad(conv_dw, ((0, 0), (0, 1), (0, 0)))
    conv = (dw, row3(conv_dw_b), row3(conv_ln_g), row3(conv_ln_b), conv_pw.astype(BF16), row3(conv_pw_b))
    tail = (w_out.astype(BF16), row3(ln_g), row3(ln_b))
    tm_x, tm_c = _tile(t, 1024), _tile(t_ctx, 256)
    tf_x, tf_c = _tile(t, 512), _tile(t_ctx, 256)
    tq_x, tq_c = _tile(t, 512), _tile(t_ctx, 256)
    ch_x, ch_c = _tile(t, 256), _tile(t_ctx, 256)

    hc = ctx
    for l in range(depth):
        need_ctx = l < depth - 1
        pc = _proj(hc, mod_all, b, tab_c, weights, gq, gkv, conv, l, tm_c, keys_only=not need_ctx)
        px = _proj(x, mod_all, None, tab_x, weights, gq, gkv, conv, l, tm_x)

        oc_ret, s_ctx = _retention(dec, pc.get("rq"), pc["rk"], pc["rvt"], s_zero, gn_g, gn_b, l, ch_c)
        o_ret, _ = _retention(dec, px["rq"], px["rk"], px["rvt"], s_ctx, gn_g, gn_b, l, ch_x)
        o_mla = _attention(px["q"], [pc["k"], px["k"]], [pc["vt"], px["vt"]], tq_x)
        x_new = _final(x, mod_all, None, o_mla, o_ret, px["conv"], px["gates"], *tail, layer=l, tm=tf_x, alpha=alpha)
        if need_ctx:
            oc_mla = _attention(pc["q"], [pc["k"]], [pc["vt"]], tq_c)
            hc = _final(hc, mod_all, b, oc_mla, oc_ret, pc["conv"], pc["gates"], *tail, layer=l, tm=tf_c,
                        alpha=alpha)
        x = x_new
    return x
```

```python
import functools
import math

import jax
import jax.numpy as jnp
import numpy as np
from jax import lax
from jax.experimental import pallas as pl
from jax.experimental.pallas import tpu as pltpu

GRID_W = 64
MLA_HEADS = 8
MLA_NOPE = 64
MLA_ROPE = 32
MLA_V = 64
MLA_QK = MLA_NOPE + MLA_ROPE
MLA_WIDTH = MLA_HEADS * MLA_V
Q_LORA = 256
KV_LORA = 128
RET_HEADS = 4
RET_DK = 64
RET_DV = 64
RET_WIDTH = RET_HEADS * RET_DV
CONV_WIDTH = 256
CONV_K = 31
ROPE_BASE = 10000.0
EPS = 1e-5

LANES = 128
HEAD_SLOT = LANES
VMEM_LIMIT_BYTES = 56 * 1024 * 1024
LOG2E = 1.4426950408889634

F32 = jnp.float32
BF16 = jnp.bfloat16

_NT = (((1,), (1,)), ((), ()))


def _dot(a, b):
    return jnp.dot(a, b, preferred_element_type=F32)


def _dot_nt(a, b):
    return lax.dot_general(a, b, _NT, preferred_element_type=F32)


def _params(semantics):
    return pltpu.CompilerParams(dimension_semantics=semantics, vmem_limit_bytes=VMEM_LIMIT_BYTES)


def _rot_cols(w, unit):
    xp = np if isinstance(w, np.ndarray) else jnp
    q = unit // 4
    w5 = w.reshape(w.shape[:-1] + (w.shape[-1] // unit, 2, 2, q))
    return xp.stack([-w5[..., 1, :], w5[..., 0, :]], axis=-2).reshape(w.shape)


def _rope_tables(length, unit, reps):
    d2 = unit // 2
    t = np.arange(length, dtype=np.int32)
    inv = np.float32(ROPE_BASE) ** (-np.arange(0, d2, 2, dtype=np.float32) / np.float32(d2))

    def half(pos):
        ang = pos.astype(np.float32)[:, None] * inv[None, :]
        return np.concatenate([np.cos(ang)] * 2, axis=-1), np.concatenate([np.sin(ang)] * 2, axis=-1)

    cr, sr = half(t // GRID_W)
    cc, sc = half(t % GRID_W)
    cos = np.concatenate([cr, cc], axis=-1).astype(np.float32)
    sin = np.concatenate([sr, sc], axis=-1).astype(np.float32)
    return np.tile(cos, (1, reps)), np.tile(sin, (1, reps))


def _rope_table_block(length, rotate):
    if rotate:
        cos_r, sin_r = _rope_tables(length, RET_DK, RET_HEADS)
        cos_m, sin_m = _rope_tables(length, MLA_ROPE, MLA_HEADS)
    else:
        cos_r = np.ones((length, RET_WIDTH), np.float32)
        sin_r = np.zeros((length, RET_WIDTH), np.float32)
        cos_m = np.ones((length, MLA_HEADS * MLA_ROPE), np.float32)
        sin_m = np.zeros((length, MLA_HEADS * MLA_ROPE), np.float32)
    kslot = np.concatenate([cos_m[:, :MLA_ROPE], sin_m[:, :MLA_ROPE],
                            np.zeros((length, LANES - 2 * MLA_ROPE), np.float32)], axis=-1)
    return jnp.asarray(np.concatenate([cos_r, sin_r, cos_m, sin_m, kslot], axis=-1))


def _placement_matrices():
    q_place = np.zeros((MLA_HEADS, MLA_ROPE, MLA_HEADS, HEAD_SLOT), np.float32)
    k_place = np.zeros((LANES, MLA_HEADS, HEAD_SLOT), np.float32)
    for d in range(MLA_ROPE):
        for h in range(MLA_HEADS):
            q_place[h, d, h, MLA_NOPE + d] = 1.0
            k_place[d, h, MLA_NOPE + d] = 1.0
            k_place[MLA_ROPE + d, h, MLA_NOPE + d] = 1.0
    return (q_place.reshape(MLA_HEADS * MLA_ROPE, MLA_HEADS * HEAD_SLOT),
            k_place.reshape(LANES, MLA_HEADS * HEAD_SLOT))


def _prep_weights(w_in, w_uq, w_ukv):
    depth, d, _ = w_in.shape
    sizes = (Q_LORA, KV_LORA, MLA_ROPE, MLA_WIDTH, RET_HEADS * RET_DK, RET_HEADS * RET_DK, RET_WIDTH,
             RET_WIDTH, 2 * CONV_WIDTH, CONV_WIDTH)
    offs = [0]
    for s in sizes:
        offs.append(offs[-1] + s)
    seg = [w_in[..., offs[i]:offs[i + 1]] for i in range(len(sizes))]
    wq, wkv, wkr, wg_mla, wrq, wrk, wrv, wg_ret, wglu, wg_conv = seg
    kslot = jnp.concatenate([wkr, _rot_cols(wkr, MLA_ROPE), jnp.zeros((depth, d, LANES - 2 * MLA_ROPE), w_in.dtype)],
                            axis=-1)
    w_row = jnp.concatenate([wq, wkv, kslot, wrq, wrk, wg_mla, wg_ret, wg_conv, wglu],
                            axis=-1).astype(BF16)
    w_rvt = jnp.swapaxes(wrv, 1, 2).astype(BF16)

    uq = w_uq.reshape(depth, Q_LORA, MLA_HEADS, MLA_QK)
    pad = ((0, 0), (0, 0), (0, 0), (0, HEAD_SLOT - MLA_NOPE))
    uq_nope = jnp.pad(uq[..., :MLA_NOPE], pad).reshape(depth, Q_LORA, -1)
    uq_rope = uq[..., MLA_NOPE:].reshape(depth, Q_LORA, MLA_HEADS * MLA_ROPE)
    w_q2 = jnp.concatenate([uq_rope, _rot_cols(uq_rope, MLA_ROPE)], axis=-1).astype(BF16)
    q_place, k_place = (jnp.broadcast_to(jnp.asarray(m, BF16), (depth,) + m.shape) for m in _placement_matrices())
    w_qcomb = jnp.concatenate([uq_nope.astype(BF16), q_place], axis=1)

    ukv = w_ukv.reshape(depth, KV_LORA, MLA_HEADS, MLA_NOPE + MLA_V)
    uk = jnp.pad(ukv[..., :MLA_NOPE], pad).reshape(depth, KV_LORA, -1)
    w_kcomb = jnp.concatenate([uk.astype(BF16), k_place], axis=1)
    w_uvt = jnp.swapaxes(ukv[..., MLA_NOPE:].reshape(depth, KV_LORA, MLA_WIDTH), 1, 2).astype(BF16)
    return w_row, w_rvt, w_q2, w_qcomb, w_kcomb, w_uvt


def _mod_kernel(c_ref, w_ref, b_ref, o_ref):
    c = c_ref[...]
    a = (c * jax.nn.sigmoid(c)).astype(BF16)
    o_ref[0] = _dot(a, w_ref[0].astype(BF16)) + b_ref[0]


def _modulation(cc, w_mod, b_mod):
    depth, d, d3 = w_mod.shape
    nblk = d3 // d
    return pl.pallas_call(
        _mod_kernel,
        grid=(depth, nblk),
        in_specs=[pl.BlockSpec(cc.shape, lambda l, j: (0, 0)),
                  pl.BlockSpec((1, d, d), lambda l, j: (l, 0, j)),
                  pl.BlockSpec((1, 1, d), lambda l, j: (l, 0, j))],
        out_specs=pl.BlockSpec((1, cc.shape[0], d), lambda l, j: (l, 0, j)),
        out_shape=jax.ShapeDtypeStruct((depth, cc.shape[0], d3), F32),
        compiler_params=_params(("arbitrary", "arbitrary")),
        name="mod",
    )(cc, w_mod, b_mod.reshape(depth, 1, d3))


_C_Q, _C_KV, _C_KS, _C_RQ, _C_RK, _C_G, _C_GLU, _C_END = 0, 256, 384, 512, 768, 1024, 2048, 2560
_T_CR, _T_SR, _T_CM, _T_SM, _T_KS, _T_END = 0, 256, 512, 768, 1024, 1152


_HALO = 16
_CONV_ROWS = 64


def _rms(x, g):
    return x * lax.rsqrt(jnp.mean(x * x, axis=-1, keepdims=True) + EPS) * g


def _proj_kernel(x_ref, xp_ref, xn_ref, mod_ref, tab_ref, rot_ref, w_ref, wrvt_ref, gq_ref, gkv_ref, wq2_ref, wqc_ref,
                 wkc_ref, wuvt_ref, dw_ref, dwb_ref, clg_ref, clb_ref, pw_ref, pwb_ref, *out_refs, keys_only):
    if keys_only:
        k_ref, vt_ref, rk_ref, rvt_ref = out_refs
    else:
        q_ref, k_ref, vt_ref, rq_ref, rk_ref, rvt_ref, g_ref, oc_ref, upad, ushift = out_refs
    tm = x_ref.shape[1]
    shift = mod_ref[0, 0:1, :]
    scale = mod_ref[0, 1:2, :]
    pair = 2 * HEAD_SLOT

    def modulate(x):
        mu = jnp.mean(x, axis=-1, keepdims=True)
        xc = x - mu
        var = jnp.mean(xc * xc, axis=-1, keepdims=True)
        return (xc * lax.rsqrt(var + EPS) * (1.0 + scale) + shift).astype(BF16)

    u = modulate(x_ref[0])

    def seg(lo, hi):
        return _dot(u, w_ref[:, lo:hi])

    def store_pairs(ref, val):
        for p in range(MLA_HEADS // 2):
            ref[0, p] = val[:, p * pair:(p + 1) * pair]

    def keys():
        nkv = _rms(seg(_C_KV, _C_KS), gkv_ref[...]).astype(BF16)
        ks = seg(_C_KS, _C_RQ) * tab_ref[:, _T_KS:_T_END]
        store_pairs(k_ref, _dot(jnp.concatenate([nkv, ks.astype(BF16)], axis=1), wkc_ref[...]).astype(BF16))
        vt_ref[0] = _dot_nt(wuvt_ref[...], nkv).astype(BF16)

    def ret_rope(raw):
        rot = _dot(raw.astype(BF16), rot_ref[...])
        return raw * tab_ref[:, _T_CR:_T_SR] + rot * tab_ref[:, _T_SR:_T_CM]

    def ret_keys():
        rk_ref[0] = ret_rope(seg(_C_RK, _C_G)) * (RET_DK ** -0.5)
        rvt_ref[0] = _dot_nt(wrvt_ref[...], u).astype(BF16)

    if keys_only:
        keys()
        ret_keys()
        return

    i = pl.program_id(0)
    u_ext = jnp.concatenate([modulate(xp_ref[0]), u, modulate(xn_ref[0])], axis=0)
    glu = _dot(u_ext, w_ref[:, _C_GLU:_C_END])
    uc = glu[:, :CONV_WIDTH] * jax.nn.sigmoid(glu[:, CONV_WIDTH:])
    row = lax.broadcasted_iota(jnp.int32, uc.shape, 0)
    inside = ((row >= _HALO) | (i > 0)) & ((row < _HALO + tm) | (i < pl.num_programs(0) - 1))
    upad[...] = jnp.where(inside, uc, 0.0)
    span = tm + 2 * _HALO - 8
    for r in range(1, 8):
        ushift[r - 1] = upad[r:r + span, :]
    base = _HALO - CONV_K // 2
    conv_blocks = []

    def conv_rows(n_blocks):
        for _ in range(n_blocks):
            rb = len(conv_blocks) * _CONV_ROWS
            acc = jnp.zeros((_CONV_ROWS, CONV_WIDTH), F32) + dwb_ref[...]
            for j in range(CONV_K):
                a, r = divmod(base + j, 8)
                src = upad if r == 0 else ushift.at[r - 1]
                acc = acc + src[8 * a + rb:8 * a + rb + _CONV_ROWS, :] * dw_ref[j:j + 1, :]
            mu = jnp.mean(acc, axis=-1, keepdims=True)
            ac = acc - mu
            var = jnp.mean(ac * ac, axis=-1, keepdims=True)
            yc = ac * lax.rsqrt(var + EPS) * clg_ref[...] + clb_ref[...]
            conv_blocks.append((yc * jax.nn.sigmoid(yc)).astype(BF16))

    n_conv = tm // _CONV_ROWS
    per_stage = -(-n_conv // 4)

    nq = _rms(seg(_C_Q, _C_KV), gq_ref[...]).astype(BF16)
    qc = _dot(nq, wq2_ref[...])
    nr = MLA_HEADS * MLA_ROPE
    q_rope = qc[:, :nr] * tab_ref[:, _T_CM:_T_SM] + qc[:, nr:] * tab_ref[:, _T_SM:_T_KS]
    q = _dot(jnp.concatenate([nq, q_rope.astype(BF16)], axis=1), wqc_ref[...])
    store_pairs(q_ref, (q * (MLA_QK ** -0.5 * LOG2E)).astype(BF16))
    conv_rows(min(per_stage, n_conv - len(conv_blocks)))

    keys()
    conv_rows(min(per_stage, n_conv - len(conv_blocks)))

    rq_ref[0] = ret_rope(seg(_C_RQ, _C_RK))
    ret_keys()
    conv_rows(min(per_stage, n_conv - len(conv_blocks)))

    g = seg(_C_G, _C_GLU)
    g_ref[0] = (g * jax.nn.sigmoid(g)).astype(g_ref.dtype)
    conv_rows(n_conv - len(conv_blocks))

    oc_ref[0] = (_dot(jnp.concatenate(conv_blocks, axis=0), pw_ref[...]) + pwb_ref[...]).astype(oc_ref.dtype)


def _layer_spec(a, layer):
    return pl.BlockSpec((None,) + a.shape[1:], lambda *_: (layer,) + (0,) * (a.ndim - 1),
                        pipeline_mode=pl.Buffered(1))


def _mod_spec(mod_all, layer, shared_row, batch_axis):
    def index(*grid):
        return (layer, grid[batch_axis] if shared_row is None else shared_row, 0, 0)
    return pl.BlockSpec((None, 1) + mod_all.shape[2:], index)


def _proj(x, mod_all, mod_row, tab, weights, gq, gkv, conv, layer, tm, keys_only=False):
    b, t, d = x.shape
    w_row, w_rvt, w_q2, w_qcomb, w_kcomb, w_uvt = weights
    const = lambda a: _layer_spec(a, layer)
    row = lambda n: pl.BlockSpec((1, tm, n), lambda i, j: (j, i, 0))
    col = lambda n: pl.BlockSpec((1, n, tm), lambda i, j: (j, 0, i))
    n_pair, pair = MLA_HEADS // 2, 2 * HEAD_SLOT
    pairs = pl.BlockSpec((1, n_pair, tm, pair), lambda i, j: (j, 0, i, 0))
    per_tile = tm // _HALO
    last = t // _HALO - 1
    halo_prev = pl.BlockSpec((1, _HALO, d), lambda i, j: (j, jnp.maximum(i * per_tile - 1, 0), 0))
    halo_next = pl.BlockSpec((1, _HALO, d), lambda i, j: (j, jnp.minimum((i + 1) * per_tile, last), 0))
    rot = jnp.asarray(_rot_cols(np.eye(RET_WIDTH, dtype=np.float32), RET_DK), BF16)
    sds = jax.ShapeDtypeStruct
    outs = dict(q=(sds((b, n_pair, t, pair), BF16), pairs),
                k=(sds((b, n_pair, t, pair), BF16), pairs),
                vt=(sds((b, MLA_WIDTH, t), BF16), col(MLA_WIDTH)),
                rq=(sds((b, t, RET_WIDTH), F32), row(RET_WIDTH)),
                rk=(sds((b, t, RET_WIDTH), F32), row(RET_WIDTH)),
                rvt=(sds((b, RET_WIDTH, t), BF16), col(RET_WIDTH)),
                gates=(sds((b, t, d), BF16), row(d)),
                conv=(sds((b, t, CONV_WIDTH), BF16), row(CONV_WIDTH)))
    names = ("k", "vt", "rk", "rvt") if keys_only else tuple(outs)
    scratch = [] if keys_only else [pltpu.VMEM((tm + 2 * _HALO, CONV_WIDTH), F32),
                                    pltpu.VMEM((7, tm + 2 * _HALO - 8, CONV_WIDTH), F32)]
    res = pl.pallas_call(
        functools.partial(_proj_kernel, keys_only=keys_only),
        grid=(t // tm, b),
        in_specs=[row(d), halo_prev, halo_next,
                  _mod_spec(mod_all, layer, mod_row, 1),
                  pl.BlockSpec((tm, _T_END), lambda i, j: (i, 0)),
                  pl.BlockSpec(rot.shape, lambda i, j: (0, 0)),
                  const(w_row), const(w_rvt), const(gq), const(gkv), const(w_q2), const(w_qcomb), const(w_kcomb),
                  const(w_uvt)] + [const(a) for a in conv],
        out_specs=tuple(outs[n][1] for n in names),
        out_shape=tuple(outs[n][0] for n in names),
        scratch_shapes=scratch,
        compiler_params=_params(("arbitrary", "arbitrary")),
        name="proj",
    )(x, x, x, mod_all, tab, rot, w_row, w_rvt, gq, gkv, w_q2, w_qcomb, w_kcomb, w_uvt, *conv)
    res = dict(zip(names, res))
    res["vt"] = res["vt"].reshape(b, n_pair, 2 * MLA_V, t)
    return res


_ATTN_KEY_BLOCK = 512
_SCORE_SKEW = 8


def _attn_kernel(*refs, n_seg, tq):
    q_ref = refs[0]
    k_refs = refs[1:1 + n_seg]
    vt_refs = refs[1 + n_seg:1 + 2 * n_seg]
    o_ref = refs[1 + 2 * n_seg]
    s_bufs = refs[2 + 2 * n_seg:4 + 2 * n_seg]
    m_bufs = refs[4 + 2 * n_seg:6 + 2 * n_seg]
    n_pair = q_ref.shape[1]
    n_q = q_ref.shape[2] // tq
    n_items = n_pair * n_q
    lanes = [slice(j * HEAD_SLOT, (j + 1) * HEAD_SLOT) for j in range(2)]
    rows = [slice(j * MLA_V, (j + 1) * MLA_V) for j in range(2)]

    blocks, off = [], 0
    for si in range(n_seg):
        ts = k_refs[si].shape[2]
        kb = min(_ATTN_KEY_BLOCK, ts)
        blocks += [(si, b0, kb, off + b0) for b0 in range(0, ts, kb)]
        off += ts

    def locate(item):
        if isinstance(item, int):
            return item // n_q, (item % n_q) * tq
        pair = lax.div(item, jnp.int32(n_q))
        return pair, pl.multiple_of((item - pair * n_q) * tq, tq)

    def phase(nxt, cur):
        if nxt is not None:
            item_n, s_n, m_n, skew_n = nxt
            pn, rn = locate(item_n)
            qn = [q_ref[0, pn, pl.ds(rn, tq), lanes[j]] for j in range(2)]
            mx = [None, None]
        if cur is not None:
            item_c, s_c, m_c, skew_c = cur
            pc, rc = locate(item_c)
            mc = [m_c[j] for j in range(2)]
            acc, l = [None, None], [None, None]
        for si, b0, kb, o0 in blocks:
            for j in range(2):
                if nxt is not None:
                    s = _dot_nt(k_refs[si][0, pn, b0:b0 + kb, lanes[j]], qn[j])
                    s_n[j, skew_n + o0:skew_n + o0 + kb, :] = s
                    mb = jnp.max(s, axis=0, keepdims=True)
                    mx[j] = mb if mx[j] is None else jnp.maximum(mx[j], mb)
                if cur is not None:
                    p = jnp.exp2(s_c[j, skew_c + o0:skew_c + o0 + kb, :] - mc[j])
                    ls = jnp.sum(p, axis=0, keepdims=True)
                    pv = _dot(vt_refs[si][0, pc, rows[j], b0:b0 + kb], p.astype(BF16))
                    l[j] = ls if l[j] is None else l[j] + ls
                    acc[j] = pv if acc[j] is None else acc[j] + pv
        if nxt is not None:
            for j in range(2):
                m_n[j] = mx[j]
        if cur is not None:
            outs = [acc[j] * (1.0 / l[j]) for j in range(2)]
            o_ref[0, pc, pl.ds(rc, tq), :] = jnp.concatenate(outs, axis=0).T.astype(o_ref.dtype)

    buf = [(s_bufs[0], m_bufs[0], 0), (s_bufs[1], m_bufs[1], _SCORE_SKEW)]
    phase((0, *buf[0]), None)
    assert n_items % 2 == 0

    def body(i, carry):
        n = 2 * i
        phase((n + 1, *buf[1]), (n, *buf[0]))
        phase((jnp.minimum(n + 2, n_items - 1), *buf[0]), (n + 1, *buf[1]))
        return carry

    lax.fori_loop(0, n_items // 2, body, 0)


def _attention(q, ks, vts, tq):
    b, n_pair, t, pair = q.shape
    n_seg = len(ks)
    n_keys = sum(k.shape[2] for k in ks)
    whole = lambda a: pl.BlockSpec((1,) + a.shape[1:], lambda i: (i, 0, 0, 0))
    score_buf = pltpu.VMEM((2, n_keys + _SCORE_SKEW, tq), F32)
    max_buf = pltpu.VMEM((2, 1, tq), F32)
    return pl.pallas_call(
        functools.partial(_attn_kernel, n_seg=n_seg, tq=tq),
        grid=(b,),
        in_specs=[whole(q)] + [whole(k) for k in ks] + [whole(v) for v in vts],
        out_specs=pl.BlockSpec((1, n_pair, t, 2 * MLA_V), lambda i: (i, 0, 0, 0)),
        out_shape=jax.ShapeDtypeStruct((b, n_pair, t, 2 * MLA_V), BF16),
        scratch_shapes=[score_buf, score_buf, max_buf, max_buf],
        compiler_params=_params(("arbitrary",)),
        name="attention",
    )(q, *ks, *vts)


_RET_GROUP = 8


def _log_sigmoid(x):
    return jnp.minimum(x, 0.0) - jnp.log1p(jnp.exp(-jnp.abs(x)))


def _ret_kernel(dec_ref, *refs, chunk, group, layer, states_only):
    if states_only:
        k_ref, vt_ref, s0_ref, sfin_ref, sf_scr, sb_scr, kd_scr, qd_scr, dt_scr, cc_scr = refs
    else:
        (q_ref, k_ref, vt_ref, s0_ref, gng_ref, gnb_ref, o_ref, sfin_ref, sf_scr, sb_scr, kd_scr, qd_scr, dt_scr,
         cc_scr) = refs
    c_len = chunk
    t = k_ref.shape[1]
    n_c = t // c_len
    hp = pl.program_id(0)

    @pl.when(pl.program_id(1) == 0)
    def _():
        r = lax.broadcasted_iota(jnp.int32, (c_len, LANES), 0).astype(F32)
        lane = lax.broadcasted_iota(jnp.int32, (c_len, LANES), 1)
        km = lax.broadcasted_iota(jnp.int32, (c_len, c_len), 0)
        qn = lax.broadcasted_iota(jnp.int32, (c_len, c_len), 1)
        diff = (qn - km).astype(F32)
        for j in range(2):
            h = hp * 2 + j
            lg_f = _log_sigmoid(jnp.full((c_len, LANES), dec_ref[layer, 0, h], F32))
            lg_b = _log_sigmoid(jnp.full((c_len, LANES), dec_ref[layer, 1, h], F32))
            own = ((lane >= j * RET_DK) & (lane < (j + 1) * RET_DK)).astype(F32)
            lg_f2 = _log_sigmoid(jnp.full((c_len, c_len), dec_ref[layer, 0, h], F32))
            lg_b2 = _log_sigmoid(jnp.full((c_len, c_len), dec_ref[layer, 1, h], F32))
            kd_scr[j] = jnp.concatenate([jnp.exp(lg_f * (c_len - 1.0 - r)) * own, jnp.exp(lg_b * r) * own, own], axis=1)
            qd_scr[j] = jnp.concatenate([jnp.exp(lg_f * (r + 1.0)), jnp.exp(lg_b * (c_len - r))], axis=1)
            dt_scr[j] = jnp.exp(jnp.where(diff >= 0, lg_f2 * diff, -lg_b2 * diff))
            cc_scr[j] = jnp.concatenate(
                [jnp.exp(_log_sigmoid(jnp.full((RET_DV, LANES), dec_ref[layer, 0, h], F32)) * c_len),
                 jnp.exp(_log_sigmoid(jnp.full((RET_DV, LANES), dec_ref[layer, 1, h], F32)) * c_len)], axis=1)

    heads = []
    for j in range(2):
        heads.append(dict(
            kdf=kd_scr[j, :, 0:LANES],
            kdb=kd_scr[j, :, LANES:2 * LANES],
            own=kd_scr[j, :, 2 * LANES:3 * LANES],
            qd=qd_scr[j],
            dt=dt_scr[j],
            cf=cc_scr[j, :, 0:LANES],
            cb=cc_scr[j, :, LANES:2 * LANES],
            rows=slice(j * RET_DV, (j + 1) * RET_DV),
        ))

    def local(i, carry):
        for g in range(group):
            c = i * group + g
            c0 = pl.multiple_of(c * c_len, c_len)
            kc = k_ref[0, pl.ds(c0, c_len), :]
            for j, hd in enumerate(heads):
                vt = vt_ref[0, hd["rows"], pl.ds(c0, c_len)]
                sf_scr[j, c] = _dot(vt, (kc * hd["kdf"]).astype(BF16))
                sb_scr[j, c] = _dot(vt, (kc * hd["kdb"]).astype(BF16))
        return carry

    lax.fori_loop(0, n_c // group, local, 0)

    for j, hd in enumerate(heads):
        sf = s0_ref[0, 0, 2 * j]
        for c in range(n_c):
            a = sf_scr[j, c]
            sf_scr[j, c] = sf
            sf = hd["cf"] * sf + a
        sfin_ref[0, 0, 2 * j] = sf
        sb = s0_ref[0, 0, 2 * j + 1]
        for c in reversed(range(n_c)):
            a = sb_scr[j, c]
            sb_scr[j, c] = sb
            sb = hd["cb"] * sb + a
        sfin_ref[0, 0, 2 * j + 1] = sb

    if states_only:
        return

    def outputs(i, carry):
        items = []
        for g in range(group):
            c = i * group + g
            c0 = pl.multiple_of(c * c_len, c_len)
            qc = q_ref[0, pl.ds(c0, c_len), :]
            kc = k_ref[0, pl.ds(c0, c_len), :]
            items.append(dict(c=c, c0=c0, kc=kc, qb=qc.astype(BF16), qq=jnp.concatenate([qc, qc], axis=1)))
        work = [(it, j, hd) for it in items for j, hd in enumerate(heads)]
        st = [_dot_nt((it["kc"] * hd["own"]).astype(BF16), it["qb"]) for it, j, hd in work]
        cross = [_dot_nt(jnp.concatenate([sf_scr[j, it["c"]], sb_scr[j, it["c"]]], axis=1).astype(BF16),
                         (it["qq"] * hd["qd"]).astype(BF16)) for it, j, hd in work]
        o = [_dot(vt_ref[0, hd["rows"], pl.ds(it["c0"], c_len)], (s * hd["dt"]).astype(BF16)) + x
             for (it, j, hd), s, x in zip(work, st, cross)]
        ys = []
        for v in o:
            mu = jnp.mean(v, axis=0, keepdims=True)
            vc = v - mu
            var = jnp.mean(vc * vc, axis=0, keepdims=True)
            ys.append(vc * lax.rsqrt(var + EPS))
        for g, it in enumerate(items):
            y = jnp.concatenate(ys[2 * g:2 * g + 2], axis=0).T
            o_ref[0, pl.ds(it["c0"], c_len), :] = y * gng_ref[...] + gnb_ref[...]
        return carry

    lax.fori_loop(0, n_c // group, outputs, 0)


def _retention(dec, rq, rk, rvt, s0, gn_g, gn_b, layer, chunk):
    b, t, _ = rk.shape
    states_only = rq is None
    n_pair = RET_HEADS // 2
    pair = 2 * RET_DK
    n_c = t // chunk
    group = math.gcd(n_c, _RET_GROUP)
    state_buf = pltpu.VMEM((2, n_c, RET_DV, pair), F32)
    tables = [pltpu.VMEM((2, chunk, 3 * LANES), F32), pltpu.VMEM((2, chunk, 2 * LANES), F32),
              pltpu.VMEM((2, chunk, chunk), F32), pltpu.VMEM((2, RET_DV, 2 * LANES), F32)]
    seq = pl.BlockSpec((1, t, pair), lambda h, i: (i, 0, h))
    state = pl.BlockSpec((1, 1, 4, RET_DV, pair), lambda h, i: (i, h, 0, 0, 0))
    affine = pl.BlockSpec((None, 1, pair), lambda h, i: (layer, 0, h))
    state_shape = jax.ShapeDtypeStruct((b, n_pair, 4, RET_DV, pair), F32)
    smem = pl.BlockSpec(memory_space=pltpu.SMEM)
    vt_spec = pl.BlockSpec((1, 2 * RET_DV, t), lambda h, i: (i, h, 0))
    if states_only:
        in_specs, args = [smem, seq, vt_spec, state], (dec, rk, rvt, s0)
        out_specs, out_shape = state, state_shape
    else:
        in_specs, args = [smem, seq, seq, vt_spec, state, affine, affine], (dec, rq, rk, rvt, s0, gn_g, gn_b)
        out_specs, out_shape = (seq, state), (jax.ShapeDtypeStruct((b, t, RET_WIDTH), F32), state_shape)
    res = pl.pallas_call(
        functools.partial(_ret_kernel, chunk=chunk, group=group, layer=layer, states_only=states_only),
        grid=(n_pair, b),
        in_specs=in_specs,
        out_specs=out_specs,
        out_shape=out_shape,
        scratch_shapes=[state_buf, state_buf] + tables,
        compiler_params=_params(("arbitrary", "arbitrary")),
        name="retention",
    )(*args)
    return (None, res) if states_only else res


def _final_kernel(x_ref, mod_ref, om_ref, or_ref, oc_ref, g_ref, wo_ref, lng_ref, lnb_ref, o_ref, *, alpha):
    n1, n2 = MLA_WIDTH, MLA_WIDTH + RET_WIDTH
    o_mla = jnp.concatenate([om_ref[0, p] for p in range(om_ref.shape[1])], axis=1)
    y = _dot((o_mla * g_ref[0, :, :n1]).astype(BF16), wo_ref[:n1, :])
    y += _dot((or_ref[0] * g_ref[0, :, n1:n2]).astype(BF16), wo_ref[n1:n2, :])
    y += _dot((oc_ref[0] * g_ref[0, :, n2:]).astype(BF16), wo_ref[n2:, :])

    v = alpha * x_ref[0] + mod_ref[0, 2:3, :] * y
    mu = jnp.mean(v, axis=-1, keepdims=True)
    vc = v - mu
    var = jnp.mean(vc * vc, axis=-1, keepdims=True)
    o_ref[0] = vc * lax.rsqrt(var + EPS) * lng_ref[...] + lnb_ref[...]


def _final(x, mod_all, mod_row, o_mla, o_ret, o_conv, gates, w_out, ln_g, ln_b, layer, tm, alpha):
    b, t, d = x.shape
    const = lambda a: _layer_spec(a, layer)
    row = lambda n: pl.BlockSpec((1, tm, n), lambda i, j: (i, j, 0))
    return pl.pallas_call(
        functools.partial(_final_kernel, alpha=alpha),
        grid=(b, t // tm),
        in_specs=[row(d),
                  _mod_spec(mod_all, layer, mod_row, 0),
                  pl.BlockSpec((1, o_mla.shape[1], tm, o_mla.shape[3]), lambda i, j: (i, 0, j, 0)),
                  row(RET_WIDTH), row(CONV_WIDTH), row(d),
                  const(w_out), const(ln_g), const(ln_b)],
        out_specs=row(d),
        out_shape=jax.ShapeDtypeStruct((b, t, d), F32),
        compiler_params=_params(("arbitrary", "arbitrary")),
        name="final",
    )(x, mod_all, o_mla, o_ret, o_conv, gates, w_out, ln_g, ln_b)


def _tile(n, pref):
    return pref if n % pref == 0 else n


def kernel(x, c, ctx, c_ctx, w_mod, b_mod, w_in, mla_q_norm, w_uq, mla_kv_norm, w_ukv, ret_decay_fwd, ret_decay_bwd,
           ret_gn_g, ret_gn_b, conv_dw, conv_dw_b, conv_ln_g, conv_ln_b, conv_pw, conv_pw_b, w_out, ln_g, ln_b):
    depth = w_mod.shape[0]
    b, t, d = x.shape
    t_ctx = ctx.shape[1]
    alpha = (2 * depth) ** 0.25
    n_mod_rows = 16
    cc = jnp.concatenate([c, c_ctx[None, :], jnp.zeros((n_mod_rows - b - 1, d), c.dtype)], axis=0)
    mod_all = _modulation(cc, w_mod, b_mod).reshape(depth, n_mod_rows, 3, d)
    tab_x = _rope_table_block(t, True)
    tab_c = _rope_table_block(t_ctx, False)
    s_zero = jnp.zeros((b, RET_HEADS // 2, 4, RET_DV, 2 * RET_DK), F32)

    row3 = lambda a: a.reshape(depth, 1, -1)
    weights = _prep_weights(w_in, w_uq, w_ukv)
    gq, gkv = row3(mla_q_norm), row3(mla_kv_norm)
    dec = jnp.stack([ret_decay_fwd, ret_decay_bwd], axis=1).astype(F32)
    gn_g, gn_b = row3(ret_gn_g), row3(ret_gn_b)
    dw = jnp.pad(conv_dw, ((0, 0), (0, 1), (0, 0)))
    conv = (dw, row3(conv_dw_b), row3(conv_ln_g), row3(conv_ln_b), conv_pw.astype(BF16), row3(conv_pw_b))
    tail = (w_out.astype(BF16), row3(ln_g), row3(ln_b))
    tm_x, tm_c = _tile(t, 1024), _tile(t_ctx, 256)
    tf_x, tf_c = _tile(t, 512), _tile(t_ctx, 256)
    tq_x, tq_c = _tile(t, 256), _tile(t_ctx, 256)
    ch_x, ch_c = _tile(t, 256), _tile(t_ctx, 256)

    hc = ctx
    for l in range(depth):
        need_ctx = l < depth - 1
        pc = _proj(hc, mod_all, b, tab_c, weights, gq, gkv, conv, l, tm_c, keys_only=not need_ctx)
        px = _proj(x, mod_all, None, tab_x, weights, gq, gkv, conv, l, tm_x)

        oc_ret, s_ctx = _retention(dec, pc.get("rq"), pc["rk"], pc["rvt"], s_zero, gn_g, gn_b, l, ch_c)
        o_ret, _ = _retention(dec, px["rq"], px["rk"], px["rvt"], s_ctx, gn_g, gn_b, l, ch_x)
        o_mla = _attention(px["q"], [pc["k"], px["k"]], [pc["vt"], px["vt"]], tq_x)
        x_new = _final(x, mod_all, None, o_mla, o_ret, px["conv"], px["gates"], *tail, layer=l, tm=tf_x, alpha=alpha)
        if need_ctx:
            oc_mla = _attention(pc["q"], [pc["k"]], [pc["vt"]], tq_c)
            hc = _final(hc, mod_all, b, oc_mla, oc_ret, pc["conv"], pc["gates"], *tail, layer=l, tm=tf_c,
                        alpha=alpha)
        x = x_new
    return x
```

```python
import functools
import math

import jax
import jax.numpy as jnp
import numpy as np
from jax import lax
from jax.experimental import pallas as pl
from jax.experimental.pallas import tpu as pltpu

GRID_W = 64
MLA_HEADS = 8
MLA_NOPE = 64
MLA_ROPE = 32
MLA_V = 64
MLA_QK = MLA_NOPE + MLA_ROPE
MLA_WIDTH = MLA_HEADS * MLA_V
Q_LORA = 256
KV_LORA = 128
RET_HEADS = 4
RET_DK = 64
RET_DV = 64
RET_WIDTH = RET_HEADS * RET_DV
CONV_WIDTH = 256
CONV_K = 31
ROPE_BASE = 10000.0
EPS = 1e-5

LANES = 128
HEAD_SLOT = LANES
VMEM_LIMIT_BYTES = 56 * 1024 * 1024
LOG2E = 1.4426950408889634

F32 = jnp.float32
BF16 = jnp.bfloat16

_NT = (((1,), (1,)), ((), ()))


def _dot(a, b):
    return jnp.dot(a, b, preferred_element_type=F32)


def _dot_nt(a, b):
    return lax.dot_general(a, b, _NT, preferred_element_type=F32)


def _params(semantics):
    return pltpu.CompilerParams(dimension_semantics=semantics, vmem_limit_bytes=VMEM_LIMIT_BYTES)


def _rot_cols(w, unit):
    xp = np if isinstance(w, np.ndarray) else jnp
    q = unit // 4
    w5 = w.reshape(w.shape[:-1] + (w.shape[-1] // unit, 2, 2, q))
    return xp.stack([-w5[..., 1, :], w5[..., 0, :]], axis=-2).reshape(w.shape)


def _rope_tables(length, unit, reps):
    d2 = unit // 2
    t = np.arange(length, dtype=np.int32)
    inv = np.float32(ROPE_BASE) ** (-np.arange(0, d2, 2, dtype=np.float32) / np.float32(d2))

    def half(pos):
        ang = pos.astype(np.float32)[:, None] * inv[None, :]
        return np.concatenate([np.cos(ang)] * 2, axis=-1), np.concatenate([np.sin(ang)] * 2, axis=-1)

    cr, sr = half(t // GRID_W)
    cc, sc = half(t % GRID_W)
    cos = np.concatenate([cr, cc], axis=-1).astype(np.float32)
    sin = np.concatenate([sr, sc], axis=-1).astype(np.float32)
    return np.tile(cos, (1, reps)), np.tile(sin, (1, reps))


def _rope_table_block(length, rotate):
    if rotate:
        cos_r, sin_r = _rope_tables(length, RET_DK, RET_HEADS)
        cos_m, sin_m = _rope_tables(length, MLA_ROPE, MLA_HEADS)
    else:
        cos_r = np.ones((length, RET_WIDTH), np.float32)
        sin_r = np.zeros((length, RET_WIDTH), np.float32)
        cos_m = np.ones((length, MLA_HEADS * MLA_ROPE), np.float32)
        sin_m = np.zeros((length, MLA_HEADS * MLA_ROPE), np.float32)
    kslot = np.concatenate([cos_m[:, :MLA_ROPE], sin_m[:, :MLA_ROPE],
                            np.zeros((length, LANES - 2 * MLA_ROPE), np.float32)], axis=-1)
    return jnp.asarray(np.concatenate([cos_r, sin_r, cos_m, sin_m, kslot], axis=-1))


def _placement_matrices():
    q_place = np.zeros((MLA_HEADS, MLA_ROPE, MLA_HEADS, HEAD_SLOT), np.float32)
    k_place = np.zeros((LANES, MLA_HEADS, HEAD_SLOT), np.float32)
    for d in range(MLA_ROPE):
        for h in range(MLA_HEADS):
            q_place[h, d, h, MLA_NOPE + d] = 1.0
            k_place[d, h, MLA_NOPE + d] = 1.0
            k_place[MLA_ROPE + d, h, MLA_NOPE + d] = 1.0
    return (q_place.reshape(MLA_HEADS * MLA_ROPE, MLA_HEADS * HEAD_SLOT),
            k_place.reshape(LANES, MLA_HEADS * HEAD_SLOT))


def _prep_weights(w_in, w_uq, w_ukv):
    depth, d, _ = w_in.shape
    sizes = (Q_LORA, KV_LORA, MLA_ROPE, MLA_WIDTH, RET_HEADS * RET_DK, RET_HEADS * RET_DK, RET_WIDTH,
             RET_WIDTH, 2 * CONV_WIDTH, CONV_WIDTH)
    offs = [0]
    for s in sizes:
        offs.append(offs[-1] + s)
    seg = [w_in[..., offs[i]:offs[i + 1]] for i in range(len(sizes))]
    wq, wkv, wkr, wg_mla, wrq, wrk, wrv, wg_ret, wglu, wg_conv = seg
    kslot = jnp.concatenate([wkr, _rot_cols(wkr, MLA_ROPE), jnp.zeros((depth, d, LANES - 2 * MLA_ROPE), w_in.dtype)],
                            axis=-1)
    w_row = jnp.concatenate([wq, wkv, kslot, wrq, wrk, wg_mla, wg_ret, wg_conv, wglu],
                            axis=-1).astype(BF16)
    w_rvt = jnp.swapaxes(wrv, 1, 2).astype(BF16)

    uq = w_uq.reshape(depth, Q_LORA, MLA_HEADS, MLA_QK)
    pad = ((0, 0), (0, 0), (0, 0), (0, HEAD_SLOT - MLA_NOPE))
    uq_nope = jnp.pad(uq[..., :MLA_NOPE], pad).reshape(depth, Q_LORA, -1)
    uq_rope = uq[..., MLA_NOPE:].reshape(depth, Q_LORA, MLA_HEADS * MLA_ROPE)
    w_q2 = jnp.concatenate([uq_rope, _rot_cols(uq_rope, MLA_ROPE)], axis=-1).astype(BF16)
    q_place, k_place = (jnp.broadcast_to(jnp.asarray(m, BF16), (depth,) + m.shape) for m in _placement_matrices())
    w_qcomb = jnp.concatenate([uq_nope.astype(BF16), q_place], axis=1)

    ukv = w_ukv.reshape(depth, KV_LORA, MLA_HEADS, MLA_NOPE + MLA_V)
    uk = jnp.pad(ukv[..., :MLA_NOPE], pad).reshape(depth, KV_LORA, -1)
    w_kcomb = jnp.concatenate([uk.astype(BF16), k_place], axis=1)
    w_uvt = jnp.swapaxes(ukv[..., MLA_NOPE:].reshape(depth, KV_LORA, MLA_WIDTH), 1, 2).astype(BF16)
    return w_row, w_rvt, w_q2, w_qcomb, w_kcomb, w_uvt


def _mod_kernel(c_ref, w_ref, b_ref, o_ref):
    c = c_ref[...]
    a = (c * jax.nn.sigmoid(c)).astype(BF16)
    o_ref[0] = _dot(a, w_ref[0].astype(BF16)) + b_ref[0]


def _modulation(cc, w_mod, b_mod):
    depth, d, d3 = w_mod.shape
    nblk = d3 // d
    return pl.pallas_call(
        _mod_kernel,
        grid=(depth, nblk),
        in_specs=[pl.BlockSpec(cc.shape, lambda l, j: (0, 0)),
                  pl.BlockSpec((1, d, d), lambda l, j: (l, 0, j)),
                  pl.BlockSpec((1, 1, d), lambda l, j: (l, 0, j))],
        out_specs=pl.BlockSpec((1, cc.shape[0], d), lambda l, j: (l, 0, j)),
        out_shape=jax.ShapeDtypeStruct((depth, cc.shape[0], d3), F32),
        compiler_params=_params(("arbitrary", "arbitrary")),
        name="mod",
    )(cc, w_mod, b_mod.reshape(depth, 1, d3))


_C_Q, _C_KV, _C_KS, _C_RQ, _C_RK, _C_G, _C_GLU, _C_END = 0, 256, 384, 512, 768, 1024, 2048, 2560
_T_CR, _T_SR, _T_CM, _T_SM, _T_KS, _T_END = 0, 256, 512, 768, 1024, 1152


_HALO = 16
_CONV_ROWS = 64


def _rms(x, g):
    return x * lax.rsqrt(jnp.mean(x * x, axis=-1, keepdims=True) + EPS) * g


def _proj_kernel(x_ref, xp_ref, xn_ref, mod_ref, tab_ref, rot_ref, w_ref, wrvt_ref, gq_ref, gkv_ref, wq2_ref, wqc_ref,
                 wkc_ref, wuvt_ref, dw_ref, dwb_ref, clg_ref, clb_ref, pw_ref, pwb_ref, *out_refs, keys_only):
    if keys_only:
        k_ref, vt_ref, rk_ref, rvt_ref = out_refs
    else:
        q_ref, k_ref, vt_ref, rq_ref, rk_ref, rvt_ref, gm_ref, gr_ref, oc_ref, upad, ushift = out_refs
    tm = x_ref.shape[1]
    shift = mod_ref[0, 0:1, :]
    scale = mod_ref[0, 1:2, :]
    pair = 2 * HEAD_SLOT

    def modulate(x):
        mu = jnp.mean(x, axis=-1, keepdims=True)
        xc = x - mu
        var = jnp.mean(xc * xc, axis=-1, keepdims=True)
        return (xc * lax.rsqrt(var + EPS) * (1.0 + scale) + shift).astype(BF16)

    u = modulate(x_ref[0])

    def seg(lo, hi):
        return _dot(u, w_ref[:, lo:hi])

    def store_pairs(ref, val):
        for p in range(MLA_HEADS // 2):
            ref[0, p] = val[:, p * pair:(p + 1) * pair]

    def keys():
        nkv = _rms(seg(_C_KV, _C_KS), gkv_ref[...]).astype(BF16)
        ks = seg(_C_KS, _C_RQ) * tab_ref[:, _T_KS:_T_END]
        store_pairs(k_ref, _dot(jnp.concatenate([nkv, ks.astype(BF16)], axis=1), wkc_ref[...]).astype(BF16))
        vt_ref[0] = _dot_nt(wuvt_ref[...], nkv).astype(BF16)

    def ret_rope(raw):
        rot = _dot(raw.astype(BF16), rot_ref[...])
        return raw * tab_ref[:, _T_CR:_T_SR] + rot * tab_ref[:, _T_SR:_T_CM]

    def ret_keys():
        rk_ref[0] = ret_rope(seg(_C_RK, _C_G)) * (RET_DK ** -0.5)
        rvt_ref[0] = _dot_nt(wrvt_ref[...], u).astype(BF16)

    if keys_only:
        keys()
        ret_keys()
        return

    i = pl.program_id(0)
    u_ext = jnp.concatenate([modulate(xp_ref[0]), u, modulate(xn_ref[0])], axis=0)
    glu = _dot(u_ext, w_ref[:, _C_GLU:_C_END])
    uc = glu[:, :CONV_WIDTH] * jax.nn.sigmoid(glu[:, CONV_WIDTH:])
    row = lax.broadcasted_iota(jnp.int32, uc.shape, 0)
    inside = ((row >= _HALO) | (i > 0)) & ((row < _HALO + tm) | (i < pl.num_programs(0) - 1))
    upad[...] = jnp.where(inside, uc, 0.0)
    span = tm + 2 * _HALO - 8
    for r in range(1, 8):
        ushift[r - 1] = upad[r:r + span, :]
    base = _HALO - CONV_K // 2
    conv_blocks = []

    def conv_rows(n_blocks):
        for _ in range(n_blocks):
            rb = len(conv_blocks) * _CONV_ROWS
            acc = jnp.zeros((_CONV_ROWS, CONV_WIDTH), F32) + dwb_ref[...]
            for j in range(CONV_K):
                a, r = divmod(base + j, 8)
                src = upad if r == 0 else ushift.at[r - 1]
                acc = acc + src[8 * a + rb:8 * a + rb + _CONV_ROWS, :] * dw_ref[j:j + 1, :]
            mu = jnp.mean(acc, axis=-1, keepdims=True)
            ac = acc - mu
            var = jnp.mean(ac * ac, axis=-1, keepdims=True)
            yc = ac * lax.rsqrt(var + EPS) * clg_ref[...] + clb_ref[...]
            conv_blocks.append((yc * jax.nn.sigmoid(yc)).astype(BF16))

    n_conv = tm // _CONV_ROWS
    per_stage = -(-n_conv // 4)

    nq = _rms(seg(_C_Q, _C_KV), gq_ref[...]).astype(BF16)
    qc = _dot(nq, wq2_ref[...])
    nr = MLA_HEADS * MLA_ROPE
    q_rope = qc[:, :nr] * tab_ref[:, _T_CM:_T_SM] + qc[:, nr:] * tab_ref[:, _T_SM:_T_KS]
    q = _dot(jnp.concatenate([nq, q_rope.astype(BF16)], axis=1), wqc_ref[...])
    store_pairs(q_ref, (q * (MLA_QK ** -0.5 * LOG2E)).astype(BF16))
    conv_rows(min(per_stage, n_conv - len(conv_blocks)))

    keys()
    conv_rows(min(per_stage, n_conv - len(conv_blocks)))

    rq_ref[0] = ret_rope(seg(_C_RQ, _C_RK))
    ret_keys()
    conv_rows(min(per_stage, n_conv - len(conv_blocks)))

    g = seg(_C_G, _C_GLU)
    g = g * jax.nn.sigmoid(g)
    for p in range(MLA_HEADS // 2):
        gm_ref[0, p] = g[:, p * LANES:(p + 1) * LANES].astype(gm_ref.dtype)
    gr_ref[0] = g[:, MLA_WIDTH:MLA_WIDTH + RET_WIDTH].astype(gr_ref.dtype)
    conv_rows(n_conv - len(conv_blocks))

    o_conv = _dot(jnp.concatenate(conv_blocks, axis=0), pw_ref[...]) + pwb_ref[...]
    oc_ref[0] = (o_conv * g[:, MLA_WIDTH + RET_WIDTH:]).astype(oc_ref.dtype)


def _layer_spec(a, layer):
    return pl.BlockSpec((None,) + a.shape[1:], lambda *_: (layer,) + (0,) * (a.ndim - 1),
                        pipeline_mode=pl.Buffered(1))


def _mod_spec(mod_all, layer, shared_row, batch_axis):
    def index(*grid):
        return (layer, grid[batch_axis] if shared_row is None else shared_row, 0, 0)
    return pl.BlockSpec((None, 1) + mod_all.shape[2:], index)


def _proj(x, mod_all, mod_row, tab, weights, gq, gkv, conv, layer, tm, keys_only=False):
    b, t, d = x.shape
    w_row, w_rvt, w_q2, w_qcomb, w_kcomb, w_uvt = weights
    const = lambda a: _layer_spec(a, layer)
    row = lambda n: pl.BlockSpec((1, tm, n), lambda i, j: (j, i, 0))
    col = lambda n: pl.BlockSpec((1, n, tm), lambda i, j: (j, 0, i))
    n_pair, pair = MLA_HEADS // 2, 2 * HEAD_SLOT
    pairs = pl.BlockSpec((1, n_pair, tm, pair), lambda i, j: (j, 0, i, 0))
    per_tile = tm // _HALO
    last = t // _HALO - 1
    halo_prev = pl.BlockSpec((1, _HALO, d), lambda i, j: (j, jnp.maximum(i * per_tile - 1, 0), 0))
    halo_next = pl.BlockSpec((1, _HALO, d), lambda i, j: (j, jnp.minimum((i + 1) * per_tile, last), 0))
    rot = jnp.asarray(_rot_cols(np.eye(RET_WIDTH, dtype=np.float32), RET_DK), BF16)
    sds = jax.ShapeDtypeStruct
    outs = dict(q=(sds((b, n_pair, t, pair), BF16), pairs),
                k=(sds((b, n_pair, t, pair), BF16), pairs),
                vt=(sds((b, MLA_WIDTH, t), BF16), col(MLA_WIDTH)),
                rq=(sds((b, t, RET_WIDTH), F32), row(RET_WIDTH)),
                rk=(sds((b, t, RET_WIDTH), F32), row(RET_WIDTH)),
                rvt=(sds((b, RET_WIDTH, t), BF16), col(RET_WIDTH)),
                gate_mla=(sds((b, n_pair, t, 2 * MLA_V), BF16),
                          pl.BlockSpec((1, n_pair, tm, 2 * MLA_V), lambda i, j: (j, 0, i, 0))),
                gate_ret=(sds((b, t, RET_WIDTH), BF16), row(RET_WIDTH)),
                conv=(sds((b, t, CONV_WIDTH), BF16), row(CONV_WIDTH)))
    names = ("k", "vt", "rk", "rvt") if keys_only else tuple(outs)
    scratch = [] if keys_only else [pltpu.VMEM((tm + 2 * _HALO, CONV_WIDTH), F32),
                                    pltpu.VMEM((7, tm + 2 * _HALO - 8, CONV_WIDTH), F32)]
    res = pl.pallas_call(
        functools.partial(_proj_kernel, keys_only=keys_only),
        grid=(t // tm, b),
        in_specs=[row(d), halo_prev, halo_next,
                  _mod_spec(mod_all, layer, mod_row, 1),
                  pl.BlockSpec((tm, _T_END), lambda i, j: (i, 0)),
                  pl.BlockSpec(rot.shape, lambda i, j: (0, 0)),
                  const(w_row), const(w_rvt), const(gq), const(gkv), const(w_q2), const(w_qcomb), const(w_kcomb),
                  const(w_uvt)] + [const(a) for a in conv],
        out_specs=tuple(outs[n][1] for n in names),
        out_shape=tuple(outs[n][0] for n in names),
        scratch_shapes=scratch,
        compiler_params=_params(("arbitrary", "arbitrary")),
        name="proj",
    )(x, x, x, mod_all, tab, rot, w_row, w_rvt, gq, gkv, w_q2, w_qcomb, w_kcomb, w_uvt, *conv)
    res = dict(zip(names, res))
    res["vt"] = res["vt"].reshape(b, n_pair, 2 * MLA_V, t)
    return res


_ATTN_KEY_BLOCK = 512


def _attn_kernel(*refs, n_seg, tq):
    q_ref, g_ref = refs[:2]
    k_refs = refs[2:2 + n_seg]
    vt_refs = refs[2 + n_seg:2 + 2 * n_seg]
    o_ref = refs[2 + 2 * n_seg]
    s_bufs = refs[3 + 2 * n_seg:5 + 2 * n_seg]
    m_bufs = refs[5 + 2 * n_seg:7 + 2 * n_seg]
    n_pair = q_ref.shape[1]
    n_q = q_ref.shape[2] // tq
    n_items = n_pair * n_q
    lanes = [slice(j * HEAD_SLOT, (j + 1) * HEAD_SLOT) for j in range(2)]
    rows = [slice(j * MLA_V, (j + 1) * MLA_V) for j in range(2)]

    blocks, off = [], 0
    for si in range(n_seg):
        ts = k_refs[si].shape[2]
        kb = min(_ATTN_KEY_BLOCK, ts)
        blocks += [(si, b0, kb, off + b0) for b0 in range(0, ts, kb)]
        off += ts

    def locate(item):
        if isinstance(item, int):
            return item // n_q, (item % n_q) * tq
        pair = lax.div(item, jnp.int32(n_q))
        return pair, pl.multiple_of((item - pair * n_q) * tq, tq)

    def phase(nxt, cur):
        if nxt is not None:
            item_n, s_n, m_n = nxt
            pn, rn = locate(item_n)
            qn = [q_ref[0, pn, pl.ds(rn, tq), lanes[j]] for j in range(2)]
            mx = [None, None]
        if cur is not None:
            item_c, s_c, m_c = cur
            pc, rc = locate(item_c)
            mc = [m_c[j] for j in range(2)]
            acc, l = [None, None], [None, None]
        for si, b0, kb, o0 in blocks:
            for j in range(2):
                if nxt is not None:
                    s = _dot_nt(k_refs[si][0, pn, b0:b0 + kb, lanes[j]], qn[j])
                    s_n[j, o0:o0 + kb, :] = s
                    mb = jnp.max(s, axis=0, keepdims=True)
                    mx[j] = mb if mx[j] is None else jnp.maximum(mx[j], mb)
                if cur is not None:
                    p = jnp.exp2(s_c[j, o0:o0 + kb, :] - mc[j])
                    ls = jnp.sum(p, axis=0, keepdims=True)
                    pv = _dot(vt_refs[si][0, pc, rows[j], b0:b0 + kb], p.astype(BF16))
                    l[j] = ls if l[j] is None else l[j] + ls
                    acc[j] = pv if acc[j] is None else acc[j] + pv
        if nxt is not None:
            for j in range(2):
                m_n[j] = mx[j]
        if cur is not None:
            outs = [acc[j] * (1.0 / l[j]) for j in range(2)]
            gate = g_ref[0, pc, pl.ds(rc, tq), :].astype(F32)
            o_ref[0, pc, pl.ds(rc, tq), :] = (jnp.concatenate(outs, axis=0).T * gate).astype(o_ref.dtype)

    buf = [(s_bufs[0], m_bufs[0]), (s_bufs[1], m_bufs[1])]
    phase((0, *buf[0]), None)
    assert n_items % 2 == 0

    def body(i, carry):
        n = 2 * i
        phase((n + 1, *buf[1]), (n, *buf[0]))
        phase((jnp.minimum(n + 2, n_items - 1), *buf[0]), (n + 1, *buf[1]))
        return carry

    lax.fori_loop(0, n_items // 2, body, 0)


def _attention(q, gate, ks, vts, tq):
    b, n_pair, t, pair = q.shape
    n_seg = len(ks)
    n_keys = sum(k.shape[2] for k in ks)
    whole = lambda a: pl.BlockSpec((1,) + a.shape[1:], lambda i: (i, 0, 0, 0))
    score_buf = pltpu.VMEM((2, n_keys, tq), F32)
    max_buf = pltpu.VMEM((2, 1, tq), F32)
    return pl.pallas_call(
        functools.partial(_attn_kernel, n_seg=n_seg, tq=tq),
        grid=(b,),
        in_specs=[whole(q), whole(gate)] + [whole(k) for k in ks] + [whole(v) for v in vts],
        out_specs=pl.BlockSpec((1, n_pair, t, 2 * MLA_V), lambda i: (i, 0, 0, 0)),
        out_shape=jax.ShapeDtypeStruct((b, n_pair, t, 2 * MLA_V), BF16),
        scratch_shapes=[score_buf, score_buf, max_buf, max_buf],
        compiler_params=_params(("arbitrary",)),
        name="attention",
    )(q, gate, *ks, *vts)


_RET_GROUP = 8


def _log_sigmoid(x):
    return jnp.minimum(x, 0.0) - jnp.log1p(jnp.exp(-jnp.abs(x)))


def _ret_kernel(dec_ref, *refs, chunk, group, layer, states_only):
    if states_only:
        k_ref, vt_ref, s0_ref, sfin_ref, sf_scr, sb_scr, kd_scr, qd_scr, dt_scr, cc_scr = refs
    else:
        (q_ref, k_ref, vt_ref, s0_ref, gng_ref, gnb_ref, gate_ref, o_ref, sfin_ref, sf_scr, sb_scr, kd_scr, qd_scr,
         dt_scr, cc_scr) = refs
    c_len = chunk
    t = k_ref.shape[1]
    n_c = t // c_len
    hp = pl.program_id(0)

    @pl.when(pl.program_id(1) == 0)
    def _():
        r = lax.broadcasted_iota(jnp.int32, (c_len, LANES), 0).astype(F32)
        lane = lax.broadcasted_iota(jnp.int32, (c_len, LANES), 1)
        km = lax.broadcasted_iota(jnp.int32, (c_len, c_len), 0)
        qn = lax.broadcasted_iota(jnp.int32, (c_len, c_len), 1)
        diff = (qn - km).astype(F32)
        for j in range(2):
            h = hp * 2 + j
            lg_f = _log_sigmoid(jnp.full((c_len, LANES), dec_ref[layer, 0, h], F32))
            lg_b = _log_sigmoid(jnp.full((c_len, LANES), dec_ref[layer, 1, h], F32))
            own = ((lane >= j * RET_DK) & (lane < (j + 1) * RET_DK)).astype(F32)
            lg_f2 = _log_sigmoid(jnp.full((c_len, c_len), dec_ref[layer, 0, h], F32))
            lg_b2 = _log_sigmoid(jnp.full((c_len, c_len), dec_ref[layer, 1, h], F32))
            kd_scr[j] = jnp.concatenate([jnp.exp(lg_f * (c_len - 1.0 - r)) * own, jnp.exp(lg_b * r) * own, own], axis=1)
            qd_scr[j] = jnp.concatenate([jnp.exp(lg_f * (r + 1.0)), jnp.exp(lg_b * (c_len - r))], axis=1)
            dt_scr[j] = jnp.exp(jnp.where(diff >= 0, lg_f2 * diff, -lg_b2 * diff))
            cc_scr[j] = jnp.concatenate(
                [jnp.exp(_log_sigmoid(jnp.full((RET_DV, LANES), dec_ref[layer, 0, h], F32)) * c_len),
                 jnp.exp(_log_sigmoid(jnp.full((RET_DV, LANES), dec_ref[layer, 1, h], F32)) * c_len)], axis=1)

    heads = []
    for j in range(2):
        heads.append(dict(
            kdf=kd_scr[j, :, 0:LANES],
            kdb=kd_scr[j, :, LANES:2 * LANES],
            own=kd_scr[j, :, 2 * LANES:3 * LANES],
            qd=qd_scr[j],
            dt=dt_scr[j],
            cf=cc_scr[j, :, 0:LANES],
            cb=cc_scr[j, :, LANES:2 * LANES],
            rows=slice(j * RET_DV, (j + 1) * RET_DV),
        ))

    def local(i, carry):
        for g in range(group):
            c = i * group + g
            c0 = pl.multiple_of(c * c_len, c_len)
            kc = k_ref[0, pl.ds(c0, c_len), :]
            for j, hd in enumerate(heads):
                vt = vt_ref[0, hd["rows"], pl.ds(c0, c_len)]
                sf_scr[j, c] = _dot(vt, (kc * hd["kdf"]).astype(BF16))
                sb_scr[j, c] = _dot(vt, (kc * hd["kdb"]).astype(BF16))
        return carry

    lax.fori_loop(0, n_c // group, local, 0)

    for j, hd in enumerate(heads):
        sf = s0_ref[0, 0, 2 * j]
        for c in range(n_c):
            a = sf_scr[j, c]
            sf_scr[j, c] = sf
            sf = hd["cf"] * sf + a
        sfin_ref[0, 0, 2 * j] = sf
        sb = s0_ref[0, 0, 2 * j + 1]
        for c in reversed(range(n_c)):
            a = sb_scr[j, c]
            sb_scr[j, c] = sb
            sb = hd["cb"] * sb + a
        sfin_ref[0, 0, 2 * j + 1] = sb

    if states_only:
        return

    def outputs(i, carry):
        items = []
        for g in range(group):
            c = i * group + g
            c0 = pl.multiple_of(c * c_len, c_len)
            qc = q_ref[0, pl.ds(c0, c_len), :]
            kc = k_ref[0, pl.ds(c0, c_len), :]
            items.append(dict(c=c, c0=c0, kc=kc, qb=qc.astype(BF16), qq=jnp.concatenate([qc, qc], axis=1)))
        work = [(it, j, hd) for it in items for j, hd in enumerate(heads)]
        st = [_dot_nt((it["kc"] * hd["own"]).astype(BF16), it["qb"]) for it, j, hd in work]
        cross = [_dot_nt(jnp.concatenate([sf_scr[j, it["c"]], sb_scr[j, it["c"]]], axis=1).astype(BF16),
                         (it["qq"] * hd["qd"]).astype(BF16)) for it, j, hd in work]
        o = [_dot(vt_ref[0, hd["rows"], pl.ds(it["c0"], c_len)], (s * hd["dt"]).astype(BF16)) + x
             for (it, j, hd), s, x in zip(work, st, cross)]
        ys = []
        for v in o:
            mu = jnp.mean(v, axis=0, keepdims=True)
            vc = v - mu
            var = jnp.mean(vc * vc, axis=0, keepdims=True)
            ys.append(vc * lax.rsqrt(var + EPS))
        for g, it in enumerate(items):
            y = jnp.concatenate(ys[2 * g:2 * g + 2], axis=0).T
            gate = gate_ref[0, pl.ds(it["c0"], c_len), :].astype(F32)
            o_ref[0, pl.ds(it["c0"], c_len), :] = ((y * gng_ref[...] + gnb_ref[...]) * gate).astype(o_ref.dtype)
        return carry

    lax.fori_loop(0, n_c // group, outputs, 0)


def _retention(dec, rq, rk, rvt, s0, gn_g, gn_b, gate, layer, chunk):
    b, t, _ = rk.shape
    states_only = rq is None
    n_pair = RET_HEADS // 2
    pair = 2 * RET_DK
    n_c = t // chunk
    group = math.gcd(n_c, _RET_GROUP)
    state_buf = pltpu.VMEM((2, n_c, RET_DV, pair), F32)
    tables = [pltpu.VMEM((2, chunk, 3 * LANES), F32), pltpu.VMEM((2, chunk, 2 * LANES), F32),
              pltpu.VMEM((2, chunk, chunk), F32), pltpu.VMEM((2, RET_DV, 2 * LANES), F32)]
    seq = pl.BlockSpec((1, t, pair), lambda h, i: (i, 0, h))
    state = pl.BlockSpec((1, 1, 4, RET_DV, pair), lambda h, i: (i, h, 0, 0, 0))
    affine = pl.BlockSpec((None, 1, pair), lambda h, i: (layer, 0, h))
    state_shape = jax.ShapeDtypeStruct((b, n_pair, 4, RET_DV, pair), F32)
    smem = pl.BlockSpec(memory_space=pltpu.SMEM)
    vt_spec = pl.BlockSpec((1, 2 * RET_DV, t), lambda h, i: (i, h, 0))
    if states_only:
        in_specs, args = [smem, seq, vt_spec, state], (dec, rk, rvt, s0)
        out_specs, out_shape = state, state_shape
    else:
        in_specs = [smem, seq, seq, vt_spec, state, affine, affine, seq]
        args = (dec, rq, rk, rvt, s0, gn_g, gn_b, gate)
        out_specs, out_shape = (seq, state), (jax.ShapeDtypeStruct((b, t, RET_WIDTH), BF16), state_shape)
    res = pl.pallas_call(
        functools.partial(_ret_kernel, chunk=chunk, group=group, layer=layer, states_only=states_only),
        grid=(n_pair, b),
        in_specs=in_specs,
        out_specs=out_specs,
        out_shape=out_shape,
        scratch_shapes=[state_buf, state_buf] + tables,
        compiler_params=_params(("arbitrary", "arbitrary")),
        name="retention",
    )(*args)
    return (None, res) if states_only else res


def _final_kernel(x_ref, mod_ref, om_ref, or_ref, oc_ref, wo_ref, lng_ref, lnb_ref, o_ref, *, alpha):
    o_cat = jnp.concatenate([om_ref[0, p] for p in range(om_ref.shape[1])] + [or_ref[0], oc_ref[0]], axis=1)
    y = _dot(o_cat, wo_ref[...])

    v = alpha * x_ref[0] + mod_ref[0, 2:3, :] * y
    mu = jnp.mean(v, axis=-1, keepdims=True)
    vc = v - mu
    var = jnp.mean(vc * vc, axis=-1, keepdims=True)
    o_ref[0] = vc * lax.rsqrt(var + EPS) * lng_ref[...] + lnb_ref[...]


def _final(x, mod_all, mod_row, o_mla, o_ret, o_conv, w_out, ln_g, ln_b, layer, tm, alpha):
    b, t, d = x.shape
    const = lambda a: _layer_spec(a, layer)
    row = lambda n: pl.BlockSpec((1, tm, n), lambda i, j: (i, j, 0))
    return pl.pallas_call(
        functools.partial(_final_kernel, alpha=alpha),
        grid=(b, t // tm),
        in_specs=[row(d),
                  _mod_spec(mod_all, layer, mod_row, 0),
                  pl.BlockSpec((1, o_mla.shape[1], tm, o_mla.shape[3]), lambda i, j: (i, 0, j, 0)),
                  row(RET_WIDTH), row(CONV_WIDTH),
                  const(w_out), const(ln_g), const(ln_b)],
        out_specs=row(d),
        out_shape=jax.ShapeDtypeStruct((b, t, d), F32),
        compiler_params=_params(("arbitrary", "arbitrary")),
        name="final",
    )(x, mod_all, o_mla, o_ret, o_conv, w_out, ln_g, ln_b)


def _tile(n, pref):
    return pref if n % pref == 0 else n


def kernel(x, c, ctx, c_ctx, w_mod, b_mod, w_in, mla_q_norm, w_uq, mla_kv_norm, w_ukv, ret_decay_fwd, ret_decay_bwd,
           ret_gn_g, ret_gn_b, conv_dw, conv_dw_b, conv_ln_g, conv_ln_b, conv_pw, conv_pw_b, w_out, ln_g, ln_b):
    depth = w_mod.shape[0]
    b, t, d = x.shape
    t_ctx = ctx.shape[1]
    alpha = (2 * depth) ** 0.25
    n_mod_rows = 16
    cc = jnp.concatenate([c, c_ctx[None, :], jnp.zeros((n_mod_rows - b - 1, d), c.dtype)], axis=0)
    mod_all = _modulation(cc, w_mod, b_mod).reshape(depth, n_mod_rows, 3, d)
    tab_x = _rope_table_block(t, True)
    tab_c = _rope_table_block(t_ctx, False)
    s_zero = jnp.zeros((b, RET_HEADS // 2, 4, RET_DV, 2 * RET_DK), F32)

    row3 = lambda a: a.reshape(depth, 1, -1)
    weights = _prep_weights(w_in, w_uq, w_ukv)
    gq, gkv = row3(mla_q_norm), row3(mla_kv_norm)
    dec = jnp.stack([ret_decay_fwd, ret_decay_bwd], axis=1).astype(F32)
    gn_g, gn_b = row3(ret_gn_g), row3(ret_gn_b)
    dw = jnp.pad(conv_dw, ((0, 0), (0, 1), (0, 0)))
    conv = (dw, row3(conv_dw_b), row3(conv_ln_g), row3(conv_ln_b), conv_pw.astype(BF16), row3(conv_pw_b))
    tail = (w_out.astype(BF16), row3(ln_g), row3(ln_b))
    tm_x, tm_c = _tile(t, 1024), _tile(t_ctx, 256)
    tf_x, tf_c = _tile(t, 512), _tile(t_ctx, 256)
    tq_x, tq_c = _tile(t, 256), _tile(t_ctx, 256)
    ch_x, ch_c = _tile(t, 256), _tile(t_ctx, 256)

    hc = ctx
    for l in range(depth):
        need_ctx = l < depth - 1
        pc = _proj(hc, mod_all, b, tab_c, weights, gq, gkv, conv, l, tm_c, keys_only=not need_ctx)
        px = _proj(x, mod_all, None, tab_x, weights, gq, gkv, conv, l, tm_x)

        oc_ret, s_ctx = _retention(dec, pc.get("rq"), pc["rk"], pc["rvt"], s_zero, gn_g, gn_b, pc.get("gate_ret"),
                                   l, ch_c)
        o_ret, _ = _retention(dec, px["rq"], px["rk"], px["rvt"], s_ctx, gn_g, gn_b, px["gate_ret"], l, ch_x)
        o_mla = _attention(px["q"], px["gate_mla"], [pc["k"], px["k"]], [pc["vt"], px["vt"]], tq_x)
        x_new = _final(x, mod_all, None, o_mla, o_ret, px["conv"], *tail, layer=l, tm=tf_x, alpha=alpha)
        if need_ctx:
            oc_mla = _attention(pc["q"], pc["gate_mla"], [pc["k"]], [pc["vt"]], tq_c)
            hc = _final(hc, mod_all, b, oc_mla, oc_ret, pc["conv"], *tail, layer=l, tm=tf_c, alpha=alpha)
        x = x_new
    return x
```

```python
import functools
import math

import jax
import jax.numpy as jnp
import numpy as np
from jax import lax
from jax.experimental import pallas as pl
from jax.experimental.pallas import tpu as pltpu

GRID_W = 64
MLA_HEADS = 8
MLA_NOPE = 64
MLA_ROPE = 32
MLA_V = 64
MLA_QK = MLA_NOPE + MLA_ROPE
MLA_WIDTH = MLA_HEADS * MLA_V
Q_LORA = 256
KV_LORA = 128
RET_HEADS = 4
RET_DK = 64
RET_DV = 64
RET_WIDTH = RET_HEADS * RET_DV
CONV_WIDTH = 256
CONV_K = 31
ROPE_BASE = 10000.0
EPS = 1e-5

LANES = 128
HEAD_SLOT = LANES
VMEM_LIMIT_BYTES = 56 * 1024 * 1024
LOG2E = 1.4426950408889634

F32 = jnp.float32
BF16 = jnp.bfloat16

_NT = (((1,), (1,)), ((), ()))


def _dot(a, b):
    return jnp.dot(a, b, preferred_element_type=F32)


def _dot_nt(a, b):
    return lax.dot_general(a, b, _NT, preferred_element_type=F32)


def _params(semantics):
    return pltpu.CompilerParams(dimension_semantics=semantics, vmem_limit_bytes=VMEM_LIMIT_BYTES)


def _rot_cols(w, unit):
    xp = np if isinstance(w, np.ndarray) else jnp
    q = unit // 4
    w5 = w.reshape(w.shape[:-1] + (w.shape[-1] // unit, 2, 2, q))
    return xp.stack([-w5[..., 1, :], w5[..., 0, :]], axis=-2).reshape(w.shape)


def _rope_tables(length, unit, reps):
    d2 = unit // 2
    t = np.arange(length, dtype=np.int32)
    inv = np.float32(ROPE_BASE) ** (-np.arange(0, d2, 2, dtype=np.float32) / np.float32(d2))

    def half(pos):
        ang = pos.astype(np.float32)[:, None] * inv[None, :]
        return np.concatenate([np.cos(ang)] * 2, axis=-1), np.concatenate([np.sin(ang)] * 2, axis=-1)

    cr, sr = half(t // GRID_W)
    cc, sc = half(t % GRID_W)
    cos = np.concatenate([cr, cc], axis=-1).astype(np.float32)
    sin = np.concatenate([sr, sc], axis=-1).astype(np.float32)
    return np.tile(cos, (1, reps)), np.tile(sin, (1, reps))


def _rope_table_block(length, rotate):
    if rotate:
        cos_r, sin_r = _rope_tables(length, RET_DK, RET_HEADS)
        cos_m, sin_m = _rope_tables(length, MLA_ROPE, MLA_HEADS)
    else:
        cos_r = np.ones((length, RET_WIDTH), np.float32)
        sin_r = np.zeros((length, RET_WIDTH), np.float32)
        cos_m = np.ones((length, MLA_HEADS * MLA_ROPE), np.float32)
        sin_m = np.zeros((length, MLA_HEADS * MLA_ROPE), np.float32)
    kslot = np.concatenate([cos_m[:, :MLA_ROPE], sin_m[:, :MLA_ROPE],
                            np.zeros((length, LANES - 2 * MLA_ROPE), np.float32)], axis=-1)
    return jnp.asarray(np.concatenate([cos_r, sin_r, cos_m, sin_m, kslot], axis=-1))


def _placement_matrices():
    q_place = np.zeros((MLA_HEADS, MLA_ROPE, MLA_HEADS, HEAD_SLOT), np.float32)
    k_place = np.zeros((LANES, MLA_HEADS, HEAD_SLOT), np.float32)
    for d in range(MLA_ROPE):
        for h in range(MLA_HEADS):
            q_place[h, d, h, MLA_NOPE + d] = 1.0
            k_place[d, h, MLA_NOPE + d] = 1.0
            k_place[MLA_ROPE + d, h, MLA_NOPE + d] = 1.0
    return (q_place.reshape(MLA_HEADS * MLA_ROPE, MLA_HEADS * HEAD_SLOT),
            k_place.reshape(LANES, MLA_HEADS * HEAD_SLOT))


def _prep_weights(w_in, w_uq, w_ukv):
    depth, d, _ = w_in.shape
    sizes = (Q_LORA, KV_LORA, MLA_ROPE, MLA_WIDTH, RET_HEADS * RET_DK, RET_HEADS * RET_DK, RET_WIDTH,
             RET_WIDTH, 2 * CONV_WIDTH, CONV_WIDTH)
    offs = [0]
    for s in sizes:
        offs.append(offs[-1] + s)
    seg = [w_in[..., offs[i]:offs[i + 1]] for i in range(len(sizes))]
    wq, wkv, wkr, wg_mla, wrq, wrk, wrv, wg_ret, wglu, wg_conv = seg
    kslot = jnp.concatenate([wkr, _rot_cols(wkr, MLA_ROPE), jnp.zeros((depth, d, LANES - 2 * MLA_ROPE), w_in.dtype)],
                            axis=-1)
    w_row = jnp.concatenate([wq, wkv, kslot, wrq, wrk, wg_mla, wg_ret, wg_conv, wglu],
                            axis=-1).astype(BF16)
    w_rvt = jnp.swapaxes(wrv, 1, 2).astype(BF16)

    uq = w_uq.reshape(depth, Q_LORA, MLA_HEADS, MLA_QK)
    pad = ((0, 0), (0, 0), (0, 0), (0, HEAD_SLOT - MLA_NOPE))
    uq_nope = jnp.pad(uq[..., :MLA_NOPE], pad).reshape(depth, Q_LORA, -1)
    uq_rope = uq[..., MLA_NOPE:].reshape(depth, Q_LORA, MLA_HEADS * MLA_ROPE)
    w_q2 = jnp.concatenate([uq_rope, _rot_cols(uq_rope, MLA_ROPE)], axis=-1).astype(BF16)
    q_place, k_place = (jnp.broadcast_to(jnp.asarray(m, BF16), (depth,) + m.shape) for m in _placement_matrices())
    w_qcomb = jnp.concatenate([uq_nope.astype(BF16), q_place], axis=1)

    ukv = w_ukv.reshape(depth, KV_LORA, MLA_HEADS, MLA_NOPE + MLA_V)
    uk = jnp.pad(ukv[..., :MLA_NOPE], pad).reshape(depth, KV_LORA, -1)
    w_kcomb = jnp.concatenate([uk.astype(BF16), k_place], axis=1)
    w_uvt = jnp.swapaxes(ukv[..., MLA_NOPE:].reshape(depth, KV_LORA, MLA_WIDTH), 1, 2).astype(BF16)
    return w_row, w_rvt, w_q2, w_qcomb, w_kcomb, w_uvt


def _mod_kernel(c_ref, w_ref, b_ref, o_ref):
    c = c_ref[...]
    a = (c * jax.nn.sigmoid(c)).astype(BF16)
    o_ref[0] = _dot(a, w_ref[0].astype(BF16)) + b_ref[0]


def _modulation(cc, w_mod, b_mod):
    depth, d, d3 = w_mod.shape
    nblk = d3 // d
    return pl.pallas_call(
        _mod_kernel,
        grid=(depth, nblk),
        in_specs=[pl.BlockSpec(cc.shape, lambda l, j: (0, 0)),
                  pl.BlockSpec((1, d, d), lambda l, j: (l, 0, j)),
                  pl.BlockSpec((1, 1, d), lambda l, j: (l, 0, j))],
        out_specs=pl.BlockSpec((1, cc.shape[0], d), lambda l, j: (l, 0, j)),
        out_shape=jax.ShapeDtypeStruct((depth, cc.shape[0], d3), F32),
        compiler_params=_params(("arbitrary", "arbitrary")),
        name="mod",
    )(cc, w_mod, b_mod.reshape(depth, 1, d3))


_C_Q, _C_KV, _C_KS, _C_RQ, _C_RK, _C_G, _C_GLU, _C_END = 0, 256, 384, 512, 768, 1024, 2048, 2560
_T_CR, _T_SR, _T_CM, _T_SM, _T_KS, _T_END = 0, 256, 512, 768, 1024, 1152


_HALO = 16
_CONV_ROWS = 64


def _rms(x, g):
    return x * lax.rsqrt(jnp.mean(x * x, axis=-1, keepdims=True) + EPS) * g


def _proj_kernel(x_ref, xp_ref, xn_ref, mod_ref, tab_ref, rot_ref, w_ref, wrvt_ref, gq_ref, gkv_ref, wq2_ref, wqc_ref,
                 wkc_ref, wuvt_ref, dw_ref, dwb_ref, clg_ref, clb_ref, pw_ref, pwb_ref, *out_refs, keys_only):
    if keys_only:
        k_ref, vt_ref, rk_ref, rvt_ref = out_refs
    else:
        q_ref, k_ref, vt_ref, rq_ref, rk_ref, rvt_ref, gm_ref, gr_ref, oc_ref, upad, ushift = out_refs
    tm = x_ref.shape[1]
    shift = mod_ref[0, 0:1, :]
    scale = mod_ref[0, 1:2, :]
    pair = 2 * HEAD_SLOT

    def modulate(x):
        mu = jnp.mean(x, axis=-1, keepdims=True)
        xc = x - mu
        var = jnp.mean(xc * xc, axis=-1, keepdims=True)
        return (xc * lax.rsqrt(var + EPS) * (1.0 + scale) + shift).astype(BF16)

    u = modulate(x_ref[0])

    def seg(lo, hi):
        return _dot(u, w_ref[:, lo:hi])

    def store_pairs(ref, val):
        for p in range(MLA_HEADS // 2):
            ref[0, p] = val[:, p * pair:(p + 1) * pair]

    def keys(pkv, pks):
        nkv = _rms(pkv, gkv_ref[...]).astype(BF16)
        ks = pks * tab_ref[:, _T_KS:_T_END]
        store_pairs(k_ref, _dot(jnp.concatenate([nkv, ks.astype(BF16)], axis=1), wkc_ref[...]).astype(BF16))
        vt_ref[0] = _dot_nt(wuvt_ref[...], nkv).astype(BF16)

    def ret_rope(raw):
        rot = _dot(raw.astype(BF16), rot_ref[...])
        return raw * tab_ref[:, _T_CR:_T_SR] + rot * tab_ref[:, _T_SR:_T_CM]

    def ret_keys(prk):
        rk_ref[0] = ret_rope(prk) * (RET_DK ** -0.5)
        rvt_ref[0] = _dot_nt(wrvt_ref[...], u).astype(BF16)

    if keys_only:
        keys(seg(_C_KV, _C_KS), seg(_C_KS, _C_RQ))
        ret_keys(seg(_C_RK, _C_G))
        return

    i = pl.program_id(0)
    u_ext = jnp.concatenate([modulate(xp_ref[0]), u, modulate(xn_ref[0])], axis=0)
    glu = _dot(u_ext, w_ref[:, _C_GLU:_C_END])
    uc = glu[:, :CONV_WIDTH] * jax.nn.sigmoid(glu[:, CONV_WIDTH:])
    row = lax.broadcasted_iota(jnp.int32, uc.shape, 0)
    inside = ((row >= _HALO) | (i > 0)) & ((row < _HALO + tm) | (i < pl.num_programs(0) - 1))
    upad[...] = jnp.where(inside, uc, 0.0)
    span = tm + 2 * _HALO - 8
    for r in range(1, 8):
        ushift[r - 1] = upad[r:r + span, :]
    base = _HALO - CONV_K // 2
    conv_blocks = []

    def conv_rows(n_blocks):
        for _ in range(n_blocks):
            rb = len(conv_blocks) * _CONV_ROWS
            acc = jnp.zeros((_CONV_ROWS, CONV_WIDTH), F32) + dwb_ref[...]
            for j in range(CONV_K):
                a, r = divmod(base + j, 8)
                src = upad if r == 0 else ushift.at[r - 1]
                acc = acc + src[8 * a + rb:8 * a + rb + _CONV_ROWS, :] * dw_ref[j:j + 1, :]
            mu = jnp.mean(acc, axis=-1, keepdims=True)
            ac = acc - mu
            var = jnp.mean(ac * ac, axis=-1, keepdims=True)
            yc = ac * lax.rsqrt(var + EPS) * clg_ref[...] + clb_ref[...]
            conv_blocks.append((yc * jax.nn.sigmoid(yc)).astype(BF16))

    n_conv = tm // _CONV_ROWS
    per_stage = -(-n_conv // 4)

    pq, pkv, pks = seg(_C_Q, _C_KV), seg(_C_KV, _C_KS), seg(_C_KS, _C_RQ)
    prq, prk = seg(_C_RQ, _C_RK), seg(_C_RK, _C_G)
    conv_rows(min(per_stage, n_conv - len(conv_blocks)))

    g = seg(_C_G, _C_GLU)
    g = g * jax.nn.sigmoid(g)
    for p in range(MLA_HEADS // 2):
        gm_ref[0, p] = g[:, p * LANES:(p + 1) * LANES].astype(gm_ref.dtype)
    gr_ref[0] = g[:, MLA_WIDTH:MLA_WIDTH + RET_WIDTH].astype(gr_ref.dtype)
    conv_rows(min(per_stage, n_conv - len(conv_blocks)))

    nq = _rms(pq, gq_ref[...]).astype(BF16)
    qc = _dot(nq, wq2_ref[...])
    keys(pkv, pks)
    conv_rows(min(per_stage, n_conv - len(conv_blocks)))

    rq_ref[0] = ret_rope(prq)
    ret_keys(prk)

    nr = MLA_HEADS * MLA_ROPE
    q_rope = qc[:, :nr] * tab_ref[:, _T_CM:_T_SM] + qc[:, nr:] * tab_ref[:, _T_SM:_T_KS]
    q = _dot(jnp.concatenate([nq, q_rope.astype(BF16)], axis=1), wqc_ref[...])
    store_pairs(q_ref, (q * (MLA_QK ** -0.5 * LOG2E)).astype(BF16))
    conv_rows(n_conv - len(conv_blocks))

    o_conv = _dot(jnp.concatenate(conv_blocks, axis=0), pw_ref[...]) + pwb_ref[...]
    oc_ref[0] = (o_conv * g[:, MLA_WIDTH + RET_WIDTH:]).astype(oc_ref.dtype)


def _layer_spec(a, layer):
    return pl.BlockSpec((None,) + a.shape[1:], lambda *_: (layer,) + (0,) * (a.ndim - 1),
                        pipeline_mode=pl.Buffered(1))


def _mod_spec(mod_all, layer, shared_row, batch_axis):
    def index(*grid):
        return (layer, grid[batch_axis] if shared_row is None else shared_row, 0, 0)
    return pl.BlockSpec((None, 1) + mod_all.shape[2:], index)


def _proj(x, mod_all, mod_row, tab, weights, gq, gkv, conv, layer, tm, keys_only=False):
    b, t, d = x.shape
    w_row, w_rvt, w_q2, w_qcomb, w_kcomb, w_uvt = weights
    const = lambda a: _layer_spec(a, layer)
    row = lambda n: pl.BlockSpec((1, tm, n), lambda i, j: (j, i, 0))
    col = lambda n: pl.BlockSpec((1, n, tm), lambda i, j: (j, 0, i))
    n_pair, pair = MLA_HEADS // 2, 2 * HEAD_SLOT
    pairs = pl.BlockSpec((1, n_pair, tm, pair), lambda i, j: (j, 0, i, 0))
    per_tile = tm // _HALO
    last = t // _HALO - 1
    halo_prev = pl.BlockSpec((1, _HALO, d), lambda i, j: (j, jnp.maximum(i * per_tile - 1, 0), 0))
    halo_next = pl.BlockSpec((1, _HALO, d), lambda i, j: (j, jnp.minimum((i + 1) * per_tile, last), 0))
    rot = jnp.asarray(_rot_cols(np.eye(RET_WIDTH, dtype=np.float32), RET_DK), BF16)
    sds = jax.ShapeDtypeStruct
    outs = dict(q=(sds((b, n_pair, t, pair), BF16), pairs),
                k=(sds((b, n_pair, t, pair), BF16), pairs),
                vt=(sds((b, MLA_WIDTH, t), BF16), col(MLA_WIDTH)),
                rq=(sds((b, t, RET_WIDTH), F32), row(RET_WIDTH)),
                rk=(sds((b, t, RET_WIDTH), F32), row(RET_WIDTH)),
                rvt=(sds((b, RET_WIDTH, t), BF16), col(RET_WIDTH)),
                gate_mla=(sds((b, n_pair, t, 2 * MLA_V), BF16),
                          pl.BlockSpec((1, n_pair, tm, 2 * MLA_V), lambda i, j: (j, 0, i, 0))),
                gate_ret=(sds((b, t, RET_WIDTH), BF16), row(RET_WIDTH)),
                conv=(sds((b, t, CONV_WIDTH), BF16), row(CONV_WIDTH)))
    names = ("k", "vt", "rk", "rvt") if keys_only else tuple(outs)
    scratch = [] if keys_only else [pltpu.VMEM((tm + 2 * _HALO, CONV_WIDTH), F32),
                                    pltpu.VMEM((7, tm + 2 * _HALO - 8, CONV_WIDTH), F32)]
    res = pl.pallas_call(
        functools.partial(_proj_kernel, keys_only=keys_only),
        grid=(t // tm, b),
        in_specs=[row(d), halo_prev, halo_next,
                  _mod_spec(mod_all, layer, mod_row, 1),
                  pl.BlockSpec((tm, _T_END), lambda i, j: (i, 0)),
                  pl.BlockSpec(rot.shape, lambda i, j: (0, 0)),
                  const(w_row), const(w_rvt), const(gq), const(gkv), const(w_q2), const(w_qcomb), const(w_kcomb),
                  const(w_uvt)] + [const(a) for a in conv],
        out_specs=tuple(outs[n][1] for n in names),
        out_shape=tuple(outs[n][0] for n in names),
        scratch_shapes=scratch,
        compiler_params=_params(("arbitrary", "arbitrary")),
        name="proj",
    )(x, x, x, mod_all, tab, rot, w_row, w_rvt, gq, gkv, w_q2, w_qcomb, w_kcomb, w_uvt, *conv)
    res = dict(zip(names, res))
    res["vt"] = res["vt"].reshape(b, n_pair, 2 * MLA_V, t)
    return res


_ATTN_KEY_BLOCK = 512


def _attn_kernel(*refs, n_seg, tq):
    q_ref, g_ref = refs[:2]
    k_refs = refs[2:2 + n_seg]
    vt_refs = refs[2 + n_seg:2 + 2 * n_seg]
    o_ref = refs[2 + 2 * n_seg]
    s_bufs = refs[3 + 2 * n_seg:5 + 2 * n_seg]
    m_bufs = refs[5 + 2 * n_seg:7 + 2 * n_seg]
    n_pair = q_ref.shape[1]
    n_q = q_ref.shape[2] // tq
    n_items = n_pair * n_q
    lanes = [slice(j * HEAD_SLOT, (j + 1) * HEAD_SLOT) for j in range(2)]
    rows = [slice(j * MLA_V, (j + 1) * MLA_V) for j in range(2)]

    blocks, off = [], 0
    for si in range(n_seg):
        ts = k_refs[si].shape[2]
        kb = min(_ATTN_KEY_BLOCK, ts)
        blocks += [(si, b0, kb, off + b0) for b0 in range(0, ts, kb)]
        off += ts

    def locate(item):
        if isinstance(item, int):
            return item // n_q, (item % n_q) * tq
        pair = lax.div(item, jnp.int32(n_q))
        return pair, pl.multiple_of((item - pair * n_q) * tq, tq)

    def phase(nxt, cur):
        if nxt is not None:
            item_n, s_n, m_n = nxt
            pn, rn = locate(item_n)
            qn = [q_ref[0, pn, pl.ds(rn, tq), lanes[j]] for j in range(2)]
            mx = [None, None]
        if cur is not None:
            item_c, s_c, m_c = cur
            pc, rc = locate(item_c)
            mc = [m_c[j] for j in range(2)]
            acc, l = [None, None], [None, None]
        for si, b0, kb, o0 in blocks:
            for j in range(2):
                if nxt is not None:
                    s = _dot_nt(k_refs[si][0, pn, b0:b0 + kb, lanes[j]], qn[j])
                    s_n[j, o0:o0 + kb, :] = s
                    mb = jnp.max(s, axis=0, keepdims=True)
                    mx[j] = mb if mx[j] is None else jnp.maximum(mx[j], mb)
                if cur is not None:
                    p = jnp.exp2(s_c[j, o0:o0 + kb, :] - mc[j])
                    ls = jnp.sum(p, axis=0, keepdims=True)
                    pv = _dot(vt_refs[si][0, pc, rows[j], b0:b0 + kb], p.astype(BF16))
                    l[j] = ls if l[j] is None else l[j] + ls
                    acc[j] = pv if acc[j] is None else acc[j] + pv
        if nxt is not None:
            for j in range(2):
                m_n[j] = mx[j]
        if cur is not None:
            outs = [acc[j] * (1.0 / l[j]) for j in range(2)]
            gate = g_ref[0, pc, pl.ds(rc, tq), :].astype(F32)
            o_ref[0, pc, pl.ds(rc, tq), :] = (jnp.concatenate(outs, axis=0).T * gate).astype(o_ref.dtype)

    buf = [(s_bufs[0], m_bufs[0]), (s_bufs[1], m_bufs[1])]
    phase((0, *buf[0]), None)
    assert n_items % 2 == 0

    def body(i, carry):
        n = 2 * i
        phase((n + 1, *buf[1]), (n, *buf[0]))
        phase((jnp.minimum(n + 2, n_items - 1), *buf[0]), (n + 1, *buf[1]))
        return carry

    lax.fori_loop(0, n_items // 2, body, 0)


def _attention(q, gate, ks, vts, tq):
    b, n_pair, t, pair = q.shape
    n_seg = len(ks)
    n_keys = sum(k.shape[2] for k in ks)
    whole = lambda a: pl.BlockSpec((1,) + a.shape[1:], lambda i: (i, 0, 0, 0))
    score_buf = pltpu.VMEM((2, n_keys, tq), F32)
    max_buf = pltpu.VMEM((2, 1, tq), F32)
    return pl.pallas_call(
        functools.partial(_attn_kernel, n_seg=n_seg, tq=tq),
        grid=(b,),
        in_specs=[whole(q), whole(gate)] + [whole(k) for k in ks] + [whole(v) for v in vts],
        out_specs=pl.BlockSpec((1, n_pair, t, 2 * MLA_V), lambda i: (i, 0, 0, 0)),
        out_shape=jax.ShapeDtypeStruct((b, n_pair, t, 2 * MLA_V), BF16),
        scratch_shapes=[score_buf, score_buf, max_buf, max_buf],
        compiler_params=_params(("arbitrary",)),
        name="attention",
    )(q, gate, *ks, *vts)


_RET_GROUP = 8


def _log_sigmoid(x):
    return jnp.minimum(x, 0.0) - jnp.log1p(jnp.exp(-jnp.abs(x)))


def _ret_kernel(dec_ref, *refs, chunk, group, layer, states_only):
    if states_only:
        k_ref, vt_ref, s0_ref, sfin_ref, sf_scr, sb_scr, kd_scr, qd_scr, dt_scr, cc_scr = refs
    else:
        (q_ref, k_ref, vt_ref, s0_ref, gng_ref, gnb_ref, gate_ref, o_ref, sfin_ref, sf_scr, sb_scr, kd_scr, qd_scr,
         dt_scr, cc_scr) = refs
    c_len = chunk
    t = k_ref.shape[1]
    n_c = t // c_len
    hp = pl.program_id(0)

    @pl.when(pl.program_id(1) == 0)
    def _():
        r = lax.broadcasted_iota(jnp.int32, (c_len, LANES), 0).astype(F32)
        lane = lax.broadcasted_iota(jnp.int32, (c_len, LANES), 1)
        km = lax.broadcasted_iota(jnp.int32, (c_len, c_len), 0)
        qn = lax.broadcasted_iota(jnp.int32, (c_len, c_len), 1)
        diff = (qn - km).astype(F32)
        for j in range(2):
            h = hp * 2 + j
            lg_f = _log_sigmoid(jnp.full((c_len, LANES), dec_ref[layer, 0, h], F32))
            lg_b = _log_sigmoid(jnp.full((c_len, LANES), dec_ref[layer, 1, h], F32))
            own = ((lane >= j * RET_DK) & (lane < (j + 1) * RET_DK)).astype(F32)
            lg_f2 = _log_sigmoid(jnp.full((c_len, c_len), dec_ref[layer, 0, h], F32))
            lg_b2 = _log_sigmoid(jnp.full((c_len, c_len), dec_ref[layer, 1, h], F32))
            kd_scr[j] = jnp.concatenate([jnp.exp(lg_f * (c_len - 1.0 - r)) * own, jnp.exp(lg_b * r) * own, own], axis=1)
            qd_scr[j] = jnp.concatenate([jnp.exp(lg_f * (r + 1.0)), jnp.exp(lg_b * (c_len - r))], axis=1)
            dt_scr[j] = jnp.exp(jnp.where(diff >= 0, lg_f2 * diff, -lg_b2 * diff))
            cc_scr[j] = jnp.concatenate(
                [jnp.exp(_log_sigmoid(jnp.full((RET_DV, LANES), dec_ref[layer, 0, h], F32)) * c_len),
                 jnp.exp(_log_sigmoid(jnp.full((RET_DV, LANES), dec_ref[layer, 1, h], F32)) * c_len)], axis=1)

    heads = []
    for j in range(2):
        heads.append(dict(
            kdf=kd_scr[j, :, 0:LANES],
            kdb=kd_scr[j, :, LANES:2 * LANES],
            own=kd_scr[j, :, 2 * LANES:3 * LANES],
            qd=qd_scr[j],
            dt=dt_scr[j],
            cf=cc_scr[j, :, 0:LANES],
            cb=cc_scr[j, :, LANES:2 * LANES],
            rows=slice(j * RET_DV, (j + 1) * RET_DV),
        ))

    def local(i, carry):
        for g in range(group):
            c = i * group + g
            c0 = pl.multiple_of(c * c_len, c_len)
            kc = k_ref[0, pl.ds(c0, c_len), :]
            for j, hd in enumerate(heads):
                vt = vt_ref[0, hd["rows"], pl.ds(c0, c_len)]
                sf_scr[j, c] = _dot(vt, (kc * hd["kdf"]).astype(BF16))
                sb_scr[j, c] = _dot(vt, (kc * hd["kdb"]).astype(BF16))
        return carry

    lax.fori_loop(0, n_c // group, local, 0)

    for j, hd in enumerate(heads):
        sf = s0_ref[0, 0, 2 * j]
        for c in range(n_c):
            a = sf_scr[j, c]
            sf_scr[j, c] = sf
            sf = hd["cf"] * sf + a
        sfin_ref[0, 0, 2 * j] = sf
        sb = s0_ref[0, 0, 2 * j + 1]
        for c in reversed(range(n_c)):
            a = sb_scr[j, c]
            sb_scr[j, c] = sb
            sb = hd["cb"] * sb + a
        sfin_ref[0, 0, 2 * j + 1] = sb

    if states_only:
        return

    def outputs(i, carry):
        items = []
        for g in range(group):
            c = i * group + g
            c0 = pl.multiple_of(c * c_len, c_len)
            qc = q_ref[0, pl.ds(c0, c_len), :]
            kc = k_ref[0, pl.ds(c0, c_len), :]
            items.append(dict(c=c, c0=c0, kc=kc, qb=qc.astype(BF16), qq=jnp.concatenate([qc, qc], axis=1)))
        work = [(it, j, hd) for it in items for j, hd in enumerate(heads)]
        st = [_dot_nt((it["kc"] * hd["own"]).astype(BF16), it["qb"]) for it, j, hd in work]
        cross = [_dot_nt(jnp.concatenate([sf_scr[j, it["c"]], sb_scr[j, it["c"]]], axis=1).astype(BF16),
                         (it["qq"] * hd["qd"]).astype(BF16)) for it, j, hd in work]
        o = [_dot(vt_ref[0, hd["rows"], pl.ds(it["c0"], c_len)], (s * hd["dt"]).astype(BF16)) + x
             for (it, j, hd), s, x in zip(work, st, cross)]
        ys = []
        for v in o:
            mu = jnp.mean(v, axis=0, keepdims=True)
            vc = v - mu
            var = jnp.mean(vc * vc, axis=0, keepdims=True)
            ys.append(vc * lax.rsqrt(var + EPS))
        for g, it in enumerate(items):
            y = jnp.concatenate(ys[2 * g:2 * g + 2], axis=0).T
            gate = gate_ref[0, pl.ds(it["c0"], c_len), :].astype(F32)
            o_ref[0, pl.ds(it["c0"], c_len), :] = ((y * gng_ref[...] + gnb_ref[...]) * gate).astype(o_ref.dtype)
        return carry

    lax.fori_loop(0, n_c // group, outputs, 0)


def _retention(dec, rq, rk, rvt, s0, gn_g, gn_b, gate, layer, chunk):
    b, t, _ = rk.shape
    states_only = rq is None
    n_pair = RET_HEADS // 2
    pair = 2 * RET_DK
    n_c = t // chunk
    group = math.gcd(n_c, _RET_GROUP)
    state_buf = pltpu.VMEM((2, n_c, RET_DV, pair), F32)
    tables = [pltpu.VMEM((2, chunk, 3 * LANES), F32), pltpu.VMEM((2, chunk, 2 * LANES), F32),
              pltpu.VMEM((2, chunk, chunk), F32), pltpu.VMEM((2, RET_DV, 2 * LANES), F32)]
    seq = pl.BlockSpec((1, t, pair), lambda h, i: (i, 0, h))
    state = pl.BlockSpec((1, 1, 4, RET_DV, pair), lambda h, i: (i, h, 0, 0, 0))
    affine = pl.BlockSpec((None, 1, pair), lambda h, i: (layer, 0, h))
    state_shape = jax.ShapeDtypeStruct((b, n_pair, 4, RET_DV, pair), F32)
    smem = pl.BlockSpec(memory_space=pltpu.SMEM)
    vt_spec = pl.BlockSpec((1, 2 * RET_DV, t), lambda h, i: (i, h, 0))
    if states_only:
        in_specs, args = [smem, seq, vt_spec, state], (dec, rk, rvt, s0)
        out_specs, out_shape = state, state_shape
    else:
        in_specs = [smem, seq, seq, vt_spec, state, affine, affine, seq]
        args = (dec, rq, rk, rvt, s0, gn_g, gn_b, gate)
        out_specs, out_shape = (seq, state), (jax.ShapeDtypeStruct((b, t, RET_WIDTH), BF16), state_shape)
    res = pl.pallas_call(
        functools.partial(_ret_kernel, chunk=chunk, group=group, layer=layer, states_only=states_only),
        grid=(n_pair, b),
        in_specs=in_specs,
        out_specs=out_specs,
        out_shape=out_shape,
        scratch_shapes=[state_buf, state_buf] + tables,
        compiler_params=_params(("arbitrary", "arbitrary")),
        name="retention",
    )(*args)
    return (None, res) if states_only else res


def _final_kernel(x_ref, mod_ref, om_ref, or_ref, oc_ref, wo_ref, lng_ref, lnb_ref, o_ref, *, alpha):
    o_cat = jnp.concatenate([om_ref[0, p] for p in range(om_ref.shape[1])] + [or_ref[0], oc_ref[0]], axis=1)
    y = _dot(o_cat, wo_ref[...])

    v = alpha * x_ref[0] + mod_ref[0, 2:3, :] * y
    mu = jnp.mean(v, axis=-1, keepdims=True)
    vc = v - mu
    var = jnp.mean(vc * vc, axis=-1, keepdims=True)
    o_ref[0] = vc * lax.rsqrt(var + EPS) * lng_ref[...] + lnb_ref[...]


def _final(x, mod_all, mod_row, o_mla, o_ret, o_conv, w_out, ln_g, ln_b, layer, tm, alpha):
    b, t, d = x.shape
    const = lambda a: _layer_spec(a, layer)
    row = lambda n: pl.BlockSpec((1, tm, n), lambda i, j: (i, j, 0))
    return pl.pallas_call(
        functools.partial(_final_kernel, alpha=alpha),
        grid=(b, t // tm),
        in_specs=[row(d),
                  _mod_spec(mod_all, layer, mod_row, 0),
                  pl.BlockSpec((1, o_mla.shape[1], tm, o_mla.shape[3]), lambda i, j: (i, 0, j, 0)),
                  row(RET_WIDTH), row(CONV_WIDTH),
                  const(w_out), const(ln_g), const(ln_b)],
        out_specs=row(d),
        out_shape=jax.ShapeDtypeStruct((b, t, d), F32),
        compiler_params=_params(("arbitrary", "arbitrary")),
        name="final",
    )(x, mod_all, o_mla, o_ret, o_conv, w_out, ln_g, ln_b)


def _tile(n, pref):
    return pref if n % pref == 0 else n


def kernel(x, c, ctx, c_ctx, w_mod, b_mod, w_in, mla_q_norm, w_uq, mla_kv_norm, w_ukv, ret_decay_fwd, ret_decay_bwd,
           ret_gn_g, ret_gn_b, conv_dw, conv_dw_b, conv_ln_g, conv_ln_b, conv_pw, conv_pw_b, w_out, ln_g, ln_b):
    depth = w_mod.shape[0]
    b, t, d = x.shape
    t_ctx = ctx.shape[1]
    alpha = (2 * depth) ** 0.25
    n_mod_rows = 16
    cc = jnp.concatenate([c, c_ctx[None, :], jnp.zeros((n_mod_rows - b - 1, d), c.dtype)], axis=0)
    mod_all = _modulation(cc, w_mod, b_mod).reshape(depth, n_mod_rows, 3, d)
    tab_x = _rope_table_block(t, True)
    tab_c = _rope_table_block(t_ctx, False)
    s_zero = jnp.zeros((b, RET_HEADS // 2, 4, RET_DV, 2 * RET_DK), F32)

    row3 = lambda a: a.reshape(depth, 1, -1)
    weights = _prep_weights(w_in, w_uq, w_ukv)
    gq, gkv = row3(mla_q_norm), row3(mla_kv_norm)
    dec = jnp.stack([ret_decay_fwd, ret_decay_bwd], axis=1).astype(F32)
    gn_g, gn_b = row3(ret_gn_g), row3(ret_gn_b)
    dw = jnp.pad(conv_dw, ((0, 0), (0, 1), (0, 0)))
    conv = (dw, row3(conv_dw_b), row3(conv_ln_g), row3(conv_ln_b), conv_pw.astype(BF16), row3(conv_pw_b))
    tail = (w_out.astype(BF16), row3(ln_g), row3(ln_b))
    tm_x, tm_c = _tile(t, 1024), _tile(t_ctx, 256)
    tf_x, tf_c = _tile(t, 1024), _tile(t_ctx, 256)
    tq_x, tq_c = _tile(t, 256), _tile(t_ctx, 256)
    ch_x, ch_c = _tile(t, 256), _tile(t_ctx, 256)

    hc = ctx
    for l in range(depth):
        need_ctx = l < depth - 1
        pc = _proj(hc, mod_all, b, tab_c, weights, gq, gkv, conv, l, tm_c, keys_only=not need_ctx)
        px = _proj(x, mod_all, None, tab_x, weights, gq, gkv, conv, l, tm_x)

        oc_ret, s_ctx = _retention(dec, pc.get("rq"), pc["rk"], pc["rvt"], s_zero, gn_g, gn_b, pc.get("gate_ret"),
                                   l, ch_c)
        o_ret, _ = _retention(dec, px["rq"], px["rk"], px["rvt"], s_ctx, gn_g, gn_b, px["gate_ret"], l, ch_x)
        o_mla = _attention(px["q"], px["gate_mla"], [pc["k"], px["k"]], [pc["vt"], px["vt"]], tq_x)
        x_new = _final(x, mod_all, None, o_mla, o_ret, px["conv"], *tail, layer=l, tm=tf_x, alpha=alpha)
        if need_ctx:
            oc_mla = _attention(pc["q"], pc["gate_mla"], [pc["k"]], [pc["vt"]], tq_c)
            hc = _final(hc, mod_all, b, oc_mla, oc_ret, pc["conv"], *tail, layer=l, tm=tf_c, alpha=alpha)
        x = x_new
    return x
```

```python
import functools
import math

import jax
import jax.numpy as jnp
import numpy as np
from jax import lax
from jax.experimental import pallas as pl
from jax.experimental.pallas import tpu as pltpu

GRID_W = 64
MLA_HEADS = 8
MLA_NOPE = 64
MLA_ROPE = 32
MLA_V = 64
MLA_QK = MLA_NOPE + MLA_ROPE
MLA_WIDTH = MLA_HEADS * MLA_V
Q_LORA = 256
KV_LORA = 128
RET_HEADS = 4
RET_DK = 64
RET_DV = 64
RET_WIDTH = RET_HEADS * RET_DV
CONV_WIDTH = 256
CONV_K = 31
ROPE_BASE = 10000.0
EPS = 1e-5

LANES = 128
HEAD_SLOT = LANES
VMEM_LIMIT_BYTES = 56 * 1024 * 1024
LOG2E = 1.4426950408889634

F32 = jnp.float32
BF16 = jnp.bfloat16

_NT = (((1,), (1,)), ((), ()))


def _dot(a, b):
    return jnp.dot(a, b, preferred_element_type=F32)


def _dot_nt(a, b):
    return lax.dot_general(a, b, _NT, preferred_element_type=F32)


def _params(semantics):
    return pltpu.CompilerParams(dimension_semantics=semantics, vmem_limit_bytes=VMEM_LIMIT_BYTES)


def _rot_cols(w, unit):
    xp = np if isinstance(w, np.ndarray) else jnp
    q = unit // 4
    w5 = w.reshape(w.shape[:-1] + (w.shape[-1] // unit, 2, 2, q))
    return xp.stack([-w5[..., 1, :], w5[..., 0, :]], axis=-2).reshape(w.shape)


def _rope_tables(length, unit, reps):
    d2 = unit // 2
    t = np.arange(length, dtype=np.int32)
    inv = np.float32(ROPE_BASE) ** (-np.arange(0, d2, 2, dtype=np.float32) / np.float32(d2))

    def half(pos):
        ang = pos.astype(np.float32)[:, None] * inv[None, :]
        return np.concatenate([np.cos(ang)] * 2, axis=-1), np.concatenate([np.sin(ang)] * 2, axis=-1)

    cr, sr = half(t // GRID_W)
    cc, sc = half(t % GRID_W)
    cos = np.concatenate([cr, cc], axis=-1).astype(np.float32)
    sin = np.concatenate([sr, sc], axis=-1).astype(np.float32)
    return np.tile(cos, (1, reps)), np.tile(sin, (1, reps))


def _rope_table_block(length, rotate):
    if rotate:
        cos_r, sin_r = _rope_tables(length, RET_DK, RET_HEADS)
        cos_m, sin_m = _rope_tables(length, MLA_ROPE, MLA_HEADS)
    else:
        cos_r = np.ones((length, RET_WIDTH), np.float32)
        sin_r = np.zeros((length, RET_WIDTH), np.float32)
        cos_m = np.ones((length, MLA_HEADS * MLA_ROPE), np.float32)
        sin_m = np.zeros((length, MLA_HEADS * MLA_ROPE), np.float32)
    kslot = np.concatenate([cos_m[:, :MLA_ROPE], sin_m[:, :MLA_ROPE],
                            np.zeros((length, LANES - 2 * MLA_ROPE), np.float32)], axis=-1)
    return jnp.asarray(np.concatenate([cos_r, sin_r, cos_m, sin_m, kslot], axis=-1))


def _placement_matrices():
    q_place = np.zeros((MLA_HEADS, MLA_ROPE, MLA_HEADS, HEAD_SLOT), np.float32)
    k_place = np.zeros((LANES, MLA_HEADS, HEAD_SLOT), np.float32)
    for d in range(MLA_ROPE):
        for h in range(MLA_HEADS):
            q_place[h, d, h, MLA_NOPE + d] = 1.0
            k_place[d, h, MLA_NOPE + d] = 1.0
            k_place[MLA_ROPE + d, h, MLA_NOPE + d] = 1.0
    return (q_place.reshape(MLA_HEADS * MLA_ROPE, MLA_HEADS * HEAD_SLOT),
            k_place.reshape(LANES, MLA_HEADS * HEAD_SLOT))


def _prep_weights(w_in, w_uq, w_ukv):
    depth, d, _ = w_in.shape
    sizes = (Q_LORA, KV_LORA, MLA_ROPE, MLA_WIDTH, RET_HEADS * RET_DK, RET_HEADS * RET_DK, RET_WIDTH,
             RET_WIDTH, 2 * CONV_WIDTH, CONV_WIDTH)
    offs = [0]
    for s in sizes:
        offs.append(offs[-1] + s)
    seg = [w_in[..., offs[i]:offs[i + 1]] for i in range(len(sizes))]
    wq, wkv, wkr, wg_mla, wrq, wrk, wrv, wg_ret, wglu, wg_conv = seg
    kslot = jnp.concatenate([wkr, _rot_cols(wkr, MLA_ROPE), jnp.zeros((depth, d, LANES - 2 * MLA_ROPE), w_in.dtype)],
                            axis=-1)
    w_row = jnp.concatenate([wq, wkv, kslot, wrq, wrk, wg_mla, wg_ret, wg_conv, wglu],
                            axis=-1).astype(BF16)
    w_rvt = jnp.swapaxes(wrv, 1, 2).astype(BF16)

    uq = w_uq.reshape(depth, Q_LORA, MLA_HEADS, MLA_QK)
    pad = ((0, 0), (0, 0), (0, 0), (0, HEAD_SLOT - MLA_NOPE))
    uq_nope = jnp.pad(uq[..., :MLA_NOPE], pad).reshape(depth, Q_LORA, -1)
    uq_rope = uq[..., MLA_NOPE:].reshape(depth, Q_LORA, MLA_HEADS * MLA_ROPE)
    w_q2 = jnp.concatenate([uq_rope, _rot_cols(uq_rope, MLA_ROPE)], axis=-1).astype(BF16)
    q_place, k_place = (jnp.broadcast_to(jnp.asarray(m, BF16), (depth,) + m.shape) for m in _placement_matrices())
    w_qcomb = jnp.concatenate([uq_nope.astype(BF16), q_place], axis=1)

    ukv = w_ukv.reshape(depth, KV_LORA, MLA_HEADS, MLA_NOPE + MLA_V)
    uk = jnp.pad(ukv[..., :MLA_NOPE], pad).reshape(depth, KV_LORA, -1)
    w_kcomb = jnp.concatenate([uk.astype(BF16), k_place], axis=1)
    w_uvt = jnp.swapaxes(ukv[..., MLA_NOPE:].reshape(depth, KV_LORA, MLA_WIDTH), 1, 2).astype(BF16)
    return w_row, w_rvt, w_q2, w_qcomb, w_kcomb, w_uvt


def _mod_kernel(c_ref, w_ref, b_ref, o_ref):
    c = c_ref[...]
    a = (c * jax.nn.sigmoid(c)).astype(BF16)
    o_ref[0] = _dot(a, w_ref[0].astype(BF16)) + b_ref[0]


def _modulation(cc, w_mod, b_mod):
    depth, d, d3 = w_mod.shape
    nblk = d3 // d
    return pl.pallas_call(
        _mod_kernel,
        grid=(depth, nblk),
        in_specs=[pl.BlockSpec(cc.shape, lambda l, j: (0, 0)),
                  pl.BlockSpec((1, d, d), lambda l, j: (l, 0, j)),
                  pl.BlockSpec((1, 1, d), lambda l, j: (l, 0, j))],
        out_specs=pl.BlockSpec((1, cc.shape[0], d), lambda l, j: (l, 0, j)),
        out_shape=jax.ShapeDtypeStruct((depth, cc.shape[0], d3), F32),
        compiler_params=_params(("arbitrary", "arbitrary")),
        name="mod",
    )(cc, w_mod, b_mod.reshape(depth, 1, d3))


_C_Q, _C_KV, _C_KS, _C_RQ, _C_RK, _C_G, _C_GLU, _C_END = 0, 256, 384, 512, 768, 1024, 2048, 2560
_T_CR, _T_SR, _T_CM, _T_SM, _T_KS, _T_END = 0, 256, 512, 768, 1024, 1152


_HALO = 16
_CONV_ROWS = 64


def _rms(x, g):
    return x * lax.rsqrt(jnp.mean(x * x, axis=-1, keepdims=True) + EPS) * g


def _proj_kernel(x_ref, xp_ref, xn_ref, mod_ref, tab_ref, rot_ref, w_ref, wrvt_ref, gq_ref, gkv_ref, wq2_ref, wqc_ref,
                 wkc_ref, wuvt_ref, dw_ref, dwb_ref, clg_ref, clb_ref, pw_ref, pwb_ref, *out_refs, keys_only):
    if keys_only:
        k_ref, vt_ref, rk_ref, rvt_ref = out_refs
    else:
        q_ref, k_ref, vt_ref, rq_ref, rk_ref, rvt_ref, gm_ref, gr_ref, oc_ref, upad, ushift = out_refs
    tm = x_ref.shape[1]
    shift = mod_ref[0, 0:1, :]
    scale = mod_ref[0, 1:2, :]
    pair = 2 * HEAD_SLOT

    def modulate(x):
        mu = jnp.mean(x, axis=-1, keepdims=True)
        xc = x - mu
        var = jnp.mean(xc * xc, axis=-1, keepdims=True)
        return (xc * lax.rsqrt(var + EPS) * (1.0 + scale) + shift).astype(BF16)

    u = modulate(x_ref[0])

    def seg(lo, hi):
        return _dot(u, w_ref[:, lo:hi])

    def store_pairs(ref, val):
        for p in range(MLA_HEADS // 2):
            ref[0, p] = val[:, p * pair:(p + 1) * pair]

    def keys(pkv, pks):
        nkv = _rms(pkv, gkv_ref[...]).astype(BF16)
        ks = pks * tab_ref[:, _T_KS:_T_END]
        store_pairs(k_ref, _dot(jnp.concatenate([nkv, ks.astype(BF16)], axis=1), wkc_ref[...]).astype(BF16))
        vt_ref[0] = _dot_nt(wuvt_ref[...], nkv).astype(BF16)

    def ret_rope(raw):
        rot = _dot(raw.astype(BF16), rot_ref[...])
        return raw * tab_ref[:, _T_CR:_T_SR] + rot * tab_ref[:, _T_SR:_T_CM]

    def ret_keys(prk):
        rk_ref[0] = ret_rope(prk) * (RET_DK ** -0.5)
        rvt_ref[0] = _dot_nt(wrvt_ref[...], u).astype(BF16)

    if keys_only:
        keys(seg(_C_KV, _C_KS), seg(_C_KS, _C_RQ))
        ret_keys(seg(_C_RK, _C_G))
        return

    i = pl.program_id(0)
    u_ext = jnp.concatenate([modulate(xp_ref[0]), u, modulate(xn_ref[0])], axis=0)
    glu = _dot(u_ext, w_ref[:, _C_GLU:_C_END])
    uc = glu[:, :CONV_WIDTH] * jax.nn.sigmoid(glu[:, CONV_WIDTH:])
    row = lax.broadcasted_iota(jnp.int32, uc.shape, 0)
    inside = ((row >= _HALO) | (i > 0)) & ((row < _HALO + tm) | (i < pl.num_programs(0) - 1))
    upad[...] = jnp.where(inside, uc, 0.0)
    span = tm + 2 * _HALO - 8
    for r in range(1, 8):
        ushift[r - 1] = upad[r:r + span, :]
    base = _HALO - CONV_K // 2
    conv_blocks = []

    def conv_rows(n_blocks):
        for _ in range(n_blocks):
            rb = len(conv_blocks) * _CONV_ROWS
            acc = jnp.zeros((_CONV_ROWS, CONV_WIDTH), F32) + dwb_ref[...]
            for j in range(CONV_K):
                a, r = divmod(base + j, 8)
                src = upad if r == 0 else ushift.at[r - 1]
                acc = acc + src[8 * a + rb:8 * a + rb + _CONV_ROWS, :] * dw_ref[j:j + 1, :]
            mu = jnp.mean(acc, axis=-1, keepdims=True)
            ac = acc - mu
            var = jnp.mean(ac * ac, axis=-1, keepdims=True)
            yc = ac * lax.rsqrt(var + EPS) * clg_ref[...] + clb_ref[...]
            conv_blocks.append((yc * jax.nn.sigmoid(yc)).astype(BF16))

    n_conv = tm // _CONV_ROWS
    per_stage = -(-n_conv // 4)

    pq, pkv, pks = seg(_C_Q, _C_KV), seg(_C_KV, _C_KS), seg(_C_KS, _C_RQ)
    prq, prk = seg(_C_RQ, _C_RK), seg(_C_RK, _C_G)
    conv_rows(min(per_stage, n_conv - len(conv_blocks)))

    g = seg(_C_G, _C_GLU)
    g = g * jax.nn.sigmoid(g)
    for p in range(MLA_HEADS // 2):
        gm_ref[0, p] = g[:, p * LANES:(p + 1) * LANES].astype(gm_ref.dtype)
    gr_ref[0] = g[:, MLA_WIDTH:MLA_WIDTH + RET_WIDTH].astype(gr_ref.dtype)
    conv_rows(min(per_stage, n_conv - len(conv_blocks)))

    nq = _rms(pq, gq_ref[...]).astype(BF16)
    qc = _dot(nq, wq2_ref[...])
    keys(pkv, pks)
    conv_rows(min(per_stage, n_conv - len(conv_blocks)))

    rq_ref[0] = ret_rope(prq)
    ret_keys(prk)

    nr = MLA_HEADS * MLA_ROPE
    q_rope = qc[:, :nr] * tab_ref[:, _T_CM:_T_SM] + qc[:, nr:] * tab_ref[:, _T_SM:_T_KS]
    q = _dot(jnp.concatenate([nq, q_rope.astype(BF16)], axis=1), wqc_ref[...])
    store_pairs(q_ref, (q * (MLA_QK ** -0.5 * LOG2E)).astype(BF16))
    conv_rows(n_conv - len(conv_blocks))

    o_conv = _dot(jnp.concatenate(conv_blocks, axis=0), pw_ref[...]) + pwb_ref[...]
    oc_ref[0] = (o_conv * g[:, MLA_WIDTH + RET_WIDTH:]).astype(oc_ref.dtype)


def _layer_spec(a, layer):
    return pl.BlockSpec((None,) + a.shape[1:], lambda *_: (layer,) + (0,) * (a.ndim - 1),
                        pipeline_mode=pl.Buffered(1))


def _mod_spec(mod_all, layer, shared_row, batch_axis):
    def index(*grid):
        return (layer, grid[batch_axis] if shared_row is None else shared_row, 0, 0)
    return pl.BlockSpec((None, 1) + mod_all.shape[2:], index)


def _proj(x, mod_all, mod_row, tab, weights, gq, gkv, conv, layer, tm, keys_only=False):
    b, t, d = x.shape
    w_row, w_rvt, w_q2, w_qcomb, w_kcomb, w_uvt = weights
    const = lambda a: _layer_spec(a, layer)
    row = lambda n: pl.BlockSpec((1, tm, n), lambda i, j: (j, i, 0))
    col = lambda n: pl.BlockSpec((1, n, tm), lambda i, j: (j, 0, i))
    n_pair, pair = MLA_HEADS // 2, 2 * HEAD_SLOT
    pairs = pl.BlockSpec((1, n_pair, tm, pair), lambda i, j: (j, 0, i, 0))
    per_tile = tm // _HALO
    last = t // _HALO - 1
    halo_prev = pl.BlockSpec((1, _HALO, d), lambda i, j: (j, jnp.maximum(i * per_tile - 1, 0), 0))
    halo_next = pl.BlockSpec((1, _HALO, d), lambda i, j: (j, jnp.minimum((i + 1) * per_tile, last), 0))
    rot = jnp.asarray(_rot_cols(np.eye(RET_WIDTH, dtype=np.float32), RET_DK), BF16)
    sds = jax.ShapeDtypeStruct
    outs = dict(q=(sds((b, n_pair, t, pair), BF16), pairs),
                k=(sds((b, n_pair, t, pair), BF16), pairs),
                vt=(sds((b, MLA_WIDTH, t), BF16), col(MLA_WIDTH)),
                rq=(sds((b, t, RET_WIDTH), F32), row(RET_WIDTH)),
                rk=(sds((b, t, RET_WIDTH), F32), row(RET_WIDTH)),
                rvt=(sds((b, RET_WIDTH, t), BF16), col(RET_WIDTH)),
                gate_mla=(sds((b, n_pair, t, 2 * MLA_V), BF16),
                          pl.BlockSpec((1, n_pair, tm, 2 * MLA_V), lambda i, j: (j, 0, i, 0))),
                gate_ret=(sds((b, t, RET_WIDTH), BF16), row(RET_WIDTH)),
                conv=(sds((b, t, CONV_WIDTH), BF16), row(CONV_WIDTH)))
    names = ("k", "vt", "rk", "rvt") if keys_only else tuple(outs)
    scratch = [] if keys_only else [pltpu.VMEM((tm + 2 * _HALO, CONV_WIDTH), F32),
                                    pltpu.VMEM((7, tm + 2 * _HALO - 8, CONV_WIDTH), F32)]
    res = pl.pallas_call(
        functools.partial(_proj_kernel, keys_only=keys_only),
        grid=(t // tm, b),
        in_specs=[row(d), halo_prev, halo_next,
                  _mod_spec(mod_all, layer, mod_row, 1),
                  pl.BlockSpec((tm, _T_END), lambda i, j: (i, 0)),
                  pl.BlockSpec(rot.shape, lambda i, j: (0, 0)),
                  const(w_row), const(w_rvt), const(gq), const(gkv), const(w_q2), const(w_qcomb), const(w_kcomb),
                  const(w_uvt)] + [const(a) for a in conv],
        out_specs=tuple(outs[n][1] for n in names),
        out_shape=tuple(outs[n][0] for n in names),
        scratch_shapes=scratch,
        compiler_params=_params(("arbitrary", "arbitrary")),
        name="proj",
    )(x, x, x, mod_all, tab, rot, w_row, w_rvt, gq, gkv, w_q2, w_qcomb, w_kcomb, w_uvt, *conv)
    res = dict(zip(names, res))
    res["vt"] = res["vt"].reshape(b, n_pair, 2 * MLA_V, t)
    return res


_ATTN_KEY_BLOCK = 256


def _attn_kernel(*refs, n_seg, tq):
    q_ref, g_ref = refs[:2]
    k_refs = refs[2:2 + n_seg]
    vt_refs = refs[2 + n_seg:2 + 2 * n_seg]
    o_ref = refs[2 + 2 * n_seg]
    s_bufs = refs[3 + 2 * n_seg:5 + 2 * n_seg]
    m_bufs = refs[5 + 2 * n_seg:7 + 2 * n_seg]
    n_pair = q_ref.shape[1]
    n_q = q_ref.shape[2] // tq
    n_items = n_pair * n_q
    lanes = [slice(j * HEAD_SLOT, (j + 1) * HEAD_SLOT) for j in range(2)]
    rows = [slice(j * MLA_V, (j + 1) * MLA_V) for j in range(2)]

    blocks, off = [], 0
    for si in range(n_seg):
        ts = k_refs[si].shape[2]
        kb = min(_ATTN_KEY_BLOCK, ts)
        blocks += [(si, b0, kb, off + b0) for b0 in range(0, ts, kb)]
        off += ts

    def locate(item):
        if isinstance(item, int):
            return item // n_q, (item % n_q) * tq
        pair = lax.div(item, jnp.int32(n_q))
        return pair, pl.multiple_of((item - pair * n_q) * tq, tq)

    def phase(nxt, cur):
        if nxt is not None:
            item_n, s_n, m_n = nxt
            pn, rn = locate(item_n)
            qn = [q_ref[0, pn, pl.ds(rn, tq), lanes[j]] for j in range(2)]
            mx = [None, None]
        if cur is not None:
            item_c, s_c, m_c = cur
            pc, rc = locate(item_c)
            mc = [m_c[j] for j in range(2)]
            acc, l = [None, None], [None, None]
        for si, b0, kb, o0 in blocks:
            for j in range(2):
                if nxt is not None:
                    s = _dot_nt(k_refs[si][0, pn, b0:b0 + kb, lanes[j]], qn[j])
                    s_n[j, o0:o0 + kb, :] = s
                    mb = jnp.max(s, axis=0, keepdims=True)
                    mx[j] = mb if mx[j] is None else jnp.maximum(mx[j], mb)
                if cur is not None:
                    p = jnp.exp2(s_c[j, o0:o0 + kb, :] - mc[j])
                    ls = jnp.sum(p, axis=0, keepdims=True)
                    pv = _dot(vt_refs[si][0, pc, rows[j], b0:b0 + kb], p.astype(BF16))
                    l[j] = ls if l[j] is None else l[j] + ls
                    acc[j] = pv if acc[j] is None else acc[j] + pv
        if nxt is not None:
            for j in range(2):
                m_n[j] = mx[j]
        if cur is not None:
            outs = [acc[j] * (1.0 / l[j]) for j in range(2)]
            gate = g_ref[0, pc, pl.ds(rc, tq), :].astype(F32)
            o_ref[0, pc, pl.ds(rc, tq), :] = (jnp.concatenate(outs, axis=0).T * gate).astype(o_ref.dtype)

    buf = [(s_bufs[0], m_bufs[0]), (s_bufs[1], m_bufs[1])]
    phase((0, *buf[0]), None)
    assert n_items % 2 == 0

    def body(i, carry):
        n = 2 * i
        phase((n + 1, *buf[1]), (n, *buf[0]))
        phase((jnp.minimum(n + 2, n_items - 1), *buf[0]), (n + 1, *buf[1]))
        return carry

    lax.fori_loop(0, n_items // 2, body, 0)


def _attention(q, gate, ks, vts, tq):
    b, n_pair, t, pair = q.shape
    n_seg = len(ks)
    n_keys = sum(k.shape[2] for k in ks)
    whole = lambda a: pl.BlockSpec((1,) + a.shape[1:], lambda i: (i, 0, 0, 0))
    score_buf = pltpu.VMEM((2, n_keys, tq), F32)
    max_buf = pltpu.VMEM((2, 1, tq), F32)
    return pl.pallas_call(
        functools.partial(_attn_kernel, n_seg=n_seg, tq=tq),
        grid=(b,),
        in_specs=[whole(q), whole(gate)] + [whole(k) for k in ks] + [whole(v) for v in vts],
        out_specs=pl.BlockSpec((1, n_pair, t, 2 * MLA_V), lambda i: (i, 0, 0, 0)),
        out_shape=jax.ShapeDtypeStruct((b, n_pair, t, 2 * MLA_V), BF16),
        scratch_shapes=[score_buf, score_buf, max_buf, max_buf],
        compiler_params=_params(("arbitrary",)),
        name="attention",
    )(q, gate, *ks, *vts)


_RET_GROUP = 8


def _log_sigmoid(x):
    return jnp.minimum(x, 0.0) - jnp.log1p(jnp.exp(-jnp.abs(x)))


def _ret_kernel(dec_ref, *refs, chunk, group, layer, states_only):
    if states_only:
        k_ref, vt_ref, s0_ref, sfin_ref, sf_scr, sb_scr, kd_scr, qd_scr, dt_scr, cc_scr = refs
    else:
        (q_ref, k_ref, vt_ref, s0_ref, gng_ref, gnb_ref, gate_ref, o_ref, sfin_ref, sf_scr, sb_scr, kd_scr, qd_scr,
         dt_scr, cc_scr) = refs
    c_len = chunk
    t = k_ref.shape[1]
    n_c = t // c_len
    hp = pl.program_id(0)

    @pl.when(pl.program_id(1) == 0)
    def _():
        r = lax.broadcasted_iota(jnp.int32, (c_len, LANES), 0).astype(F32)
        lane = lax.broadcasted_iota(jnp.int32, (c_len, LANES), 1)
        km = lax.broadcasted_iota(jnp.int32, (c_len, c_len), 0)
        qn = lax.broadcasted_iota(jnp.int32, (c_len, c_len), 1)
        diff = (qn - km).astype(F32)
        for j in range(2):
            h = hp * 2 + j
            lg_f = _log_sigmoid(jnp.full((c_len, LANES), dec_ref[layer, 0, h], F32))
            lg_b = _log_sigmoid(jnp.full((c_len, LANES), dec_ref[layer, 1, h], F32))
            own = ((lane >= j * RET_DK) & (lane < (j + 1) * RET_DK)).astype(F32)
            lg_f2 = _log_sigmoid(jnp.full((c_len, c_len), dec_ref[layer, 0, h], F32))
            lg_b2 = _log_sigmoid(jnp.full((c_len, c_len), dec_ref[layer, 1, h], F32))
            kd_scr[j] = jnp.concatenate([jnp.exp(lg_f * (c_len - 1.0 - r)) * own, jnp.exp(lg_b * r) * own, own], axis=1)
            qd_scr[j] = jnp.concatenate([jnp.exp(lg_f * (r + 1.0)), jnp.exp(lg_b * (c_len - r))], axis=1)
            dt_scr[j] = jnp.exp(jnp.where(diff >= 0, lg_f2 * diff, -lg_b2 * diff))
            cc_scr[j] = jnp.concatenate(
                [jnp.exp(_log_sigmoid(jnp.full((RET_DV, LANES), dec_ref[layer, 0, h], F32)) * c_len),
                 jnp.exp(_log_sigmoid(jnp.full((RET_DV, LANES), dec_ref[layer, 1, h], F32)) * c_len)], axis=1)

    heads = []
    for j in range(2):
        heads.append(dict(
            kdf=kd_scr[j, :, 0:LANES],
            kdb=kd_scr[j, :, LANES:2 * LANES],
            own=kd_scr[j, :, 2 * LANES:3 * LANES],
            qd=qd_scr[j],
            dt=dt_scr[j],
            cf=cc_scr[j, :, 0:LANES],
            cb=cc_scr[j, :, LANES:2 * LANES],
            rows=slice(j * RET_DV, (j + 1) * RET_DV),
        ))

    def local(i, carry):
        for g in range(group):
            c = i * group + g
            c0 = pl.multiple_of(c * c_len, c_len)
            kc = k_ref[0, pl.ds(c0, c_len), :]
            for j, hd in enumerate(heads):
                vt = vt_ref[0, hd["rows"], pl.ds(c0, c_len)]
                sf_scr[j, c] = _dot(vt, (kc * hd["kdf"]).astype(BF16))
                sb_scr[j, c] = _dot(vt, (kc * hd["kdb"]).astype(BF16))
        return carry

    lax.fori_loop(0, n_c // group, local, 0)

    for j, hd in enumerate(heads):
        sf = s0_ref[0, 0, 2 * j]
        for c in range(n_c):
            a = sf_scr[j, c]
            sf_scr[j, c] = sf
            sf = hd["cf"] * sf + a
        sfin_ref[0, 0, 2 * j] = sf
        sb = s0_ref[0, 0, 2 * j + 1]
        for c in reversed(range(n_c)):
            a = sb_scr[j, c]
            sb_scr[j, c] = sb
            sb = hd["cb"] * sb + a
        sfin_ref[0, 0, 2 * j + 1] = sb

    if states_only:
        return

    def outputs(i, carry):
        items = []
        for g in range(group):
            c = i * group + g
            c0 = pl.multiple_of(c * c_len, c_len)
            qc = q_ref[0, pl.ds(c0, c_len), :]
            kc = k_ref[0, pl.ds(c0, c_len), :]
            items.append(dict(c=c, c0=c0, kc=kc, qb=qc.astype(BF16), qq=jnp.concatenate([qc, qc], axis=1)))
        work = [(it, j, hd) for it in items for j, hd in enumerate(heads)]
        st = [_dot_nt((it["kc"] * hd["own"]).astype(BF16), it["qb"]) for it, j, hd in work]
        cross = [_dot_nt(jnp.concatenate([sf_scr[j, it["c"]], sb_scr[j, it["c"]]], axis=1).astype(BF16),
                         (it["qq"] * hd["qd"]).astype(BF16)) for it, j, hd in work]
        o = [_dot(vt_ref[0, hd["rows"], pl.ds(it["c0"], c_len)], (s * hd["dt"]).astype(BF16)) + x
             for (it, j, hd), s, x in zip(work, st, cross)]
        ys = []
        for v in o:
            mu = jnp.mean(v, axis=0, keepdims=True)
            vc = v - mu
            var = jnp.mean(vc * vc, axis=0, keepdims=True)
            ys.append(vc * lax.rsqrt(var + EPS))
        for g, it in enumerate(items):
            y = jnp.concatenate(ys[2 * g:2 * g + 2], axis=0).T
            gate = gate_ref[0, pl.ds(it["c0"], c_len), :].astype(F32)
            o_ref[0, pl.ds(it["c0"], c_len), :] = ((y * gng_ref[...] + gnb_ref[...]) * gate).astype(o_ref.dtype)
        return carry

    lax.fori_loop(0, n_c // group, outputs, 0)


def _retention(dec, rq, rk, rvt, s0, gn_g, gn_b, gate, layer, chunk):
    b, t, _ = rk.shape
    states_only = rq is None
    n_pair = RET_HEADS // 2
    pair = 2 * RET_DK
    n_c = t // chunk
    group = math.gcd(n_c, _RET_GROUP)
    state_buf = pltpu.VMEM((2, n_c, RET_DV, pair), F32)
    tables = [pltpu.VMEM((2, chunk, 3 * LANES), F32), pltpu.VMEM((2, chunk, 2 * LANES), F32),
              pltpu.VMEM((2, chunk, chunk), F32), pltpu.VMEM((2, RET_DV, 2 * LANES), F32)]
    seq = pl.BlockSpec((1, t, pair), lambda h, i: (i, 0, h))
    state = pl.BlockSpec((1, 1, 4, RET_DV, pair), lambda h, i: (i, h, 0, 0, 0))
    affine = pl.BlockSpec((None, 1, pair), lambda h, i: (layer, 0, h))
    state_shape = jax.ShapeDtypeStruct((b, n_pair, 4, RET_DV, pair), F32)
    smem = pl.BlockSpec(memory_space=pltpu.SMEM)
    vt_spec = pl.BlockSpec((1, 2 * RET_DV, t), lambda h, i: (i, h, 0))
    if states_only:
        in_specs, args = [smem, seq, vt_spec, state], (dec, rk, rvt, s0)
        out_specs, out_shape = state, state_shape
    else:
        in_specs = [smem, seq, seq, vt_spec, state, affine, affine, seq]
        args = (dec, rq, rk, rvt, s0, gn_g, gn_b, gate)
        out_specs, out_shape = (seq, state), (jax.ShapeDtypeStruct((b, t, RET_WIDTH), BF16), state_shape)
    res = pl.pallas_call(
        functools.partial(_ret_kernel, chunk=chunk, group=group, layer=layer, states_only=states_only),
        grid=(n_pair, b),
        in_specs=in_specs,
        out_specs=out_specs,
        out_shape=out_shape,
        scratch_shapes=[state_buf, state_buf] + tables,
        compiler_params=_params(("arbitrary", "arbitrary")),
        name="retention",
    )(*args)
    return (None, res) if states_only else res


def _final_kernel(x_ref, mod_ref, om_ref, or_ref, oc_ref, wo_ref, lng_ref, lnb_ref, o_ref, *, alpha):
    o_cat = jnp.concatenate([om_ref[0, p] for p in range(om_ref.shape[1])] + [or_ref[0], oc_ref[0]], axis=1)
    y = _dot(o_cat, wo_ref[...])

    v = alpha * x_ref[0] + mod_ref[0, 2:3, :] * y
    mu = jnp.mean(v, axis=-1, keepdims=True)
    vc = v - mu
    var = jnp.mean(vc * vc, axis=-1, keepdims=True)
    o_ref[0] = vc * lax.rsqrt(var + EPS) * lng_ref[...] + lnb_ref[...]


def _final(x, mod_all, mod_row, o_mla, o_ret, o_conv, w_out, ln_g, ln_b, layer, tm, alpha):
    b, t, d = x.shape
    const = lambda a: _layer_spec(a, layer)
    row = lambda n: pl.BlockSpec((1, tm, n), lambda i, j: (i, j, 0))
    return pl.pallas_call(
        functools.partial(_final_kernel, alpha=alpha),
        grid=(b, t // tm),
        in_specs=[row(d),
                  _mod_spec(mod_all, layer, mod_row, 0),
                  pl.BlockSpec((1, o_mla.shape[1], tm, o_mla.shape[3]), lambda i, j: (i, 0, j, 0)),
                  row(RET_WIDTH), row(CONV_WIDTH),
                  const(w_out), const(ln_g), const(ln_b)],
        out_specs=row(d),
        out_shape=jax.ShapeDtypeStruct((b, t, d), F32),
        compiler_params=_params(("arbitrary", "arbitrary")),
        name="final",
    )(x, mod_all, o_mla, o_ret, o_conv, w_out, ln_g, ln_b)


def _tile(n, pref):
    return pref if n % pref == 0 else n


def kernel(x, c, ctx, c_ctx, w_mod, b_mod, w_in, mla_q_norm, w_uq, mla_kv_norm, w_ukv, ret_decay_fwd, ret_decay_bwd,
           ret_gn_g, ret_gn_b, conv_dw, conv_dw_b, conv_ln_g, conv_ln_b, conv_pw, conv_pw_b, w_out, ln_g, ln_b):
    depth = w_mod.shape[0]
    b, t, d = x.shape
    t_ctx = ctx.shape[1]
    alpha = (2 * depth) ** 0.25
    n_mod_rows = 16
    cc = jnp.concatenate([c, c_ctx[None, :], jnp.zeros((n_mod_rows - b - 1, d), c.dtype)], axis=0)
    mod_all = _modulation(cc, w_mod, b_mod).reshape(depth, n_mod_rows, 3, d)
    tab_x = _rope_table_block(t, True)
    tab_c = _rope_table_block(t_ctx, False)
    s_zero = jnp.zeros((b, RET_HEADS // 2, 4, RET_DV, 2 * RET_DK), F32)

    row3 = lambda a: a.reshape(depth, 1, -1)
    weights = _prep_weights(w_in, w_uq, w_ukv)
    gq, gkv = row3(mla_q_norm), row3(mla_kv_norm)
    dec = jnp.stack([ret_decay_fwd, ret_decay_bwd], axis=1).astype(F32)
    gn_g, gn_b = row3(ret_gn_g), row3(ret_gn_b)
    dw = jnp.pad(conv_dw, ((0, 0), (0, 1), (0, 0)))
    conv = (dw, row3(conv_dw_b), row3(conv_ln_g), row3(conv_ln_b), conv_pw.astype(BF16), row3(conv_pw_b))
    tail = (w_out.astype(BF16), row3(ln_g), row3(ln_b))
    tm_x, tm_c = _tile(t, 1024), _tile(t_ctx, 256)
    tf_x, tf_c = _tile(t, 1024), _tile(t_ctx, 256)
    tq_x, tq_c = _tile(t, 256), _tile(t_ctx, 256)
    ch_x, ch_c = _tile(t, 256), _tile(t_ctx, 256)

    hc = ctx
    for l in range(depth):
        need_ctx = l < depth - 1
        pc = _proj(hc, mod_all, b, tab_c, weights, gq, gkv, conv, l, tm_c, keys_only=not need_ctx)
        px = _proj(x, mod_all, None, tab_x, weights, gq, gkv, conv, l, tm_x)

        oc_ret, s_ctx = _retention(dec, pc.get("rq"), pc["rk"], pc["rvt"], s_zero, gn_g, gn_b, pc.get("gate_ret"),
                                   l, ch_c)
        o_ret, _ = _retention(dec, px["rq"], px["rk"], px["rvt"], s_ctx, gn_g, gn_b, px["gate_ret"], l, ch_x)
        o_mla = _attention(px["q"], px["gate_mla"], [pc["k"], px["k"]], [pc["vt"], px["vt"]], tq_x)
        x_new = _final(x, mod_all, None, o_mla, o_ret, px["conv"], *tail, layer=l, tm=tf_x, alpha=alpha)
        if need_ctx:
            oc_mla = _attention(pc["q"], pc["gate_mla"], [pc["k"]], [pc["vt"]], tq_c)
            hc = _final(hc, mod_all, b, oc_mla, oc_ret, pc["conv"], *tail, layer=l, tm=tf_c, alpha=alpha)
        x = x_new
    return x
```

```python
import functools
import math

import jax
import jax.numpy as jnp
import numpy as np
from jax import lax
from jax.experimental import pallas as pl
from jax.experimental.pallas import tpu as pltpu

GRID_W = 64
MLA_HEADS = 8
MLA_NOPE = 64
MLA_ROPE = 32
MLA_V = 64
MLA_QK = MLA_NOPE + MLA_ROPE
MLA_WIDTH = MLA_HEADS * MLA_V
Q_LORA = 256
KV_LORA = 128
RET_HEADS = 4
RET_DK = 64
RET_DV = 64
RET_WIDTH = RET_HEADS * RET_DV
CONV_WIDTH = 256
CONV_K = 31
ROPE_BASE = 10000.0
EPS = 1e-5

LANES = 128
HEAD_SLOT = LANES
VMEM_LIMIT_BYTES = 56 * 1024 * 1024
LOG2E = 1.4426950408889634

F32 = jnp.float32
BF16 = jnp.bfloat16

_NT = (((1,), (1,)), ((), ()))


def _dot(a, b):
    return jnp.dot(a, b, preferred_element_type=F32)


def _dot_nt(a, b):
    return lax.dot_general(a, b, _NT, preferred_element_type=F32)


def _params(semantics):
    return pltpu.CompilerParams(dimension_semantics=semantics, vmem_limit_bytes=VMEM_LIMIT_BYTES)


def _rot_cols(w, unit):
    xp = np if isinstance(w, np.ndarray) else jnp
    q = unit // 4
    w5 = w.reshape(w.shape[:-1] + (w.shape[-1] // unit, 2, 2, q))
    return xp.stack([-w5[..., 1, :], w5[..., 0, :]], axis=-2).reshape(w.shape)


def _rope_tables(length, unit, reps):
    d2 = unit // 2
    t = np.arange(length, dtype=np.int32)
    inv = np.float32(ROPE_BASE) ** (-np.arange(0, d2, 2, dtype=np.float32) / np.float32(d2))

    def half(pos):
        ang = pos.astype(np.float32)[:, None] * inv[None, :]
        return np.concatenate([np.cos(ang)] * 2, axis=-1), np.concatenate([np.sin(ang)] * 2, axis=-1)

    cr, sr = half(t // GRID_W)
    cc, sc = half(t % GRID_W)
    cos = np.concatenate([cr, cc], axis=-1).astype(np.float32)
    sin = np.concatenate([sr, sc], axis=-1).astype(np.float32)
    return np.tile(cos, (1, reps)), np.tile(sin, (1, reps))


def _rope_table_block(length, rotate):
    if rotate:
        cos_r, sin_r = _rope_tables(length, RET_DK, RET_HEADS)
        cos_m, sin_m = _rope_tables(length, MLA_ROPE, MLA_HEADS)
    else:
        cos_r = np.ones((length, RET_WIDTH), np.float32)
        sin_r = np.zeros((length, RET_WIDTH), np.float32)
        cos_m = np.ones((length, MLA_HEADS * MLA_ROPE), np.float32)
        sin_m = np.zeros((length, MLA_HEADS * MLA_ROPE), np.float32)
    kslot = np.concatenate([cos_m[:, :MLA_ROPE], sin_m[:, :MLA_ROPE],
                            np.zeros((length, LANES - 2 * MLA_ROPE), np.float32)], axis=-1)
    return jnp.asarray(np.concatenate([cos_r, sin_r, cos_m, sin_m, kslot], axis=-1))


def _placement_matrices():
    q_place = np.zeros((MLA_HEADS, MLA_ROPE, MLA_HEADS, HEAD_SLOT), np.float32)
    k_place = np.zeros((LANES, MLA_HEADS, HEAD_SLOT), np.float32)
    for d in range(MLA_ROPE):
        for h in range(MLA_HEADS):
            q_place[h, d, h, MLA_NOPE + d] = 1.0
            k_place[d, h, MLA_NOPE + d] = 1.0
            k_place[MLA_ROPE + d, h, MLA_NOPE + d] = 1.0
    return (q_place.reshape(MLA_HEADS * MLA_ROPE, MLA_HEADS * HEAD_SLOT),
            k_place.reshape(LANES, MLA_HEADS * HEAD_SLOT))


_IN_SIZES = dict(q=Q_LORA, kv=KV_LORA, kr=MLA_ROPE, g_mla=MLA_WIDTH, rq=RET_HEADS * RET_DK, rk=RET_HEADS * RET_DK,
                 rv=RET_WIDTH, g_ret=RET_WIDTH, glu=2 * CONV_WIDTH, g_conv=CONV_WIDTH)


def _offsets(sizes):
    out, at = {}, 0
    for name, n in sizes.items():
        out[name] = (at, at + n)
        at += n
    return out


_IN_COLS = _offsets(_IN_SIZES)
_MAIN_COLS = _offsets({k: v for k, v in _IN_SIZES.items() if k != "kr"})
_PREP_ROWS = 256


def _relayout_kernel(w_ref, main_ref, rvt_ref):
    w = w_ref[0]
    lo, hi = _IN_COLS["kr"]
    main_ref[0] = jnp.concatenate([w[:, :lo], w[:, hi:]], axis=1).astype(BF16)
    a, b = _IN_COLS["rv"]
    rvt_ref[0] = w[:, a:b].T.astype(BF16)


def _relayout_w_in(w_in):
    depth, d, n = w_in.shape
    n_main = n - MLA_ROPE
    return pl.pallas_call(
        _relayout_kernel,
        grid=(depth, d // _PREP_ROWS),
        in_specs=[pl.BlockSpec((1, _PREP_ROWS, n), lambda l, i: (l, i, 0))],
        out_specs=(pl.BlockSpec((1, _PREP_ROWS, n_main), lambda l, i: (l, i, 0)),
                   pl.BlockSpec((1, RET_WIDTH, _PREP_ROWS), lambda l, i: (l, 0, i))),
        out_shape=(jax.ShapeDtypeStruct((depth, d, n_main), BF16), jax.ShapeDtypeStruct((depth, RET_WIDTH, d), BF16)),
        compiler_params=_params(("arbitrary", "arbitrary")),
        name="relayout",
    )(w_in)


def _prep_weights(w_in, w_uq, w_ukv):
    depth, d, _ = w_in.shape
    w_main, w_rvt = _relayout_w_in(w_in)
    wkr = w_in[..., _IN_COLS["kr"][0]:_IN_COLS["kr"][1]]
    w_ks = jnp.concatenate([wkr, _rot_cols(wkr, MLA_ROPE), jnp.zeros((depth, d, LANES - 2 * MLA_ROPE), w_in.dtype)],
                           axis=-1).astype(BF16)

    uq = w_uq.reshape(depth, Q_LORA, MLA_HEADS, MLA_QK)
    pad = ((0, 0), (0, 0), (0, 0), (0, HEAD_SLOT - MLA_NOPE))
    uq_nope = jnp.pad(uq[..., :MLA_NOPE], pad).reshape(depth, Q_LORA, -1)
    uq_rope = uq[..., MLA_NOPE:].reshape(depth, Q_LORA, MLA_HEADS * MLA_ROPE)
    w_q2 = jnp.concatenate([uq_rope, _rot_cols(uq_rope, MLA_ROPE)], axis=-1).astype(BF16)
    q_place, k_place = (jnp.broadcast_to(jnp.asarray(m, BF16), (depth,) + m.shape) for m in _placement_matrices())
    w_qcomb = jnp.concatenate([uq_nope.astype(BF16), q_place], axis=1)

    ukv = w_ukv.reshape(depth, KV_LORA, MLA_HEADS, MLA_NOPE + MLA_V)
    uk = jnp.pad(ukv[..., :MLA_NOPE], pad).reshape(depth, KV_LORA, -1)
    w_kcomb = jnp.concatenate([uk.astype(BF16), k_place], axis=1)
    w_uvt = jnp.swapaxes(ukv[..., MLA_NOPE:].reshape(depth, KV_LORA, MLA_WIDTH), 1, 2).astype(BF16)
    return w_main, w_ks, w_rvt, w_q2, w_qcomb, w_kcomb, w_uvt


def _mod_kernel(c_ref, w_ref, b_ref, o_ref):
    c = c_ref[...]
    a = (c * jax.nn.sigmoid(c)).astype(BF16)
    o_ref[0] = _dot(a, w_ref[0].astype(BF16)) + b_ref[0]


def _modulation(cc, w_mod, b_mod):
    depth, d, d3 = w_mod.shape
    nblk = d3 // d
    return pl.pallas_call(
        _mod_kernel,
        grid=(depth, nblk),
        in_specs=[pl.BlockSpec(cc.shape, lambda l, j: (0, 0)),
                  pl.BlockSpec((1, d, d), lambda l, j: (l, 0, j)),
                  pl.BlockSpec((1, 1, d), lambda l, j: (l, 0, j))],
        out_specs=pl.BlockSpec((1, cc.shape[0], d), lambda l, j: (l, 0, j)),
        out_shape=jax.ShapeDtypeStruct((depth, cc.shape[0], d3), F32),
        compiler_params=_params(("arbitrary", "arbitrary")),
        name="mod",
    )(cc, w_mod, b_mod.reshape(depth, 1, d3))


_T_CR, _T_SR, _T_CM, _T_SM, _T_KS, _T_END = 0, 256, 512, 768, 1024, 1152


_HALO = 16
_CONV_ROWS = 64


def _rms(x, g):
    return x * lax.rsqrt(jnp.mean(x * x, axis=-1, keepdims=True) + EPS) * g


def _proj_kernel(x_ref, xp_ref, xn_ref, mod_ref, tab_ref, rot_ref, w_ref, wks_ref, wrvt_ref, gq_ref, gkv_ref, wq2_ref,
                 wqc_ref, wkc_ref, wuvt_ref, dw_ref, dwb_ref, clg_ref, clb_ref, pw_ref, pwb_ref, *out_refs, keys_only):
    if keys_only:
        k_ref, vt_ref, rk_ref, rvt_ref = out_refs
    else:
        q_ref, k_ref, vt_ref, rq_ref, rk_ref, rvt_ref, gm_ref, gr_ref, oc_ref, upad, ushift = out_refs
    tm = x_ref.shape[1]
    shift = mod_ref[0, 0:1, :]
    scale = mod_ref[0, 1:2, :]
    pair = 2 * HEAD_SLOT

    def modulate(x):
        mu = jnp.mean(x, axis=-1, keepdims=True)
        xc = x - mu
        var = jnp.mean(xc * xc, axis=-1, keepdims=True)
        return (xc * lax.rsqrt(var + EPS) * (1.0 + scale) + shift).astype(BF16)

    u = modulate(x_ref[0])

    def seg(name):
        lo, hi = _MAIN_COLS[name]
        return _dot(u, w_ref[:, lo:hi])

    def silu(v):
        return v * jax.nn.sigmoid(v)

    def store_pairs(ref, val):
        for p in range(MLA_HEADS // 2):
            ref[0, p] = val[:, p * pair:(p + 1) * pair]

    def keys(pkv, pks):
        nkv = _rms(pkv, gkv_ref[...]).astype(BF16)
        ks = pks * tab_ref[:, _T_KS:_T_END]
        store_pairs(k_ref, _dot(jnp.concatenate([nkv, ks.astype(BF16)], axis=1), wkc_ref[...]).astype(BF16))
        vt_ref[0] = _dot_nt(wuvt_ref[...], nkv).astype(BF16)

    def ret_rope(raw):
        rot = _dot(raw.astype(BF16), rot_ref[...])
        return raw * tab_ref[:, _T_CR:_T_SR] + rot * tab_ref[:, _T_SR:_T_CM]

    def ret_keys(prk):
        rk_ref[0] = ret_rope(prk) * (RET_DK ** -0.5)
        rvt_ref[0] = _dot_nt(wrvt_ref[...], u).astype(BF16)

    if keys_only:
        keys(seg("kv"), _dot(u, wks_ref[...]))
        ret_keys(seg("rk"))
        return

    i = pl.program_id(0)
    u_ext = jnp.concatenate([modulate(xp_ref[0]), u, modulate(xn_ref[0])], axis=0)
    glu = _dot(u_ext, w_ref[:, _MAIN_COLS["glu"][0]:_MAIN_COLS["glu"][1]])
    uc = glu[:, :CONV_WIDTH] * jax.nn.sigmoid(glu[:, CONV_WIDTH:])
    row = lax.broadcasted_iota(jnp.int32, uc.shape, 0)
    inside = ((row >= _HALO) | (i > 0)) & ((row < _HALO + tm) | (i < pl.num_programs(0) - 1))
    upad[...] = jnp.where(inside, uc, 0.0)
    span = tm + 2 * _HALO - 8
    for r in range(1, 8):
        ushift[r - 1] = upad[r:r + span, :]
    base = _HALO - CONV_K // 2
    conv_blocks = []

    def conv_rows(n_blocks):
        for _ in range(n_blocks):
            rb = len(conv_blocks) * _CONV_ROWS
            acc = jnp.zeros((_CONV_ROWS, CONV_WIDTH), F32) + dwb_ref[...]
            for j in range(CONV_K):
                a, r = divmod(base + j, 8)
                src = upad if r == 0 else ushift.at[r - 1]
                acc = acc + src[8 * a + rb:8 * a + rb + _CONV_ROWS, :] * dw_ref[j:j + 1, :]
            mu = jnp.mean(acc, axis=-1, keepdims=True)
            ac = acc - mu
            var = jnp.mean(ac * ac, axis=-1, keepdims=True)
            yc = ac * lax.rsqrt(var + EPS) * clg_ref[...] + clb_ref[...]
            conv_blocks.append((yc * jax.nn.sigmoid(yc)).astype(BF16))

    n_conv = tm // _CONV_ROWS
    per_stage = -(-n_conv // 4)

    pq, pkv, pks = seg("q"), seg("kv"), _dot(u, wks_ref[...])
    prq, prk = seg("rq"), seg("rk")
    conv_rows(min(per_stage, n_conv - len(conv_blocks)))

    g_mla = silu(seg("g_mla"))
    for p in range(MLA_HEADS // 2):
        gm_ref[0, p] = g_mla[:, p * LANES:(p + 1) * LANES].astype(gm_ref.dtype)
    gr_ref[0] = silu(seg("g_ret")).astype(gr_ref.dtype)
    g_conv = silu(seg("g_conv"))
    conv_rows(min(per_stage, n_conv - len(conv_blocks)))

    nq = _rms(pq, gq_ref[...]).astype(BF16)
    qc = _dot(nq, wq2_ref[...])
    keys(pkv, pks)
    conv_rows(min(per_stage, n_conv - len(conv_blocks)))

    rq_ref[0] = ret_rope(prq)
    ret_keys(prk)

    nr = MLA_HEADS * MLA_ROPE
    q_rope = qc[:, :nr] * tab_ref[:, _T_CM:_T_SM] + qc[:, nr:] * tab_ref[:, _T_SM:_T_KS]
    q = _dot(jnp.concatenate([nq, q_rope.astype(BF16)], axis=1), wqc_ref[...])
    store_pairs(q_ref, (q * (MLA_QK ** -0.5 * LOG2E)).astype(BF16))
    conv_rows(n_conv - len(conv_blocks))

    o_conv = _dot(jnp.concatenate(conv_blocks, axis=0), pw_ref[...]) + pwb_ref[...]
    oc_ref[0] = (o_conv * g_conv).astype(oc_ref.dtype)


def _layer_spec(a, layer):
    return pl.BlockSpec((None,) + a.shape[1:], lambda *_: (layer,) + (0,) * (a.ndim - 1),
                        pipeline_mode=pl.Buffered(1))


def _mod_spec(mod_all, layer, shared_row, batch_axis):
    def index(*grid):
        return (layer, grid[batch_axis] if shared_row is None else shared_row, 0, 0)
    return pl.BlockSpec((None, 1) + mod_all.shape[2:], index)


def _proj(x, mod_all, mod_row, tab, weights, gq, gkv, conv, layer, tm, keys_only=False):
    b, t, d = x.shape
    w_main, w_ks, w_rvt, w_q2, w_qcomb, w_kcomb, w_uvt = weights
    const = lambda a: _layer_spec(a, layer)
    row = lambda n: pl.BlockSpec((1, tm, n), lambda i, j: (j, i, 0))
    col = lambda n: pl.BlockSpec((1, n, tm), lambda i, j: (j, 0, i))
    n_pair, pair = MLA_HEADS // 2, 2 * HEAD_SLOT
    pairs = pl.BlockSpec((1, n_pair, tm, pair), lambda i, j: (j, 0, i, 0))
    per_tile = tm // _HALO
    last = t // _HALO - 1
    halo_prev = pl.BlockSpec((1, _HALO, d), lambda i, j: (j, jnp.maximum(i * per_tile - 1, 0), 0))
    halo_next = pl.BlockSpec((1, _HALO, d), lambda i, j: (j, jnp.minimum((i + 1) * per_tile, last), 0))
    rot = jnp.asarray(_rot_cols(np.eye(RET_WIDTH, dtype=np.float32), RET_DK), BF16)
    sds = jax.ShapeDtypeStruct
    outs = dict(q=(sds((b, n_pair, t, pair), BF16), pairs),
                k=(sds((b, n_pair, t, pair), BF16), pairs),
                vt=(sds((b, MLA_WIDTH, t), BF16), col(MLA_WIDTH)),
                rq=(sds((b, t, RET_WIDTH), F32), row(RET_WIDTH)),
                rk=(sds((b, t, RET_WIDTH), F32), row(RET_WIDTH)),
                rvt=(sds((b, RET_WIDTH, t), BF16), col(RET_WIDTH)),
                gate_mla=(sds((b, n_pair, t, 2 * MLA_V), BF16),
                          pl.BlockSpec((1, n_pair, tm, 2 * MLA_V), lambda i, j: (j, 0, i, 0))),
                gate_ret=(sds((b, t, RET_WIDTH), BF16), row(RET_WIDTH)),
                conv=(sds((b, t, CONV_WIDTH), BF16), row(CONV_WIDTH)))
    names = ("k", "vt", "rk", "rvt") if keys_only else tuple(outs)
    scratch = [] if keys_only else [pltpu.VMEM((tm + 2 * _HALO, CONV_WIDTH), F32),
                                    pltpu.VMEM((7, tm + 2 * _HALO - 8, CONV_WIDTH), F32)]
    res = pl.pallas_call(
        functools.partial(_proj_kernel, keys_only=keys_only),
        grid=(t // tm, b),
        in_specs=[row(d), halo_prev, halo_next,
                  _mod_spec(mod_all, layer, mod_row, 1),
                  pl.BlockSpec((tm, _T_END), lambda i, j: (i, 0)),
                  pl.BlockSpec(rot.shape, lambda i, j: (0, 0)),
                  const(w_main), const(w_ks), const(w_rvt), const(gq), const(gkv), const(w_q2), const(w_qcomb),
                  const(w_kcomb),
                  const(w_uvt)] + [const(a) for a in conv],
        out_specs=tuple(outs[n][1] for n in names),
        out_shape=tuple(outs[n][0] for n in names),
        scratch_shapes=scratch,
        compiler_params=_params(("arbitrary", "arbitrary")),
        name="proj",
    )(x, x, x, mod_all, tab, rot, w_main, w_ks, w_rvt, gq, gkv, w_q2, w_qcomb, w_kcomb, w_uvt, *conv)
    res = dict(zip(names, res))
    res["vt"] = res["vt"].reshape(b, n_pair, 2 * MLA_V, t)
    return res


_ATTN_KEY_BLOCK = 512


def _attn_kernel(*refs, n_seg, tq):
    q_ref, g_ref = refs[:2]
    k_refs = refs[2:2 + n_seg]
    vt_refs = refs[2 + n_seg:2 + 2 * n_seg]
    o_ref = refs[2 + 2 * n_seg]
    s_bufs = refs[3 + 2 * n_seg:5 + 2 * n_seg]
    m_bufs = refs[5 + 2 * n_seg:7 + 2 * n_seg]
    n_pair = q_ref.shape[1]
    n_q = q_ref.shape[2] // tq
    n_items = n_pair * n_q
    lanes = [slice(j * HEAD_SLOT, (j + 1) * HEAD_SLOT) for j in range(2)]
    rows = [slice(j * MLA_V, (j + 1) * MLA_V) for j in range(2)]

    blocks, off = [], 0
    for si in range(n_seg):
        ts = k_refs[si].shape[2]
        kb = min(_ATTN_KEY_BLOCK, ts)
        blocks += [(si, b0, kb, off + b0) for b0 in range(0, ts, kb)]
        off += ts

    def locate(item):
        if isinstance(item, int):
            return item // n_q, (item % n_q) * tq
        pair = lax.div(item, jnp.int32(n_q))
        return pair, pl.multiple_of((item - pair * n_q) * tq, tq)

    def phase(nxt, cur):
        if nxt is not None:
            item_n, s_n, m_n = nxt
            pn, rn = locate(item_n)
            qn = [q_ref[0, pn, pl.ds(rn, tq), lanes[j]] for j in range(2)]
            mx = [None, None]
        if cur is not None:
            item_c, s_c, m_c = cur
            pc, rc = locate(item_c)
            mc = [m_c[j] for j in range(2)]
            acc, l = [None, None], [None, None]
        for si, b0, kb, o0 in blocks:
            for j in range(2):
                if nxt is not None:
                    s = _dot_nt(k_refs[si][0, pn, b0:b0 + kb, lanes[j]], qn[j])
                    s_n[j, o0:o0 + kb, :] = s
                    mb = jnp.max(s, axis=0, keepdims=True)
                    mx[j] = mb if mx[j] is None else jnp.maximum(mx[j], mb)
                if cur is not None:
                    p = jnp.exp2(s_c[j, o0:o0 + kb, :] - mc[j])
                    ls = jnp.sum(p, axis=0, keepdims=True)
                    pv = _dot(vt_refs[si][0, pc, rows[j], b0:b0 + kb], p.astype(BF16))
                    l[j] = ls if l[j] is None else l[j] + ls
                    acc[j] = pv if acc[j] is None else acc[j] + pv
        if nxt is not None:
            for j in range(2):
                m_n[j] = mx[j]
        if cur is not None:
            outs = [acc[j] * (1.0 / l[j]) for j in range(2)]
            gate = g_ref[0, pc, pl.ds(rc, tq), :].astype(F32)
            o_ref[0, pc, pl.ds(rc, tq), :] = (jnp.concatenate(outs, axis=0).T * gate).astype(o_ref.dtype)

    buf = [(s_bufs[0], m_bufs[0]), (s_bufs[1], m_bufs[1])]
    phase((0, *buf[0]), None)
    assert n_items % 2 == 0

    def body(i, carry):
        n = 2 * i
        phase((n + 1, *buf[1]), (n, *buf[0]))
        phase((jnp.minimum(n + 2, n_items - 1), *buf[0]), (n + 1, *buf[1]))
        return carry

    lax.fori_loop(0, n_items // 2, body, 0)


def _attention(q, gate, ks, vts, tq):
    b, n_pair, t, pair = q.shape
    n_seg = len(ks)
    n_keys = sum(k.shape[2] for k in ks)
    whole = lambda a: pl.BlockSpec((1,) + a.shape[1:], lambda i: (i, 0, 0, 0))
    score_buf = pltpu.VMEM((2, n_keys, tq), F32)
    max_buf = pltpu.VMEM((2, 1, tq), F32)
    return pl.pallas_call(
        functools.partial(_attn_kernel, n_seg=n_seg, tq=tq),
        grid=(b,),
        in_specs=[whole(q), whole(gate)] + [whole(k) for k in ks] + [whole(v) for v in vts],
        out_specs=pl.BlockSpec((1, n_pair, t, 2 * MLA_V), lambda i: (i, 0, 0, 0)),
        out_shape=jax.ShapeDtypeStruct((b, n_pair, t, 2 * MLA_V), BF16),
        scratch_shapes=[score_buf, score_buf, max_buf, max_buf],
        compiler_params=_params(("arbitrary",)),
        name="attention",
    )(q, gate, *ks, *vts)


_RET_GROUP = 8


def _log_sigmoid(x):
    return jnp.minimum(x, 0.0) - jnp.log1p(jnp.exp(-jnp.abs(x)))


def _ret_kernel(dec_ref, *refs, chunk, group, layer, states_only):
    if states_only:
        k_ref, vt_ref, s0_ref, sfin_ref, sf_scr, sb_scr, kd_scr, qd_scr, dt_scr, cc_scr = refs
    else:
        (q_ref, k_ref, vt_ref, s0_ref, gng_ref, gnb_ref, gate_ref, o_ref, sfin_ref, sf_scr, sb_scr, kd_scr, qd_scr,
         dt_scr, cc_scr) = refs
    c_len = chunk
    t = k_ref.shape[1]
    n_c = t // c_len
    hp = pl.program_id(0)

    @pl.when(pl.program_id(1) == 0)
    def _():
        r = lax.broadcasted_iota(jnp.int32, (c_len, LANES), 0).astype(F32)
        lane = lax.broadcasted_iota(jnp.int32, (c_len, LANES), 1)
        km = lax.broadcasted_iota(jnp.int32, (c_len, c_len), 0)
        qn = lax.broadcasted_iota(jnp.int32, (c_len, c_len), 1)
        diff = (qn - km).astype(F32)
        for j in range(2):
            h = hp * 2 + j
            lg_f = _log_sigmoid(jnp.full((c_len, LANES), dec_ref[layer, 0, h], F32))
            lg_b = _log_sigmoid(jnp.full((c_len, LANES), dec_ref[layer, 1, h], F32))
            own = ((lane >= j * RET_DK) & (lane < (j + 1) * RET_DK)).astype(F32)
            lg_f2 = _log_sigmoid(jnp.full((c_len, c_len), dec_ref[layer, 0, h], F32))
            lg_b2 = _log_sigmoid(jnp.full((c_len, c_len), dec_ref[layer, 1, h], F32))
            kd_scr[j] = jnp.concatenate([jnp.exp(lg_f * (c_len - 1.0 - r)) * own, jnp.exp(lg_b * r) * own, own], axis=1)
            qd_scr[j] = jnp.concatenate([jnp.exp(lg_f * (r + 1.0)), jnp.exp(lg_b * (c_len - r))], axis=1)
            dt_scr[j] = jnp.exp(jnp.where(diff >= 0, lg_f2 * diff, -lg_b2 * diff))
            cc_scr[j] = jnp.concatenate(
                [jnp.exp(_log_sigmoid(jnp.full((RET_DV, LANES), dec_ref[layer, 0, h], F32)) * c_len),
                 jnp.exp(_log_sigmoid(jnp.full((RET_DV, LANES), dec_ref[layer, 1, h], F32)) * c_len)], axis=1)

    heads = []
    for j in range(2):
        heads.append(dict(
            kdf=kd_scr[j, :, 0:LANES],
            kdb=kd_scr[j, :, LANES:2 * LANES],
            own=kd_scr[j, :, 2 * LANES:3 * LANES],
            qd=qd_scr[j],
            dt=dt_scr[j],
            cf=cc_scr[j, :, 0:LANES],
            cb=cc_scr[j, :, LANES:2 * LANES],
            rows=slice(j * RET_DV, (j + 1) * RET_DV),
        ))

    def local(i, carry):
        for g in range(group):
            c = i * group + g
            c0 = pl.multiple_of(c * c_len, c_len)
            kc = k_ref[0, pl.ds(c0, c_len), :]
            for j, hd in enumerate(heads):
                vt = vt_ref[0, hd["rows"], pl.ds(c0, c_len)]
                sf_scr[j, c] = _dot(vt, (kc * hd["kdf"]).astype(BF16))
                sb_scr[j, c] = _dot(vt, (kc * hd["kdb"]).astype(BF16))
        return carry

    lax.fori_loop(0, n_c // group, local, 0)

    for j, hd in enumerate(heads):
        sf = s0_ref[0, 0, 2 * j]
        for c in range(n_c):
            a = sf_scr[j, c]
            sf_scr[j, c] = sf
            sf = hd["cf"] * sf + a
        sfin_ref[0, 0, 2 * j] = sf
        sb = s0_ref[0, 0, 2 * j + 1]
        for c in reversed(range(n_c)):
            a = sb_scr[j, c]
            sb_scr[j, c] = sb
            sb = hd["cb"] * sb + a
        sfin_ref[0, 0, 2 * j + 1] = sb

    if states_only:
        return

    def outputs(i, carry):
        items = []
        for g in range(group):
            c = i * group + g
            c0 = pl.multiple_of(c * c_len, c_len)
            qc = q_ref[0, pl.ds(c0, c_len), :]
            kc = k_ref[0, pl.ds(c0, c_len), :]
            items.append(dict(c=c, c0=c0, kc=kc, qb=qc.astype(BF16), qq=jnp.concatenate([qc, qc], axis=1)))
        work = [(it, j, hd) for it in items for j, hd in enumerate(heads)]
        st = [_dot_nt((it["kc"] * hd["own"]).astype(BF16), it["qb"]) for it, j, hd in work]
        cross = [_dot_nt(jnp.concatenate([sf_scr[j, it["c"]], sb_scr[j, it["c"]]], axis=1).astype(BF16),
                         (it["qq"] * hd["qd"]).astype(BF16)) for it, j, hd in work]
        o = [_dot(vt_ref[0, hd["rows"], pl.ds(it["c0"], c_len)], (s * hd["dt"]).astype(BF16)) + x
             for (it, j, hd), s, x in zip(work, st, cross)]
        ys = []
        for v in o:
            mu = jnp.mean(v, axis=0, keepdims=True)
            vc = v - mu
            var = jnp.mean(vc * vc, axis=0, keepdims=True)
            ys.append(vc * lax.rsqrt(var + EPS))
        for g, it in enumerate(items):
            y = jnp.concatenate(ys[2 * g:2 * g + 2], axis=0).T
            gate = gate_ref[0, pl.ds(it["c0"], c_len), :].astype(F32)
            o_ref[0, pl.ds(it["c0"], c_len), :] = ((y * gng_ref[...] + gnb_ref[...]) * gate).astype(o_ref.dtype)
        return carry

    lax.fori_loop(0, n_c // group, outputs, 0)


def _retention(dec, rq, rk, rvt, s0, gn_g, gn_b, gate, layer, chunk):
    b, t, _ = rk.shape
    states_only = rq is None
    n_pair = RET_HEADS // 2
    pair = 2 * RET_DK
    n_c = t // chunk
    group = math.gcd(n_c, _RET_GROUP)
    state_buf = pltpu.VMEM((2, n_c, RET_DV, pair), F32)
    tables = [pltpu.VMEM((2, chunk, 3 * LANES), F32), pltpu.VMEM((2, chunk, 2 * LANES), F32),
              pltpu.VMEM((2, chunk, chunk), F32), pltpu.VMEM((2, RET_DV, 2 * LANES), F32)]
    seq = pl.BlockSpec((1, t, pair), lambda h, i: (i, 0, h))
    state = pl.BlockSpec((1, 1, 4, RET_DV, pair), lambda h, i: (i, h, 0, 0, 0))
    affine = pl.BlockSpec((None, 1, pair), lambda h, i: (layer, 0, h))
    state_shape = jax.ShapeDtypeStruct((b, n_pair, 4, RET_DV, pair), F32)
    smem = pl.BlockSpec(memory_space=pltpu.SMEM)
    vt_spec = pl.BlockSpec((1, 2 * RET_DV, t), lambda h, i: (i, h, 0))
    if states_only:
        in_specs, args = [smem, seq, vt_spec, state], (dec, rk, rvt, s0)
        out_specs, out_shape = state, state_shape
    else:
        in_specs = [smem, seq, seq, vt_spec, state, affine, affine, seq]
        args = (dec, rq, rk, rvt, s0, gn_g, gn_b, gate)
        out_specs, out_shape = (seq, state), (jax.ShapeDtypeStruct((b, t, RET_WIDTH), BF16), state_shape)
    res = pl.pallas_call(
        functools.partial(_ret_kernel, chunk=chunk, group=group, layer=layer, states_only=states_only),
        grid=(n_pair, b),
        in_specs=in_specs,
        out_specs=out_specs,
        out_shape=out_shape,
        scratch_shapes=[state_buf, state_buf] + tables,
        compiler_params=_params(("arbitrary", "arbitrary")),
        name="retention",
    )(*args)
    return (None, res) if states_only else res


def _final_kernel(x_ref, mod_ref, om_ref, or_ref, oc_ref, wo_ref, lng_ref, lnb_ref, o_ref, *, alpha):
    o_cat = jnp.concatenate([om_ref[0, p] for p in range(om_ref.shape[1])] + [or_ref[0], oc_ref[0]], axis=1)
    y = _dot(o_cat, wo_ref[...])

    v = alpha * x_ref[0] + mod_ref[0, 2:3, :] * y
    mu = jnp.mean(v, axis=-1, keepdims=True)
    vc = v - mu
    var = jnp.mean(vc * vc, axis=-1, keepdims=True)
    o_ref[0] = vc * lax.rsqrt(var + EPS) * lng_ref[...] + lnb_ref[...]


def _final(x, mod_all, mod_row, o_mla, o_ret, o_conv, w_out, ln_g, ln_b, layer, tm, alpha):
    b, t, d = x.shape
    const = lambda a: _layer_spec(a, layer)
    row = lambda n: pl.BlockSpec((1, tm, n), lambda i, j: (i, j, 0))
    return pl.pallas_call(
        functools.partial(_final_kernel, alpha=alpha),
        grid=(b, t // tm),
        in_specs=[row(d),
                  _mod_spec(mod_all, layer, mod_row, 0),
                  pl.BlockSpec((1, o_mla.shape[1], tm, o_mla.shape[3]), lambda i, j: (i, 0, j, 0)),
                  row(RET_WIDTH), row(CONV_WIDTH),
                  const(w_out), const(ln_g), const(ln_b)],
        out_specs=row(d),
        out_shape=jax.ShapeDtypeStruct((b, t, d), F32),
        compiler_params=_params(("arbitrary", "arbitrary")),
        name="final",
    )(x, mod_all, o_mla, o_ret, o_conv, w_out, ln_g, ln_b)


def _tile(n, pref):
    return pref if n % pref == 0 else n


def kernel(x, c, ctx, c_ctx, w_mod, b_mod, w_in, mla_q_norm, w_uq, mla_kv_norm, w_ukv, ret_decay_fwd, ret_decay_bwd,
           ret_gn_g, ret_gn_b, conv_dw, conv_dw_b, conv_ln_g, conv_ln_b, conv_pw, conv_pw_b, w_out, ln_g, ln_b):
    depth = w_mod.shape[0]
    b, t, d = x.shape
    t_ctx = ctx.shape[1]
    alpha = (2 * depth) ** 0.25
    n_mod_rows = 16
    cc = jnp.concatenate([c, c_ctx[None, :], jnp.zeros((n_mod_rows - b - 1, d), c.dtype)], axis=0)
    mod_all = _modulation(cc, w_mod, b_mod).reshape(depth, n_mod_rows, 3, d)
    tab_x = _rope_table_block(t, True)
    tab_c = _rope_table_block(t_ctx, False)
    s_zero = jnp.zeros((b, RET_HEADS // 2, 4, RET_DV, 2 * RET_DK), F32)

    row3 = lambda a: a.reshape(depth, 1, -1)
    weights = _prep_weights(w_in, w_uq, w_ukv)
    gq, gkv = row3(mla_q_norm), row3(mla_kv_norm)
    dec = jnp.stack([ret_decay_fwd, ret_decay_bwd], axis=1).astype(F32)
    gn_g, gn_b = row3(ret_gn_g), row3(ret_gn_b)
    dw = jnp.pad(conv_dw, ((0, 0), (0, 1), (0, 0)))
    conv = (dw, row3(conv_dw_b), row3(conv_ln_g), row3(conv_ln_b), conv_pw.astype(BF16), row3(conv_pw_b))
    tail = (w_out.astype(BF16), row3(ln_g), row3(ln_b))
    tm_x, tm_c = _tile(t, 1024), _tile(t_ctx, 256)
    tf_x, tf_c = _tile(t, 1024), _tile(t_ctx, 256)
    tq_x, tq_c = _tile(t, 256), _tile(t_ctx, 256)
    ch_x, ch_c = _tile(t, 256), _tile(t_ctx, 256)

    hc = ctx
    for l in range(depth):
        need_ctx = l < depth - 1
        pc = _proj(hc, mod_all, b, tab_c, weights, gq, gkv, conv, l, tm_c, keys_only=not need_ctx)
        px = _proj(x, mod_all, None, tab_x, weights, gq, gkv, conv, l, tm_x)

        oc_ret, s_ctx = _retention(dec, pc.get("rq"), pc["rk"], pc["rvt"], s_zero, gn_g, gn_b, pc.get("gate_ret"),
                                   l, ch_c)
        o_ret, _ = _retention(dec, px["rq"], px["rk"], px["rvt"], s_ctx, gn_g, gn_b, px["gate_ret"], l, ch_x)
        o_mla = _attention(px["q"], px["gate_mla"], [pc["k"], px["k"]], [pc["vt"], px["vt"]], tq_x)
        x_new = _final(x, mod_all, None, o_mla, o_ret, px["conv"], *tail, layer=l, tm=tf_x, alpha=alpha)
        if need_ctx:
            oc_mla = _attention(pc["q"], pc["gate_mla"], [pc["k"]], [pc["vt"]], tq_c)
            hc = _final(hc, mod_all, b, oc_mla, oc_ret, pc["conv"], *tail, layer=l, tm=tf_c, alpha=alpha)
        x = x_new
    return x
```

```python
import functools
import math

import jax
import jax.numpy as jnp
import numpy as np
from jax import lax
from jax.experimental import pallas as pl
from jax.experimental.pallas import tpu as pltpu

GRID_W = 64
MLA_HEADS = 8
MLA_NOPE = 64
MLA_ROPE = 32
MLA_V = 64
MLA_QK = MLA_NOPE + MLA_ROPE
MLA_WIDTH = MLA_HEADS * MLA_V
Q_LORA = 256
KV_LORA = 128
RET_HEADS = 4
RET_DK = 64
RET_DV = 64
RET_WIDTH = RET_HEADS * RET_DV
CONV_WIDTH = 256
CONV_K = 31
ROPE_BASE = 10000.0
EPS = 1e-5

LANES = 128
HEAD_SLOT = LANES
VMEM_LIMIT_BYTES = 56 * 1024 * 1024
LOG2E = 1.4426950408889634

F32 = jnp.float32
BF16 = jnp.bfloat16

_NT = (((1,), (1,)), ((), ()))


def _dot(a, b):
    return jnp.dot(a, b, preferred_element_type=F32)


def _dot_nt(a, b):
    return lax.dot_general(a, b, _NT, preferred_element_type=F32)


def _params(semantics):
    return pltpu.CompilerParams(dimension_semantics=semantics, vmem_limit_bytes=VMEM_LIMIT_BYTES)


def _rot_cols(w, unit):
    xp = np if isinstance(w, np.ndarray) else jnp
    q = unit // 4
    w5 = w.reshape(w.shape[:-1] + (w.shape[-1] // unit, 2, 2, q))
    return xp.stack([-w5[..., 1, :], w5[..., 0, :]], axis=-2).reshape(w.shape)


def _rope_tables(length, unit, reps):
    d2 = unit // 2
    t = np.arange(length, dtype=np.int32)
    inv = np.float32(ROPE_BASE) ** (-np.arange(0, d2, 2, dtype=np.float32) / np.float32(d2))

    def half(pos):
        ang = pos.astype(np.float32)[:, None] * inv[None, :]
        return np.concatenate([np.cos(ang)] * 2, axis=-1), np.concatenate([np.sin(ang)] * 2, axis=-1)

    cr, sr = half(t // GRID_W)
    cc, sc = half(t % GRID_W)
    cos = np.concatenate([cr, cc], axis=-1).astype(np.float32)
    sin = np.concatenate([sr, sc], axis=-1).astype(np.float32)
    return np.tile(cos, (1, reps)), np.tile(sin, (1, reps))


def _rope_table_block(length, rotate):
    if rotate:
        cos_r, sin_r = _rope_tables(length, RET_DK, RET_HEADS)
        cos_m, sin_m = _rope_tables(length, MLA_ROPE, MLA_HEADS)
    else:
        cos_r = np.ones((length, RET_WIDTH), np.float32)
        sin_r = np.zeros((length, RET_WIDTH), np.float32)
        cos_m = np.ones((length, MLA_HEADS * MLA_ROPE), np.float32)
        sin_m = np.zeros((length, MLA_HEADS * MLA_ROPE), np.float32)
    kslot = np.concatenate([cos_m[:, :MLA_ROPE], sin_m[:, :MLA_ROPE],
                            np.zeros((length, LANES - 2 * MLA_ROPE), np.float32)], axis=-1)
    return jnp.asarray(np.concatenate([cos_r, sin_r, cos_m, sin_m, kslot], axis=-1))


def _placement_matrices():
    q_place = np.zeros((MLA_HEADS, MLA_ROPE, MLA_HEADS, HEAD_SLOT), np.float32)
    k_place = np.zeros((LANES, MLA_HEADS, HEAD_SLOT), np.float32)
    for d in range(MLA_ROPE):
        for h in range(MLA_HEADS):
            q_place[h, d, h, MLA_NOPE + d] = 1.0
            k_place[d, h, MLA_NOPE + d] = 1.0
            k_place[MLA_ROPE + d, h, MLA_NOPE + d] = 1.0
    return (q_place.reshape(MLA_HEADS * MLA_ROPE, MLA_HEADS * HEAD_SLOT),
            k_place.reshape(LANES, MLA_HEADS * HEAD_SLOT))


_IN_SIZES = dict(q=Q_LORA, kv=KV_LORA, kr=MLA_ROPE, g_mla=MLA_WIDTH, rq=RET_HEADS * RET_DK, rk=RET_HEADS * RET_DK,
                 rv=RET_WIDTH, g_ret=RET_WIDTH, glu=2 * CONV_WIDTH, g_conv=CONV_WIDTH)


def _offsets(sizes):
    out, at = {}, 0
    for name, n in sizes.items():
        out[name] = (at, at + n)
        at += n
    return out


_IN_COLS = _offsets(_IN_SIZES)


def _prep_weights(w_in, w_uq, w_ukv):
    depth, d, _ = w_in.shape
    w_t = jnp.swapaxes(w_in, 1, 2).astype(BF16)
    wkr = w_in[..., _IN_COLS["kr"][0]:_IN_COLS["kr"][1]]
    w_ks = jnp.concatenate([wkr, _rot_cols(wkr, MLA_ROPE), jnp.zeros((depth, d, LANES - 2 * MLA_ROPE), w_in.dtype)],
                           axis=-1)
    w_ks = jnp.swapaxes(w_ks, 1, 2).astype(BF16)

    uq = w_uq.reshape(depth, Q_LORA, MLA_HEADS, MLA_QK)
    pad = ((0, 0), (0, 0), (0, 0), (0, HEAD_SLOT - MLA_NOPE))
    uq_nope = jnp.pad(uq[..., :MLA_NOPE], pad).reshape(depth, Q_LORA, -1)
    uq_rope = uq[..., MLA_NOPE:].reshape(depth, Q_LORA, MLA_HEADS * MLA_ROPE)
    w_q2 = jnp.concatenate([uq_rope, _rot_cols(uq_rope, MLA_ROPE)], axis=-1).astype(BF16)
    q_place, k_place = (jnp.broadcast_to(jnp.asarray(m, BF16), (depth,) + m.shape) for m in _placement_matrices())
    w_qcomb = jnp.concatenate([uq_nope.astype(BF16), q_place], axis=1)

    ukv = w_ukv.reshape(depth, KV_LORA, MLA_HEADS, MLA_NOPE + MLA_V)
    uk = jnp.pad(ukv[..., :MLA_NOPE], pad).reshape(depth, KV_LORA, -1)
    w_kcomb = jnp.concatenate([uk.astype(BF16), k_place], axis=1)
    w_uvt = jnp.swapaxes(ukv[..., MLA_NOPE:].reshape(depth, KV_LORA, MLA_WIDTH), 1, 2).astype(BF16)
    return w_t, w_ks, w_q2, w_qcomb, w_kcomb, w_uvt


def _mod_kernel(c_ref, w_ref, b_ref, o_ref):
    c = c_ref[...]
    a = (c * jax.nn.sigmoid(c)).astype(BF16)
    o_ref[0] = _dot(a, w_ref[0].astype(BF16)) + b_ref[0]


def _modulation(cc, w_mod, b_mod):
    depth, d, d3 = w_mod.shape
    nblk = d3 // d
    return pl.pallas_call(
        _mod_kernel,
        grid=(depth, nblk),
        in_specs=[pl.BlockSpec(cc.shape, lambda l, j: (0, 0)),
                  pl.BlockSpec((1, d, d), lambda l, j: (l, 0, j)),
                  pl.BlockSpec((1, 1, d), lambda l, j: (l, 0, j))],
        out_specs=pl.BlockSpec((1, cc.shape[0], d), lambda l, j: (l, 0, j)),
        out_shape=jax.ShapeDtypeStruct((depth, cc.shape[0], d3), F32),
        compiler_params=_params(("arbitrary", "arbitrary")),
        name="mod",
    )(cc, w_mod, b_mod.reshape(depth, 1, d3))


_T_CR, _T_SR, _T_CM, _T_SM, _T_KS, _T_END = 0, 256, 512, 768, 1024, 1152


_HALO = 16
_CONV_ROWS = 64


def _rms(x, g):
    return x * lax.rsqrt(jnp.mean(x * x, axis=-1, keepdims=True) + EPS) * g


def _proj_kernel(x_ref, xp_ref, xn_ref, mod_ref, tab_ref, rot_ref, wt_ref, wks_ref, gq_ref, gkv_ref, wq2_ref,
                 wqc_ref, wkc_ref, wuvt_ref, dw_ref, dwb_ref, clg_ref, clb_ref, pw_ref, pwb_ref, *out_refs, keys_only):
    if keys_only:
        k_ref, vt_ref, rk_ref, rvt_ref = out_refs
    else:
        q_ref, k_ref, vt_ref, rq_ref, rk_ref, rvt_ref, gm_ref, gr_ref, oc_ref, upad, ushift = out_refs
    tm = x_ref.shape[1]
    shift = mod_ref[0, 0:1, :]
    scale = mod_ref[0, 1:2, :]
    pair = 2 * HEAD_SLOT

    def modulate(x):
        mu = jnp.mean(x, axis=-1, keepdims=True)
        xc = x - mu
        var = jnp.mean(xc * xc, axis=-1, keepdims=True)
        return (xc * lax.rsqrt(var + EPS) * (1.0 + scale) + shift).astype(BF16)

    u = modulate(x_ref[0])

    def w_rows(name):
        lo, hi = _IN_COLS[name]
        return wt_ref[lo:hi, :]

    def seg(name):
        return _dot_nt(u, w_rows(name))

    def silu(v):
        return v * jax.nn.sigmoid(v)

    def store_pairs(ref, val):
        for p in range(MLA_HEADS // 2):
            ref[0, p] = val[:, p * pair:(p + 1) * pair]

    def keys(pkv, pks):
        nkv = _rms(pkv, gkv_ref[...]).astype(BF16)
        ks = pks * tab_ref[:, _T_KS:_T_END]
        store_pairs(k_ref, _dot(jnp.concatenate([nkv, ks.astype(BF16)], axis=1), wkc_ref[...]).astype(BF16))
        vt_ref[0] = _dot_nt(wuvt_ref[...], nkv).astype(BF16)

    def ret_rope(raw):
        rot = _dot(raw.astype(BF16), rot_ref[...])
        return raw * tab_ref[:, _T_CR:_T_SR] + rot * tab_ref[:, _T_SR:_T_CM]

    def ret_keys(prk):
        rk_ref[0] = ret_rope(prk) * (RET_DK ** -0.5)
        rvt_ref[0] = _dot_nt(w_rows("rv"), u).astype(BF16)

    if keys_only:
        keys(seg("kv"), _dot_nt(u, wks_ref[...]))
        ret_keys(seg("rk"))
        return

    i = pl.program_id(0)
    u_ext = jnp.concatenate([modulate(xp_ref[0]), u, modulate(xn_ref[0])], axis=0)
    glu = _dot_nt(u_ext, w_rows("glu"))
    uc = glu[:, :CONV_WIDTH] * jax.nn.sigmoid(glu[:, CONV_WIDTH:])
    row = lax.broadcasted_iota(jnp.int32, uc.shape, 0)
    inside = ((row >= _HALO) | (i > 0)) & ((row < _HALO + tm) | (i < pl.num_programs(0) - 1))
    upad[...] = jnp.where(inside, uc, 0.0)
    span = tm + 2 * _HALO - 8
    for r in range(1, 8):
        ushift[r - 1] = upad[r:r + span, :]
    base = _HALO - CONV_K // 2
    conv_blocks = []

    def conv_rows(n_blocks):
        for _ in range(n_blocks):
            rb = len(conv_blocks) * _CONV_ROWS
            acc = jnp.zeros((_CONV_ROWS, CONV_WIDTH), F32) + dwb_ref[...]
            for j in range(CONV_K):
                a, r = divmod(base + j, 8)
                src = upad if r == 0 else ushift.at[r - 1]
                acc = acc + src[8 * a + rb:8 * a + rb + _CONV_ROWS, :] * dw_ref[j:j + 1, :]
            mu = jnp.mean(acc, axis=-1, keepdims=True)
            ac = acc - mu
            var = jnp.mean(ac * ac, axis=-1, keepdims=True)
            yc = ac * lax.rsqrt(var + EPS) * clg_ref[...] + clb_ref[...]
            conv_blocks.append((yc * jax.nn.sigmoid(yc)).astype(BF16))

    n_conv = tm // _CONV_ROWS
    per_stage = -(-n_conv // 4)

    pq, pkv, pks = seg("q"), seg("kv"), _dot_nt(u, wks_ref[...])
    prq, prk = seg("rq"), seg("rk")
    conv_rows(min(per_stage, n_conv - len(conv_blocks)))

    g_mla = silu(seg("g_mla"))
    for p in range(MLA_HEADS // 2):
        gm_ref[0, p] = g_mla[:, p * LANES:(p + 1) * LANES].astype(gm_ref.dtype)
    gr_ref[0] = silu(seg("g_ret")).astype(gr_ref.dtype)
    g_conv = silu(seg("g_conv"))
    conv_rows(min(per_stage, n_conv - len(conv_blocks)))

    nq = _rms(pq, gq_ref[...]).astype(BF16)
    qc = _dot(nq, wq2_ref[...])
    keys(pkv, pks)
    conv_rows(min(per_stage, n_conv - len(conv_blocks)))

    rq_ref[0] = ret_rope(prq)
    ret_keys(prk)

    nr = MLA_HEADS * MLA_ROPE
    q_rope = qc[:, :nr] * tab_ref[:, _T_CM:_T_SM] + qc[:, nr:] * tab_ref[:, _T_SM:_T_KS]
    q = _dot(jnp.concatenate([nq, q_rope.astype(BF16)], axis=1), wqc_ref[...])
    store_pairs(q_ref, (q * (MLA_QK ** -0.5 * LOG2E)).astype(BF16))
    conv_rows(n_conv - len(conv_blocks))

    o_conv = _dot(jnp.concatenate(conv_blocks, axis=0), pw_ref[...]) + pwb_ref[...]
    oc_ref[0] = (o_conv * g_conv).astype(oc_ref.dtype)


def _layer_spec(a, layer):
    return pl.BlockSpec((None,) + a.shape[1:], lambda *_: (layer,) + (0,) * (a.ndim - 1),
                        pipeline_mode=pl.Buffered(1))


def _mod_spec(mod_all, layer, shared_row, batch_axis):
    def index(*grid):
        return (layer, grid[batch_axis] if shared_row is None else shared_row, 0, 0)
    return pl.BlockSpec((None, 1) + mod_all.shape[2:], index)


def _proj(x, mod_all, mod_row, tab, weights, gq, gkv, conv, layer, tm, keys_only=False):
    b, t, d = x.shape
    w_t, w_ks, w_q2, w_qcomb, w_kcomb, w_uvt = weights
    const = lambda a: _layer_spec(a, layer)
    row = lambda n: pl.BlockSpec((1, tm, n), lambda i, j: (j, i, 0))
    col = lambda n: pl.BlockSpec((1, n, tm), lambda i, j: (j, 0, i))
    n_pair, pair = MLA_HEADS // 2, 2 * HEAD_SLOT
    pairs = pl.BlockSpec((1, n_pair, tm, pair), lambda i, j: (j, 0, i, 0))
    per_tile = tm // _HALO
    last = t // _HALO - 1
    halo_prev = pl.BlockSpec((1, _HALO, d), lambda i, j: (j, jnp.maximum(i * per_tile - 1, 0), 0))
    halo_next = pl.BlockSpec((1, _HALO, d), lambda i, j: (j, jnp.minimum((i + 1) * per_tile, last), 0))
    rot = jnp.asarray(_rot_cols(np.eye(RET_WIDTH, dtype=np.float32), RET_DK), BF16)
    sds = jax.ShapeDtypeStruct
    outs = dict(q=(sds((b, n_pair, t, pair), BF16), pairs),
                k=(sds((b, n_pair, t, pair), BF16), pairs),
                vt=(sds((b, MLA_WIDTH, t), BF16), col(MLA_WIDTH)),
                rq=(sds((b, t, RET_WIDTH), F32), row(RET_WIDTH)),
                rk=(sds((b, t, RET_WIDTH), F32), row(RET_WIDTH)),
                rvt=(sds((b, RET_WIDTH, t), BF16), col(RET_WIDTH)),
                gate_mla=(sds((b, n_pair, t, 2 * MLA_V), BF16),
                          pl.BlockSpec((1, n_pair, tm, 2 * MLA_V), lambda i, j: (j, 0, i, 0))),
                gate_ret=(sds((b, t, RET_WIDTH), BF16), row(RET_WIDTH)),
                conv=(sds((b, t, CONV_WIDTH), BF16), row(CONV_WIDTH)))
    names = ("k", "vt", "rk", "rvt") if keys_only else tuple(outs)
    scratch = [] if keys_only else [pltpu.VMEM((tm + 2 * _HALO, CONV_WIDTH), F32),
                                    pltpu.VMEM((7, tm + 2 * _HALO - 8, CONV_WIDTH), F32)]
    res = pl.pallas_call(
        functools.partial(_proj_kernel, keys_only=keys_only),
        grid=(t // tm, b),
        in_specs=[row(d), halo_prev, halo_next,
                  _mod_spec(mod_all, layer, mod_row, 1),
                  pl.BlockSpec((tm, _T_END), lambda i, j: (i, 0)),
                  pl.BlockSpec(rot.shape, lambda i, j: (0, 0)),
                  const(w_t), const(w_ks), const(gq), const(gkv), const(w_q2), const(w_qcomb), const(w_kcomb),
                  const(w_uvt)] + [const(a) for a in conv],
        out_specs=tuple(outs[n][1] for n in names),
        out_shape=tuple(outs[n][0] for n in names),
        scratch_shapes=scratch,
        compiler_params=_params(("arbitrary", "arbitrary")),
        name="proj",
    )(x, x, x, mod_all, tab, rot, w_t, w_ks, gq, gkv, w_q2, w_qcomb, w_kcomb, w_uvt, *conv)
    res = dict(zip(names, res))
    res["vt"] = res["vt"].reshape(b, n_pair, 2 * MLA_V, t)
    return res


_ATTN_KEY_BLOCK = 512


def _attn_kernel(*refs, n_seg, tq):
    q_ref, g_ref = refs[:2]
    k_refs = refs[2:2 + n_seg]
    vt_refs = refs[2 + n_seg:2 + 2 * n_seg]
    o_ref = refs[2 + 2 * n_seg]
    s_bufs = refs[3 + 2 * n_seg:5 + 2 * n_seg]
    m_bufs = refs[5 + 2 * n_seg:7 + 2 * n_seg]
    n_pair = q_ref.shape[1]
    n_q = q_ref.shape[2] // tq
    n_items = n_pair * n_q
    lanes = [slice(j * HEAD_SLOT, (j + 1) * HEAD_SLOT) for j in range(2)]
    rows = [slice(j * MLA_V, (j + 1) * MLA_V) for j in range(2)]

    blocks, off = [], 0
    for si in range(n_seg):
        ts = k_refs[si].shape[2]
        kb = min(_ATTN_KEY_BLOCK, ts)
        blocks += [(si, b0, kb, off + b0) for b0 in range(0, ts, kb)]
        off += ts

    def locate(item):
        if isinstance(item, int):
            return item // n_q, (item % n_q) * tq
        pair = lax.div(item, jnp.int32(n_q))
        return pair, pl.multiple_of((item - pair * n_q) * tq, tq)

    def phase(nxt, cur):
        if nxt is not None:
            item_n, s_n, m_n = nxt
            pn, rn = locate(item_n)
            qn = [q_ref[0, pn, pl.ds(rn, tq), lanes[j]] for j in range(2)]
            mx = [None, None]
        if cur is not None:
            item_c, s_c, m_c = cur
            pc, rc = locate(item_c)
            mc = [m_c[j] for j in range(2)]
            acc, l = [None, None], [None, None]
        for si, b0, kb, o0 in blocks:
            for j in range(2):
                if nxt is not None:
                    s = _dot_nt(k_refs[si][0, pn, b0:b0 + kb, lanes[j]], qn[j])
                    s_n[j, o0:o0 + kb, :] = s
                    mb = jnp.max(s, axis=0, keepdims=True)
                    mx[j] = mb if mx[j] is None else jnp.maximum(mx[j], mb)
                if cur is not None:
                    p = jnp.exp2(s_c[j, o0:o0 + kb, :] - mc[j])
                    ls = jnp.sum(p, axis=0, keepdims=True)
                    pv = _dot(vt_refs[si][0, pc, rows[j], b0:b0 + kb], p.astype(BF16))
                    l[j] = ls if l[j] is None else l[j] + ls
                    acc[j] = pv if acc[j] is None else acc[j] + pv
        if nxt is not None:
            for j in range(2):
                m_n[j] = mx[j]
        if cur is not None:
            outs = [acc[j] * (1.0 / l[j]) for j in range(2)]
            gate = g_ref[0, pc, pl.ds(rc, tq), :].astype(F32)
            o_ref[0, pc, pl.ds(rc, tq), :] = (jnp.concatenate(outs, axis=0).T * gate).astype(o_ref.dtype)

    buf = [(s_bufs[0], m_bufs[0]), (s_bufs[1], m_bufs[1])]
    phase((0, *buf[0]), None)
    assert n_items % 2 == 0

    def body(i, carry):
        n = 2 * i
        phase((n + 1, *buf[1]), (n, *buf[0]))
        phase((jnp.minimum(n + 2, n_items - 1), *buf[0]), (n + 1, *buf[1]))
        return carry

    lax.fori_loop(0, n_items // 2, body, 0)


def _attention(q, gate, ks, vts, tq):
    b, n_pair, t, pair = q.shape
    n_seg = len(ks)
    n_keys = sum(k.shape[2] for k in ks)
    whole = lambda a: pl.BlockSpec((1,) + a.shape[1:], lambda i: (i, 0, 0, 0))
    score_buf = pltpu.VMEM((2, n_keys, tq), F32)
    max_buf = pltpu.VMEM((2, 1, tq), F32)
    return pl.pallas_call(
        functools.partial(_attn_kernel, n_seg=n_seg, tq=tq),
        grid=(b,),
        in_specs=[whole(q), whole(gate)] + [whole(k) for k in ks] + [whole(v) for v in vts],
        out_specs=pl.BlockSpec((1, n_pair, t, 2 * MLA_V), lambda i: (i, 0, 0, 0)),
        out_shape=jax.ShapeDtypeStruct((b, n_pair, t, 2 * MLA_V), BF16),
        scratch_shapes=[score_buf, score_buf, max_buf, max_buf],
        compiler_params=_params(("arbitrary",)),
        name="attention",
    )(q, gate, *ks, *vts)


_RET_GROUP = 8


def _log_sigmoid(x):
    return jnp.minimum(x, 0.0) - jnp.log1p(jnp.exp(-jnp.abs(x)))


def _ret_kernel(dec_ref, *refs, chunk, group, layer, states_only):
    if states_only:
        k_ref, vt_ref, s0_ref, sfin_ref, sf_scr, sb_scr, kd_scr, qd_scr, dt_scr, cc_scr = refs
    else:
        (q_ref, k_ref, vt_ref, s0_ref, gng_ref, gnb_ref, gate_ref, o_ref, sfin_ref, sf_scr, sb_scr, kd_scr, qd_scr,
         dt_scr, cc_scr) = refs
    c_len = chunk
    t = k_ref.shape[1]
    n_c = t // c_len
    hp = pl.program_id(0)

    @pl.when(pl.program_id(1) == 0)
    def _():
        r = lax.broadcasted_iota(jnp.int32, (c_len, LANES), 0).astype(F32)
        lane = lax.broadcasted_iota(jnp.int32, (c_len, LANES), 1)
        km = lax.broadcasted_iota(jnp.int32, (c_len, c_len), 0)
        qn = lax.broadcasted_iota(jnp.int32, (c_len, c_len), 1)
        diff = (qn - km).astype(F32)
        for j in range(2):
            h = hp * 2 + j
            lg_f = _log_sigmoid(jnp.full((c_len, LANES), dec_ref[layer, 0, h], F32))
            lg_b = _log_sigmoid(jnp.full((c_len, LANES), dec_ref[layer, 1, h], F32))
            own = ((lane >= j * RET_DK) & (lane < (j + 1) * RET_DK)).astype(F32)
            lg_f2 = _log_sigmoid(jnp.full((c_len, c_len), dec_ref[layer, 0, h], F32))
            lg_b2 = _log_sigmoid(jnp.full((c_len, c_len), dec_ref[layer, 1, h], F32))
            kd_scr[j] = jnp.concatenate([jnp.exp(lg_f * (c_len - 1.0 - r)) * own, jnp.exp(lg_b * r) * own, own], axis=1)
            qd_scr[j] = jnp.concatenate([jnp.exp(lg_f * (r + 1.0)), jnp.exp(lg_b * (c_len - r))], axis=1)
            dt_scr[j] = jnp.exp(jnp.where(diff >= 0, lg_f2 * diff, -lg_b2 * diff))
            cc_scr[j] = jnp.concatenate(
                [jnp.exp(_log_sigmoid(jnp.full((RET_DV, LANES), dec_ref[layer, 0, h], F32)) * c_len),
                 jnp.exp(_log_sigmoid(jnp.full((RET_DV, LANES), dec_ref[layer, 1, h], F32)) * c_len)], axis=1)

    heads = []
    for j in range(2):
        heads.append(dict(
            kdf=kd_scr[j, :, 0:LANES],
            kdb=kd_scr[j, :, LANES:2 * LANES],
            own=kd_scr[j, :, 2 * LANES:3 * LANES],
            qd=qd_scr[j],
            dt=dt_scr[j],
            cf=cc_scr[j, :, 0:LANES],
            cb=cc_scr[j, :, LANES:2 * LANES],
            rows=slice(j * RET_DV, (j + 1) * RET_DV),
        ))

    def local(i, carry):
        for g in range(group):
            c = i * group + g
            c0 = pl.multiple_of(c * c_len, c_len)
            kc = k_ref[0, pl.ds(c0, c_len), :]
            for j, hd in enumerate(heads):
                vt = vt_ref[0, hd["rows"], pl.ds(c0, c_len)]
                sf_scr[j, c] = _dot(vt, (kc * hd["kdf"]).astype(BF16))
                sb_scr[j, c] = _dot(vt, (kc * hd["kdb"]).astype(BF16))
        return carry

    lax.fori_loop(0, n_c // group, local, 0)

    for j, hd in enumerate(heads):
        sf = s0_ref[0, 0, 2 * j]
        for c in range(n_c):
            a = sf_scr[j, c]
            sf_scr[j, c] = sf
            sf = hd["cf"] * sf + a
        sfin_ref[0, 0, 2 * j] = sf
        sb = s0_ref[0, 0, 2 * j + 1]
        for c in reversed(range(n_c)):
            a = sb_scr[j, c]
            sb_scr[j, c] = sb
            sb = hd["cb"] * sb + a
        sfin_ref[0, 0, 2 * j + 1] = sb

    if states_only:
        return

    def outputs(i, carry):
        items = []
        for g in range(group):
            c = i * group + g
            c0 = pl.multiple_of(c * c_len, c_len)
            qc = q_ref[0, pl.ds(c0, c_len), :]
            kc = k_ref[0, pl.ds(c0, c_len), :]
            items.append(dict(c=c, c0=c0, kc=kc, qb=qc.astype(BF16), qq=jnp.concatenate([qc, qc], axis=1)))
        work = [(it, j, hd) for it in items for j, hd in enumerate(heads)]
        st = [_dot_nt((it["kc"] * hd["own"]).astype(BF16), it["qb"]) for it, j, hd in work]
        cross = [_dot_nt(jnp.concatenate([sf_scr[j, it["c"]], sb_scr[j, it["c"]]], axis=1).astype(BF16),
                         (it["qq"] * hd["qd"]).astype(BF16)) for it, j, hd in work]
        o = [_dot(vt_ref[0, hd["rows"], pl.ds(it["c0"], c_len)], (s * hd["dt"]).astype(BF16)) + x
             for (it, j, hd), s, x in zip(work, st, cross)]
        ys = []
        for v in o:
            mu = jnp.mean(v, axis=0, keepdims=True)
            vc = v - mu
            var = jnp.mean(vc * vc, axis=0, keepdims=True)
            ys.append(vc * lax.rsqrt(var + EPS))
        for g, it in enumerate(items):
            y = jnp.concatenate(ys[2 * g:2 * g + 2], axis=0).T
            gate = gate_ref[0, pl.ds(it["c0"], c_len), :].astype(F32)
            o_ref[0, pl.ds(it["c0"], c_len), :] = ((y * gng_ref[...] + gnb_ref[...]) * gate).astype(o_ref.dtype)
        return carry

    lax.fori_loop(0, n_c // group, outputs, 0)


def _retention(dec, rq, rk, rvt, s0, gn_g, gn_b, gate, layer, chunk):
    b, t, _ = rk.shape
    states_only = rq is None
    n_pair = RET_HEADS // 2
    pair = 2 * RET_DK
    n_c = t // chunk
    group = math.gcd(n_c, _RET_GROUP)
    state_buf = pltpu.VMEM((2, n_c, RET_DV, pair), F32)
    tables = [pltpu.VMEM((2, chunk, 3 * LANES), F32), pltpu.VMEM((2, chunk, 2 * LANES), F32),
              pltpu.VMEM((2, chunk, chunk), F32), pltpu.VMEM((2, RET_DV, 2 * LANES), F32)]
    seq = pl.BlockSpec((1, t, pair), lambda h, i: (i, 0, h))
    state = pl.BlockSpec((1, 1, 4, RET_DV, pair), lambda h, i: (i, h, 0, 0, 0))
    affine = pl.BlockSpec((None, 1, pair), lambda h, i: (layer, 0, h))
    state_shape = jax.ShapeDtypeStruct((b, n_pair, 4, RET_DV, pair), F32)
    smem = pl.BlockSpec(memory_space=pltpu.SMEM)
    vt_spec = pl.BlockSpec((1, 2 * RET_DV, t), lambda h, i: (i, h, 0))
    if states_only:
        in_specs, args = [smem, seq, vt_spec, state], (dec, rk, rvt, s0)
        out_specs, out_shape = state, state_shape
    else:
        in_specs = [smem, seq, seq, vt_spec, state, affine, affine, seq]
        args = (dec, rq, rk, rvt, s0, gn_g, gn_b, gate)
        out_specs, out_shape = (seq, state), (jax.ShapeDtypeStruct((b, t, RET_WIDTH), BF16), state_shape)
    res = pl.pallas_call(
        functools.partial(_ret_kernel, chunk=chunk, group=group, layer=layer, states_only=states_only),
        grid=(n_pair, b),
        in_specs=in_specs,
        out_specs=out_specs,
        out_shape=out_shape,
        scratch_shapes=[state_buf, state_buf] + tables,
        compiler_params=_params(("arbitrary", "arbitrary")),
        name="retention",
    )(*args)
    return (None, res) if states_only else res


def _final_kernel(x_ref, mod_ref, om_ref, or_ref, oc_ref, wo_ref, lng_ref, lnb_ref, o_ref, *, alpha):
    o_cat = jnp.concatenate([om_ref[0, p] for p in range(om_ref.shape[1])] + [or_ref[0], oc_ref[0]], axis=1)
    y = _dot(o_cat, wo_ref[...])

    v = alpha * x_ref[0] + mod_ref[0, 2:3, :] * y
    mu = jnp.mean(v, axis=-1, keepdims=True)
    vc = v - mu
    var = jnp.mean(vc * vc, axis=-1, keepdims=True)
    o_ref[0] = vc * lax.rsqrt(var + EPS) * lng_ref[...] + lnb_ref[...]


def _final(x, mod_all, mod_row, o_mla, o_ret, o_conv, w_out, ln_g, ln_b, layer, tm, alpha):
    b, t, d = x.shape
    const = lambda a: _layer_spec(a, layer)
    row = lambda n: pl.BlockSpec((1, tm, n), lambda i, j: (i, j, 0))
    return pl.pallas_call(
        functools.partial(_final_kernel, alpha=alpha),
        grid=(b, t // tm),
        in_specs=[row(d),
                  _mod_spec(mod_all, layer, mod_row, 0),
                  pl.BlockSpec((1, o_mla.shape[1], tm, o_mla.shape[3]), lambda i, j: (i, 0, j, 0)),
                  row(RET_WIDTH), row(CONV_WIDTH),
                  const(w_out), const(ln_g), const(ln_b)],
        out_specs=row(d),
        out_shape=jax.ShapeDtypeStruct((b, t, d), F32),
        compiler_params=_params(("arbitrary", "arbitrary")),
        name="final",
    )(x, mod_all, o_mla, o_ret, o_conv, w_out, ln_g, ln_b)


def _tile(n, pref):
    return pref if n % pref == 0 else n


def kernel(x, c, ctx, c_ctx, w_mod, b_mod, w_in, mla_q_norm, w_uq, mla_kv_norm, w_ukv, ret_decay_fwd, ret_decay_bwd,
           ret_gn_g, ret_gn_b, conv_dw, conv_dw_b, conv_ln_g, conv_ln_b, conv_pw, conv_pw_b, w_out, ln_g, ln_b):
    depth = w_mod.shape[0]
    b, t, d = x.shape
    t_ctx = ctx.shape[1]
    alpha = (2 * depth) ** 0.25
    n_mod_rows = 16
    cc = jnp.concatenate([c, c_ctx[None, :], jnp.zeros((n_mod_rows - b - 1, d), c.dtype)], axis=0)
    mod_all = _modulation(cc, w_mod, b_mod).reshape(depth, n_mod_rows, 3, d)
    tab_x = _rope_table_block(t, True)
    tab_c = _rope_table_block(t_ctx, False)
    s_zero = jnp.zeros((b, RET_HEADS // 2, 4, RET_DV, 2 * RET_DK), F32)

    row3 = lambda a: a.reshape(depth, 1, -1)
    weights = _prep_weights(w_in, w_uq, w_ukv)
    gq, gkv = row3(mla_q_norm), row3(mla_kv_norm)
    dec = jnp.stack([ret_decay_fwd, ret_decay_bwd], axis=1).astype(F32)
    gn_g, gn_b = row3(ret_gn_g), row3(ret_gn_b)
    dw = jnp.pad(conv_dw, ((0, 0), (0, 1), (0, 0)))
    conv = (dw, row3(conv_dw_b), row3(conv_ln_g), row3(conv_ln_b), conv_pw.astype(BF16), row3(conv_pw_b))
    tail = (w_out.astype(BF16), row3(ln_g), row3(ln_b))
    tm_x, tm_c = _tile(t, 1024), _tile(t_ctx, 256)
    tf_x, tf_c = _tile(t, 1024), _tile(t_ctx, 256)
    tq_x, tq_c = _tile(t, 256), _tile(t_ctx, 256)
    ch_x, ch_c = _tile(t, 256), _tile(t_ctx, 256)

    hc = ctx
    for l in range(depth):
        need_ctx = l < depth - 1
        pc = _proj(hc, mod_all, b, tab_c, weights, gq, gkv, conv, l, tm_c, keys_only=not need_ctx)
        px = _proj(x, mod_all, None, tab_x, weights, gq, gkv, conv, l, tm_x)

        oc_ret, s_ctx = _retention(dec, pc.get("rq"), pc["rk"], pc["rvt"], s_zero, gn_g, gn_b, pc.get("gate_ret"),
                                   l, ch_c)
        o_ret, _ = _retention(dec, px["rq"], px["rk"], px["rvt"], s_ctx, gn_g, gn_b, px["gate_ret"], l, ch_x)
        o_mla = _attention(px["q"], px["gate_mla"], [pc["k"], px["k"]], [pc["vt"], px["vt"]], tq_x)
        x_new = _final(x, mod_all, None, o_mla, o_ret, px["conv"], *tail, layer=l, tm=tf_x, alpha=alpha)
        if need_ctx:
            oc_mla = _attention(pc["q"], pc["gate_mla"], [pc["k"]], [pc["vt"]], tq_c)
            hc = _final(hc, mod_all, b, oc_mla, oc_ret, pc["conv"], *tail, layer=l, tm=tf_c, alpha=alpha)
        x = x_new
    return x
```

```python
import functools
import math

import jax
import jax.numpy as jnp
import numpy as np
from jax import lax
from jax.experimental import pallas as pl
from jax.experimental.pallas import tpu as pltpu

GRID_W = 64
MLA_HEADS = 8
MLA_NOPE = 64
MLA_ROPE = 32
MLA_V = 64
MLA_QK = MLA_NOPE + MLA_ROPE
MLA_WIDTH = MLA_HEADS * MLA_V
Q_LORA = 256
KV_LORA = 128
RET_HEADS = 4
RET_DK = 64
RET_DV = 64
RET_WIDTH = RET_HEADS * RET_DV
CONV_WIDTH = 256
CONV_K = 31
ROPE_BASE = 10000.0
EPS = 1e-5

LANES = 128
HEAD_SLOT = LANES
VMEM_LIMIT_BYTES = 56 * 1024 * 1024
LOG2E = 1.4426950408889634

F32 = jnp.float32
BF16 = jnp.bfloat16

_NT = (((1,), (1,)), ((), ()))


def _dot(a, b):
    return jnp.dot(a, b, preferred_element_type=F32)


def _dot_nt(a, b):
    return lax.dot_general(a, b, _NT, preferred_element_type=F32)


def _params(semantics):
    return pltpu.CompilerParams(dimension_semantics=semantics, vmem_limit_bytes=VMEM_LIMIT_BYTES)


def _rot_cols(w, unit):
    xp = np if isinstance(w, np.ndarray) else jnp
    q = unit // 4
    w5 = w.reshape(w.shape[:-1] + (w.shape[-1] // unit, 2, 2, q))
    return xp.stack([-w5[..., 1, :], w5[..., 0, :]], axis=-2).reshape(w.shape)


def _rope_tables(length, unit, reps):
    d2 = unit // 2
    t = np.arange(length, dtype=np.int32)
    inv = np.float32(ROPE_BASE) ** (-np.arange(0, d2, 2, dtype=np.float32) / np.float32(d2))

    def half(pos):
        ang = pos.astype(np.float32)[:, None] * inv[None, :]
        return np.concatenate([np.cos(ang)] * 2, axis=-1), np.concatenate([np.sin(ang)] * 2, axis=-1)

    cr, sr = half(t // GRID_W)
    cc, sc = half(t % GRID_W)
    cos = np.concatenate([cr, cc], axis=-1).astype(np.float32)
    sin = np.concatenate([sr, sc], axis=-1).astype(np.float32)
    return np.tile(cos, (1, reps)), np.tile(sin, (1, reps))


def _rope_table_block(length, rotate):
    if rotate:
        cos_r, sin_r = _rope_tables(length, RET_DK, RET_HEADS)
        cos_m, sin_m = _rope_tables(length, MLA_ROPE, MLA_HEADS)
    else:
        cos_r = np.ones((length, RET_WIDTH), np.float32)
        sin_r = np.zeros((length, RET_WIDTH), np.float32)
        cos_m = np.ones((length, MLA_HEADS * MLA_ROPE), np.float32)
        sin_m = np.zeros((length, MLA_HEADS * MLA_ROPE), np.float32)
    kslot = np.concatenate([cos_m[:, :MLA_ROPE], sin_m[:, :MLA_ROPE],
                            np.zeros((length, LANES - 2 * MLA_ROPE), np.float32)], axis=-1)
    return jnp.asarray(np.concatenate([cos_r, sin_r, cos_m, sin_m, kslot], axis=-1))


def _placement_matrices():
    q_place = np.zeros((MLA_HEADS, MLA_ROPE, MLA_HEADS, HEAD_SLOT), np.float32)
    k_place = np.zeros((LANES, MLA_HEADS, HEAD_SLOT), np.float32)
    for d in range(MLA_ROPE):
        for h in range(MLA_HEADS):
            q_place[h, d, h, MLA_NOPE + d] = 1.0
            k_place[d, h, MLA_NOPE + d] = 1.0
            k_place[MLA_ROPE + d, h, MLA_NOPE + d] = 1.0
    return (q_place.reshape(MLA_HEADS * MLA_ROPE, MLA_HEADS * HEAD_SLOT),
            k_place.reshape(LANES, MLA_HEADS * HEAD_SLOT))


_IN_SIZES = dict(q=Q_LORA, kv=KV_LORA, kr=MLA_ROPE, g_mla=MLA_WIDTH, rq=RET_HEADS * RET_DK, rk=RET_HEADS * RET_DK,
                 rv=RET_WIDTH, g_ret=RET_WIDTH, glu=2 * CONV_WIDTH, g_conv=CONV_WIDTH)


def _offsets(sizes):
    out, at = {}, 0
    for name, n in sizes.items():
        out[name] = (at, at + n)
        at += n
    return out


_IN_COLS = _offsets(_IN_SIZES)


def _prep_weights(w_in, w_uq, w_ukv):
    depth, d, _ = w_in.shape
    w_t = jnp.swapaxes(w_in, 1, 2).astype(BF16)
    wkr = w_in[..., _IN_COLS["kr"][0]:_IN_COLS["kr"][1]]
    w_ks = jnp.concatenate([wkr, _rot_cols(wkr, MLA_ROPE), jnp.zeros((depth, d, LANES - 2 * MLA_ROPE), w_in.dtype)],
                           axis=-1)
    w_ks = jnp.swapaxes(w_ks, 1, 2).astype(BF16)

    uq = w_uq.reshape(depth, Q_LORA, MLA_HEADS, MLA_QK)
    pad = ((0, 0), (0, 0), (0, 0), (0, HEAD_SLOT - MLA_NOPE))
    uq_nope = jnp.pad(uq[..., :MLA_NOPE], pad).reshape(depth, Q_LORA, -1)
    uq_rope = uq[..., MLA_NOPE:].reshape(depth, Q_LORA, MLA_HEADS * MLA_ROPE)
    w_q2 = jnp.concatenate([uq_rope, _rot_cols(uq_rope, MLA_ROPE)], axis=-1).astype(BF16)
    q_place, k_place = (jnp.broadcast_to(jnp.asarray(m, BF16), (depth,) + m.shape) for m in _placement_matrices())
    w_qcomb = jnp.concatenate([uq_nope.astype(BF16), q_place], axis=1)

    ukv = w_ukv.reshape(depth, KV_LORA, MLA_HEADS, MLA_NOPE + MLA_V)
    uk = jnp.pad(ukv[..., :MLA_NOPE], pad).reshape(depth, KV_LORA, -1)
    w_kcomb = jnp.concatenate([uk.astype(BF16), k_place], axis=1)
    w_uvt = jnp.swapaxes(ukv[..., MLA_NOPE:].reshape(depth, KV_LORA, MLA_WIDTH), 1, 2).astype(BF16)
    return w_t, w_ks, w_q2, w_qcomb, w_kcomb, w_uvt


def _mod_kernel(c_ref, w_ref, b_ref, o_ref):
    c = c_ref[...]
    a = (c * jax.nn.sigmoid(c)).astype(BF16)
    o_ref[0, 0] = _dot(a, w_ref[0].astype(BF16)) + b_ref[pl.ds(pl.program_id(0), 1), :]


def _modulation(cc, w_mod, b_mod):
    depth, d, d3 = w_mod.shape
    nblk = d3 // d
    return pl.pallas_call(
        _mod_kernel,
        grid=(depth, nblk),
        in_specs=[pl.BlockSpec(cc.shape, lambda l, j: (0, 0)),
                  pl.BlockSpec((1, d, d), lambda l, j: (l, 0, j)),
                  pl.BlockSpec((depth, d), lambda l, j: (0, j))],
        out_specs=pl.BlockSpec((1, 1, cc.shape[0], d), lambda l, j: (l, j, 0, 0)),
        out_shape=jax.ShapeDtypeStruct((depth, nblk, cc.shape[0], d), F32),
        compiler_params=_params(("arbitrary", "arbitrary")),
        name="mod",
    )(cc, w_mod, b_mod)


_T_CR, _T_SR, _T_CM, _T_SM, _T_KS, _T_END = 0, 256, 512, 768, 1024, 1152


_HALO = 16
_CONV_ROWS = 64


def _rms(x, g):
    return x * lax.rsqrt(jnp.mean(x * x, axis=-1, keepdims=True) + EPS) * g


def _proj_kernel(x_ref, xp_ref, xn_ref, mod_ref, tab_ref, rot_ref, wt_ref, wks_ref, gq_ref, gkv_ref, wq2_ref,
                 wqc_ref, wkc_ref, wuvt_ref, dw_ref, dwb_ref, clg_ref, clb_ref, pw_ref, pwb_ref, *out_refs, keys_only,
                 layer, mod_row):
    if keys_only:
        k_ref, vt_ref, rk_ref, rvt_ref = out_refs
    else:
        q_ref, k_ref, vt_ref, rq_ref, rk_ref, rvt_ref, gm_ref, gr_ref, oc_ref, upad, ushift = out_refs
    tm = x_ref.shape[1]
    shift = _mod_row(mod_ref, 0, mod_row, 1)
    scale = _mod_row(mod_ref, 1, mod_row, 1)
    pair = 2 * HEAD_SLOT
    vec = lambda ref: ref[layer:layer + 1, :]

    def modulate(x):
        mu = jnp.mean(x, axis=-1, keepdims=True)
        xc = x - mu
        var = jnp.mean(xc * xc, axis=-1, keepdims=True)
        return (xc * lax.rsqrt(var + EPS) * (1.0 + scale) + shift).astype(BF16)

    u = modulate(x_ref[0])

    def w_rows(name):
        lo, hi = _IN_COLS[name]
        return wt_ref[lo:hi, :]

    def seg(name):
        return _dot_nt(u, w_rows(name))

    def silu(v):
        return v * jax.nn.sigmoid(v)

    def store_pairs(ref, val):
        for p in range(MLA_HEADS // 2):
            ref[0, p] = val[:, p * pair:(p + 1) * pair]

    def keys(pkv, pks):
        nkv = _rms(pkv, vec(gkv_ref)).astype(BF16)
        ks = pks * tab_ref[:, _T_KS:_T_END]
        store_pairs(k_ref, _dot(jnp.concatenate([nkv, ks.astype(BF16)], axis=1), wkc_ref[...]).astype(BF16))
        vt_ref[0] = _dot_nt(wuvt_ref[...], nkv).astype(BF16)

    def ret_rope(raw):
        rot = _dot(raw.astype(BF16), rot_ref[...])
        return raw * tab_ref[:, _T_CR:_T_SR] + rot * tab_ref[:, _T_SR:_T_CM]

    def ret_keys(prk):
        rk_ref[0] = ret_rope(prk) * (RET_DK ** -0.5)
        rvt_ref[0] = _dot_nt(w_rows("rv"), u).astype(BF16)

    if keys_only:
        keys(seg("kv"), _dot_nt(u, wks_ref[...]))
        ret_keys(seg("rk"))
        return

    i = pl.program_id(0)
    u_ext = jnp.concatenate([modulate(xp_ref[0]), u, modulate(xn_ref[0])], axis=0)
    glu = _dot_nt(u_ext, w_rows("glu"))
    uc = glu[:, :CONV_WIDTH] * jax.nn.sigmoid(glu[:, CONV_WIDTH:])
    row = lax.broadcasted_iota(jnp.int32, uc.shape, 0)
    inside = ((row >= _HALO) | (i > 0)) & ((row < _HALO + tm) | (i < pl.num_programs(0) - 1))
    upad[...] = jnp.where(inside, uc, 0.0)
    span = tm + 2 * _HALO - 8
    for r in range(1, 8):
        ushift[r - 1] = upad[r:r + span, :]
    base = _HALO - CONV_K // 2
    conv_blocks = []

    def conv_rows(n_blocks):
        for _ in range(n_blocks):
            rb = len(conv_blocks) * _CONV_ROWS
            acc = jnp.zeros((_CONV_ROWS, CONV_WIDTH), F32) + vec(dwb_ref)
            for j in range(CONV_K):
                a, r = divmod(base + j, 8)
                src = upad if r == 0 else ushift.at[r - 1]
                acc = acc + src[8 * a + rb:8 * a + rb + _CONV_ROWS, :] * dw_ref[j:j + 1, :]
            mu = jnp.mean(acc, axis=-1, keepdims=True)
            ac = acc - mu
            var = jnp.mean(ac * ac, axis=-1, keepdims=True)
            yc = ac * lax.rsqrt(var + EPS) * vec(clg_ref) + vec(clb_ref)
            conv_blocks.append((yc * jax.nn.sigmoid(yc)).astype(BF16))

    n_conv = tm // _CONV_ROWS
    per_stage = -(-n_conv // 4)

    pq, pkv, pks = seg("q"), seg("kv"), _dot_nt(u, wks_ref[...])
    prq, prk = seg("rq"), seg("rk")
    conv_rows(min(per_stage, n_conv - len(conv_blocks)))

    g_mla = silu(seg("g_mla"))
    for p in range(MLA_HEADS // 2):
        gm_ref[0, p] = g_mla[:, p * LANES:(p + 1) * LANES].astype(gm_ref.dtype)
    gr_ref[0] = silu(seg("g_ret")).astype(gr_ref.dtype)
    g_conv = silu(seg("g_conv"))
    conv_rows(min(per_stage, n_conv - len(conv_blocks)))

    nq = _rms(pq, vec(gq_ref)).astype(BF16)
    qc = _dot(nq, wq2_ref[...])
    keys(pkv, pks)
    conv_rows(min(per_stage, n_conv - len(conv_blocks)))

    rq_ref[0] = ret_rope(prq)
    ret_keys(prk)

    nr = MLA_HEADS * MLA_ROPE
    q_rope = qc[:, :nr] * tab_ref[:, _T_CM:_T_SM] + qc[:, nr:] * tab_ref[:, _T_SM:_T_KS]
    q = _dot(jnp.concatenate([nq, q_rope.astype(BF16)], axis=1), wqc_ref[...])
    store_pairs(q_ref, (q * (MLA_QK ** -0.5 * LOG2E)).astype(BF16))
    conv_rows(n_conv - len(conv_blocks))

    o_conv = _dot(jnp.concatenate(conv_blocks, axis=0), pw_ref[...]) + vec(pwb_ref)
    oc_ref[0] = (o_conv * g_conv).astype(oc_ref.dtype)


def _layer_spec(a, layer):
    return pl.BlockSpec((None,) + a.shape[1:], lambda *_: (layer,) + (0,) * (a.ndim - 1),
                        pipeline_mode=pl.Buffered(1))


def _whole_spec(a):
    return pl.BlockSpec(a.shape, lambda *_: (0,) * a.ndim, pipeline_mode=pl.Buffered(1))


def _mod_row(mod_ref, k, shared_row, batch_axis):
    if shared_row is None:
        return mod_ref[k, pl.ds(pl.program_id(batch_axis), 1), :]
    return mod_ref[k, shared_row:shared_row + 1, :]


def _proj(x, mod_all, mod_row, tab, weights, gq, gkv, conv, layer, tm, keys_only=False):
    b, t, d = x.shape
    w_t, w_ks, w_q2, w_qcomb, w_kcomb, w_uvt = weights
    const = lambda a: _layer_spec(a, layer)
    row = lambda n: pl.BlockSpec((1, tm, n), lambda i, j: (j, i, 0))
    col = lambda n: pl.BlockSpec((1, n, tm), lambda i, j: (j, 0, i))
    n_pair, pair = MLA_HEADS // 2, 2 * HEAD_SLOT
    pairs = pl.BlockSpec((1, n_pair, tm, pair), lambda i, j: (j, 0, i, 0))
    per_tile = tm // _HALO
    last = t // _HALO - 1
    halo_prev = pl.BlockSpec((1, _HALO, d), lambda i, j: (j, jnp.maximum(i * per_tile - 1, 0), 0))
    halo_next = pl.BlockSpec((1, _HALO, d), lambda i, j: (j, jnp.minimum((i + 1) * per_tile, last), 0))
    rot = jnp.asarray(_rot_cols(np.eye(RET_WIDTH, dtype=np.float32), RET_DK), BF16)
    sds = jax.ShapeDtypeStruct
    outs = dict(q=(sds((b, n_pair, t, pair), BF16), pairs),
                k=(sds((b, n_pair, t, pair), BF16), pairs),
                vt=(sds((b, MLA_WIDTH, t), BF16), col(MLA_WIDTH)),
                rq=(sds((b, t, RET_WIDTH), F32), row(RET_WIDTH)),
                rk=(sds((b, t, RET_WIDTH), F32), row(RET_WIDTH)),
                rvt=(sds((b, RET_WIDTH, t), BF16), col(RET_WIDTH)),
                gate_mla=(sds((b, n_pair, t, 2 * MLA_V), BF16),
                          pl.BlockSpec((1, n_pair, tm, 2 * MLA_V), lambda i, j: (j, 0, i, 0))),
                gate_ret=(sds((b, t, RET_WIDTH), BF16), row(RET_WIDTH)),
                conv=(sds((b, t, CONV_WIDTH), BF16), row(CONV_WIDTH)))
    names = ("k", "vt", "rk", "rvt") if keys_only else tuple(outs)
    scratch = [] if keys_only else [pltpu.VMEM((tm + 2 * _HALO, CONV_WIDTH), F32),
                                    pltpu.VMEM((7, tm + 2 * _HALO - 8, CONV_WIDTH), F32)]
    res = pl.pallas_call(
        functools.partial(_proj_kernel, keys_only=keys_only, layer=layer, mod_row=mod_row),
        grid=(t // tm, b),
        in_specs=[row(d), halo_prev, halo_next,
                  const(mod_all),
                  pl.BlockSpec((tm, _T_END), lambda i, j: (i, 0)),
                  pl.BlockSpec(rot.shape, lambda i, j: (0, 0)),
                  const(w_t), const(w_ks), _whole_spec(gq), _whole_spec(gkv), const(w_q2), const(w_qcomb),
                  const(w_kcomb), const(w_uvt)] + [const(a) if a.ndim == 3 else _whole_spec(a) for a in conv],
        out_specs=tuple(outs[n][1] for n in names),
        out_shape=tuple(outs[n][0] for n in names),
        scratch_shapes=scratch,
        compiler_params=_params(("arbitrary", "arbitrary")),
        name="proj",
    )(x, x, x, mod_all, tab, rot, w_t, w_ks, gq, gkv, w_q2, w_qcomb, w_kcomb, w_uvt, *conv)
    res = dict(zip(names, res))
    res["vt"] = res["vt"].reshape(b, n_pair, 2 * MLA_V, t)
    return res


_ATTN_KEY_BLOCK = 512


def _attn_kernel(*refs, n_seg, tq):
    q_ref, g_ref = refs[:2]
    k_refs = refs[2:2 + n_seg]
    vt_refs = refs[2 + n_seg:2 + 2 * n_seg]
    o_ref = refs[2 + 2 * n_seg]
    s_bufs = refs[3 + 2 * n_seg:5 + 2 * n_seg]
    m_bufs = refs[5 + 2 * n_seg:7 + 2 * n_seg]
    n_pair = q_ref.shape[1]
    n_q = q_ref.shape[2] // tq
    n_items = n_pair * n_q
    lanes = [slice(j * HEAD_SLOT, (j + 1) * HEAD_SLOT) for j in range(2)]
    rows = [slice(j * MLA_V, (j + 1) * MLA_V) for j in range(2)]

    blocks, off = [], 0
    for si in range(n_seg):
        ts = k_refs[si].shape[2]
        kb = min(_ATTN_KEY_BLOCK, ts)
        blocks += [(si, b0, kb, off + b0) for b0 in range(0, ts, kb)]
        off += ts

    def locate(item):
        if isinstance(item, int):
            return item // n_q, (item % n_q) * tq
        pair = lax.div(item, jnp.int32(n_q))
        return pair, pl.multiple_of((item - pair * n_q) * tq, tq)

    def phase(nxt, cur):
        if nxt is not None:
            item_n, s_n, m_n = nxt
            pn, rn = locate(item_n)
            qn = [q_ref[0, pn, pl.ds(rn, tq), lanes[j]] for j in range(2)]
            mx = [None, None]
        if cur is not None:
            item_c, s_c, m_c = cur
            pc, rc = locate(item_c)
            mc = [m_c[j] for j in range(2)]
            acc, l = [None, None], [None, None]
        for si, b0, kb, o0 in blocks:
            for j in range(2):
                if nxt is not None:
                    s = _dot_nt(k_refs[si][0, pn, b0:b0 + kb, lanes[j]], qn[j])
                    s_n[j, o0:o0 + kb, :] = s
                    mb = jnp.max(s, axis=0, keepdims=True)
                    mx[j] = mb if mx[j] is None else jnp.maximum(mx[j], mb)
                if cur is not None:
                    p = jnp.exp2(s_c[j, o0:o0 + kb, :] - mc[j])
                    ls = jnp.sum(p, axis=0, keepdims=True)
                    pv = _dot(vt_refs[si][0, pc, rows[j], b0:b0 + kb], p.astype(BF16))
                    l[j] = ls if l[j] is None else l[j] + ls
                    acc[j] = pv if acc[j] is None else acc[j] + pv
        if nxt is not None:
            for j in range(2):
                m_n[j] = mx[j]
        if cur is not None:
            outs = [acc[j] * (1.0 / l[j]) for j in range(2)]
            gate = g_ref[0, pc, pl.ds(rc, tq), :].astype(F32)
            o_ref[0, pc, pl.ds(rc, tq), :] = (jnp.concatenate(outs, axis=0).T * gate).astype(o_ref.dtype)

    buf = [(s_bufs[0], m_bufs[0]), (s_bufs[1], m_bufs[1])]
    phase((0, *buf[0]), None)
    assert n_items % 2 == 0

    def body(i, carry):
        n = 2 * i
        phase((n + 1, *buf[1]), (n, *buf[0]))
        phase((jnp.minimum(n + 2, n_items - 1), *buf[0]), (n + 1, *buf[1]))
        return carry

    lax.fori_loop(0, n_items // 2, body, 0)


def _attention(q, gate, ks, vts, tq):
    b, n_pair, t, pair = q.shape
    n_seg = len(ks)
    n_keys = sum(k.shape[2] for k in ks)
    whole = lambda a: pl.BlockSpec((1,) + a.shape[1:], lambda i: (i, 0, 0, 0))
    score_buf = pltpu.VMEM((2, n_keys, tq), F32)
    max_buf = pltpu.VMEM((2, 1, tq), F32)
    return pl.pallas_call(
        functools.partial(_attn_kernel, n_seg=n_seg, tq=tq),
        grid=(b,),
        in_specs=[whole(q), whole(gate)] + [whole(k) for k in ks] + [whole(v) for v in vts],
        out_specs=pl.BlockSpec((1, n_pair, t, 2 * MLA_V), lambda i: (i, 0, 0, 0)),
        out_shape=jax.ShapeDtypeStruct((b, n_pair, t, 2 * MLA_V), BF16),
        scratch_shapes=[score_buf, score_buf, max_buf, max_buf],
        compiler_params=_params(("arbitrary",)),
        name="attention",
    )(q, gate, *ks, *vts)


_RET_GROUP = 8


def _log_sigmoid(x):
    return jnp.minimum(x, 0.0) - jnp.log1p(jnp.exp(-jnp.abs(x)))


def _ret_kernel(dec_ref, *refs, chunk, group, layer, states_only):
    if states_only:
        k_ref, vt_ref, s0_ref, sfin_ref, sf_scr, sb_scr, kd_scr, qd_scr, dt_scr, cc_scr = refs
    else:
        (q_ref, k_ref, vt_ref, s0_ref, gng_ref, gnb_ref, gate_ref, o_ref, sfin_ref, sf_scr, sb_scr, kd_scr, qd_scr,
         dt_scr, cc_scr) = refs
    c_len = chunk
    t = k_ref.shape[1]
    n_c = t // c_len
    hp = pl.program_id(0)

    @pl.when(pl.program_id(1) == 0)
    def _():
        r = lax.broadcasted_iota(jnp.int32, (c_len, LANES), 0).astype(F32)
        lane = lax.broadcasted_iota(jnp.int32, (c_len, LANES), 1)
        km = lax.broadcasted_iota(jnp.int32, (c_len, c_len), 0)
        qn = lax.broadcasted_iota(jnp.int32, (c_len, c_len), 1)
        diff = (qn - km).astype(F32)
        for j in range(2):
            h = hp * 2 + j
            lg_f = _log_sigmoid(jnp.full((c_len, LANES), dec_ref[layer, 0, h], F32))
            lg_b = _log_sigmoid(jnp.full((c_len, LANES), dec_ref[layer, 1, h], F32))
            own = ((lane >= j * RET_DK) & (lane < (j + 1) * RET_DK)).astype(F32)
            lg_f2 = _log_sigmoid(jnp.full((c_len, c_len), dec_ref[layer, 0, h], F32))
            lg_b2 = _log_sigmoid(jnp.full((c_len, c_len), dec_ref[layer, 1, h], F32))
            kd_scr[j] = jnp.concatenate([jnp.exp(lg_f * (c_len - 1.0 - r)) * own, jnp.exp(lg_b * r) * own, own], axis=1)
            qd_scr[j] = jnp.concatenate([jnp.exp(lg_f * (r + 1.0)), jnp.exp(lg_b * (c_len - r))], axis=1)
            dt_scr[j] = jnp.exp(jnp.where(diff >= 0, lg_f2 * diff, -lg_b2 * diff))
            cc_scr[j] = jnp.concatenate(
                [jnp.exp(_log_sigmoid(jnp.full((RET_DV, LANES), dec_ref[layer, 0, h], F32)) * c_len),
                 jnp.exp(_log_sigmoid(jnp.full((RET_DV, LANES), dec_ref[layer, 1, h], F32)) * c_len)], axis=1)

    heads = []
    for j in range(2):
        heads.append(dict(
            kdf=kd_scr[j, :, 0:LANES],
            kdb=kd_scr[j, :, LANES:2 * LANES],
            own=kd_scr[j, :, 2 * LANES:3 * LANES],
            qd=qd_scr[j],
            dt=dt_scr[j],
            cf=cc_scr[j, :, 0:LANES],
            cb=cc_scr[j, :, LANES:2 * LANES],
            rows=slice(j * RET_DV, (j + 1) * RET_DV),
        ))

    def local(i, carry):
        for g in range(group):
            c = i * group + g
            c0 = pl.multiple_of(c * c_len, c_len)
            kc = k_ref[0, pl.ds(c0, c_len), :]
            for j, hd in enumerate(heads):
                vt = vt_ref[0, hd["rows"], pl.ds(c0, c_len)]
                sf_scr[j, c] = _dot(vt, (kc * hd["kdf"]).astype(BF16))
                sb_scr[j, c] = _dot(vt, (kc * hd["kdb"]).astype(BF16))
        return carry

    lax.fori_loop(0, n_c // group, local, 0)

    for j, hd in enumerate(heads):
        sf = s0_ref[0, 0, 2 * j]
        for c in range(n_c):
            a = sf_scr[j, c]
            sf_scr[j, c] = sf
            sf = hd["cf"] * sf + a
        sfin_ref[0, 0, 2 * j] = sf
        sb = s0_ref[0, 0, 2 * j + 1]
        for c in reversed(range(n_c)):
            a = sb_scr[j, c]
            sb_scr[j, c] = sb
            sb = hd["cb"] * sb + a
        sfin_ref[0, 0, 2 * j + 1] = sb

    if states_only:
        return

    def outputs(i, carry):
        items = []
        for g in range(group):
            c = i * group + g
            c0 = pl.multiple_of(c * c_len, c_len)
            qc = q_ref[0, pl.ds(c0, c_len), :]
            kc = k_ref[0, pl.ds(c0, c_len), :]
            items.append(dict(c=c, c0=c0, kc=kc, qb=qc.astype(BF16), qq=jnp.concatenate([qc, qc], axis=1)))
        work = [(it, j, hd) for it in items for j, hd in enumerate(heads)]
        st = [_dot_nt((it["kc"] * hd["own"]).astype(BF16), it["qb"]) for it, j, hd in work]
        cross = [_dot_nt(jnp.concatenate([sf_scr[j, it["c"]], sb_scr[j, it["c"]]], axis=1).astype(BF16),
                         (it["qq"] * hd["qd"]).astype(BF16)) for it, j, hd in work]
        o = [_dot(vt_ref[0, hd["rows"], pl.ds(it["c0"], c_len)], (s * hd["dt"]).astype(BF16)) + x
             for (it, j, hd), s, x in zip(work, st, cross)]
        ys = []
        for v in o:
            mu = jnp.mean(v, axis=0, keepdims=True)
            vc = v - mu
            var = jnp.mean(vc * vc, axis=0, keepdims=True)
            ys.append(vc * lax.rsqrt(var + EPS))
        for g, it in enumerate(items):
            y = jnp.concatenate(ys[2 * g:2 * g + 2], axis=0).T
            gate = gate_ref[0, pl.ds(it["c0"], c_len), :].astype(F32)
            y = y * gng_ref[layer:layer + 1, :] + gnb_ref[layer:layer + 1, :]
            o_ref[0, pl.ds(it["c0"], c_len), :] = (y * gate).astype(o_ref.dtype)
        return carry

    lax.fori_loop(0, n_c // group, outputs, 0)


def _retention(dec, rq, rk, rvt, s0, gn_g, gn_b, gate, layer, chunk):
    b, t, _ = rk.shape
    states_only = rq is None
    n_pair = RET_HEADS // 2
    pair = 2 * RET_DK
    n_c = t // chunk
    group = math.gcd(n_c, _RET_GROUP)
    state_buf = pltpu.VMEM((2, n_c, RET_DV, pair), F32)
    tables = [pltpu.VMEM((2, chunk, 3 * LANES), F32), pltpu.VMEM((2, chunk, 2 * LANES), F32),
              pltpu.VMEM((2, chunk, chunk), F32), pltpu.VMEM((2, RET_DV, 2 * LANES), F32)]
    seq = pl.BlockSpec((1, t, pair), lambda h, i: (i, 0, h))
    state = pl.BlockSpec((1, 1, 4, RET_DV, pair), lambda h, i: (i, h, 0, 0, 0))
    affine = pl.BlockSpec((gn_g.shape[0], pair), lambda h, i: (0, h))
    state_shape = jax.ShapeDtypeStruct((b, n_pair, 4, RET_DV, pair), F32)
    smem = pl.BlockSpec(memory_space=pltpu.SMEM)
    vt_spec = pl.BlockSpec((1, 2 * RET_DV, t), lambda h, i: (i, h, 0))
    if states_only:
        in_specs, args = [smem, seq, vt_spec, state], (dec, rk, rvt, s0)
        out_specs, out_shape = state, state_shape
    else:
        in_specs = [smem, seq, seq, vt_spec, state, affine, affine, seq]
        args = (dec, rq, rk, rvt, s0, gn_g, gn_b, gate)
        out_specs, out_shape = (seq, state), (jax.ShapeDtypeStruct((b, t, RET_WIDTH), BF16), state_shape)
    res = pl.pallas_call(
        functools.partial(_ret_kernel, chunk=chunk, group=group, layer=layer, states_only=states_only),
        grid=(n_pair, b),
        in_specs=in_specs,
        out_specs=out_specs,
        out_shape=out_shape,
        scratch_shapes=[state_buf, state_buf] + tables,
        compiler_params=_params(("arbitrary", "arbitrary")),
        name="retention",
    )(*args)
    return (None, res) if states_only else res


def _final_kernel(x_ref, mod_ref, om_ref, or_ref, oc_ref, wo_ref, lng_ref, lnb_ref, o_ref, *, alpha, layer, mod_row):
    o_cat = jnp.concatenate([om_ref[0, p] for p in range(om_ref.shape[1])] + [or_ref[0], oc_ref[0]], axis=1)
    y = _dot(o_cat, wo_ref[...].astype(BF16))

    v = alpha * x_ref[0] + _mod_row(mod_ref, 2, mod_row, 0) * y
    mu = jnp.mean(v, axis=-1, keepdims=True)
    vc = v - mu
    var = jnp.mean(vc * vc, axis=-1, keepdims=True)
    o_ref[0] = vc * lax.rsqrt(var + EPS) * lng_ref[layer:layer + 1, :] + lnb_ref[layer:layer + 1, :]


def _final(x, mod_all, mod_row, o_mla, o_ret, o_conv, w_out, ln_g, ln_b, layer, tm, alpha):
    b, t, d = x.shape
    const = lambda a: _layer_spec(a, layer)
    row = lambda n: pl.BlockSpec((1, tm, n), lambda i, j: (i, j, 0))
    return pl.pallas_call(
        functools.partial(_final_kernel, alpha=alpha, layer=layer, mod_row=mod_row),
        grid=(b, t // tm),
        in_specs=[row(d),
                  const(mod_all),
                  pl.BlockSpec((1, o_mla.shape[1], tm, o_mla.shape[3]), lambda i, j: (i, 0, j, 0)),
                  row(RET_WIDTH), row(CONV_WIDTH),
                  const(w_out), _whole_spec(ln_g), _whole_spec(ln_b)],
        out_specs=row(d),
        out_shape=jax.ShapeDtypeStruct((b, t, d), F32),
        compiler_params=_params(("arbitrary", "arbitrary")),
        name="final",
    )(x, mod_all, o_mla, o_ret, o_conv, w_out, ln_g, ln_b)


def _tile(n, pref):
    return pref if n % pref == 0 else n


def kernel(x, c, ctx, c_ctx, w_mod, b_mod, w_in, mla_q_norm, w_uq, mla_kv_norm, w_ukv, ret_decay_fwd, ret_decay_bwd,
           ret_gn_g, ret_gn_b, conv_dw, conv_dw_b, conv_ln_g, conv_ln_b, conv_pw, conv_pw_b, w_out, ln_g, ln_b):
    depth = w_mod.shape[0]
    b, t, d = x.shape
    t_ctx = ctx.shape[1]
    alpha = (2 * depth) ** 0.25
    n_mod_rows = 16
    cc = jnp.concatenate([c, c_ctx[None, :], jnp.zeros((n_mod_rows - b - 1, d), c.dtype)], axis=0)
    mod_all = _modulation(cc, w_mod, b_mod)
    tab_x = _rope_table_block(t, True)
    tab_c = _rope_table_block(t_ctx, False)
    s_zero = jnp.zeros((b, RET_HEADS // 2, 4, RET_DV, 2 * RET_DK), F32)

    weights = _prep_weights(w_in, w_uq, w_ukv)
    gq, gkv = mla_q_norm, mla_kv_norm
    dec = jnp.stack([ret_decay_fwd, ret_decay_bwd], axis=1).astype(F32)
    gn_g, gn_b = ret_gn_g, ret_gn_b
    conv = (conv_dw, conv_dw_b, conv_ln_g, conv_ln_b, conv_pw.astype(BF16), conv_pw_b)
    tail = (w_out, ln_g, ln_b)
    tm_x, tm_c = _tile(t, 1024), _tile(t_ctx, 256)
    tf_x, tf_c = _tile(t, 1024), _tile(t_ctx, 256)
    tq_x, tq_c = _tile(t, 256), _tile(t_ctx, 256)
    ch_x, ch_c = _tile(t, 256), _tile(t_ctx, 256)

    hc = ctx
    for l in range(depth):
        need_ctx = l < depth - 1
        pc = _proj(hc, mod_all, b, tab_c, weights, gq, gkv, conv, l, tm_c, keys_only=not need_ctx)
        px = _proj(x, mod_all, None, tab_x, weights, gq, gkv, conv, l, tm_x)

        oc_ret, s_ctx = _retention(dec, pc.get("rq"), pc["rk"], pc["rvt"], s_zero, gn_g, gn_b, pc.get("gate_ret"),
                                   l, ch_c)
        o_ret, _ = _retention(dec, px["rq"], px["rk"], px["rvt"], s_ctx, gn_g, gn_b, px["gate_ret"], l, ch_x)
        o_mla = _attention(px["q"], px["gate_mla"], [pc["k"], px["k"]], [pc["vt"], px["vt"]], tq_x)
        x_new = _final(x, mod_all, None, o_mla, o_ret, px["conv"], *tail, layer=l, tm=tf_x, alpha=alpha)
        if need_ctx:
            oc_mla = _attention(pc["q"], pc["gate_mla"], [pc["k"]], [pc["vt"]], tq_c)
            hc = _final(hc, mod_all, b, oc_mla, oc_ret, pc["conv"], *tail, layer=l, tm=tf_c, alpha=alpha)
        x = x_new
    return x
```

```python
import functools
import math

import jax
import jax.numpy as jnp
import numpy as np
from jax import lax
from jax.experimental import pallas as pl
from jax.experimental.pallas import tpu as pltpu

GRID_W = 64
MLA_HEADS = 8
MLA_NOPE = 64
MLA_ROPE = 32
MLA_V = 64
MLA_QK = MLA_NOPE + MLA_ROPE
MLA_WIDTH = MLA_HEADS * MLA_V
Q_LORA = 256
KV_LORA = 128
RET_HEADS = 4
RET_DK = 64
RET_DV = 64
RET_WIDTH = RET_HEADS * RET_DV
CONV_WIDTH = 256
CONV_K = 31
ROPE_BASE = 10000.0
EPS = 1e-5

LANES = 128
HEAD_SLOT = LANES
VMEM_LIMIT_BYTES = 56 * 1024 * 1024
LOG2E = 1.4426950408889634

F32 = jnp.float32
BF16 = jnp.bfloat16

_NT = (((1,), (1,)), ((), ()))


def _dot(a, b):
    return jnp.dot(a, b, preferred_element_type=F32)


def _dot_nt(a, b):
    return lax.dot_general(a, b, _NT, preferred_element_type=F32)


def _params(semantics):
    return pltpu.CompilerParams(dimension_semantics=semantics, vmem_limit_bytes=VMEM_LIMIT_BYTES)


def _rot_cols(w, unit):
    xp = np if isinstance(w, np.ndarray) else jnp
    q = unit // 4
    w5 = w.reshape(w.shape[:-1] + (w.shape[-1] // unit, 2, 2, q))
    return xp.stack([-w5[..., 1, :], w5[..., 0, :]], axis=-2).reshape(w.shape)


def _rope_tables(length, unit, reps):
    d2 = unit // 2
    t = np.arange(length, dtype=np.int32)
    inv = np.float32(ROPE_BASE) ** (-np.arange(0, d2, 2, dtype=np.float32) / np.float32(d2))

    def half(pos):
        ang = pos.astype(np.float32)[:, None] * inv[None, :]
        return np.concatenate([np.cos(ang)] * 2, axis=-1), np.concatenate([np.sin(ang)] * 2, axis=-1)

    cr, sr = half(t // GRID_W)
    cc, sc = half(t % GRID_W)
    cos = np.concatenate([cr, cc], axis=-1).astype(np.float32)
    sin = np.concatenate([sr, sc], axis=-1).astype(np.float32)
    return np.tile(cos, (1, reps)), np.tile(sin, (1, reps))


def _rope_table_block(length, rotate):
    if rotate:
        cos_r, sin_r = _rope_tables(length, RET_DK, RET_HEADS)
        cos_m, sin_m = _rope_tables(length, MLA_ROPE, MLA_HEADS)
    else:
        cos_r = np.ones((length, RET_WIDTH), np.float32)
        sin_r = np.zeros((length, RET_WIDTH), np.float32)
        cos_m = np.ones((length, MLA_HEADS * MLA_ROPE), np.float32)
        sin_m = np.zeros((length, MLA_HEADS * MLA_ROPE), np.float32)
    kslot = np.concatenate([cos_m[:, :MLA_ROPE], sin_m[:, :MLA_ROPE],
                            np.zeros((length, LANES - 2 * MLA_ROPE), np.float32)], axis=-1)
    return jnp.asarray(np.concatenate([cos_r, sin_r, cos_m, sin_m, kslot], axis=-1))


def _placement_matrices():
    q_place = np.zeros((MLA_HEADS, MLA_ROPE, MLA_HEADS, HEAD_SLOT), np.float32)
    k_place = np.zeros((LANES, MLA_HEADS, HEAD_SLOT), np.float32)
    for d in range(MLA_ROPE):
        for h in range(MLA_HEADS):
            q_place[h, d, h, MLA_NOPE + d] = 1.0
            k_place[d, h, MLA_NOPE + d] = 1.0
            k_place[MLA_ROPE + d, h, MLA_NOPE + d] = 1.0
    return (q_place.reshape(MLA_HEADS * MLA_ROPE, MLA_HEADS * HEAD_SLOT),
            k_place.reshape(LANES, MLA_HEADS * HEAD_SLOT))


_IN_SIZES = dict(q=Q_LORA, kv=KV_LORA, kr=MLA_ROPE, g_mla=MLA_WIDTH, rq=RET_HEADS * RET_DK, rk=RET_HEADS * RET_DK,
                 rv=RET_WIDTH, g_ret=RET_WIDTH, glu=2 * CONV_WIDTH, g_conv=CONV_WIDTH)


def _offsets(sizes):
    out, at = {}, 0
    for name, n in sizes.items():
        out[name] = (at, at + n)
        at += n
    return out


_IN_COLS = _offsets(_IN_SIZES)


def _prep_weights(w_in, w_uq, w_ukv):
    depth, d, _ = w_in.shape
    w_t = jnp.swapaxes(w_in, 1, 2).astype(BF16)
    wkr = w_in[..., _IN_COLS["kr"][0]:_IN_COLS["kr"][1]]
    w_ks = jnp.concatenate([wkr, _rot_cols(wkr, MLA_ROPE), jnp.zeros((depth, d, LANES - 2 * MLA_ROPE), w_in.dtype)],
                           axis=-1)
    w_ks = jnp.swapaxes(w_ks, 1, 2).astype(BF16)

    uq = w_uq.reshape(depth, Q_LORA, MLA_HEADS, MLA_QK)
    pad = ((0, 0), (0, 0), (0, 0), (0, HEAD_SLOT - MLA_NOPE))
    uq_nope = jnp.pad(uq[..., :MLA_NOPE], pad).reshape(depth, Q_LORA, -1)
    uq_rope = uq[..., MLA_NOPE:].reshape(depth, Q_LORA, MLA_HEADS * MLA_ROPE)
    w_q2 = jnp.concatenate([uq_rope, _rot_cols(uq_rope, MLA_ROPE)], axis=-1).astype(BF16)
    q_place, k_place = (jnp.broadcast_to(jnp.asarray(m, BF16), (depth,) + m.shape) for m in _placement_matrices())
    w_qcomb = jnp.concatenate([uq_nope.astype(BF16), q_place], axis=1)

    ukv = w_ukv.reshape(depth, KV_LORA, MLA_HEADS, MLA_NOPE + MLA_V)
    uk = jnp.pad(ukv[..., :MLA_NOPE], pad).reshape(depth, KV_LORA, -1)
    w_kcomb = jnp.concatenate([uk.astype(BF16), k_place], axis=1)
    w_uvt = jnp.swapaxes(ukv[..., MLA_NOPE:].reshape(depth, KV_LORA, MLA_WIDTH), 1, 2).astype(BF16)
    return w_t, w_ks, w_q2, w_qcomb, w_kcomb, w_uvt


def _mod_kernel(c_ref, w_ref, b_ref, o_ref):
    c = c_ref[...]
    a = (c * jax.nn.sigmoid(c)).astype(BF16)
    o_ref[0, 0] = _dot(a, w_ref[0].astype(BF16)) + b_ref[pl.ds(pl.program_id(0), 1), :]


def _modulation(cc, w_mod, b_mod):
    depth, d, d3 = w_mod.shape
    nblk = d3 // d
    return pl.pallas_call(
        _mod_kernel,
        grid=(depth, nblk),
        in_specs=[pl.BlockSpec(cc.shape, lambda l, j: (0, 0)),
                  pl.BlockSpec((1, d, d), lambda l, j: (l, 0, j)),
                  pl.BlockSpec((depth, d), lambda l, j: (0, j))],
        out_specs=pl.BlockSpec((1, 1, cc.shape[0], d), lambda l, j: (l, j, 0, 0)),
        out_shape=jax.ShapeDtypeStruct((depth, nblk, cc.shape[0], d), F32),
        compiler_params=_params(("arbitrary", "arbitrary")),
        name="mod",
    )(cc, w_mod, b_mod)


_T_CR, _T_SR, _T_CM, _T_SM, _T_KS, _T_END = 0, 256, 512, 768, 1024, 1152


_HALO = 16
_CONV_ROWS = 64


def _rms(x, g):
    return x * lax.rsqrt(jnp.mean(x * x, axis=-1, keepdims=True) + EPS) * g


def _proj_kernel(x_ref, xp_ref, xn_ref, mod_ref, tab_ref, rot_ref, wt_ref, wks_ref, gq_ref, gkv_ref, wq2_ref,
                 wqc_ref, wkc_ref, wuvt_ref, dw_ref, dwb_ref, clg_ref, clb_ref, pw_ref, pwb_ref, *out_refs, keys_only,
                 layer, mod_row):
    if keys_only:
        k_ref, vt_ref, rk_ref, rvt_ref = out_refs
    else:
        q_ref, k_ref, vt_ref, rq_ref, rk_ref, rvt_ref, gm_ref, gr_ref, oc_ref, upad, ushift = out_refs
    tm = x_ref.shape[1]
    shift = _mod_row(mod_ref, 0, mod_row, 1)
    scale = _mod_row(mod_ref, 1, mod_row, 1)
    pair = 2 * HEAD_SLOT
    vec = lambda ref: ref[layer:layer + 1, :]

    def modulate(x):
        mu = jnp.mean(x, axis=-1, keepdims=True)
        xc = x - mu
        var = jnp.mean(xc * xc, axis=-1, keepdims=True)
        return (xc * lax.rsqrt(var + EPS) * (1.0 + scale) + shift).astype(BF16)

    u = modulate(x_ref[0])

    def w_rows(name):
        lo, hi = _IN_COLS[name]
        return wt_ref[lo:hi, :]

    def seg(name):
        return _dot_nt(u, w_rows(name))

    def silu(v):
        return v * jax.nn.sigmoid(v)

    def store_pairs(ref, val):
        for p in range(MLA_HEADS // 2):
            ref[0, p] = val[:, p * pair:(p + 1) * pair]

    def keys(pkv, pks):
        nkv = _rms(pkv, vec(gkv_ref)).astype(BF16)
        ks = pks * tab_ref[:, _T_KS:_T_END]
        store_pairs(k_ref, _dot(jnp.concatenate([nkv, ks.astype(BF16)], axis=1), wkc_ref[...]).astype(BF16))
        vt_ref[0] = _dot_nt(wuvt_ref[...], nkv).astype(BF16)

    def ret_rope(raw):
        rot = _dot(raw.astype(BF16), rot_ref[...])
        return raw * tab_ref[:, _T_CR:_T_SR] + rot * tab_ref[:, _T_SR:_T_CM]

    def ret_keys(prk):
        rk_ref[0] = ret_rope(prk) * (RET_DK ** -0.5)
        rvt_ref[0] = _dot_nt(w_rows("rv"), u).astype(BF16)

    if keys_only:
        keys(seg("kv"), _dot_nt(u, wks_ref[...]))
        ret_keys(seg("rk"))
        return

    i = pl.program_id(0)
    u_ext = jnp.concatenate([modulate(xp_ref[0]), u, modulate(xn_ref[0])], axis=0)
    glu = _dot_nt(u_ext, w_rows("glu"))
    uc = glu[:, :CONV_WIDTH] * jax.nn.sigmoid(glu[:, CONV_WIDTH:])
    row = lax.broadcasted_iota(jnp.int32, uc.shape, 0)
    inside = ((row >= _HALO) | (i > 0)) & ((row < _HALO + tm) | (i < pl.num_programs(0) - 1))
    upad[...] = jnp.where(inside, uc, 0.0)
    span = tm + 2 * _HALO - 8
    for r in range(1, 8):
        ushift[r - 1] = upad[r:r + span, :]
    base = _HALO - CONV_K // 2
    conv_blocks = []

    def conv_rows(n_blocks):
        for _ in range(n_blocks):
            rb = len(conv_blocks) * _CONV_ROWS
            acc = jnp.zeros((_CONV_ROWS, CONV_WIDTH), F32) + vec(dwb_ref)
            for j in range(CONV_K):
                a, r = divmod(base + j, 8)
                src = upad if r == 0 else ushift.at[r - 1]
                acc = acc + src[8 * a + rb:8 * a + rb + _CONV_ROWS, :] * dw_ref[j:j + 1, :]
            mu = jnp.mean(acc, axis=-1, keepdims=True)
            ac = acc - mu
            var = jnp.mean(ac * ac, axis=-1, keepdims=True)
            yc = ac * lax.rsqrt(var + EPS) * vec(clg_ref) + vec(clb_ref)
            conv_blocks.append((yc * jax.nn.sigmoid(yc)).astype(BF16))

    n_conv = tm // _CONV_ROWS
    per_stage = -(-n_conv // 4)

    pq, pkv, pks = seg("q"), seg("kv"), _dot_nt(u, wks_ref[...])
    prq, prk = seg("rq"), seg("rk")
    conv_rows(min(per_stage, n_conv - len(conv_blocks)))

    g_mla = silu(seg("g_mla"))
    for p in range(MLA_HEADS // 2):
        gm_ref[0, p] = g_mla[:, p * LANES:(p + 1) * LANES].astype(gm_ref.dtype)
    gr_ref[0] = silu(seg("g_ret")).astype(gr_ref.dtype)
    g_conv = silu(seg("g_conv"))
    conv_rows(min(per_stage, n_conv - len(conv_blocks)))

    nq = _rms(pq, vec(gq_ref)).astype(BF16)
    qc = _dot(nq, wq2_ref[...])
    keys(pkv, pks)
    conv_rows(min(per_stage, n_conv - len(conv_blocks)))

    rq_ref[0] = ret_rope(prq)
    ret_keys(prk)

    nr = MLA_HEADS * MLA_ROPE
    q_rope = qc[:, :nr] * tab_ref[:, _T_CM:_T_SM] + qc[:, nr:] * tab_ref[:, _T_SM:_T_KS]
    q = _dot(jnp.concatenate([nq, q_rope.astype(BF16)], axis=1), wqc_ref[...])
    store_pairs(q_ref, (q * (MLA_QK ** -0.5 * LOG2E)).astype(BF16))
    conv_rows(n_conv - len(conv_blocks))

    o_conv = _dot(jnp.concatenate(conv_blocks, axis=0), pw_ref[...]) + vec(pwb_ref)
    oc_ref[0] = (o_conv * g_conv).astype(oc_ref.dtype)


def _layer_spec(a, layer):
    return pl.BlockSpec((None,) + a.shape[1:], lambda *_: (layer,) + (0,) * (a.ndim - 1),
                        pipeline_mode=pl.Buffered(1))


def _whole_spec(a):
    return pl.BlockSpec(a.shape, lambda *_: (0,) * a.ndim, pipeline_mode=pl.Buffered(1))


def _mod_row(mod_ref, k, shared_row, batch_axis):
    if shared_row is None:
        return mod_ref[k, pl.ds(pl.program_id(batch_axis), 1), :]
    return mod_ref[k, shared_row:shared_row + 1, :]


def _proj(x, mod_all, mod_row, tab, weights, gq, gkv, conv, layer, tm, keys_only=False):
    b, t, d = x.shape
    w_t, w_ks, w_q2, w_qcomb, w_kcomb, w_uvt = weights
    const = lambda a: _layer_spec(a, layer)
    row = lambda n: pl.BlockSpec((1, tm, n), lambda i, j: (j, i, 0))
    col = lambda n: pl.BlockSpec((1, n, tm), lambda i, j: (j, 0, i))
    n_pair, pair = MLA_HEADS // 2, 2 * HEAD_SLOT
    pairs = pl.BlockSpec((1, n_pair, tm, pair), lambda i, j: (j, 0, i, 0))
    per_tile = tm // _HALO
    last = t // _HALO - 1
    halo_prev = pl.BlockSpec((1, _HALO, d), lambda i, j: (j, jnp.maximum(i * per_tile - 1, 0), 0))
    halo_next = pl.BlockSpec((1, _HALO, d), lambda i, j: (j, jnp.minimum((i + 1) * per_tile, last), 0))
    rot = jnp.asarray(_rot_cols(np.eye(RET_WIDTH, dtype=np.float32), RET_DK), BF16)
    sds = jax.ShapeDtypeStruct
    outs = dict(q=(sds((b, n_pair, t, pair), BF16), pairs),
                k=(sds((b, n_pair, t, pair), BF16), pairs),
                vt=(sds((b, MLA_WIDTH, t), BF16), col(MLA_WIDTH)),
                rq=(sds((b, t, RET_WIDTH), F32), row(RET_WIDTH)),
                rk=(sds((b, t, RET_WIDTH), F32), row(RET_WIDTH)),
                rvt=(sds((b, RET_WIDTH, t), BF16), col(RET_WIDTH)),
                gate_mla=(sds((b, n_pair, t, 2 * MLA_V), BF16),
                          pl.BlockSpec((1, n_pair, tm, 2 * MLA_V), lambda i, j: (j, 0, i, 0))),
                gate_ret=(sds((b, t, RET_WIDTH), BF16), row(RET_WIDTH)),
                conv=(sds((b, t, CONV_WIDTH), BF16), row(CONV_WIDTH)))
    names = ("k", "vt", "rk", "rvt") if keys_only else tuple(outs)
    scratch = [] if keys_only else [pltpu.VMEM((tm + 2 * _HALO, CONV_WIDTH), F32),
                                    pltpu.VMEM((7, tm + 2 * _HALO - 8, CONV_WIDTH), F32)]
    res = pl.pallas_call(
        functools.partial(_proj_kernel, keys_only=keys_only, layer=layer, mod_row=mod_row),
        grid=(t // tm, b),
        in_specs=[row(d), halo_prev, halo_next,
                  const(mod_all),
                  pl.BlockSpec((tm, _T_END), lambda i, j: (i, 0)),
                  pl.BlockSpec(rot.shape, lambda i, j: (0, 0)),
                  const(w_t), const(w_ks), _whole_spec(gq), _whole_spec(gkv), const(w_q2), const(w_qcomb),
                  const(w_kcomb), const(w_uvt)] + [const(a) if a.ndim == 3 else _whole_spec(a) for a in conv],
        out_specs=tuple(outs[n][1] for n in names),
        out_shape=tuple(outs[n][0] for n in names),
        scratch_shapes=scratch,
        compiler_params=_params(("arbitrary", "arbitrary")),
        name="proj",
    )(x, x, x, mod_all, tab, rot, w_t, w_ks, gq, gkv, w_q2, w_qcomb, w_kcomb, w_uvt, *conv)
    res = dict(zip(names, res))
    res["vt"] = res["vt"].reshape(b, n_pair, 2 * MLA_V, t)
    return res


_ATTN_KEY_BLOCK = 512


def _attn_kernel(*refs, n_seg, tq):
    q_ref, g_ref = refs[:2]
    k_refs = refs[2:2 + n_seg]
    vt_refs = refs[2 + n_seg:2 + 2 * n_seg]
    o_ref = refs[2 + 2 * n_seg]
    s_bufs = refs[3 + 2 * n_seg:5 + 2 * n_seg]
    m_bufs = refs[5 + 2 * n_seg:7 + 2 * n_seg]
    n_pair = q_ref.shape[1]
    n_q = q_ref.shape[2] // tq
    n_items = n_pair * n_q
    lanes = [slice(j * HEAD_SLOT, (j + 1) * HEAD_SLOT) for j in range(2)]
    rows = [slice(j * MLA_V, (j + 1) * MLA_V) for j in range(2)]

    blocks, off = [], 0
    for si in range(n_seg):
        ts = k_refs[si].shape[2]
        kb = min(_ATTN_KEY_BLOCK, ts)
        blocks += [(si, b0, kb, off + b0) for b0 in range(0, ts, kb)]
        off += ts

    def locate(item):
        if isinstance(item, int):
            return item // n_q, (item % n_q) * tq
        pair = lax.div(item, jnp.int32(n_q))
        return pair, pl.multiple_of((item - pair * n_q) * tq, tq)

    def phase(nxt, cur):
        if nxt is not None:
            item_n, s_n, m_n = nxt
            pn, rn = locate(item_n)
            qn = [q_ref[0, pn, pl.ds(rn, tq), lanes[j]] for j in range(2)]
            mx = [None, None]
        if cur is not None:
            item_c, s_c, m_c = cur
            pc, rc = locate(item_c)
            mc = [m_c[j] for j in range(2)]
            acc, l = [None, None], [None, None]
        for si, b0, kb, o0 in blocks:
            for j in range(2):
                if nxt is not None:
                    s = _dot_nt(k_refs[si][0, pn, b0:b0 + kb, lanes[j]], qn[j])
                    s_n[j, o0:o0 + kb, :] = s
                    mb = jnp.max(s, axis=0, keepdims=True)
                    mx[j] = mb if mx[j] is None else jnp.maximum(mx[j], mb)
                if cur is not None:
                    p = jnp.exp2(s_c[j, o0:o0 + kb, :] - mc[j])
                    ls = jnp.sum(p, axis=0, keepdims=True)
                    pv = _dot(vt_refs[si][0, pc, rows[j], b0:b0 + kb], p.astype(BF16))
                    l[j] = ls if l[j] is None else l[j] + ls
                    acc[j] = pv if acc[j] is None else acc[j] + pv
        if nxt is not None:
            for j in range(2):
                m_n[j] = mx[j]
        if cur is not None:
            outs = [acc[j] * (1.0 / l[j]) for j in range(2)]
            gate = g_ref[0, pc, pl.ds(rc, tq), :].astype(F32)
            o_ref[0, pc, pl.ds(rc, tq), :] = (jnp.concatenate(outs, axis=0).T * gate).astype(o_ref.dtype)

    buf = [(s_bufs[0], m_bufs[0]), (s_bufs[1], m_bufs[1])]
    phase((0, *buf[0]), None)
    assert n_items % 2 == 0

    def body(i, carry):
        n = 2 * i
        phase((n + 1, *buf[1]), (n, *buf[0]))
        phase((jnp.minimum(n + 2, n_items - 1), *buf[0]), (n + 1, *buf[1]))
        return carry

    lax.fori_loop(0, n_items // 2, body, 0)


def _attention(q, gate, ks, vts, tq):
    b, n_pair, t, pair = q.shape
    n_seg = len(ks)
    n_keys = sum(k.shape[2] for k in ks)
    whole = lambda a: pl.BlockSpec((1,) + a.shape[1:], lambda i: (i, 0, 0, 0))
    score_buf = pltpu.VMEM((2, n_keys, tq), F32)
    max_buf = pltpu.VMEM((2, 1, tq), F32)
    return pl.pallas_call(
        functools.partial(_attn_kernel, n_seg=n_seg, tq=tq),
        grid=(b,),
        in_specs=[whole(q), whole(gate)] + [whole(k) for k in ks] + [whole(v) for v in vts],
        out_specs=pl.BlockSpec((1, n_pair, t, 2 * MLA_V), lambda i: (i, 0, 0, 0)),
        out_shape=jax.ShapeDtypeStruct((b, n_pair, t, 2 * MLA_V), BF16),
        scratch_shapes=[score_buf, score_buf, max_buf, max_buf],
        compiler_params=_params(("arbitrary",)),
        name="attention",
    )(q, gate, *ks, *vts)


def _log_sigmoid(x):
    return jnp.minimum(x, 0.0) - jnp.log1p(jnp.exp(-jnp.abs(x)))


def _ret_kernel(dec_ref, *refs, chunk, layer, ctx_out):
    if ctx_out:
        (cq_ref, ck_ref, cvt_ref, cgate_ref, q_ref, k_ref, vt_ref, gate_ref, gng_ref, gnb_ref, oc_ref, o_ref,
         sf_scr, sb_scr, kd_scr, qd_scr, dt_scr, cc_scr) = refs
    else:
        (ck_ref, cvt_ref, q_ref, k_ref, vt_ref, gate_ref, gng_ref, gnb_ref, o_ref,
         sf_scr, sb_scr, kd_scr, qd_scr, dt_scr, cc_scr) = refs
        cq_ref = cgate_ref = oc_ref = None
    c_len = chunk
    hp = pl.program_id(0)
    chunks = []
    for q_r, k_r, vt_r, g_r, o_r in ((cq_ref, ck_ref, cvt_ref, cgate_ref, oc_ref),
                                     (q_ref, k_ref, vt_ref, gate_ref, o_ref)):
        for c in range(k_r.shape[1] // c_len):
            chunks.append(dict(q=q_r, k=k_r, vt=vt_r, gate=g_r, out=o_r, c0=c * c_len, idx=len(chunks)))
    n_ctx = ck_ref.shape[1] // c_len

    @pl.when(pl.program_id(1) == 0)
    def _():
        r = lax.broadcasted_iota(jnp.int32, (c_len, LANES), 0).astype(F32)
        lane = lax.broadcasted_iota(jnp.int32, (c_len, LANES), 1)
        km = lax.broadcasted_iota(jnp.int32, (c_len, c_len), 0)
        qn = lax.broadcasted_iota(jnp.int32, (c_len, c_len), 1)
        diff = (qn - km).astype(F32)
        for j in range(2):
            h = hp * 2 + j
            lg_f = _log_sigmoid(jnp.full((c_len, LANES), dec_ref[layer, 0, h], F32))
            lg_b = _log_sigmoid(jnp.full((c_len, LANES), dec_ref[layer, 1, h], F32))
            own = ((lane >= j * RET_DK) & (lane < (j + 1) * RET_DK)).astype(F32)
            lg_f2 = _log_sigmoid(jnp.full((c_len, c_len), dec_ref[layer, 0, h], F32))
            lg_b2 = _log_sigmoid(jnp.full((c_len, c_len), dec_ref[layer, 1, h], F32))
            kd_scr[j] = jnp.concatenate([jnp.exp(lg_f * (c_len - 1.0 - r)) * own, jnp.exp(lg_b * r) * own, own], axis=1)
            qd_scr[j] = jnp.concatenate([jnp.exp(lg_f * (r + 1.0)), jnp.exp(lg_b * (c_len - r))], axis=1)
            dt_scr[j] = jnp.exp(jnp.where(diff >= 0, lg_f2 * diff, -lg_b2 * diff))
            cc_scr[j] = jnp.concatenate(
                [jnp.exp(_log_sigmoid(jnp.full((RET_DV, LANES), dec_ref[layer, 0, h], F32)) * c_len),
                 jnp.exp(_log_sigmoid(jnp.full((RET_DV, LANES), dec_ref[layer, 1, h], F32)) * c_len)], axis=1)

    heads = []
    for j in range(2):
        heads.append(dict(
            kdf=kd_scr[j, :, 0:LANES],
            kdb=kd_scr[j, :, LANES:2 * LANES],
            own=kd_scr[j, :, 2 * LANES:3 * LANES],
            qd=qd_scr[j],
            dt=dt_scr[j],
            cf=cc_scr[j, :, 0:LANES],
            cb=cc_scr[j, :, LANES:2 * LANES],
            rows=slice(j * RET_DV, (j + 1) * RET_DV),
        ))

    def rows(ref, ch):
        return ref[0, ch["c0"]:ch["c0"] + c_len, :]

    def vt_rows(ch, hd):
        return ch["vt"][0, hd["rows"], ch["c0"]:ch["c0"] + c_len]

    for ch in chunks:
        kc = rows(ch["k"], ch)
        for j, hd in enumerate(heads):
            sf_scr[j, ch["idx"]] = _dot(vt_rows(ch, hd), (kc * hd["kdf"]).astype(BF16))
            sb_scr[j, ch["idx"]] = _dot(vt_rows(ch, hd), (kc * hd["kdb"]).astype(BF16))

    backward_order = list(reversed(chunks[:n_ctx])) + list(reversed(chunks[n_ctx:]))
    for j, hd in enumerate(heads):
        for scr, decay, order in ((sf_scr, hd["cf"], chunks), (sb_scr, hd["cb"], backward_order)):
            state = jnp.zeros((RET_DV, LANES), F32)
            for ch in order:
                own = scr[j, ch["idx"]]
                scr[j, ch["idx"]] = state
                state = decay * state + own

    items = []
    for ch in chunks:
        if ch["q"] is not None:
            qc = rows(ch["q"], ch)
            items.append(dict(ch=ch, kc=rows(ch["k"], ch), qb=qc.astype(BF16), qq=jnp.concatenate([qc, qc], axis=1)))
    work = [(it, j, hd) for it in items for j, hd in enumerate(heads)]
    st = [_dot_nt((it["kc"] * hd["own"]).astype(BF16), it["qb"]) for it, j, hd in work]
    cross = [_dot_nt(jnp.concatenate([sf_scr[j, it["ch"]["idx"]], sb_scr[j, it["ch"]["idx"]]], axis=1).astype(BF16),
                     (it["qq"] * hd["qd"]).astype(BF16)) for it, j, hd in work]
    o = [_dot(vt_rows(it["ch"], hd), (s * hd["dt"]).astype(BF16)) + x for (it, j, hd), s, x in zip(work, st, cross)]
    ys = []
    for v in o:
        mu = jnp.mean(v, axis=0, keepdims=True)
        vc = v - mu
        var = jnp.mean(vc * vc, axis=0, keepdims=True)
        ys.append(vc * lax.rsqrt(var + EPS))
    for g, it in enumerate(items):
        ch = it["ch"]
        y = jnp.concatenate(ys[2 * g:2 * g + 2], axis=0).T
        y = y * gng_ref[layer:layer + 1, :] + gnb_ref[layer:layer + 1, :]
        ch["out"][0, ch["c0"]:ch["c0"] + c_len, :] = (y * rows(ch["gate"], ch).astype(F32)).astype(ch["out"].dtype)


def _retention(dec, ctx, lat, gn_g, gn_b, layer, chunk):
    b, t, _ = lat["rk"].shape
    t_ctx = ctx["rk"].shape[1]
    ctx_out = "rq" in ctx
    n_pair = RET_HEADS // 2
    pair = 2 * RET_DK
    state_buf = pltpu.VMEM((2, (t + t_ctx) // chunk, RET_DV, pair), F32)
    tables = [pltpu.VMEM((2, chunk, 3 * LANES), F32), pltpu.VMEM((2, chunk, 2 * LANES), F32),
              pltpu.VMEM((2, chunk, chunk), F32), pltpu.VMEM((2, RET_DV, 2 * LANES), F32)]
    seq = lambda n: pl.BlockSpec((1, n, pair), lambda h, i: (i, 0, h))
    vts = lambda n: pl.BlockSpec((1, 2 * RET_DV, n), lambda h, i: (i, h, 0))
    affine = pl.BlockSpec((gn_g.shape[0], pair), lambda h, i: (0, h))
    smem = pl.BlockSpec(memory_space=pltpu.SMEM)
    out = lambda n: jax.ShapeDtypeStruct((b, n, RET_WIDTH), BF16)
    if ctx_out:
        in_specs = [smem, seq(t_ctx), seq(t_ctx), vts(t_ctx), seq(t_ctx)]
        args = (dec, ctx["rq"], ctx["rk"], ctx["rvt"], ctx["gate_ret"])
        out_specs, out_shape = (seq(t_ctx), seq(t)), (out(t_ctx), out(t))
    else:
        in_specs, args = [smem, seq(t_ctx), vts(t_ctx)], (dec, ctx["rk"], ctx["rvt"])
        out_specs, out_shape = seq(t), out(t)
    in_specs += [seq(t), seq(t), vts(t), seq(t), affine, affine]
    args += (lat["rq"], lat["rk"], lat["rvt"], lat["gate_ret"], gn_g, gn_b)
    res = pl.pallas_call(
        functools.partial(_ret_kernel, chunk=chunk, layer=layer, ctx_out=ctx_out),
        grid=(n_pair, b),
        in_specs=in_specs,
        out_specs=out_specs,
        out_shape=out_shape,
        scratch_shapes=[state_buf, state_buf] + tables,
        compiler_params=_params(("arbitrary", "arbitrary")),
        name="retention",
    )(*args)
    return res if ctx_out else (None, res)


def _final_kernel(x_ref, mod_ref, om_ref, or_ref, oc_ref, wo_ref, lng_ref, lnb_ref, o_ref, *, alpha, layer, mod_row):
    o_cat = jnp.concatenate([om_ref[0, p] for p in range(om_ref.shape[1])] + [or_ref[0], oc_ref[0]], axis=1)
    y = _dot(o_cat, wo_ref[...].astype(BF16))

    v = alpha * x_ref[0] + _mod_row(mod_ref, 2, mod_row, 0) * y
    mu = jnp.mean(v, axis=-1, keepdims=True)
    vc = v - mu
    var = jnp.mean(vc * vc, axis=-1, keepdims=True)
    o_ref[0] = vc * lax.rsqrt(var + EPS) * lng_ref[layer:layer + 1, :] + lnb_ref[layer:layer + 1, :]


def _final(x, mod_all, mod_row, o_mla, o_ret, o_conv, w_out, ln_g, ln_b, layer, tm, alpha):
    b, t, d = x.shape
    const = lambda a: _layer_spec(a, layer)
    row = lambda n: pl.BlockSpec((1, tm, n), lambda i, j: (i, j, 0))
    return pl.pallas_call(
        functools.partial(_final_kernel, alpha=alpha, layer=layer, mod_row=mod_row),
        grid=(b, t // tm),
        in_specs=[row(d),
                  const(mod_all),
                  pl.BlockSpec((1, o_mla.shape[1], tm, o_mla.shape[3]), lambda i, j: (i, 0, j, 0)),
                  row(RET_WIDTH), row(CONV_WIDTH),
                  const(w_out), _whole_spec(ln_g), _whole_spec(ln_b)],
        out_specs=row(d),
        out_shape=jax.ShapeDtypeStruct((b, t, d), F32),
        compiler_params=_params(("arbitrary", "arbitrary")),
        name="final",
    )(x, mod_all, o_mla, o_ret, o_conv, w_out, ln_g, ln_b)


def _tile(n, pref):
    return pref if n % pref == 0 else n


def kernel(x, c, ctx, c_ctx, w_mod, b_mod, w_in, mla_q_norm, w_uq, mla_kv_norm, w_ukv, ret_decay_fwd, ret_decay_bwd,
           ret_gn_g, ret_gn_b, conv_dw, conv_dw_b, conv_ln_g, conv_ln_b, conv_pw, conv_pw_b, w_out, ln_g, ln_b):
    depth = w_mod.shape[0]
    b, t, d = x.shape
    t_ctx = ctx.shape[1]
    alpha = (2 * depth) ** 0.25
    n_mod_rows = 16
    cc = jnp.concatenate([c, c_ctx[None, :], jnp.zeros((n_mod_rows - b - 1, d), c.dtype)], axis=0)
    mod_all = _modulation(cc, w_mod, b_mod)
    tab_x = _rope_table_block(t, True)
    tab_c = _rope_table_block(t_ctx, False)

    weights = _prep_weights(w_in, w_uq, w_ukv)
    gq, gkv = mla_q_norm, mla_kv_norm
    dec = jnp.stack([ret_decay_fwd, ret_decay_bwd], axis=1).astype(F32)
    gn_g, gn_b = ret_gn_g, ret_gn_b
    conv = (conv_dw, conv_dw_b, conv_ln_g, conv_ln_b, conv_pw.astype(BF16), conv_pw_b)
    tail = (w_out, ln_g, ln_b)
    tm_x, tm_c = _tile(t, 1024), _tile(t_ctx, 256)
    tf_x, tf_c = _tile(t, 1024), _tile(t_ctx, 256)
    tq_x, tq_c = _tile(t, 256), _tile(t_ctx, 256)
    chunk = math.gcd(math.gcd(t, t_ctx), 256)

    hc = ctx
    for l in range(depth):
        need_ctx = l < depth - 1
        pc = _proj(hc, mod_all, b, tab_c, weights, gq, gkv, conv, l, tm_c, keys_only=not need_ctx)
        px = _proj(x, mod_all, None, tab_x, weights, gq, gkv, conv, l, tm_x)

        oc_ret, o_ret = _retention(dec, pc, px, gn_g, gn_b, l, chunk)
        o_mla = _attention(px["q"], px["gate_mla"], [pc["k"], px["k"]], [pc["vt"], px["vt"]], tq_x)
        x_new = _final(x, mod_all, None, o_mla, o_ret, px["conv"], *tail, layer=l, tm=tf_x, alpha=alpha)
        if need_ctx:
            oc_mla = _attention(pc["q"], pc["gate_mla"], [pc["k"]], [pc["vt"]], tq_c)
            hc = _final(hc, mod_all, b, oc_mla, oc_ret, pc["conv"], *tail, layer=l, tm=tf_c, alpha=alpha)
        x = x_new
    return x
```

```python
import functools
import math

import jax
import jax.numpy as jnp
import numpy as np
from jax import lax
from jax.experimental import pallas as pl
from jax.experimental.pallas import tpu as pltpu

GRID_W = 64
MLA_HEADS = 8
MLA_NOPE = 64
MLA_ROPE = 32
MLA_V = 64
MLA_QK = MLA_NOPE + MLA_ROPE
MLA_WIDTH = MLA_HEADS * MLA_V
Q_LORA = 256
KV_LORA = 128
RET_HEADS = 4
RET_DK = 64
RET_DV = 64
RET_WIDTH = RET_HEADS * RET_DV
CONV_WIDTH = 256
CONV_K = 31
ROPE_BASE = 10000.0
EPS = 1e-5

LANES = 128
HEAD_SLOT = LANES
VMEM_LIMIT_BYTES = 56 * 1024 * 1024
LOG2E = 1.4426950408889634

F32 = jnp.float32
BF16 = jnp.bfloat16

_NT = (((1,), (1,)), ((), ()))


def _dot(a, b):
    return jnp.dot(a, b, preferred_element_type=F32)


def _dot_nt(a, b):
    return lax.dot_general(a, b, _NT, preferred_element_type=F32)


def _params(semantics):
    return pltpu.CompilerParams(dimension_semantics=semantics, vmem_limit_bytes=VMEM_LIMIT_BYTES)


def _rot_cols(w, unit):
    xp = np if isinstance(w, np.ndarray) else jnp
    q = unit // 4
    w5 = w.reshape(w.shape[:-1] + (w.shape[-1] // unit, 2, 2, q))
    return xp.stack([-w5[..., 1, :], w5[..., 0, :]], axis=-2).reshape(w.shape)


def _rope_tables(length, unit, reps):
    d2 = unit // 2
    t = np.arange(length, dtype=np.int32)
    inv = np.float32(ROPE_BASE) ** (-np.arange(0, d2, 2, dtype=np.float32) / np.float32(d2))

    def half(pos):
        ang = pos.astype(np.float32)[:, None] * inv[None, :]
        return np.concatenate([np.cos(ang)] * 2, axis=-1), np.concatenate([np.sin(ang)] * 2, axis=-1)

    cr, sr = half(t // GRID_W)
    cc, sc = half(t % GRID_W)
    cos = np.concatenate([cr, cc], axis=-1).astype(np.float32)
    sin = np.concatenate([sr, sc], axis=-1).astype(np.float32)
    return np.tile(cos, (1, reps)), np.tile(sin, (1, reps))


def _rope_table_block(length, rotate):
    if rotate:
        cos_r, sin_r = _rope_tables(length, RET_DK, RET_HEADS)
        cos_m, sin_m = _rope_tables(length, MLA_ROPE, MLA_HEADS)
    else:
        cos_r = np.ones((length, RET_WIDTH), np.float32)
        sin_r = np.zeros((length, RET_WIDTH), np.float32)
        cos_m = np.ones((length, MLA_HEADS * MLA_ROPE), np.float32)
        sin_m = np.zeros((length, MLA_HEADS * MLA_ROPE), np.float32)
    kslot = np.concatenate([cos_m[:, :MLA_ROPE], sin_m[:, :MLA_ROPE],
                            np.zeros((length, LANES - 2 * MLA_ROPE), np.float32)], axis=-1)
    return jnp.asarray(np.concatenate([cos_r, sin_r, cos_m, sin_m, kslot], axis=-1))


def _placement_matrices():
    q_place = np.zeros((MLA_HEADS, MLA_ROPE, MLA_HEADS, HEAD_SLOT), np.float32)
    k_place = np.zeros((LANES, MLA_HEADS, HEAD_SLOT), np.float32)
    for d in range(MLA_ROPE):
        for h in range(MLA_HEADS):
            q_place[h, d, h, MLA_NOPE + d] = 1.0
            k_place[d, h, MLA_NOPE + d] = 1.0
            k_place[MLA_ROPE + d, h, MLA_NOPE + d] = 1.0
    return (q_place.reshape(MLA_HEADS * MLA_ROPE, MLA_HEADS * HEAD_SLOT),
            k_place.reshape(LANES, MLA_HEADS * HEAD_SLOT))


_IN_SIZES = dict(q=Q_LORA, kv=KV_LORA, kr=MLA_ROPE, g_mla=MLA_WIDTH, rq=RET_HEADS * RET_DK, rk=RET_HEADS * RET_DK,
                 rv=RET_WIDTH, g_ret=RET_WIDTH, glu=2 * CONV_WIDTH, g_conv=CONV_WIDTH)


def _offsets(sizes):
    out, at = {}, 0
    for name, n in sizes.items():
        out[name] = (at, at + n)
        at += n
    return out


_IN_COLS = _offsets(_IN_SIZES)


def _prep_weights(w_in, w_uq, w_ukv):
    depth, d, _ = w_in.shape
    w_t = jnp.swapaxes(w_in, 1, 2).astype(BF16)
    wkr = w_in[..., _IN_COLS["kr"][0]:_IN_COLS["kr"][1]]
    w_ks = jnp.concatenate([wkr, _rot_cols(wkr, MLA_ROPE), jnp.zeros((depth, d, LANES - 2 * MLA_ROPE), w_in.dtype)],
                           axis=-1)
    w_ks = jnp.swapaxes(w_ks, 1, 2).astype(BF16)

    uq = w_uq.reshape(depth, Q_LORA, MLA_HEADS, MLA_QK)
    pad = ((0, 0), (0, 0), (0, 0), (0, HEAD_SLOT - MLA_NOPE))
    uq_nope = jnp.pad(uq[..., :MLA_NOPE], pad).reshape(depth, Q_LORA, -1)
    uq_rope = uq[..., MLA_NOPE:].reshape(depth, Q_LORA, MLA_HEADS * MLA_ROPE)
    w_q2 = jnp.concatenate([uq_rope, _rot_cols(uq_rope, MLA_ROPE)], axis=-1).astype(BF16)
    q_place, k_place = (jnp.broadcast_to(jnp.asarray(m, BF16), (depth,) + m.shape) for m in _placement_matrices())
    w_qcomb = jnp.concatenate([uq_nope.astype(BF16), q_place], axis=1)

    ukv = w_ukv.reshape(depth, KV_LORA, MLA_HEADS, MLA_NOPE + MLA_V)
    uk = jnp.pad(ukv[..., :MLA_NOPE], pad).reshape(depth, KV_LORA, -1)
    w_kcomb = jnp.concatenate([uk.astype(BF16), k_place], axis=1)
    w_uvt = jnp.swapaxes(ukv[..., MLA_NOPE:].reshape(depth, KV_LORA, MLA_WIDTH), 1, 2).astype(BF16)
    return w_t, w_ks, w_q2, w_qcomb, w_kcomb, w_uvt


def _mod_kernel(c_ref, w_ref, b_ref, o_ref):
    c = c_ref[...]
    a = (c * jax.nn.sigmoid(c)).astype(BF16)
    o_ref[0, 0] = _dot(a, w_ref[0].astype(BF16)) + b_ref[pl.ds(pl.program_id(0), 1), :]


def _modulation(cc, w_mod, b_mod):
    depth, d, d3 = w_mod.shape
    nblk = d3 // d
    return pl.pallas_call(
        _mod_kernel,
        grid=(depth, nblk),
        in_specs=[pl.BlockSpec(cc.shape, lambda l, j: (0, 0)),
                  pl.BlockSpec((1, d, d), lambda l, j: (l, 0, j)),
                  pl.BlockSpec((depth, d), lambda l, j: (0, j))],
        out_specs=pl.BlockSpec((1, 1, cc.shape[0], d), lambda l, j: (l, j, 0, 0)),
        out_shape=jax.ShapeDtypeStruct((depth, nblk, cc.shape[0], d), F32),
        compiler_params=_params(("arbitrary", "arbitrary")),
        name="mod",
    )(cc, w_mod, b_mod)


_T_CR = 0
_T_SR = _T_CR + RET_WIDTH
_T_CM = _T_SR + RET_WIDTH
_T_SM = _T_CM + MLA_HEADS * MLA_ROPE
_T_KS = _T_SM + MLA_HEADS * MLA_ROPE
_T_END = _T_KS + LANES


_HALO = 16
_CONV_ROWS = 64


def _rms(x, g):
    return x * lax.rsqrt(jnp.mean(x * x, axis=-1, keepdims=True) + EPS) * g


def _proj_kernel(x_ref, xp_ref, xn_ref, mod_ref, tab_ref, rot_ref, wt_ref, wks_ref, gq_ref, gkv_ref, wq2_ref,
                 wqc_ref, wkc_ref, wuvt_ref, dw_ref, dwb_ref, clg_ref, clb_ref, pw_ref, pwb_ref, *out_refs, keys_only,
                 layer, mod_row):
    if keys_only:
        k_ref, vt_ref, rk_ref, rvt_ref = out_refs
    else:
        q_ref, k_ref, vt_ref, rq_ref, rk_ref, rvt_ref, gm_ref, gr_ref, oc_ref, upad, ushift = out_refs
    tm = x_ref.shape[1]
    shift = _mod_row(mod_ref, 0, mod_row, 1)
    scale = _mod_row(mod_ref, 1, mod_row, 1)
    pair = 2 * HEAD_SLOT
    vec = lambda ref: ref[layer:layer + 1, :]

    def modulate(x):
        mu = jnp.mean(x, axis=-1, keepdims=True)
        xc = x - mu
        var = jnp.mean(xc * xc, axis=-1, keepdims=True)
        return (xc * lax.rsqrt(var + EPS) * (1.0 + scale) + shift).astype(BF16)

    u = modulate(x_ref[0])

    def w_rows(name):
        lo, hi = _IN_COLS[name]
        return wt_ref[lo:hi, :]

    def seg(name):
        return _dot_nt(u, w_rows(name))

    def silu(v):
        return v * jax.nn.sigmoid(v)

    def store_pairs(ref, val):
        for p in range(MLA_HEADS // 2):
            ref[0, p] = val[:, p * pair:(p + 1) * pair]

    def keys(pkv, pks):
        nkv = _rms(pkv, vec(gkv_ref)).astype(BF16)
        ks = pks * tab_ref[:, _T_KS:_T_END]
        store_pairs(k_ref, _dot(jnp.concatenate([nkv, ks.astype(BF16)], axis=1), wkc_ref[...]).astype(BF16))
        vt_ref[0] = _dot_nt(wuvt_ref[...], nkv).astype(BF16)

    def ret_rope(raw):
        rot = _dot(raw.astype(BF16), rot_ref[...])
        return raw * tab_ref[:, _T_CR:_T_SR] + rot * tab_ref[:, _T_SR:_T_CM]

    def ret_keys(prk):
        rk_ref[0] = ret_rope(prk) * (RET_DK ** -0.5)
        rvt_ref[0] = _dot_nt(w_rows("rv"), u).astype(BF16)

    if keys_only:
        keys(seg("kv"), _dot_nt(u, wks_ref[...]))
        ret_keys(seg("rk"))
        return

    i = pl.program_id(0)
    u_ext = jnp.concatenate([modulate(xp_ref[0]), u, modulate(xn_ref[0])], axis=0)
    glu = _dot_nt(u_ext, w_rows("glu"))
    uc = glu[:, :CONV_WIDTH] * jax.nn.sigmoid(glu[:, CONV_WIDTH:])
    row = lax.broadcasted_iota(jnp.int32, uc.shape, 0)
    inside = ((row >= _HALO) | (i > 0)) & ((row < _HALO + tm) | (i < pl.num_programs(0) - 1))
    upad[...] = jnp.where(inside, uc, 0.0)
    span = tm + 2 * _HALO - 8
    for r in range(1, 8):
        ushift[r - 1] = upad[r:r + span, :]
    base = _HALO - CONV_K // 2
    conv_blocks = []

    def conv_rows(n_blocks):
        for _ in range(n_blocks):
            rb = len(conv_blocks) * _CONV_ROWS
            acc = jnp.zeros((_CONV_ROWS, CONV_WIDTH), F32) + vec(dwb_ref)
            for j in range(CONV_K):
                a, r = divmod(base + j, 8)
                src = upad if r == 0 else ushift.at[r - 1]
                acc = acc + src[8 * a + rb:8 * a + rb + _CONV_ROWS, :] * dw_ref[j:j + 1, :]
            mu = jnp.mean(acc, axis=-1, keepdims=True)
            ac = acc - mu
            var = jnp.mean(ac * ac, axis=-1, keepdims=True)
            yc = ac * lax.rsqrt(var + EPS) * vec(clg_ref) + vec(clb_ref)
            conv_blocks.append((yc * jax.nn.sigmoid(yc)).astype(BF16))

    n_conv = tm // _CONV_ROWS
    per_stage = -(-n_conv // 4)

    pq, pkv, pks = seg("q"), seg("kv"), _dot_nt(u, wks_ref[...])
    prq, prk = seg("rq"), seg("rk")
    conv_rows(min(per_stage, n_conv - len(conv_blocks)))

    g_mla = silu(seg("g_mla"))
    for p in range(MLA_HEADS // 2):
        gm_ref[0, p] = g_mla[:, p * LANES:(p + 1) * LANES].astype(gm_ref.dtype)
    gr_ref[0] = silu(seg("g_ret")).astype(gr_ref.dtype)
    g_conv = silu(seg("g_conv"))
    conv_rows(min(per_stage, n_conv - len(conv_blocks)))

    nq = _rms(pq, vec(gq_ref)).astype(BF16)
    qc = _dot(nq, wq2_ref[...])
    keys(pkv, pks)
    conv_rows(min(per_stage, n_conv - len(conv_blocks)))

    rq_ref[0] = ret_rope(prq)
    ret_keys(prk)

    nr = MLA_HEADS * MLA_ROPE
    q_rope = qc[:, :nr] * tab_ref[:, _T_CM:_T_SM] + qc[:, nr:] * tab_ref[:, _T_SM:_T_KS]
    q = _dot(jnp.concatenate([nq, q_rope.astype(BF16)], axis=1), wqc_ref[...])
    store_pairs(q_ref, (q * (MLA_QK ** -0.5 * LOG2E)).astype(BF16))
    conv_rows(n_conv - len(conv_blocks))

    o_conv = _dot(jnp.concatenate(conv_blocks, axis=0), pw_ref[...].astype(BF16)) + vec(pwb_ref)
    oc_ref[0] = (o_conv * g_conv).astype(oc_ref.dtype)


def _layer_spec(a, layer):
    return pl.BlockSpec((None,) + a.shape[1:], lambda *_: (layer,) + (0,) * (a.ndim - 1),
                        pipeline_mode=pl.Buffered(1))


def _whole_spec(a):
    return pl.BlockSpec(a.shape, lambda *_: (0,) * a.ndim, pipeline_mode=pl.Buffered(1))


def _mod_row(mod_ref, k, shared_row, batch_axis):
    if shared_row is None:
        return mod_ref[k, pl.ds(pl.program_id(batch_axis), 1), :]
    return mod_ref[k, shared_row:shared_row + 1, :]


def _proj(x, mod_all, mod_row, tab, weights, gq, gkv, conv, layer, tm, keys_only=False):
    b, t, d = x.shape
    w_t, w_ks, w_q2, w_qcomb, w_kcomb, w_uvt = weights
    const = lambda a: _layer_spec(a, layer)
    row = lambda n: pl.BlockSpec((1, tm, n), lambda i, j: (j, i, 0))
    col = lambda n: pl.BlockSpec((1, n, tm), lambda i, j: (j, 0, i))
    n_pair, pair = MLA_HEADS // 2, 2 * HEAD_SLOT
    pairs = pl.BlockSpec((1, n_pair, tm, pair), lambda i, j: (j, 0, i, 0))
    per_tile = tm // _HALO
    last = t // _HALO - 1
    halo_prev = pl.BlockSpec((1, _HALO, d), lambda i, j: (j, jnp.maximum(i * per_tile - 1, 0), 0))
    halo_next = pl.BlockSpec((1, _HALO, d), lambda i, j: (j, jnp.minimum((i + 1) * per_tile, last), 0))
    rot = jnp.asarray(_rot_cols(np.eye(RET_WIDTH, dtype=np.float32), RET_DK), BF16)
    sds = jax.ShapeDtypeStruct
    outs = dict(q=(sds((b, n_pair, t, pair), BF16), pairs),
                k=(sds((b, n_pair, t, pair), BF16), pairs),
                vt=(sds((b, MLA_WIDTH, t), BF16), col(MLA_WIDTH)),
                rq=(sds((b, t, RET_WIDTH), F32), row(RET_WIDTH)),
                rk=(sds((b, t, RET_WIDTH), F32), row(RET_WIDTH)),
                rvt=(sds((b, RET_WIDTH, t), BF16), col(RET_WIDTH)),
                gate_mla=(sds((b, n_pair, t, 2 * MLA_V), BF16),
                          pl.BlockSpec((1, n_pair, tm, 2 * MLA_V), lambda i, j: (j, 0, i, 0))),
                gate_ret=(sds((b, t, RET_WIDTH), BF16), row(RET_WIDTH)),
                conv=(sds((b, t, CONV_WIDTH), BF16), row(CONV_WIDTH)))
    names = ("k", "vt", "rk", "rvt") if keys_only else tuple(outs)
    scratch = [] if keys_only else [pltpu.VMEM((tm + 2 * _HALO, CONV_WIDTH), F32),
                                    pltpu.VMEM((7, tm + 2 * _HALO - 8, CONV_WIDTH), F32)]
    res = pl.pallas_call(
        functools.partial(_proj_kernel, keys_only=keys_only, layer=layer, mod_row=mod_row),
        grid=(t // tm, b),
        in_specs=[row(d), halo_prev, halo_next,
                  const(mod_all),
                  pl.BlockSpec((tm, _T_END), lambda i, j: (i, 0)),
                  pl.BlockSpec(rot.shape, lambda i, j: (0, 0)),
                  const(w_t), const(w_ks), _whole_spec(gq), _whole_spec(gkv), const(w_q2), const(w_qcomb),
                  const(w_kcomb), const(w_uvt)] + [const(a) if a.ndim == 3 else _whole_spec(a) for a in conv],
        out_specs=tuple(outs[n][1] for n in names),
        out_shape=tuple(outs[n][0] for n in names),
        scratch_shapes=scratch,
        compiler_params=_params(("arbitrary", "arbitrary")),
        name="proj",
    )(x, x, x, mod_all, tab, rot, w_t, w_ks, gq, gkv, w_q2, w_qcomb, w_kcomb, w_uvt, *conv)
    res = dict(zip(names, res))
    res["vt"] = res["vt"].reshape(b, n_pair, 2 * MLA_V, t)
    return res


_ATTN_KEY_BLOCK = 512


def _attn_kernel(*refs, n_seg, tq):
    q_ref, g_ref = refs[:2]
    k_refs = refs[2:2 + n_seg]
    vt_refs = refs[2 + n_seg:2 + 2 * n_seg]
    o_ref = refs[2 + 2 * n_seg]
    s_bufs = refs[3 + 2 * n_seg:5 + 2 * n_seg]
    m_bufs = refs[5 + 2 * n_seg:7 + 2 * n_seg]
    n_pair = q_ref.shape[1]
    n_q = q_ref.shape[2] // tq
    n_items = n_pair * n_q
    lanes = [slice(j * HEAD_SLOT, (j + 1) * HEAD_SLOT) for j in range(2)]
    rows = [slice(j * MLA_V, (j + 1) * MLA_V) for j in range(2)]

    blocks, off = [], 0
    for si in range(n_seg):
        ts = k_refs[si].shape[2]
        kb = min(_ATTN_KEY_BLOCK, ts)
        blocks += [(si, b0, kb, off + b0) for b0 in range(0, ts, kb)]
        off += ts

    def locate(item):
        if isinstance(item, int):
            return item // n_q, (item % n_q) * tq
        pair = lax.div(item, jnp.int32(n_q))
        return pair, pl.multiple_of((item - pair * n_q) * tq, tq)

    def phase(nxt, cur):
        if nxt is not None:
            item_n, s_n, m_n = nxt
            pn, rn = locate(item_n)
            qn = [q_ref[0, pn, pl.ds(rn, tq), lanes[j]] for j in range(2)]
            mx = [None, None]
        if cur is not None:
            item_c, s_c, m_c = cur
            pc, rc = locate(item_c)
            mc = [m_c[j] for j in range(2)]
            acc, l = [None, None], [None, None]
        for si, b0, kb, o0 in blocks:
            for j in range(2):
                if nxt is not None:
                    s = _dot_nt(k_refs[si][0, pn, b0:b0 + kb, lanes[j]], qn[j])
                    s_n[j, o0:o0 + kb, :] = s
                    mb = jnp.max(s, axis=0, keepdims=True)
                    mx[j] = mb if mx[j] is None else jnp.maximum(mx[j], mb)
                if cur is not None:
                    p = jnp.exp2(s_c[j, o0:o0 + kb, :] - mc[j])
                    ls = jnp.sum(p, axis=0, keepdims=True)
                    pv = _dot(vt_refs[si][0, pc, rows[j], b0:b0 + kb], p.astype(BF16))
                    l[j] = ls if l[j] is None else l[j] + ls
                    acc[j] = pv if acc[j] is None else acc[j] + pv
        if nxt is not None:
            for j in range(2):
                m_n[j] = mx[j]
        if cur is not None:
            outs = [acc[j] * (1.0 / l[j]) for j in range(2)]
            gate = g_ref[0, pc, pl.ds(rc, tq), :].astype(F32)
            o_ref[0, pc, pl.ds(rc, tq), :] = (jnp.concatenate(outs, axis=0).T * gate).astype(o_ref.dtype)

    buf = [(s_bufs[0], m_bufs[0]), (s_bufs[1], m_bufs[1])]
    phase((0, *buf[0]), None)
    assert n_items % 2 == 0

    def body(i, carry):
        n = 2 * i
        phase((n + 1, *buf[1]), (n, *buf[0]))
        phase((jnp.minimum(n + 2, n_items - 1), *buf[0]), (n + 1, *buf[1]))
        return carry

    lax.fori_loop(0, n_items // 2, body, 0)


def _attention(q, gate, ks, vts, tq):
    b, n_pair, t, pair = q.shape
    n_seg = len(ks)
    n_keys = sum(k.shape[2] for k in ks)
    whole = lambda a: pl.BlockSpec((1,) + a.shape[1:], lambda i: (i, 0, 0, 0))
    score_buf = pltpu.VMEM((2, n_keys, tq), F32)
    max_buf = pltpu.VMEM((2, 1, tq), F32)
    return pl.pallas_call(
        functools.partial(_attn_kernel, n_seg=n_seg, tq=tq),
        grid=(b,),
        in_specs=[whole(q), whole(gate)] + [whole(k) for k in ks] + [whole(v) for v in vts],
        out_specs=pl.BlockSpec((1, n_pair, t, 2 * MLA_V), lambda i: (i, 0, 0, 0)),
        out_shape=jax.ShapeDtypeStruct((b, n_pair, t, 2 * MLA_V), BF16),
        scratch_shapes=[score_buf, score_buf, max_buf, max_buf],
        compiler_params=_params(("arbitrary",)),
        name="attention",
    )(q, gate, *ks, *vts)


def _log_sigmoid(x):
    return jnp.minimum(x, 0.0) - jnp.log1p(jnp.exp(-jnp.abs(x)))


def _ret_kernel(dec_ref, *refs, chunk, layer, ctx_out):
    if ctx_out:
        (cq_ref, ck_ref, cvt_ref, cgate_ref, q_ref, k_ref, vt_ref, gate_ref, gng_ref, gnb_ref, oc_ref, o_ref,
         sf_scr, sb_scr, kd_scr, qd_scr, dt_scr, cc_scr) = refs
    else:
        (ck_ref, cvt_ref, q_ref, k_ref, vt_ref, gate_ref, gng_ref, gnb_ref, o_ref,
         sf_scr, sb_scr, kd_scr, qd_scr, dt_scr, cc_scr) = refs
        cq_ref = cgate_ref = oc_ref = None
    c_len = chunk
    hp = pl.program_id(0)
    chunks = []
    for q_r, k_r, vt_r, g_r, o_r in ((cq_ref, ck_ref, cvt_ref, cgate_ref, oc_ref),
                                     (q_ref, k_ref, vt_ref, gate_ref, o_ref)):
        for c in range(k_r.shape[1] // c_len):
            chunks.append(dict(q=q_r, k=k_r, vt=vt_r, gate=g_r, out=o_r, c0=c * c_len, idx=len(chunks)))
    n_ctx = ck_ref.shape[1] // c_len

    @pl.when(pl.program_id(1) == 0)
    def _():
        r = lax.broadcasted_iota(jnp.int32, (c_len, LANES), 0).astype(F32)
        lane = lax.broadcasted_iota(jnp.int32, (c_len, LANES), 1)
        km = lax.broadcasted_iota(jnp.int32, (c_len, c_len), 0)
        qn = lax.broadcasted_iota(jnp.int32, (c_len, c_len), 1)
        diff = (qn - km).astype(F32)
        for j in range(2):
            h = hp * 2 + j
            lg_f = _log_sigmoid(jnp.full((c_len, LANES), dec_ref[layer, 0, h], F32))
            lg_b = _log_sigmoid(jnp.full((c_len, LANES), dec_ref[layer, 1, h], F32))
            own = ((lane >= j * RET_DK) & (lane < (j + 1) * RET_DK)).astype(F32)
            lg_f2 = _log_sigmoid(jnp.full((c_len, c_len), dec_ref[layer, 0, h], F32))
            lg_b2 = _log_sigmoid(jnp.full((c_len, c_len), dec_ref[layer, 1, h], F32))
            kd_scr[j] = jnp.concatenate([jnp.exp(lg_f * (c_len - 1.0 - r)) * own, jnp.exp(lg_b * r) * own, own], axis=1)
            qd_scr[j] = jnp.concatenate([jnp.exp(lg_f * (r + 1.0)), jnp.exp(lg_b * (c_len - r))], axis=1)
            dt_scr[j] = jnp.exp(jnp.where(diff >= 0, lg_f2 * diff, -lg_b2 * diff))
            cc_scr[j] = jnp.concatenate(
                [jnp.exp(_log_sigmoid(jnp.full((RET_DV, LANES), dec_ref[layer, 0, h], F32)) * c_len),
                 jnp.exp(_log_sigmoid(jnp.full((RET_DV, LANES), dec_ref[layer, 1, h], F32)) * c_len)], axis=1)

    heads = []
    for j in range(2):
        heads.append(dict(
            kdf=kd_scr[j, :, 0:LANES],
            kdb=kd_scr[j, :, LANES:2 * LANES],
            own=kd_scr[j, :, 2 * LANES:3 * LANES],
            qd=qd_scr[j],
            dt=dt_scr[j],
            cf=cc_scr[j, :, 0:LANES],
            cb=cc_scr[j, :, LANES:2 * LANES],
            rows=slice(j * RET_DV, (j + 1) * RET_DV),
        ))

    def rows(ref, ch):
        return ref[0, ch["c0"]:ch["c0"] + c_len, :]

    def vt_rows(ch, hd):
        return ch["vt"][0, hd["rows"], ch["c0"]:ch["c0"] + c_len]

    for ch in chunks:
        kc = rows(ch["k"], ch)
        for j, hd in enumerate(heads):
            sf_scr[j, ch["idx"]] = _dot(vt_rows(ch, hd), (kc * hd["kdf"]).astype(BF16))
            sb_scr[j, ch["idx"]] = _dot(vt_rows(ch, hd), (kc * hd["kdb"]).astype(BF16))

    backward_order = list(reversed(chunks[:n_ctx])) + list(reversed(chunks[n_ctx:]))
    for j, hd in enumerate(heads):
        for scr, decay, order in ((sf_scr, hd["cf"], chunks), (sb_scr, hd["cb"], backward_order)):
            state = jnp.zeros((RET_DV, LANES), F32)
            for ch in order:
                own = scr[j, ch["idx"]]
                scr[j, ch["idx"]] = state
                state = decay * state + own

    items = []
    for ch in chunks:
        if ch["q"] is not None:
            qc = rows(ch["q"], ch)
            items.append(dict(ch=ch, kc=rows(ch["k"], ch), qb=qc.astype(BF16), qq=jnp.concatenate([qc, qc], axis=1)))
    work = [(it, j, hd) for it in items for j, hd in enumerate(heads)]
    st = [_dot_nt((it["kc"] * hd["own"]).astype(BF16), it["qb"]) for it, j, hd in work]
    cross = [_dot_nt(jnp.concatenate([sf_scr[j, it["ch"]["idx"]], sb_scr[j, it["ch"]["idx"]]], axis=1).astype(BF16),
                     (it["qq"] * hd["qd"]).astype(BF16)) for it, j, hd in work]
    o = [_dot(vt_rows(it["ch"], hd), (s * hd["dt"]).astype(BF16)) + x for (it, j, hd), s, x in zip(work, st, cross)]
    ys = []
    for v in o:
        mu = jnp.mean(v, axis=0, keepdims=True)
        vc = v - mu
        var = jnp.mean(vc * vc, axis=0, keepdims=True)
        ys.append(vc * lax.rsqrt(var + EPS))
    for g, it in enumerate(items):
        ch = it["ch"]
        y = jnp.concatenate(ys[2 * g:2 * g + 2], axis=0).T
        y = y * gng_ref[layer:layer + 1, :] + gnb_ref[layer:layer + 1, :]
        ch["out"][0, ch["c0"]:ch["c0"] + c_len, :] = (y * rows(ch["gate"], ch).astype(F32)).astype(ch["out"].dtype)


def _retention(dec, ctx, lat, gn_g, gn_b, layer, chunk):
    b, t, _ = lat["rk"].shape
    t_ctx = ctx["rk"].shape[1]
    ctx_out = "rq" in ctx
    n_pair = RET_HEADS // 2
    pair = 2 * RET_DK
    state_buf = pltpu.VMEM((2, (t + t_ctx) // chunk, RET_DV, pair), F32)
    tables = [pltpu.VMEM((2, chunk, 3 * LANES), F32), pltpu.VMEM((2, chunk, 2 * LANES), F32),
              pltpu.VMEM((2, chunk, chunk), F32), pltpu.VMEM((2, RET_DV, 2 * LANES), F32)]
    seq = lambda n: pl.BlockSpec((1, n, pair), lambda h, i: (i, 0, h))
    vts = lambda n: pl.BlockSpec((1, 2 * RET_DV, n), lambda h, i: (i, h, 0))
    affine = pl.BlockSpec((gn_g.shape[0], pair), lambda h, i: (0, h))
    smem = pl.BlockSpec(memory_space=pltpu.SMEM)
    out = lambda n: jax.ShapeDtypeStruct((b, n, RET_WIDTH), BF16)
    if ctx_out:
        in_specs = [smem, seq(t_ctx), seq(t_ctx), vts(t_ctx), seq(t_ctx)]
        args = (dec, ctx["rq"], ctx["rk"], ctx["rvt"], ctx["gate_ret"])
        out_specs, out_shape = (seq(t_ctx), seq(t)), (out(t_ctx), out(t))
    else:
        in_specs, args = [smem, seq(t_ctx), vts(t_ctx)], (dec, ctx["rk"], ctx["rvt"])
        out_specs, out_shape = seq(t), out(t)
    in_specs += [seq(t), seq(t), vts(t), seq(t), affine, affine]
    args += (lat["rq"], lat["rk"], lat["rvt"], lat["gate_ret"], gn_g, gn_b)
    res = pl.pallas_call(
        functools.partial(_ret_kernel, chunk=chunk, layer=layer, ctx_out=ctx_out),
        grid=(n_pair, b),
        in_specs=in_specs,
        out_specs=out_specs,
        out_shape=out_shape,
        scratch_shapes=[state_buf, state_buf] + tables,
        compiler_params=_params(("arbitrary", "arbitrary")),
        name="retention",
    )(*args)
    return res if ctx_out else (None, res)


_FINAL_SUBTILES = 4


def _final_kernel(x_ref, mod_ref, om_ref, or_ref, oc_ref, wo_ref, lng_ref, lnb_ref, o_ref, *, alpha, layer, mod_row):
    tm = x_ref.shape[1]
    sub = tm // _FINAL_SUBTILES if tm % (_FINAL_SUBTILES * 16) == 0 else tm
    w = wo_ref[...].astype(BF16)
    gate = _mod_row(mod_ref, 2, mod_row, 0)
    for r0 in range(0, tm, sub):
        rs = slice(r0, r0 + sub)
        o_cat = jnp.concatenate([om_ref[0, p, rs, :] for p in range(om_ref.shape[1])]
                                + [or_ref[0, rs, :], oc_ref[0, rs, :]], axis=1)
        y = _dot(o_cat, w)
        v = alpha * x_ref[0, rs, :] + gate * y
        mu = jnp.mean(v, axis=-1, keepdims=True)
        vc = v - mu
        var = jnp.mean(vc * vc, axis=-1, keepdims=True)
        o_ref[0, rs, :] = vc * lax.rsqrt(var + EPS) * lng_ref[layer:layer + 1, :] + lnb_ref[layer:layer + 1, :]


def _final(x, mod_all, mod_row, o_mla, o_ret, o_conv, w_out, ln_g, ln_b, layer, tm, alpha):
    b, t, d = x.shape
    const = lambda a: _layer_spec(a, layer)
    row = lambda n: pl.BlockSpec((1, tm, n), lambda i, j: (i, j, 0))
    return pl.pallas_call(
        functools.partial(_final_kernel, alpha=alpha, layer=layer, mod_row=mod_row),
        grid=(b, t // tm),
        in_specs=[row(d),
                  const(mod_all),
                  pl.BlockSpec((1, o_mla.shape[1], tm, o_mla.shape[3]), lambda i, j: (i, 0, j, 0)),
                  row(RET_WIDTH), row(CONV_WIDTH),
                  const(w_out), _whole_spec(ln_g), _whole_spec(ln_b)],
        out_specs=row(d),
        out_shape=jax.ShapeDtypeStruct((b, t, d), F32),
        compiler_params=_params(("arbitrary", "arbitrary")),
        name="final",
    )(x, mod_all, o_mla, o_ret, o_conv, w_out, ln_g, ln_b)


_TOKEN_TILE = 1024
_CTX_TILE = 256
_QUERY_CHUNK = 256
_RET_CHUNK = 256


def _tile(n, pref):
    return pref if n % pref == 0 else n


def kernel(x, c, ctx, c_ctx, w_mod, b_mod, w_in, mla_q_norm, w_uq, mla_kv_norm, w_ukv, ret_decay_fwd, ret_decay_bwd,
           ret_gn_g, ret_gn_b, conv_dw, conv_dw_b, conv_ln_g, conv_ln_b, conv_pw, conv_pw_b, w_out, ln_g, ln_b):
    depth = w_mod.shape[0]
    b, t, d = x.shape
    t_ctx = ctx.shape[1]
    alpha = (2 * depth) ** 0.25
    n_mod_rows = 16
    cc = jnp.concatenate([c, c_ctx[None, :], jnp.zeros((n_mod_rows - b - 1, d), c.dtype)], axis=0)
    mod_all = _modulation(cc, w_mod, b_mod)
    tab_x = _rope_table_block(t, True)
    tab_c = _rope_table_block(t_ctx, False)

    weights = _prep_weights(w_in, w_uq, w_ukv)
    gq, gkv = mla_q_norm, mla_kv_norm
    dec = jnp.stack([ret_decay_fwd, ret_decay_bwd], axis=1).astype(F32)
    gn_g, gn_b = ret_gn_g, ret_gn_b
    conv = (conv_dw, conv_dw_b, conv_ln_g, conv_ln_b, conv_pw, conv_pw_b)
    tail = (w_out, ln_g, ln_b)
    tm_x, tm_c = _tile(t, _TOKEN_TILE), _tile(t_ctx, _CTX_TILE)
    tf_x, tf_c = _tile(t, _TOKEN_TILE), _tile(t_ctx, _CTX_TILE)
    tq_x, tq_c = _tile(t, _QUERY_CHUNK), _tile(t_ctx, _QUERY_CHUNK)
    chunk = math.gcd(math.gcd(t, t_ctx), _RET_CHUNK)

    hc = ctx
    for l in range(depth):
        need_ctx = l < depth - 1
        pc = _proj(hc, mod_all, b, tab_c, weights, gq, gkv, conv, l, tm_c, keys_only=not need_ctx)
        px = _proj(x, mod_all, None, tab_x, weights, gq, gkv, conv, l, tm_x)

        oc_ret, o_ret = _retention(dec, pc, px, gn_g, gn_b, l, chunk)
        o_mla = _attention(px["q"], px["gate_mla"], [pc["k"], px["k"]], [pc["vt"], px["vt"]], tq_x)
        x_new = _final(x, mod_all, None, o_mla, o_ret, px["conv"], *tail, layer=l, tm=tf_x, alpha=alpha)
        if need_ctx:
            oc_mla = _attention(pc["q"], pc["gate_mla"], [pc["k"]], [pc["vt"]], tq_c)
            hc = _final(hc, mod_all, b, oc_mla, oc_ret, pc["conv"], *tail, layer=l, tm=tf_c, alpha=alpha)
        x = x_new
    return x
```

```python
import functools
import math

import jax
import jax.numpy as jnp
import numpy as np
from jax import lax
from jax.experimental import pallas as pl
from jax.experimental.pallas import tpu as pltpu

GRID_W = 64
MLA_HEADS = 8
MLA_NOPE = 64
MLA_ROPE = 32
MLA_V = 64
MLA_QK = MLA_NOPE + MLA_ROPE
MLA_WIDTH = MLA_HEADS * MLA_V
Q_LORA = 256
KV_LORA = 128
RET_HEADS = 4
RET_DK = 64
RET_DV = 64
RET_WIDTH = RET_HEADS * RET_DV
CONV_WIDTH = 256
CONV_K = 31
ROPE_BASE = 10000.0
EPS = 1e-5

LANES = 128
HEAD_SLOT = LANES
VMEM_LIMIT_BYTES = 56 * 1024 * 1024
LOG2E = 1.4426950408889634

F32 = jnp.float32
BF16 = jnp.bfloat16

_NT = (((1,), (1,)), ((), ()))


def _dot(a, b):
    return jnp.dot(a, b, preferred_element_type=F32)


def _dot_nt(a, b):
    return lax.dot_general(a, b, _NT, preferred_element_type=F32)


def _params(semantics):
    return pltpu.CompilerParams(dimension_semantics=semantics, vmem_limit_bytes=VMEM_LIMIT_BYTES)


def _rot_cols(w, unit):
    xp = np if isinstance(w, np.ndarray) else jnp
    q = unit // 4
    w5 = w.reshape(w.shape[:-1] + (w.shape[-1] // unit, 2, 2, q))
    return xp.stack([-w5[..., 1, :], w5[..., 0, :]], axis=-2).reshape(w.shape)


def _rope_tables(length, unit, reps):
    d2 = unit // 2
    t = np.arange(length, dtype=np.int32)
    inv = np.float32(ROPE_BASE) ** (-np.arange(0, d2, 2, dtype=np.float32) / np.float32(d2))

    def half(pos):
        ang = pos.astype(np.float32)[:, None] * inv[None, :]
        return np.concatenate([np.cos(ang)] * 2, axis=-1), np.concatenate([np.sin(ang)] * 2, axis=-1)

    cr, sr = half(t // GRID_W)
    cc, sc = half(t % GRID_W)
    cos = np.concatenate([cr, cc], axis=-1).astype(np.float32)
    sin = np.concatenate([sr, sc], axis=-1).astype(np.float32)
    return np.tile(cos, (1, reps)), np.tile(sin, (1, reps))


def _rope_table_block(length, rotate):
    if rotate:
        cos_r, sin_r = _rope_tables(length, RET_DK, RET_HEADS)
        cos_m, sin_m = _rope_tables(length, MLA_ROPE, MLA_HEADS)
    else:
        cos_r = np.ones((length, RET_WIDTH), np.float32)
        sin_r = np.zeros((length, RET_WIDTH), np.float32)
        cos_m = np.ones((length, MLA_HEADS * MLA_ROPE), np.float32)
        sin_m = np.zeros((length, MLA_HEADS * MLA_ROPE), np.float32)
    kslot = np.concatenate([cos_m[:, :MLA_ROPE], sin_m[:, :MLA_ROPE],
                            np.zeros((length, LANES - 2 * MLA_ROPE), np.float32)], axis=-1)
    return jnp.asarray(np.concatenate([cos_r, sin_r, cos_m, sin_m, kslot], axis=-1))


def _placement_matrices():
    q_place = np.zeros((MLA_HEADS, MLA_ROPE, MLA_HEADS, HEAD_SLOT), np.float32)
    k_place = np.zeros((LANES, MLA_HEADS, HEAD_SLOT), np.float32)
    for d in range(MLA_ROPE):
        for h in range(MLA_HEADS):
            q_place[h, d, h, MLA_NOPE + d] = 1.0
            k_place[d, h, MLA_NOPE + d] = 1.0
            k_place[MLA_ROPE + d, h, MLA_NOPE + d] = 1.0
    return (q_place.reshape(MLA_HEADS * MLA_ROPE, MLA_HEADS * HEAD_SLOT),
            k_place.reshape(LANES, MLA_HEADS * HEAD_SLOT))


_IN_SIZES = dict(q=Q_LORA, kv=KV_LORA, kr=MLA_ROPE, g_mla=MLA_WIDTH, rq=RET_HEADS * RET_DK, rk=RET_HEADS * RET_DK,
                 rv=RET_WIDTH, g_ret=RET_WIDTH, glu=2 * CONV_WIDTH, g_conv=CONV_WIDTH)


def _offsets(sizes):
    out, at = {}, 0
    for name, n in sizes.items():
        out[name] = (at, at + n)
        at += n
    return out


_IN_COLS = _offsets(_IN_SIZES)


def _prep_weights(w_in, w_uq, w_ukv):
    depth, d, _ = w_in.shape
    w_t = jnp.swapaxes(w_in, 1, 2).astype(BF16)
    wkr = w_in[..., _IN_COLS["kr"][0]:_IN_COLS["kr"][1]]
    w_ks = jnp.concatenate([wkr, _rot_cols(wkr, MLA_ROPE), jnp.zeros((depth, d, LANES - 2 * MLA_ROPE), w_in.dtype)],
                           axis=-1)
    w_ks = jnp.swapaxes(w_ks, 1, 2).astype(BF16)

    uq = w_uq.reshape(depth, Q_LORA, MLA_HEADS, MLA_QK)
    pad = ((0, 0), (0, 0), (0, 0), (0, HEAD_SLOT - MLA_NOPE))
    uq_nope = jnp.pad(uq[..., :MLA_NOPE], pad).reshape(depth, Q_LORA, -1)
    uq_rope = uq[..., MLA_NOPE:].reshape(depth, Q_LORA, MLA_HEADS * MLA_ROPE)
    w_q2 = jnp.concatenate([uq_rope, _rot_cols(uq_rope, MLA_ROPE)], axis=-1).astype(BF16)
    q_place, k_place = (jnp.broadcast_to(jnp.asarray(m, BF16), (depth,) + m.shape) for m in _placement_matrices())
    w_qcomb = jnp.concatenate([uq_nope.astype(BF16), q_place], axis=1)

    ukv = w_ukv.reshape(depth, KV_LORA, MLA_HEADS, MLA_NOPE + MLA_V)
    uk = jnp.pad(ukv[..., :MLA_NOPE], pad).reshape(depth, KV_LORA, -1)
    w_kcomb = jnp.concatenate([uk.astype(BF16), k_place], axis=1)
    w_uvt = jnp.swapaxes(ukv[..., MLA_NOPE:].reshape(depth, KV_LORA, MLA_WIDTH), 1, 2).astype(BF16)
    return w_t, w_ks, w_q2, w_qcomb, w_kcomb, w_uvt


def _mod_kernel(c_ref, w_ref, b_ref, o_ref):
    c = c_ref[...]
    a = (c * jax.nn.sigmoid(c)).astype(BF16)
    o_ref[0, 0] = _dot(a, w_ref[0].astype(BF16)) + b_ref[pl.ds(pl.program_id(0), 1), :]


def _modulation(cc, w_mod, b_mod):
    depth, d, d3 = w_mod.shape
    nblk = d3 // d
    return pl.pallas_call(
        _mod_kernel,
        grid=(depth, nblk),
        in_specs=[pl.BlockSpec(cc.shape, lambda l, j: (0, 0)),
                  pl.BlockSpec((1, d, d), lambda l, j: (l, 0, j)),
                  pl.BlockSpec((depth, d), lambda l, j: (0, j))],
        out_specs=pl.BlockSpec((1, 1, cc.shape[0], d), lambda l, j: (l, j, 0, 0)),
        out_shape=jax.ShapeDtypeStruct((depth, nblk, cc.shape[0], d), F32),
        compiler_params=_params(("arbitrary", "arbitrary")),
        name="mod",
    )(cc, w_mod, b_mod)


_T_CR = 0
_T_SR = _T_CR + RET_WIDTH
_T_CM = _T_SR + RET_WIDTH
_T_SM = _T_CM + MLA_HEADS * MLA_ROPE
_T_KS = _T_SM + MLA_HEADS * MLA_ROPE
_T_END = _T_KS + LANES


_HALO = 16
_CONV_ROWS = 64


def _rms(x, g):
    return x * lax.rsqrt(jnp.mean(x * x, axis=-1, keepdims=True) + EPS) * g


def _proj_kernel(x_ref, xp_ref, xn_ref, mod_ref, tab_ref, rot_ref, wt_ref, wks_ref, gq_ref, gkv_ref, wq2_ref,
                 wqc_ref, wkc_ref, wuvt_ref, dw_ref, dwb_ref, clg_ref, clb_ref, pw_ref, pwb_ref, *out_refs, keys_only,
                 layer, mod_row):
    if keys_only:
        k_ref, vt_ref, rk_ref, rvt_ref = out_refs
    else:
        q_ref, k_ref, vt_ref, rq_ref, rk_ref, rvt_ref, gm_ref, gr_ref, oc_ref, upad, ushift = out_refs
    tm = x_ref.shape[1]
    shift = _mod_row(mod_ref, 0, mod_row, pl.program_id(1))
    scale = _mod_row(mod_ref, 1, mod_row, pl.program_id(1))
    pair = 2 * HEAD_SLOT
    vec = lambda ref: ref[layer:layer + 1, :]

    def modulate(x):
        mu = jnp.mean(x, axis=-1, keepdims=True)
        xc = x - mu
        var = jnp.mean(xc * xc, axis=-1, keepdims=True)
        return (xc * lax.rsqrt(var + EPS) * (1.0 + scale) + shift).astype(BF16)

    u = modulate(x_ref[0])

    def w_rows(name):
        lo, hi = _IN_COLS[name]
        return wt_ref[lo:hi, :]

    def seg(name):
        return _dot_nt(u, w_rows(name))

    def silu(v):
        return v * jax.nn.sigmoid(v)

    def store_pairs(ref, val):
        for p in range(MLA_HEADS // 2):
            ref[0, p] = val[:, p * pair:(p + 1) * pair]

    def keys(pkv, pks):
        nkv = _rms(pkv, vec(gkv_ref)).astype(BF16)
        ks = pks * tab_ref[:, _T_KS:_T_END]
        store_pairs(k_ref, _dot(jnp.concatenate([nkv, ks.astype(BF16)], axis=1), wkc_ref[...]).astype(BF16))
        vt_ref[0] = _dot_nt(wuvt_ref[...], nkv).astype(BF16)

    def ret_rope(raw):
        rot = _dot(raw.astype(BF16), rot_ref[...])
        return raw * tab_ref[:, _T_CR:_T_SR] + rot * tab_ref[:, _T_SR:_T_CM]

    def ret_keys(prk):
        rk_ref[0] = ret_rope(prk) * (RET_DK ** -0.5)
        rvt_ref[0] = _dot_nt(w_rows("rv"), u).astype(BF16)

    if keys_only:
        keys(seg("kv"), _dot_nt(u, wks_ref[...]))
        ret_keys(seg("rk"))
        return

    i = pl.program_id(0)
    u_ext = jnp.concatenate([modulate(xp_ref[0]), u, modulate(xn_ref[0])], axis=0)
    glu = _dot_nt(u_ext, w_rows("glu"))
    uc = glu[:, :CONV_WIDTH] * jax.nn.sigmoid(glu[:, CONV_WIDTH:])
    row = lax.broadcasted_iota(jnp.int32, uc.shape, 0)
    inside = ((row >= _HALO) | (i > 0)) & ((row < _HALO + tm) | (i < pl.num_programs(0) - 1))
    upad[...] = jnp.where(inside, uc, 0.0)
    span = tm + 2 * _HALO - 8
    for r in range(1, 8):
        ushift[r - 1] = upad[r:r + span, :]
    base = _HALO - CONV_K // 2
    conv_blocks = []

    def conv_rows(n_blocks):
        for _ in range(n_blocks):
            rb = len(conv_blocks) * _CONV_ROWS
            acc = jnp.zeros((_CONV_ROWS, CONV_WIDTH), F32) + vec(dwb_ref)
            for j in range(CONV_K):
                a, r = divmod(base + j, 8)
                src = upad if r == 0 else ushift.at[r - 1]
                acc = acc + src[8 * a + rb:8 * a + rb + _CONV_ROWS, :] * dw_ref[j:j + 1, :]
            mu = jnp.mean(acc, axis=-1, keepdims=True)
            ac = acc - mu
            var = jnp.mean(ac * ac, axis=-1, keepdims=True)
            yc = ac * lax.rsqrt(var + EPS) * vec(clg_ref) + vec(clb_ref)
            conv_blocks.append((yc * jax.nn.sigmoid(yc)).astype(BF16))

    n_conv = tm // _CONV_ROWS
    per_stage = -(-n_conv // 4)

    pq, pkv, pks = seg("q"), seg("kv"), _dot_nt(u, wks_ref[...])
    prq, prk = seg("rq"), seg("rk")
    conv_rows(min(per_stage, n_conv - len(conv_blocks)))

    g_mla = silu(seg("g_mla"))
    for p in range(MLA_HEADS // 2):
        gm_ref[0, p] = g_mla[:, p * LANES:(p + 1) * LANES].astype(gm_ref.dtype)
    gr_ref[0] = silu(seg("g_ret")).astype(gr_ref.dtype)
    g_conv = silu(seg("g_conv"))
    conv_rows(min(per_stage, n_conv - len(conv_blocks)))

    nq = _rms(pq, vec(gq_ref)).astype(BF16)
    qc = _dot(nq, wq2_ref[...])
    keys(pkv, pks)
    conv_rows(min(per_stage, n_conv - len(conv_blocks)))

    rq_ref[0] = ret_rope(prq)
    ret_keys(prk)

    nr = MLA_HEADS * MLA_ROPE
    q_rope = qc[:, :nr] * tab_ref[:, _T_CM:_T_SM] + qc[:, nr:] * tab_ref[:, _T_SM:_T_KS]
    q = _dot(jnp.concatenate([nq, q_rope.astype(BF16)], axis=1), wqc_ref[...])
    store_pairs(q_ref, (q * (MLA_QK ** -0.5 * LOG2E)).astype(BF16))
    conv_rows(n_conv - len(conv_blocks))

    o_conv = _dot(jnp.concatenate(conv_blocks, axis=0), pw_ref[...].astype(BF16)) + vec(pwb_ref)
    oc_ref[0] = (o_conv * g_conv).astype(oc_ref.dtype)


def _layer_spec(a, layer):
    return pl.BlockSpec((None,) + a.shape[1:], lambda *_: (layer,) + (0,) * (a.ndim - 1),
                        pipeline_mode=pl.Buffered(1))


def _whole_spec(a):
    return pl.BlockSpec(a.shape, lambda *_: (0,) * a.ndim, pipeline_mode=pl.Buffered(1))


def _mod_row(mod_ref, k, shared_row, batch_index):
    if shared_row is None:
        return mod_ref[k, pl.ds(batch_index, 1), :]
    return mod_ref[k, shared_row:shared_row + 1, :]


def _proj(x, mod_all, mod_row, tab, weights, gq, gkv, conv, layer, tm, keys_only=False):
    b, t, d = x.shape
    w_t, w_ks, w_q2, w_qcomb, w_kcomb, w_uvt = weights
    const = lambda a: _layer_spec(a, layer)
    row = lambda n: pl.BlockSpec((1, tm, n), lambda i, j: (j, i, 0))
    col = lambda n: pl.BlockSpec((1, n, tm), lambda i, j: (j, 0, i))
    n_pair, pair = MLA_HEADS // 2, 2 * HEAD_SLOT
    pairs = pl.BlockSpec((1, n_pair, tm, pair), lambda i, j: (j, 0, i, 0))
    per_tile = tm // _HALO
    last = t // _HALO - 1
    halo_prev = pl.BlockSpec((1, _HALO, d), lambda i, j: (j, jnp.maximum(i * per_tile - 1, 0), 0))
    halo_next = pl.BlockSpec((1, _HALO, d), lambda i, j: (j, jnp.minimum((i + 1) * per_tile, last), 0))
    rot = jnp.asarray(_rot_cols(np.eye(RET_WIDTH, dtype=np.float32), RET_DK), BF16)
    sds = jax.ShapeDtypeStruct
    outs = dict(q=(sds((b, n_pair, t, pair), BF16), pairs),
                k=(sds((b, n_pair, t, pair), BF16), pairs),
                vt=(sds((b, MLA_WIDTH, t), BF16), col(MLA_WIDTH)),
                rq=(sds((b, t, RET_WIDTH), F32), row(RET_WIDTH)),
                rk=(sds((b, t, RET_WIDTH), F32), row(RET_WIDTH)),
                rvt=(sds((b, RET_WIDTH, t), BF16), col(RET_WIDTH)),
                gate_mla=(sds((b, n_pair, t, 2 * MLA_V), BF16),
                          pl.BlockSpec((1, n_pair, tm, 2 * MLA_V), lambda i, j: (j, 0, i, 0))),
                gate_ret=(sds((b, t, RET_WIDTH), BF16), row(RET_WIDTH)),
                conv=(sds((b, t, CONV_WIDTH), BF16), row(CONV_WIDTH)))
    names = ("k", "vt", "rk", "rvt") if keys_only else tuple(outs)
    scratch = [] if keys_only else [pltpu.VMEM((tm + 2 * _HALO, CONV_WIDTH), F32),
                                    pltpu.VMEM((7, tm + 2 * _HALO - 8, CONV_WIDTH), F32)]
    res = pl.pallas_call(
        functools.partial(_proj_kernel, keys_only=keys_only, layer=layer, mod_row=mod_row),
        grid=(t // tm, b),
        in_specs=[row(d), halo_prev, halo_next,
                  const(mod_all),
                  pl.BlockSpec((tm, _T_END), lambda i, j: (i, 0)),
                  pl.BlockSpec(rot.shape, lambda i, j: (0, 0)),
                  const(w_t), const(w_ks), _whole_spec(gq), _whole_spec(gkv), const(w_q2), const(w_qcomb),
                  const(w_kcomb), const(w_uvt)] + [const(a) if a.ndim == 3 else _whole_spec(a) for a in conv],
        out_specs=tuple(outs[n][1] for n in names),
        out_shape=tuple(outs[n][0] for n in names),
        scratch_shapes=scratch,
        compiler_params=_params(("arbitrary", "arbitrary")),
        name="proj",
    )(x, x, x, mod_all, tab, rot, w_t, w_ks, gq, gkv, w_q2, w_qcomb, w_kcomb, w_uvt, *conv)
    res = dict(zip(names, res))
    res["vt"] = res["vt"].reshape(b, n_pair, 2 * MLA_V, t)
    return res


_ATTN_KEY_BLOCK = 512


def _attn_kernel(*refs, n_seg, tq):
    q_ref, g_ref = refs[:2]
    k_refs = refs[2:2 + n_seg]
    vt_refs = refs[2 + n_seg:2 + 2 * n_seg]
    o_ref = refs[2 + 2 * n_seg]
    s_bufs = refs[3 + 2 * n_seg:5 + 2 * n_seg]
    m_bufs = refs[5 + 2 * n_seg:7 + 2 * n_seg]
    n_pair = q_ref.shape[1]
    n_q = q_ref.shape[2] // tq
    n_items = n_pair * n_q
    lanes = [slice(j * HEAD_SLOT, (j + 1) * HEAD_SLOT) for j in range(2)]
    rows = [slice(j * MLA_V, (j + 1) * MLA_V) for j in range(2)]

    blocks, off = [], 0
    for si in range(n_seg):
        ts = k_refs[si].shape[2]
        kb = min(_ATTN_KEY_BLOCK, ts)
        blocks += [(si, b0, kb, off + b0) for b0 in range(0, ts, kb)]
        off += ts

    def locate(item):
        if isinstance(item, int):
            return item // n_q, (item % n_q) * tq
        pair = lax.div(item, jnp.int32(n_q))
        return pair, pl.multiple_of((item - pair * n_q) * tq, tq)

    def phase(nxt, cur):
        if nxt is not None:
            item_n, s_n, m_n = nxt
            pn, rn = locate(item_n)
            qn = [q_ref[0, pn, pl.ds(rn, tq), lanes[j]] for j in range(2)]
            mx = [None, None]
        if cur is not None:
            item_c, s_c, m_c = cur
            pc, rc = locate(item_c)
            mc = [m_c[j] for j in range(2)]
            acc, l = [None, None], [None, None]
        for si, b0, kb, o0 in blocks:
            for j in range(2):
                if nxt is not None:
                    s = _dot_nt(k_refs[si][0, pn, b0:b0 + kb, lanes[j]], qn[j])
                    s_n[j, o0:o0 + kb, :] = s
                    mb = jnp.max(s, axis=0, keepdims=True)
                    mx[j] = mb if mx[j] is None else jnp.maximum(mx[j], mb)
                if cur is not None:
                    p = jnp.exp2(s_c[j, o0:o0 + kb, :] - mc[j])
                    ls = jnp.sum(p, axis=0, keepdims=True)
                    pv = _dot(vt_refs[si][0, pc, rows[j], b0:b0 + kb], p.astype(BF16))
                    l[j] = ls if l[j] is None else l[j] + ls
                    acc[j] = pv if acc[j] is None else acc[j] + pv
        if nxt is not None:
            for j in range(2):
                m_n[j] = mx[j]
        if cur is not None:
            outs = [acc[j] * (1.0 / l[j]) for j in range(2)]
            gate = g_ref[0, pc, pl.ds(rc, tq), :].astype(F32)
            o_ref[0, pc, pl.ds(rc, tq), :] = (jnp.concatenate(outs, axis=0).T * gate).astype(o_ref.dtype)

    buf = [(s_bufs[0], m_bufs[0]), (s_bufs[1], m_bufs[1])]
    phase((0, *buf[0]), None)
    assert n_items % 2 == 0

    def body(i, carry):
        n = 2 * i
        phase((n + 1, *buf[1]), (n, *buf[0]))
        phase((jnp.minimum(n + 2, n_items - 1), *buf[0]), (n + 1, *buf[1]))
        return carry

    lax.fori_loop(0, n_items // 2, body, 0)


def _attention(q, gate, ks, vts, tq):
    b, n_pair, t, pair = q.shape
    n_seg = len(ks)
    n_keys = sum(k.shape[2] for k in ks)
    whole = lambda a: pl.BlockSpec((1,) + a.shape[1:], lambda i: (i, 0, 0, 0))
    score_buf = pltpu.VMEM((2, n_keys, tq), F32)
    max_buf = pltpu.VMEM((2, 1, tq), F32)
    return pl.pallas_call(
        functools.partial(_attn_kernel, n_seg=n_seg, tq=tq),
        grid=(b,),
        in_specs=[whole(q), whole(gate)] + [whole(k) for k in ks] + [whole(v) for v in vts],
        out_specs=pl.BlockSpec((1, n_pair, t, 2 * MLA_V), lambda i: (i, 0, 0, 0)),
        out_shape=jax.ShapeDtypeStruct((b, n_pair, t, 2 * MLA_V), BF16),
        scratch_shapes=[score_buf, score_buf, max_buf, max_buf],
        compiler_params=_params(("arbitrary",)),
        name="attention",
    )(q, gate, *ks, *vts)


def _log_sigmoid(x):
    return jnp.minimum(x, 0.0) - jnp.log1p(jnp.exp(-jnp.abs(x)))


def _ret_kernel(dec_ref, *refs, chunk, layer, ctx_out):
    if ctx_out:
        (cq_ref, ck_ref, cvt_ref, cgate_ref, q_ref, k_ref, vt_ref, gate_ref, gng_ref, gnb_ref, oc_ref, o_ref,
         sf_scr, sb_scr, kd_scr, qd_scr, dt_scr, cc_scr) = refs
    else:
        (ck_ref, cvt_ref, q_ref, k_ref, vt_ref, gate_ref, gng_ref, gnb_ref, o_ref,
         sf_scr, sb_scr, kd_scr, qd_scr, dt_scr, cc_scr) = refs
        cq_ref = cgate_ref = oc_ref = None
    c_len = chunk
    hp = pl.program_id(0)
    chunks = []
    for q_r, k_r, vt_r, g_r, o_r in ((cq_ref, ck_ref, cvt_ref, cgate_ref, oc_ref),
                                     (q_ref, k_ref, vt_ref, gate_ref, o_ref)):
        for c in range(k_r.shape[1] // c_len):
            chunks.append(dict(q=q_r, k=k_r, vt=vt_r, gate=g_r, out=o_r, c0=c * c_len, idx=len(chunks)))
    n_ctx = ck_ref.shape[1] // c_len

    @pl.when(pl.program_id(1) == 0)
    def _():
        r = lax.broadcasted_iota(jnp.int32, (c_len, LANES), 0).astype(F32)
        lane = lax.broadcasted_iota(jnp.int32, (c_len, LANES), 1)
        km = lax.broadcasted_iota(jnp.int32, (c_len, c_len), 0)
        qn = lax.broadcasted_iota(jnp.int32, (c_len, c_len), 1)
        diff = (qn - km).astype(F32)
        for j in range(2):
            h = hp * 2 + j
            lg_f = _log_sigmoid(jnp.full((c_len, LANES), dec_ref[layer, 0, h], F32))
            lg_b = _log_sigmoid(jnp.full((c_len, LANES), dec_ref[layer, 1, h], F32))
            own = ((lane >= j * RET_DK) & (lane < (j + 1) * RET_DK)).astype(F32)
            lg_f2 = _log_sigmoid(jnp.full((c_len, c_len), dec_ref[layer, 0, h], F32))
            lg_b2 = _log_sigmoid(jnp.full((c_len, c_len), dec_ref[layer, 1, h], F32))
            kd_scr[j] = jnp.concatenate([jnp.exp(lg_f * (c_len - 1.0 - r)) * own, jnp.exp(lg_b * r) * own, own], axis=1)
            qd_scr[j] = jnp.concatenate([jnp.exp(lg_f * (r + 1.0)), jnp.exp(lg_b * (c_len - r))], axis=1)
            dt_scr[j] = jnp.exp(jnp.where(diff >= 0, lg_f2 * diff, -lg_b2 * diff))
            cc_scr[j] = jnp.concatenate(
                [jnp.exp(_log_sigmoid(jnp.full((RET_DV, LANES), dec_ref[layer, 0, h], F32)) * c_len),
                 jnp.exp(_log_sigmoid(jnp.full((RET_DV, LANES), dec_ref[layer, 1, h], F32)) * c_len)], axis=1)

    heads = []
    for j in range(2):
        heads.append(dict(
            kdf=kd_scr[j, :, 0:LANES],
            kdb=kd_scr[j, :, LANES:2 * LANES],
            own=kd_scr[j, :, 2 * LANES:3 * LANES],
            qd=qd_scr[j],
            dt=dt_scr[j],
            cf=cc_scr[j, :, 0:LANES],
            cb=cc_scr[j, :, LANES:2 * LANES],
            rows=slice(j * RET_DV, (j + 1) * RET_DV),
        ))

    def rows(ref, ch):
        return ref[0, ch["c0"]:ch["c0"] + c_len, :]

    def vt_rows(ch, hd):
        return ch["vt"][0, hd["rows"], ch["c0"]:ch["c0"] + c_len]

    for ch in chunks:
        kc = rows(ch["k"], ch)
        for j, hd in enumerate(heads):
            sf_scr[j, ch["idx"]] = _dot(vt_rows(ch, hd), (kc * hd["kdf"]).astype(BF16))
            sb_scr[j, ch["idx"]] = _dot(vt_rows(ch, hd), (kc * hd["kdb"]).astype(BF16))

    backward_order = list(reversed(chunks[:n_ctx])) + list(reversed(chunks[n_ctx:]))
    for j, hd in enumerate(heads):
        for scr, decay, order in ((sf_scr, hd["cf"], chunks), (sb_scr, hd["cb"], backward_order)):
            state = jnp.zeros((RET_DV, LANES), F32)
            for ch in order:
                own = scr[j, ch["idx"]]
                scr[j, ch["idx"]] = state
                state = decay * state + own

    items = []
    for ch in chunks:
        if ch["q"] is not None:
            qc = rows(ch["q"], ch)
            items.append(dict(ch=ch, kc=rows(ch["k"], ch), qb=qc.astype(BF16), qq=jnp.concatenate([qc, qc], axis=1)))
    work = [(it, j, hd) for it in items for j, hd in enumerate(heads)]
    st = [_dot_nt((it["kc"] * hd["own"]).astype(BF16), it["qb"]) for it, j, hd in work]
    cross = [_dot_nt(jnp.concatenate([sf_scr[j, it["ch"]["idx"]], sb_scr[j, it["ch"]["idx"]]], axis=1).astype(BF16),
                     (it["qq"] * hd["qd"]).astype(BF16)) for it, j, hd in work]
    o = [_dot(vt_rows(it["ch"], hd), (s * hd["dt"]).astype(BF16)) + x for (it, j, hd), s, x in zip(work, st, cross)]
    ys = []
    for v in o:
        mu = jnp.mean(v, axis=0, keepdims=True)
        vc = v - mu
        var = jnp.mean(vc * vc, axis=0, keepdims=True)
        ys.append(vc * lax.rsqrt(var + EPS))
    for g, it in enumerate(items):
        ch = it["ch"]
        y = jnp.concatenate(ys[2 * g:2 * g + 2], axis=0).T
        y = y * gng_ref[layer:layer + 1, :] + gnb_ref[layer:layer + 1, :]
        ch["out"][0, ch["c0"]:ch["c0"] + c_len, :] = (y * rows(ch["gate"], ch).astype(F32)).astype(ch["out"].dtype)


def _retention(dec, ctx, lat, gn_g, gn_b, layer, chunk):
    b, t, _ = lat["rk"].shape
    t_ctx = ctx["rk"].shape[1]
    ctx_out = "rq" in ctx
    n_pair = RET_HEADS // 2
    pair = 2 * RET_DK
    state_buf = pltpu.VMEM((2, (t + t_ctx) // chunk, RET_DV, pair), F32)
    tables = [pltpu.VMEM((2, chunk, 3 * LANES), F32), pltpu.VMEM((2, chunk, 2 * LANES), F32),
              pltpu.VMEM((2, chunk, chunk), F32), pltpu.VMEM((2, RET_DV, 2 * LANES), F32)]
    seq = lambda n: pl.BlockSpec((1, n, pair), lambda h, i: (i, 0, h))
    vts = lambda n: pl.BlockSpec((1, 2 * RET_DV, n), lambda h, i: (i, h, 0))
    affine = pl.BlockSpec((gn_g.shape[0], pair), lambda h, i: (0, h))
    smem = pl.BlockSpec(memory_space=pltpu.SMEM)
    out = lambda n: jax.ShapeDtypeStruct((b, n, RET_WIDTH), BF16)
    if ctx_out:
        in_specs = [smem, seq(t_ctx), seq(t_ctx), vts(t_ctx), seq(t_ctx)]
        args = (dec, ctx["rq"], ctx["rk"], ctx["rvt"], ctx["gate_ret"])
        out_specs, out_shape = (seq(t_ctx), seq(t)), (out(t_ctx), out(t))
    else:
        in_specs, args = [smem, seq(t_ctx), vts(t_ctx)], (dec, ctx["rk"], ctx["rvt"])
        out_specs, out_shape = seq(t), out(t)
    in_specs += [seq(t), seq(t), vts(t), seq(t), affine, affine]
    args += (lat["rq"], lat["rk"], lat["rvt"], lat["gate_ret"], gn_g, gn_b)
    res = pl.pallas_call(
        functools.partial(_ret_kernel, chunk=chunk, layer=layer, ctx_out=ctx_out),
        grid=(n_pair, b),
        in_specs=in_specs,
        out_specs=out_specs,
        out_shape=out_shape,
        scratch_shapes=[state_buf, state_buf] + tables,
        compiler_params=_params(("arbitrary", "arbitrary")),
        name="retention",
    )(*args)
    return res if ctx_out else (None, res)


_FINAL_SUB_ROWS = 256


def _final_kernel(x_ref, mod_ref, om_ref, or_ref, oc_ref, wo_ref, lng_ref, lnb_ref, o_ref, *, alpha, layer, mod_row):
    nb, tm = x_ref.shape[:2]
    sub = _FINAL_SUB_ROWS if tm % _FINAL_SUB_ROWS == 0 else tm
    w = wo_ref[...].astype(BF16)
    for bi in range(nb):
        gate = _mod_row(mod_ref, 2, mod_row, pl.program_id(0) * nb + bi)
        for r0 in range(0, tm, sub):
            rs = slice(r0, r0 + sub)
            o_cat = jnp.concatenate([om_ref[bi, p, rs, :] for p in range(om_ref.shape[1])]
                                    + [or_ref[bi, rs, :], oc_ref[bi, rs, :]], axis=1)
            y = _dot(o_cat, w)
            v = alpha * x_ref[bi, rs, :] + gate * y
            mu = jnp.mean(v, axis=-1, keepdims=True)
            vc = v - mu
            var = jnp.mean(vc * vc, axis=-1, keepdims=True)
            o_ref[bi, rs, :] = vc * lax.rsqrt(var + EPS) * lng_ref[layer:layer + 1, :] + lnb_ref[layer:layer + 1, :]


def _final(x, mod_all, mod_row, o_mla, o_ret, o_conv, w_out, ln_g, ln_b, layer, tm, alpha):
    b, t, d = x.shape
    nb = math.gcd(b, max(_TOKEN_TILE // tm, 1)) if tm == t else 1
    const = lambda a: _layer_spec(a, layer)
    row = lambda n: pl.BlockSpec((nb, tm, n), lambda i, j: (i, j, 0))
    return pl.pallas_call(
        functools.partial(_final_kernel, alpha=alpha, layer=layer, mod_row=mod_row),
        grid=(b // nb, t // tm),
        in_specs=[row(d),
                  const(mod_all),
                  pl.BlockSpec((nb, o_mla.shape[1], tm, o_mla.shape[3]), lambda i, j: (i, 0, j, 0)),
                  row(RET_WIDTH), row(CONV_WIDTH),
                  const(w_out), _whole_spec(ln_g), _whole_spec(ln_b)],
        out_specs=row(d),
        out_shape=jax.ShapeDtypeStruct((b, t, d), F32),
        compiler_params=_params(("arbitrary", "arbitrary")),
        name="final",
    )(x, mod_all, o_mla, o_ret, o_conv, w_out, ln_g, ln_b)


_TOKEN_TILE = 1024
_CTX_TILE = 256
_QUERY_CHUNK = 256
_RET_CHUNK = 256


def _tile(n, pref):
    return pref if n % pref == 0 else n


def kernel(x, c, ctx, c_ctx, w_mod, b_mod, w_in, mla_q_norm, w_uq, mla_kv_norm, w_ukv, ret_decay_fwd, ret_decay_bwd,
           ret_gn_g, ret_gn_b, conv_dw, conv_dw_b, conv_ln_g, conv_ln_b, conv_pw, conv_pw_b, w_out, ln_g, ln_b):
    depth = w_mod.shape[0]
    b, t, d = x.shape
    t_ctx = ctx.shape[1]
    alpha = (2 * depth) ** 0.25
    n_mod_rows = 16
    cc = jnp.concatenate([c, c_ctx[None, :], jnp.zeros((n_mod_rows - b - 1, d), c.dtype)], axis=0)
    mod_all = _modulation(cc, w_mod, b_mod)
    tab_x = _rope_table_block(t, True)
    tab_c = _rope_table_block(t_ctx, False)

    weights = _prep_weights(w_in, w_uq, w_ukv)
    gq, gkv = mla_q_norm, mla_kv_norm
    dec = jnp.stack([ret_decay_fwd, ret_decay_bwd], axis=1).astype(F32)
    gn_g, gn_b = ret_gn_g, ret_gn_b
    conv = (conv_dw, conv_dw_b, conv_ln_g, conv_ln_b, conv_pw, conv_pw_b)
    tail = (w_out, ln_g, ln_b)
    tm_x, tm_c = _tile(t, _TOKEN_TILE), _tile(t_ctx, _CTX_TILE)
    tf_x, tf_c = _tile(t, _TOKEN_TILE), _tile(t_ctx, _CTX_TILE)
    tq_x, tq_c = _tile(t, _QUERY_CHUNK), _tile(t_ctx, _QUERY_CHUNK)
    chunk = math.gcd(math.gcd(t, t_ctx), _RET_CHUNK)

    hc = ctx
    for l in range(depth):
        need_ctx = l < depth - 1
        pc = _proj(hc, mod_all, b, tab_c, weights, gq, gkv, conv, l, tm_c, keys_only=not need_ctx)
        px = _proj(x, mod_all, None, tab_x, weights, gq, gkv, conv, l, tm_x)

        oc_ret, o_ret = _retention(dec, pc, px, gn_g, gn_b, l, chunk)
        o_mla = _attention(px["q"], px["gate_mla"], [pc["k"], px["k"]], [pc["vt"], px["vt"]], tq_x)
        x_new = _final(x, mod_all, None, o_mla, o_ret, px["conv"], *tail, layer=l, tm=tf_x, alpha=alpha)
        if need_ctx:
            oc_mla = _attention(pc["q"], pc["gate_mla"], [pc["k"]], [pc["vt"]], tq_c)
            hc = _final(hc, mod_all, b, oc_mla, oc_ret, pc["conv"], *tail, layer=l, tm=tf_c, alpha=alpha)
        x = x_new
    return x
```

```python
import functools
import math

import jax
import jax.numpy as jnp
import numpy as np
from jax import lax
from jax.experimental import pallas as pl
from jax.experimental.pallas import tpu as pltpu

GRID_W = 64
MLA_HEADS = 8
MLA_NOPE = 64
MLA_ROPE = 32
MLA_V = 64
MLA_QK = MLA_NOPE + MLA_ROPE
MLA_WIDTH = MLA_HEADS * MLA_V
Q_LORA = 256
KV_LORA = 128
RET_HEADS = 4
RET_DK = 64
RET_DV = 64
RET_WIDTH = RET_HEADS * RET_DV
CONV_WIDTH = 256
CONV_K = 31
ROPE_BASE = 10000.0
EPS = 1e-5

LANES = 128
HEAD_SLOT = LANES
VMEM_LIMIT_BYTES = 56 * 1024 * 1024
LOG2E = 1.4426950408889634

F32 = jnp.float32
BF16 = jnp.bfloat16

_NT = (((1,), (1,)), ((), ()))


def _dot(a, b):
    return jnp.dot(a, b, preferred_element_type=F32)


def _dot_nt(a, b):
    return lax.dot_general(a, b, _NT, preferred_element_type=F32)


def _params(semantics):
    return pltpu.CompilerParams(dimension_semantics=semantics, vmem_limit_bytes=VMEM_LIMIT_BYTES)


def _rot_cols(w, unit):
    xp = np if isinstance(w, np.ndarray) else jnp
    q = unit // 4
    w5 = w.reshape(w.shape[:-1] + (w.shape[-1] // unit, 2, 2, q))
    return xp.stack([-w5[..., 1, :], w5[..., 0, :]], axis=-2).reshape(w.shape)


def _rope_tables(length, unit, reps):
    d2 = unit // 2
    t = np.arange(length, dtype=np.int32)
    inv = np.float32(ROPE_BASE) ** (-np.arange(0, d2, 2, dtype=np.float32) / np.float32(d2))

    def half(pos):
        ang = pos.astype(np.float32)[:, None] * inv[None, :]
        return np.concatenate([np.cos(ang)] * 2, axis=-1), np.concatenate([np.sin(ang)] * 2, axis=-1)

    cr, sr = half(t // GRID_W)
    cc, sc = half(t % GRID_W)
    cos = np.concatenate([cr, cc], axis=-1).astype(np.float32)
    sin = np.concatenate([sr, sc], axis=-1).astype(np.float32)
    return np.tile(cos, (1, reps)), np.tile(sin, (1, reps))


def _rope_table_block(length, rotate):
    if rotate:
        cos_r, sin_r = _rope_tables(length, RET_DK, RET_HEADS)
        cos_m, sin_m = _rope_tables(length, MLA_ROPE, MLA_HEADS)
    else:
        cos_r = np.ones((length, RET_WIDTH), np.float32)
        sin_r = np.zeros((length, RET_WIDTH), np.float32)
        cos_m = np.ones((length, MLA_HEADS * MLA_ROPE), np.float32)
        sin_m = np.zeros((length, MLA_HEADS * MLA_ROPE), np.float32)
    kslot = np.concatenate([cos_m[:, :MLA_ROPE], sin_m[:, :MLA_ROPE],
                            np.zeros((length, LANES - 2 * MLA_ROPE), np.float32)], axis=-1)
    return jnp.asarray(np.concatenate([cos_r, sin_r, cos_m, sin_m, kslot], axis=-1))


def _placement_matrices():
    q_place = np.zeros((MLA_HEADS, MLA_ROPE, MLA_HEADS, HEAD_SLOT), np.float32)
    k_place = np.zeros((LANES, MLA_HEADS, HEAD_SLOT), np.float32)
    for d in range(MLA_ROPE):
        for h in range(MLA_HEADS):
            q_place[h, d, h, MLA_NOPE + d] = 1.0
            k_place[d, h, MLA_NOPE + d] = 1.0
            k_place[MLA_ROPE + d, h, MLA_NOPE + d] = 1.0
    return (q_place.reshape(MLA_HEADS * MLA_ROPE, MLA_HEADS * HEAD_SLOT),
            k_place.reshape(LANES, MLA_HEADS * HEAD_SLOT))


_IN_SIZES = dict(q=Q_LORA, kv=KV_LORA, kr=MLA_ROPE, g_mla=MLA_WIDTH, rq=RET_HEADS * RET_DK, rk=RET_HEADS * RET_DK,
                 rv=RET_WIDTH, g_ret=RET_WIDTH, glu=2 * CONV_WIDTH, g_conv=CONV_WIDTH)


def _offsets(sizes):
    out, at = {}, 0
    for name, n in sizes.items():
        out[name] = (at, at + n)
        at += n
    return out


_IN_COLS = _offsets(_IN_SIZES)


def _prep_weights(w_in, w_uq, w_ukv):
    depth, d, _ = w_in.shape
    w_t = jnp.swapaxes(w_in, 1, 2).astype(BF16)
    wkr = w_in[..., _IN_COLS["kr"][0]:_IN_COLS["kr"][1]]
    w_ks = jnp.concatenate([wkr, _rot_cols(wkr, MLA_ROPE), jnp.zeros((depth, d, LANES - 2 * MLA_ROPE), w_in.dtype)],
                           axis=-1)
    w_ks = jnp.swapaxes(w_ks, 1, 2).astype(BF16)

    uq = w_uq.reshape(depth, Q_LORA, MLA_HEADS, MLA_QK)
    pad = ((0, 0), (0, 0), (0, 0), (0, HEAD_SLOT - MLA_NOPE))
    uq_nope = jnp.pad(uq[..., :MLA_NOPE], pad).reshape(depth, Q_LORA, -1)
    uq_rope = uq[..., MLA_NOPE:].reshape(depth, Q_LORA, MLA_HEADS * MLA_ROPE)
    w_q2 = jnp.concatenate([uq_rope, _rot_cols(uq_rope, MLA_ROPE)], axis=-1).astype(BF16)
    q_place, k_place = (jnp.broadcast_to(jnp.asarray(m, BF16), (depth,) + m.shape) for m in _placement_matrices())
    w_qcomb = jnp.concatenate([uq_nope.astype(BF16), q_place], axis=1)

    ukv = w_ukv.reshape(depth, KV_LORA, MLA_HEADS, MLA_NOPE + MLA_V)
    uk = jnp.pad(ukv[..., :MLA_NOPE], pad).reshape(depth, KV_LORA, -1)
    w_kcomb = jnp.concatenate([uk.astype(BF16), k_place], axis=1)
    w_uvt = jnp.swapaxes(ukv[..., MLA_NOPE:].reshape(depth, KV_LORA, MLA_WIDTH), 1, 2).astype(BF16)
    return w_t, w_ks, w_q2, w_qcomb, w_kcomb, w_uvt


def _mod_kernel(c_ref, w_ref, b_ref, o_ref):
    c = c_ref[...]
    a = (c * jax.nn.sigmoid(c)).astype(BF16)
    o_ref[0, 0] = _dot(a, w_ref[0].astype(BF16)) + b_ref[pl.ds(pl.program_id(0), 1), :]


def _modulation(cc, w_mod, b_mod):
    depth, d, d3 = w_mod.shape
    nblk = d3 // d
    return pl.pallas_call(
        _mod_kernel,
        grid=(depth, nblk),
        in_specs=[pl.BlockSpec(cc.shape, lambda l, j: (0, 0)),
                  pl.BlockSpec((1, d, d), lambda l, j: (l, 0, j)),
                  pl.BlockSpec((depth, d), lambda l, j: (0, j))],
        out_specs=pl.BlockSpec((1, 1, cc.shape[0], d), lambda l, j: (l, j, 0, 0)),
        out_shape=jax.ShapeDtypeStruct((depth, nblk, cc.shape[0], d), F32),
        compiler_params=_params(("arbitrary", "arbitrary")),
        name="mod",
    )(cc, w_mod, b_mod)


_T_CR = 0
_T_SR = _T_CR + RET_WIDTH
_T_CM = _T_SR + RET_WIDTH
_T_SM = _T_CM + MLA_HEADS * MLA_ROPE
_T_KS = _T_SM + MLA_HEADS * MLA_ROPE
_T_END = _T_KS + LANES


_HALO = 16
_CONV_ROWS = 64


def _rms(x, g):
    return x * lax.rsqrt(jnp.mean(x * x, axis=-1, keepdims=True) + EPS) * g


def _proj_kernel(x_ref, xp_ref, xn_ref, mod_ref, tab_ref, rot_ref, wt_ref, wks_ref, gq_ref, gkv_ref, wq2_ref,
                 wqc_ref, wkc_ref, wuvt_ref, dw_ref, dwb_ref, clg_ref, clb_ref, pw_ref, pwb_ref, *out_refs, keys_only,
                 layer, mod_row):
    if keys_only:
        k_ref, vt_ref, rk_ref, rvt_ref = out_refs
    else:
        q_ref, k_ref, vt_ref, rq_ref, rk_ref, rvt_ref, gm_ref, gr_ref, oc_ref, upad, ushift = out_refs
    nb, tm = x_ref.shape[:2]
    seqs = [slice(bi * tm, (bi + 1) * tm) for bi in range(nb)]
    shift = _mod_row(mod_ref, 0, mod_row, pl.program_id(1))
    scale = _mod_row(mod_ref, 1, mod_row, pl.program_id(1))
    pair = 2 * HEAD_SLOT
    vec = lambda ref: ref[layer:layer + 1, :]

    def modulate(x):
        mu = jnp.mean(x, axis=-1, keepdims=True)
        xc = x - mu
        var = jnp.mean(xc * xc, axis=-1, keepdims=True)
        return (xc * lax.rsqrt(var + EPS) * (1.0 + scale) + shift).astype(BF16)

    u = modulate(x_ref[...].reshape(nb * tm, x_ref.shape[2]))

    def tab(lo, hi):
        return jnp.concatenate([tab_ref[:, lo:hi]] * nb, axis=0)

    def w_rows(name):
        lo, hi = _IN_COLS[name]
        return wt_ref[lo:hi, :]

    def seg(name):
        return _dot_nt(u, w_rows(name))

    def silu(v):
        return v * jax.nn.sigmoid(v)

    def store_pairs(ref, val):
        for bi, rows in enumerate(seqs):
            for p in range(MLA_HEADS // 2):
                ref[bi, p] = val[rows, p * pair:(p + 1) * pair]

    def keys(pkv, pks):
        nkv = _rms(pkv, vec(gkv_ref)).astype(BF16)
        ks = pks * tab(_T_KS, _T_END)
        store_pairs(k_ref, _dot(jnp.concatenate([nkv, ks.astype(BF16)], axis=1), wkc_ref[...]).astype(BF16))
        vt = _dot_nt(wuvt_ref[...], nkv).astype(BF16)
        for bi, rows in enumerate(seqs):
            vt_ref[bi] = vt[:, rows]

    def ret_rope(raw):
        rot = _dot(raw.astype(BF16), rot_ref[...])
        return raw * tab(_T_CR, _T_SR) + rot * tab(_T_SR, _T_CM)

    def ret_keys(prk):
        rk = ret_rope(prk) * (RET_DK ** -0.5)
        rvt = _dot_nt(w_rows("rv"), u).astype(BF16)
        for bi, rows in enumerate(seqs):
            rk_ref[bi] = rk[rows]
            rvt_ref[bi] = rvt[:, rows]

    if keys_only:
        keys(seg("kv"), _dot_nt(u, wks_ref[...]))
        ret_keys(seg("rk"))
        return

    i = pl.program_id(0)
    u_ext = jnp.concatenate([modulate(xp_ref[0]), u, modulate(xn_ref[0])], axis=0)
    glu = _dot_nt(u_ext, w_rows("glu"))
    uc = glu[:, :CONV_WIDTH] * jax.nn.sigmoid(glu[:, CONV_WIDTH:])
    row = lax.broadcasted_iota(jnp.int32, uc.shape, 0)
    inside = ((row >= _HALO) | (i > 0)) & ((row < _HALO + tm) | (i < pl.num_programs(0) - 1))
    upad[...] = jnp.where(inside, uc, 0.0)
    span = tm + 2 * _HALO - 8
    for r in range(1, 8):
        ushift[r - 1] = upad[r:r + span, :]
    base = _HALO - CONV_K // 2
    conv_blocks = []

    def conv_rows(n_blocks):
        for _ in range(n_blocks):
            rb = len(conv_blocks) * _CONV_ROWS
            acc = jnp.zeros((_CONV_ROWS, CONV_WIDTH), F32) + vec(dwb_ref)
            for j in range(CONV_K):
                a, r = divmod(base + j, 8)
                src = upad if r == 0 else ushift.at[r - 1]
                acc = acc + src[8 * a + rb:8 * a + rb + _CONV_ROWS, :] * dw_ref[j:j + 1, :]
            mu = jnp.mean(acc, axis=-1, keepdims=True)
            ac = acc - mu
            var = jnp.mean(ac * ac, axis=-1, keepdims=True)
            yc = ac * lax.rsqrt(var + EPS) * vec(clg_ref) + vec(clb_ref)
            conv_blocks.append((yc * jax.nn.sigmoid(yc)).astype(BF16))

    n_conv = tm // _CONV_ROWS
    per_stage = -(-n_conv // 4)

    pq, pkv, pks = seg("q"), seg("kv"), _dot_nt(u, wks_ref[...])
    prq, prk = seg("rq"), seg("rk")
    conv_rows(min(per_stage, n_conv - len(conv_blocks)))

    g_mla = silu(seg("g_mla"))
    for p in range(MLA_HEADS // 2):
        gm_ref[0, p] = g_mla[:, p * LANES:(p + 1) * LANES].astype(gm_ref.dtype)
    gr_ref[0] = silu(seg("g_ret")).astype(gr_ref.dtype)
    g_conv = silu(seg("g_conv"))
    conv_rows(min(per_stage, n_conv - len(conv_blocks)))

    nq = _rms(pq, vec(gq_ref)).astype(BF16)
    qc = _dot(nq, wq2_ref[...])
    keys(pkv, pks)
    conv_rows(min(per_stage, n_conv - len(conv_blocks)))

    rq_ref[0] = ret_rope(prq)
    ret_keys(prk)

    nr = MLA_HEADS * MLA_ROPE
    q_rope = qc[:, :nr] * tab_ref[:, _T_CM:_T_SM] + qc[:, nr:] * tab_ref[:, _T_SM:_T_KS]
    q = _dot(jnp.concatenate([nq, q_rope.astype(BF16)], axis=1), wqc_ref[...])
    store_pairs(q_ref, (q * (MLA_QK ** -0.5 * LOG2E)).astype(BF16))
    conv_rows(n_conv - len(conv_blocks))

    o_conv = _dot(jnp.concatenate(conv_blocks, axis=0), pw_ref[...].astype(BF16)) + vec(pwb_ref)
    oc_ref[0] = (o_conv * g_conv).astype(oc_ref.dtype)


def _layer_spec(a, layer):
    return pl.BlockSpec((None,) + a.shape[1:], lambda *_: (layer,) + (0,) * (a.ndim - 1),
                        pipeline_mode=pl.Buffered(1))


def _whole_spec(a):
    return pl.BlockSpec(a.shape, lambda *_: (0,) * a.ndim, pipeline_mode=pl.Buffered(1))


def _mod_row(mod_ref, k, shared_row, batch_index):
    if shared_row is None:
        return mod_ref[k, pl.ds(batch_index, 1), :]
    return mod_ref[k, shared_row:shared_row + 1, :]


def _proj(x, mod_all, mod_row, tab, weights, gq, gkv, conv, layer, tm, keys_only=False):
    b, t, d = x.shape
    w_t, w_ks, w_q2, w_qcomb, w_kcomb, w_uvt = weights
    const = lambda a: _layer_spec(a, layer)
    stack = keys_only and tm == t and mod_row is not None
    nb = math.gcd(b, max(_TOKEN_TILE // tm, 1)) if stack else 1
    row = lambda n: pl.BlockSpec((nb, tm, n), lambda i, j: (j, i, 0))
    col = lambda n: pl.BlockSpec((nb, n, tm), lambda i, j: (j, 0, i))
    n_pair, pair = MLA_HEADS // 2, 2 * HEAD_SLOT
    pairs = pl.BlockSpec((nb, n_pair, tm, pair), lambda i, j: (j, 0, i, 0))
    per_tile = tm // _HALO
    last = t // _HALO - 1
    halo_prev = pl.BlockSpec((1, _HALO, d), lambda i, j: (j, jnp.maximum(i * per_tile - 1, 0), 0))
    halo_next = pl.BlockSpec((1, _HALO, d), lambda i, j: (j, jnp.minimum((i + 1) * per_tile, last), 0))
    rot = jnp.asarray(_rot_cols(np.eye(RET_WIDTH, dtype=np.float32), RET_DK), BF16)
    sds = jax.ShapeDtypeStruct
    outs = dict(q=(sds((b, n_pair, t, pair), BF16), pairs),
                k=(sds((b, n_pair, t, pair), BF16), pairs),
                vt=(sds((b, MLA_WIDTH, t), BF16), col(MLA_WIDTH)),
                rq=(sds((b, t, RET_WIDTH), F32), row(RET_WIDTH)),
                rk=(sds((b, t, RET_WIDTH), F32), row(RET_WIDTH)),
                rvt=(sds((b, RET_WIDTH, t), BF16), col(RET_WIDTH)),
                gate_mla=(sds((b, n_pair, t, 2 * MLA_V), BF16),
                          pl.BlockSpec((1, n_pair, tm, 2 * MLA_V), lambda i, j: (j, 0, i, 0))),
                gate_ret=(sds((b, t, RET_WIDTH), BF16), row(RET_WIDTH)),
                conv=(sds((b, t, CONV_WIDTH), BF16), row(CONV_WIDTH)))
    names = ("k", "vt", "rk", "rvt") if keys_only else tuple(outs)
    scratch = [] if keys_only else [pltpu.VMEM((tm + 2 * _HALO, CONV_WIDTH), F32),
                                    pltpu.VMEM((7, tm + 2 * _HALO - 8, CONV_WIDTH), F32)]
    res = pl.pallas_call(
        functools.partial(_proj_kernel, keys_only=keys_only, layer=layer, mod_row=mod_row),
        grid=(t // tm, b // nb),
        in_specs=[row(d), halo_prev, halo_next,
                  const(mod_all),
                  pl.BlockSpec((tm, _T_END), lambda i, j: (i, 0)),
                  pl.BlockSpec(rot.shape, lambda i, j: (0, 0)),
                  const(w_t), const(w_ks), _whole_spec(gq), _whole_spec(gkv), const(w_q2), const(w_qcomb),
                  const(w_kcomb), const(w_uvt)] + [const(a) if a.ndim == 3 else _whole_spec(a) for a in conv],
        out_specs=tuple(outs[n][1] for n in names),
        out_shape=tuple(outs[n][0] for n in names),
        scratch_shapes=scratch,
        compiler_params=_params(("arbitrary", "arbitrary")),
        name="proj",
    )(x, x, x, mod_all, tab, rot, w_t, w_ks, gq, gkv, w_q2, w_qcomb, w_kcomb, w_uvt, *conv)
    res = dict(zip(names, res))
    res["vt"] = res["vt"].reshape(b, n_pair, 2 * MLA_V, t)
    return res


_ATTN_KEY_BLOCK = 512


def _attn_kernel(*refs, n_seg, tq):
    q_ref, g_ref = refs[:2]
    k_refs = refs[2:2 + n_seg]
    vt_refs = refs[2 + n_seg:2 + 2 * n_seg]
    o_ref = refs[2 + 2 * n_seg]
    s_bufs = refs[3 + 2 * n_seg:5 + 2 * n_seg]
    m_bufs = refs[5 + 2 * n_seg:7 + 2 * n_seg]
    n_pair = q_ref.shape[1]
    n_q = q_ref.shape[2] // tq
    n_items = n_pair * n_q
    lanes = [slice(j * HEAD_SLOT, (j + 1) * HEAD_SLOT) for j in range(2)]
    rows = [slice(j * MLA_V, (j + 1) * MLA_V) for j in range(2)]

    blocks, off = [], 0
    for si in range(n_seg):
        ts = k_refs[si].shape[2]
        kb = min(_ATTN_KEY_BLOCK, ts)
        blocks += [(si, b0, kb, off + b0) for b0 in range(0, ts, kb)]
        off += ts

    def locate(item):
        if isinstance(item, int):
            return item // n_q, (item % n_q) * tq
        pair = lax.div(item, jnp.int32(n_q))
        return pair, pl.multiple_of((item - pair * n_q) * tq, tq)

    def phase(nxt, cur):
        if nxt is not None:
            item_n, s_n, m_n = nxt
            pn, rn = locate(item_n)
            qn = [q_ref[0, pn, pl.ds(rn, tq), lanes[j]] for j in range(2)]
            mx = [None, None]
        if cur is not None:
            item_c, s_c, m_c = cur
            pc, rc = locate(item_c)
            mc = [m_c[j] for j in range(2)]
            acc, l = [None, None], [None, None]
        for si, b0, kb, o0 in blocks:
            for j in range(2):
                if nxt is not None:
                    s = _dot_nt(k_refs[si][0, pn, b0:b0 + kb, lanes[j]], qn[j])
                    s_n[j, o0:o0 + kb, :] = s
                    mb = jnp.max(s, axis=0, keepdims=True)
                    mx[j] = mb if mx[j] is None else jnp.maximum(mx[j], mb)
                if cur is not None:
                    p = jnp.exp2(s_c[j, o0:o0 + kb, :] - mc[j])
                    ls = jnp.sum(p, axis=0, keepdims=True)
                    pv = _dot(vt_refs[si][0, pc, rows[j], b0:b0 + kb], p.astype(BF16))
                    l[j] = ls if l[j] is None else l[j] + ls
                    acc[j] = pv if acc[j] is None else acc[j] + pv
        if nxt is not None:
            for j in range(2):
                m_n[j] = mx[j]
        if cur is not None:
            outs = [acc[j] * (1.0 / l[j]) for j in range(2)]
            gate = g_ref[0, pc, pl.ds(rc, tq), :].astype(F32)
            o_ref[0, pc, pl.ds(rc, tq), :] = (jnp.concatenate(outs, axis=0).T * gate).astype(o_ref.dtype)

    buf = [(s_bufs[0], m_bufs[0]), (s_bufs[1], m_bufs[1])]
    phase((0, *buf[0]), None)
    assert n_items % 2 == 0

    def body(i, carry):
        n = 2 * i
        phase((n + 1, *buf[1]), (n, *buf[0]))
        phase((jnp.minimum(n + 2, n_items - 1), *buf[0]), (n + 1, *buf[1]))
        return carry

    lax.fori_loop(0, n_items // 2, body, 0)


def _attention(q, gate, ks, vts, tq):
    b, n_pair, t, pair = q.shape
    n_seg = len(ks)
    n_keys = sum(k.shape[2] for k in ks)
    whole = lambda a: pl.BlockSpec((1,) + a.shape[1:], lambda i: (i, 0, 0, 0))
    score_buf = pltpu.VMEM((2, n_keys, tq), F32)
    max_buf = pltpu.VMEM((2, 1, tq), F32)
    return pl.pallas_call(
        functools.partial(_attn_kernel, n_seg=n_seg, tq=tq),
        grid=(b,),
        in_specs=[whole(q), whole(gate)] + [whole(k) for k in ks] + [whole(v) for v in vts],
        out_specs=pl.BlockSpec((1, n_pair, t, 2 * MLA_V), lambda i: (i, 0, 0, 0)),
        out_shape=jax.ShapeDtypeStruct((b, n_pair, t, 2 * MLA_V), BF16),
        scratch_shapes=[score_buf, score_buf, max_buf, max_buf],
        compiler_params=_params(("arbitrary",)),
        name="attention",
    )(q, gate, *ks, *vts)


def _log_sigmoid(x):
    return jnp.minimum(x, 0.0) - jnp.log1p(jnp.exp(-jnp.abs(x)))


def _ret_kernel(dec_ref, *refs, chunk, layer, ctx_out):
    if ctx_out:
        (cq_ref, ck_ref, cvt_ref, cgate_ref, q_ref, k_ref, vt_ref, gate_ref, gng_ref, gnb_ref, oc_ref, o_ref,
         sf_scr, sb_scr, kd_scr, qd_scr, dt_scr, cc_scr) = refs
    else:
        (ck_ref, cvt_ref, q_ref, k_ref, vt_ref, gate_ref, gng_ref, gnb_ref, o_ref,
         sf_scr, sb_scr, kd_scr, qd_scr, dt_scr, cc_scr) = refs
        cq_ref = cgate_ref = oc_ref = None
    c_len = chunk
    hp = pl.program_id(0)
    chunks = []
    for q_r, k_r, vt_r, g_r, o_r in ((cq_ref, ck_ref, cvt_ref, cgate_ref, oc_ref),
                                     (q_ref, k_ref, vt_ref, gate_ref, o_ref)):
        for c in range(k_r.shape[1] // c_len):
            chunks.append(dict(q=q_r, k=k_r, vt=vt_r, gate=g_r, out=o_r, c0=c * c_len, idx=len(chunks)))
    n_ctx = ck_ref.shape[1] // c_len

    @pl.when(pl.program_id(1) == 0)
    def _():
        r = lax.broadcasted_iota(jnp.int32, (c_len, LANES), 0).astype(F32)
        lane = lax.broadcasted_iota(jnp.int32, (c_len, LANES), 1)
        km = lax.broadcasted_iota(jnp.int32, (c_len, c_len), 0)
        qn = lax.broadcasted_iota(jnp.int32, (c_len, c_len), 1)
        diff = (qn - km).astype(F32)
        for j in range(2):
            h = hp * 2 + j
            lg_f = _log_sigmoid(jnp.full((c_len, LANES), dec_ref[layer, 0, h], F32))
            lg_b = _log_sigmoid(jnp.full((c_len, LANES), dec_ref[layer, 1, h], F32))
            own = ((lane >= j * RET_DK) & (lane < (j + 1) * RET_DK)).astype(F32)
            lg_f2 = _log_sigmoid(jnp.full((c_len, c_len), dec_ref[layer, 0, h], F32))
            lg_b2 = _log_sigmoid(jnp.full((c_len, c_len), dec_ref[layer, 1, h], F32))
            kd_scr[j] = jnp.concatenate([jnp.exp(lg_f * (c_len - 1.0 - r)) * own, jnp.exp(lg_b * r) * own, own], axis=1)
            qd_scr[j] = jnp.concatenate([jnp.exp(lg_f * (r + 1.0)), jnp.exp(lg_b * (c_len - r))], axis=1)
            dt_scr[j] = jnp.exp(jnp.where(diff >= 0, lg_f2 * diff, -lg_b2 * diff))
            cc_scr[j] = jnp.concatenate(
                [jnp.exp(_log_sigmoid(jnp.full((RET_DV, LANES), dec_ref[layer, 0, h], F32)) * c_len),
                 jnp.exp(_log_sigmoid(jnp.full((RET_DV, LANES), dec_ref[layer, 1, h], F32)) * c_len)], axis=1)

    heads = []
    for j in range(2):
        heads.append(dict(
            kdf=kd_scr[j, :, 0:LANES],
            kdb=kd_scr[j, :, LANES:2 * LANES],
            own=kd_scr[j, :, 2 * LANES:3 * LANES],
            qd=qd_scr[j],
            dt=dt_scr[j],
            cf=cc_scr[j, :, 0:LANES],
            cb=cc_scr[j, :, LANES:2 * LANES],
            rows=slice(j * RET_DV, (j + 1) * RET_DV),
        ))

    def rows(ref, ch):
        return ref[0, ch["c0"]:ch["c0"] + c_len, :]

    def vt_rows(ch, hd):
        return ch["vt"][0, hd["rows"], ch["c0"]:ch["c0"] + c_len]

    for ch in chunks:
        kc = rows(ch["k"], ch)
        for j, hd in enumerate(heads):
            sf_scr[j, ch["idx"]] = _dot(vt_rows(ch, hd), (kc * hd["kdf"]).astype(BF16))
            sb_scr[j, ch["idx"]] = _dot(vt_rows(ch, hd), (kc * hd["kdb"]).astype(BF16))

    backward_order = list(reversed(chunks[:n_ctx])) + list(reversed(chunks[n_ctx:]))
    for j, hd in enumerate(heads):
        for scr, decay, order in ((sf_scr, hd["cf"], chunks), (sb_scr, hd["cb"], backward_order)):
            state = jnp.zeros((RET_DV, LANES), F32)
            for ch in order:
                own = scr[j, ch["idx"]]
                scr[j, ch["idx"]] = state
                state = decay * state + own

    items = []
    for ch in chunks:
        if ch["q"] is not None:
            qc = rows(ch["q"], ch)
            items.append(dict(ch=ch, kc=rows(ch["k"], ch), qb=qc.astype(BF16), qq=jnp.concatenate([qc, qc], axis=1)))
    work = [(it, j, hd) for it in items for j, hd in enumerate(heads)]
    st = [_dot_nt((it["kc"] * hd["own"]).astype(BF16), it["qb"]) for it, j, hd in work]
    cross = [_dot_nt(jnp.concatenate([sf_scr[j, it["ch"]["idx"]], sb_scr[j, it["ch"]["idx"]]], axis=1).astype(BF16),
                     (it["qq"] * hd["qd"]).astype(BF16)) for it, j, hd in work]
    o = [_dot(vt_rows(it["ch"], hd), (s * hd["dt"]).astype(BF16)) + x for (it, j, hd), s, x in zip(work, st, cross)]
    ys = []
    for v in o:
        mu = jnp.mean(v, axis=0, keepdims=True)
        vc = v - mu
        var = jnp.mean(vc * vc, axis=0, keepdims=True)
        ys.append(vc * lax.rsqrt(var + EPS))
    for g, it in enumerate(items):
        ch = it["ch"]
        y = jnp.concatenate(ys[2 * g:2 * g + 2], axis=0).T
        y = y * gng_ref[layer:layer + 1, :] + gnb_ref[layer:layer + 1, :]
        ch["out"][0, ch["c0"]:ch["c0"] + c_len, :] = (y * rows(ch["gate"], ch).astype(F32)).astype(ch["out"].dtype)


def _retention(dec, ctx, lat, gn_g, gn_b, layer, chunk):
    b, t, _ = lat["rk"].shape
    t_ctx = ctx["rk"].shape[1]
    ctx_out = "rq" in ctx
    n_pair = RET_HEADS // 2
    pair = 2 * RET_DK
    state_buf = pltpu.VMEM((2, (t + t_ctx) // chunk, RET_DV, pair), F32)
    tables = [pltpu.VMEM((2, chunk, 3 * LANES), F32), pltpu.VMEM((2, chunk, 2 * LANES), F32),
              pltpu.VMEM((2, chunk, chunk), F32), pltpu.VMEM((2, RET_DV, 2 * LANES), F32)]
    seq = lambda n: pl.BlockSpec((1, n, pair), lambda h, i: (i, 0, h))
    vts = lambda n: pl.BlockSpec((1, 2 * RET_DV, n), lambda h, i: (i, h, 0))
    affine = pl.BlockSpec((gn_g.shape[0], pair), lambda h, i: (0, h))
    smem = pl.BlockSpec(memory_space=pltpu.SMEM)
    out = lambda n: jax.ShapeDtypeStruct((b, n, RET_WIDTH), BF16)
    if ctx_out:
        in_specs = [smem, seq(t_ctx), seq(t_ctx), vts(t_ctx), seq(t_ctx)]
        args = (dec, ctx["rq"], ctx["rk"], ctx["rvt"], ctx["gate_ret"])
        out_specs, out_shape = (seq(t_ctx), seq(t)), (out(t_ctx), out(t))
    else:
        in_specs, args = [smem, seq(t_ctx), vts(t_ctx)], (dec, ctx["rk"], ctx["rvt"])
        out_specs, out_shape = seq(t), out(t)
    in_specs += [seq(t), seq(t), vts(t), seq(t), affine, affine]
    args += (lat["rq"], lat["rk"], lat["rvt"], lat["gate_ret"], gn_g, gn_b)
    res = pl.pallas_call(
        functools.partial(_ret_kernel, chunk=chunk, layer=layer, ctx_out=ctx_out),
        grid=(n_pair, b),
        in_specs=in_specs,
        out_specs=out_specs,
        out_shape=out_shape,
        scratch_shapes=[state_buf, state_buf] + tables,
        compiler_params=_params(("arbitrary", "arbitrary")),
        name="retention",
    )(*args)
    return res if ctx_out else (None, res)


_FINAL_SUB_ROWS = 256


def _final_kernel(x_ref, mod_ref, om_ref, or_ref, oc_ref, wo_ref, lng_ref, lnb_ref, o_ref, *, alpha, layer, mod_row):
    nb, tm = x_ref.shape[:2]
    sub = _FINAL_SUB_ROWS if tm % _FINAL_SUB_ROWS == 0 else tm
    w = wo_ref[...].astype(BF16)
    for bi in range(nb):
        gate = _mod_row(mod_ref, 2, mod_row, pl.program_id(0) * nb + bi)
        for r0 in range(0, tm, sub):
            rs = slice(r0, r0 + sub)
            o_cat = jnp.concatenate([om_ref[bi, p, rs, :] for p in range(om_ref.shape[1])]
                                    + [or_ref[bi, rs, :], oc_ref[bi, rs, :]], axis=1)
            y = _dot(o_cat, w)
            v = alpha * x_ref[bi, rs, :] + gate * y
            mu = jnp.mean(v, axis=-1, keepdims=True)
            vc = v - mu
            var = jnp.mean(vc * vc, axis=-1, keepdims=True)
            o_ref[bi, rs, :] = vc * lax.rsqrt(var + EPS) * lng_ref[layer:layer + 1, :] + lnb_ref[layer:layer + 1, :]


def _final(x, mod_all, mod_row, o_mla, o_ret, o_conv, w_out, ln_g, ln_b, layer, tm, alpha):
    b, t, d = x.shape
    nb = math.gcd(b, max(_TOKEN_TILE // tm, 1)) if tm == t else 1
    const = lambda a: _layer_spec(a, layer)
    row = lambda n: pl.BlockSpec((nb, tm, n), lambda i, j: (i, j, 0))
    return pl.pallas_call(
        functools.partial(_final_kernel, alpha=alpha, layer=layer, mod_row=mod_row),
        grid=(b // nb, t // tm),
        in_specs=[row(d),
                  const(mod_all),
                  pl.BlockSpec((nb, o_mla.shape[1], tm, o_mla.shape[3]), lambda i, j: (i, 0, j, 0)),
                  row(RET_WIDTH), row(CONV_WIDTH),
                  const(w_out), _whole_spec(ln_g), _whole_spec(ln_b)],
        out_specs=row(d),
        out_shape=jax.ShapeDtypeStruct((b, t, d), F32),
        compiler_params=_params(("arbitrary", "arbitrary")),
        name="final",
    )(x, mod_all, o_mla, o_ret, o_conv, w_out, ln_g, ln_b)


_TOKEN_TILE = 1024
_CTX_TILE = 256
_QUERY_CHUNK = 256
_RET_CHUNK = 256


def _tile(n, pref):
    return pref if n % pref == 0 else n


def kernel(x, c, ctx, c_ctx, w_mod, b_mod, w_in, mla_q_norm, w_uq, mla_kv_norm, w_ukv, ret_decay_fwd, ret_decay_bwd,
           ret_gn_g, ret_gn_b, conv_dw, conv_dw_b, conv_ln_g, conv_ln_b, conv_pw, conv_pw_b, w_out, ln_g, ln_b):
    depth = w_mod.shape[0]
    b, t, d = x.shape
    t_ctx = ctx.shape[1]
    alpha = (2 * depth) ** 0.25
    n_mod_rows = 16
    cc = jnp.concatenate([c, c_ctx[None, :], jnp.zeros((n_mod_rows - b - 1, d), c.dtype)], axis=0)
    mod_all = _modulation(cc, w_mod, b_mod)
    tab_x = _rope_table_block(t, True)
    tab_c = _rope_table_block(t_ctx, False)

    weights = _prep_weights(w_in, w_uq, w_ukv)
    gq, gkv = mla_q_norm, mla_kv_norm
    dec = jnp.stack([ret_decay_fwd, ret_decay_bwd], axis=1).astype(F32)
    gn_g, gn_b = ret_gn_g, ret_gn_b
    conv = (conv_dw, conv_dw_b, conv_ln_g, conv_ln_b, conv_pw, conv_pw_b)
    tail = (w_out, ln_g, ln_b)
    tm_x, tm_c = _tile(t, _TOKEN_TILE), _tile(t_ctx, _CTX_TILE)
    tf_x, tf_c = _tile(t, _TOKEN_TILE), _tile(t_ctx, _CTX_TILE)
    tq_x, tq_c = _tile(t, _QUERY_CHUNK), _tile(t_ctx, _QUERY_CHUNK)
    chunk = math.gcd(math.gcd(t, t_ctx), _RET_CHUNK)

    hc = ctx
    for l in range(depth):
        need_ctx = l < depth - 1
        pc = _proj(hc, mod_all, b, tab_c, weights, gq, gkv, conv, l, tm_c, keys_only=not need_ctx)
        px = _proj(x, mod_all, None, tab_x, weights, gq, gkv, conv, l, tm_x)

        oc_ret, o_ret = _retention(dec, pc, px, gn_g, gn_b, l, chunk)
        o_mla = _attention(px["q"], px["gate_mla"], [pc["k"], px["k"]], [pc["vt"], px["vt"]], tq_x)
        x_new = _final(x, mod_all, None, o_mla, o_ret, px["conv"], *tail, layer=l, tm=tf_x, alpha=alpha)
        if need_ctx:
            oc_mla = _attention(pc["q"], pc["gate_mla"], [pc["k"]], [pc["vt"]], tq_c)
            hc = _final(hc, mod_all, b, oc_mla, oc_ret, pc["conv"], *tail, layer=l, tm=tf_c, alpha=alpha)
        x = x_new
    return x
```

```python
import functools
import math

import jax
import jax.numpy as jnp
import numpy as np
from jax import lax
from jax.experimental import pallas as pl
from jax.experimental.pallas import tpu as pltpu

GRID_W = 64
MLA_HEADS = 8
MLA_NOPE = 64
MLA_ROPE = 32
MLA_V = 64
MLA_QK = MLA_NOPE + MLA_ROPE
MLA_WIDTH = MLA_HEADS * MLA_V
Q_LORA = 256
KV_LORA = 128
RET_HEADS = 4
RET_DK = 64
RET_DV = 64
RET_WIDTH = RET_HEADS * RET_DV
CONV_WIDTH = 256
CONV_K = 31
ROPE_BASE = 10000.0
EPS = 1e-5

LANES = 128
HEAD_SLOT = LANES
VMEM_LIMIT_BYTES = 56 * 1024 * 1024
LOG2E = 1.4426950408889634

F32 = jnp.float32
BF16 = jnp.bfloat16

_NT = (((1,), (1,)), ((), ()))


def _dot(a, b):
    return jnp.dot(a, b, preferred_element_type=F32)


def _dot_nt(a, b):
    return lax.dot_general(a, b, _NT, preferred_element_type=F32)


def _params(semantics):
    return pltpu.CompilerParams(dimension_semantics=semantics, vmem_limit_bytes=VMEM_LIMIT_BYTES)


def _rot_cols(w, unit):
    xp = np if isinstance(w, np.ndarray) else jnp
    q = unit // 4
    w5 = w.reshape(w.shape[:-1] + (w.shape[-1] // unit, 2, 2, q))
    return xp.stack([-w5[..., 1, :], w5[..., 0, :]], axis=-2).reshape(w.shape)


def _rope_tables(length, unit, reps):
    d2 = unit // 2
    t = np.arange(length, dtype=np.int32)
    inv = np.float32(ROPE_BASE) ** (-np.arange(0, d2, 2, dtype=np.float32) / np.float32(d2))

    def half(pos):
        ang = pos.astype(np.float32)[:, None] * inv[None, :]
        return np.concatenate([np.cos(ang)] * 2, axis=-1), np.concatenate([np.sin(ang)] * 2, axis=-1)

    cr, sr = half(t // GRID_W)
    cc, sc = half(t % GRID_W)
    cos = np.concatenate([cr, cc], axis=-1).astype(np.float32)
    sin = np.concatenate([sr, sc], axis=-1).astype(np.float32)
    return np.tile(cos, (1, reps)), np.tile(sin, (1, reps))


def _rope_table_block(length, rotate):
    if rotate:
        cos_r, sin_r = _rope_tables(length, RET_DK, RET_HEADS)
        cos_m, sin_m = _rope_tables(length, MLA_ROPE, MLA_HEADS)
    else:
        cos_r = np.ones((length, RET_WIDTH), np.float32)
        sin_r = np.zeros((length, RET_WIDTH), np.float32)
        cos_m = np.ones((length, MLA_HEADS * MLA_ROPE), np.float32)
        sin_m = np.zeros((length, MLA_HEADS * MLA_ROPE), np.float32)
    kslot = np.concatenate([cos_m[:, :MLA_ROPE], sin_m[:, :MLA_ROPE],
                            np.zeros((length, LANES - 2 * MLA_ROPE), np.float32)], axis=-1)
    return jnp.asarray(np.concatenate([cos_r, sin_r, cos_m, sin_m, kslot], axis=-1))


def _placement_matrices():
    q_place = np.zeros((MLA_HEADS, MLA_ROPE, MLA_HEADS, HEAD_SLOT), np.float32)
    k_place = np.zeros((LANES, MLA_HEADS, HEAD_SLOT), np.float32)
    for d in range(MLA_ROPE):
        for h in range(MLA_HEADS):
            q_place[h, d, h, MLA_NOPE + d] = 1.0
            k_place[d, h, MLA_NOPE + d] = 1.0
            k_place[MLA_ROPE + d, h, MLA_NOPE + d] = 1.0
    return (q_place.reshape(MLA_HEADS * MLA_ROPE, MLA_HEADS * HEAD_SLOT),
            k_place.reshape(LANES, MLA_HEADS * HEAD_SLOT))


_IN_SIZES = dict(q=Q_LORA, kv=KV_LORA, kr=MLA_ROPE, g_mla=MLA_WIDTH, rq=RET_HEADS * RET_DK, rk=RET_HEADS * RET_DK,
                 rv=RET_WIDTH, g_ret=RET_WIDTH, glu=2 * CONV_WIDTH, g_conv=CONV_WIDTH)


def _offsets(sizes):
    out, at = {}, 0
    for name, n in sizes.items():
        out[name] = (at, at + n)
        at += n
    return out


_IN_COLS = _offsets(_IN_SIZES)


def _prep_weights(w_in, w_uq, w_ukv):
    depth, d, _ = w_in.shape
    w_t = jnp.swapaxes(w_in, 1, 2).astype(BF16)
    wkr = w_in[..., _IN_COLS["kr"][0]:_IN_COLS["kr"][1]]
    w_ks = jnp.concatenate([wkr, _rot_cols(wkr, MLA_ROPE), jnp.zeros((depth, d, LANES - 2 * MLA_ROPE), w_in.dtype)],
                           axis=-1)
    w_ks = jnp.swapaxes(w_ks, 1, 2).astype(BF16)

    uq = w_uq.reshape(depth, Q_LORA, MLA_HEADS, MLA_QK)
    pad = ((0, 0), (0, 0), (0, 0), (0, HEAD_SLOT - MLA_NOPE))
    uq_nope = jnp.pad(uq[..., :MLA_NOPE], pad).reshape(depth, Q_LORA, -1)
    uq_rope = uq[..., MLA_NOPE:].reshape(depth, Q_LORA, MLA_HEADS * MLA_ROPE)
    w_q2 = jnp.concatenate([uq_rope, _rot_cols(uq_rope, MLA_ROPE)], axis=-1).astype(BF16)
    q_place, k_place = (jnp.broadcast_to(jnp.asarray(m, BF16), (depth,) + m.shape) for m in _placement_matrices())
    w_qcomb = jnp.concatenate([uq_nope.astype(BF16), q_place], axis=1)

    ukv = w_ukv.reshape(depth, KV_LORA, MLA_HEADS, MLA_NOPE + MLA_V)
    uk = jnp.pad(ukv[..., :MLA_NOPE], pad).reshape(depth, KV_LORA, -1)
    w_kcomb = jnp.concatenate([uk.astype(BF16), k_place], axis=1)
    w_uvt = jnp.swapaxes(ukv[..., MLA_NOPE:].reshape(depth, KV_LORA, MLA_WIDTH), 1, 2).astype(BF16)
    return w_t, w_ks, w_q2, w_qcomb, w_kcomb, w_uvt


def _mod_kernel(c_ref, w_ref, b_ref, o_ref):
    c = c_ref[...]
    a = (c * jax.nn.sigmoid(c)).astype(BF16)
    o_ref[0, 0] = _dot(a, w_ref[0].astype(BF16)) + b_ref[pl.ds(pl.program_id(0), 1), :]


def _modulation(cc, w_mod, b_mod):
    depth, d, d3 = w_mod.shape
    nblk = d3 // d
    return pl.pallas_call(
        _mod_kernel,
        grid=(depth, nblk),
        in_specs=[pl.BlockSpec(cc.shape, lambda l, j: (0, 0)),
                  pl.BlockSpec((1, d, d), lambda l, j: (l, 0, j)),
                  pl.BlockSpec((depth, d), lambda l, j: (0, j))],
        out_specs=pl.BlockSpec((1, 1, cc.shape[0], d), lambda l, j: (l, j, 0, 0)),
        out_shape=jax.ShapeDtypeStruct((depth, nblk, cc.shape[0], d), F32),
        compiler_params=_params(("arbitrary", "arbitrary")),
        name="mod",
    )(cc, w_mod, b_mod)


_T_CR = 0
_T_SR = _T_CR + RET_WIDTH
_T_CM = _T_SR + RET_WIDTH
_T_SM = _T_CM + MLA_HEADS * MLA_ROPE
_T_KS = _T_SM + MLA_HEADS * MLA_ROPE
_T_END = _T_KS + LANES


_HALO = 16
_CONV_ROWS = 64


def _rms(x, g):
    return x * lax.rsqrt(jnp.mean(x * x, axis=-1, keepdims=True) + EPS) * g


def _proj_kernel(x_ref, xp_ref, xn_ref, mod_ref, tab_ref, rot_ref, wt_ref, wks_ref, gq_ref, gkv_ref, wq2_ref,
                 wqc_ref, wkc_ref, wuvt_ref, dw_ref, dwb_ref, clg_ref, clb_ref, pw_ref, pwb_ref, *out_refs, keys_only,
                 layer, mod_row):
    if keys_only:
        k_ref, vt_ref, rk_ref, rvt_ref = out_refs
    else:
        q_ref, k_ref, vt_ref, rq_ref, rk_ref, rvt_ref, gm_ref, gr_ref, oc_ref, upad, ushift = out_refs
    nb, tm = x_ref.shape[:2]
    seqs = [slice(bi * tm, (bi + 1) * tm) for bi in range(nb)]
    shift = _mod_row(mod_ref, 0, mod_row, pl.program_id(1))
    scale = _mod_row(mod_ref, 1, mod_row, pl.program_id(1))
    pair = 2 * HEAD_SLOT
    vec = lambda ref: ref[layer:layer + 1, :]

    def modulate(x):
        mu = jnp.mean(x, axis=-1, keepdims=True)
        xc = x - mu
        var = jnp.mean(xc * xc, axis=-1, keepdims=True)
        return (xc * lax.rsqrt(var + EPS) * (1.0 + scale) + shift).astype(BF16)

    u = modulate(x_ref[...].reshape(nb * tm, x_ref.shape[2]))

    def tab(lo, hi):
        return jnp.concatenate([tab_ref[:, lo:hi]] * nb, axis=0)

    def w_rows(name):
        lo, hi = _IN_COLS[name]
        return wt_ref[lo:hi, :]

    def seg(name):
        return _dot_nt(u, w_rows(name))

    def silu(v):
        return v * jax.nn.sigmoid(v)

    def store_pairs(ref, val):
        for bi, rows in enumerate(seqs):
            for p in range(MLA_HEADS // 2):
                ref[bi, p] = val[rows, p * pair:(p + 1) * pair]

    def keys(pkv, pks):
        nkv = _rms(pkv, vec(gkv_ref)).astype(BF16)
        ks = pks * tab(_T_KS, _T_END)
        store_pairs(k_ref, _dot(jnp.concatenate([nkv, ks.astype(BF16)], axis=1), wkc_ref[...]).astype(BF16))
        vt = _dot_nt(wuvt_ref[...], nkv).astype(BF16)
        for bi, rows in enumerate(seqs):
            vt_ref[bi] = vt[:, rows]

    def ret_rope(raw):
        rot = _dot(raw.astype(BF16), rot_ref[...])
        return raw * tab(_T_CR, _T_SR) + rot * tab(_T_SR, _T_CM)

    def ret_keys(prk):
        rk = ret_rope(prk) * (RET_DK ** -0.5)
        rvt = _dot_nt(w_rows("rv"), u).astype(BF16)
        for bi, rows in enumerate(seqs):
            rk_ref[bi] = rk[rows]
            rvt_ref[bi] = rvt[:, rows]

    if keys_only:
        keys(seg("kv"), _dot_nt(u, wks_ref[...]))
        ret_keys(seg("rk"))
        return

    padded = tm + 2 * _HALO
    if nb == 1:
        i = pl.program_id(0)
        u_ext = jnp.concatenate([modulate(xp_ref[0]), u, modulate(xn_ref[0])], axis=0)
        glu = _dot_nt(u_ext, w_rows("glu"))
        uc = glu[:, :CONV_WIDTH] * jax.nn.sigmoid(glu[:, CONV_WIDTH:])
        row = lax.broadcasted_iota(jnp.int32, uc.shape, 0)
        inside = ((row >= _HALO) | (i > 0)) & ((row < _HALO + tm) | (i < pl.num_programs(0) - 1))
        upad[...] = jnp.where(inside, uc, 0.0)
    else:
        glu = _dot_nt(u, w_rows("glu"))
        uc = glu[:, :CONV_WIDTH] * jax.nn.sigmoid(glu[:, CONV_WIDTH:])
        zeros = jnp.zeros((_HALO, CONV_WIDTH), F32)
        upad[...] = jnp.concatenate([part for rows in seqs for part in (zeros, uc[rows], zeros)], axis=0)
    span = nb * padded - 8
    for r in range(1, 8):
        ushift[r - 1] = upad[r:r + span, :]
    base = _HALO - CONV_K // 2
    conv_blocks = []

    def conv_rows(n_blocks):
        for _ in range(n_blocks):
            bi, k = divmod(len(conv_blocks), tm // _CONV_ROWS)
            rb = bi * padded + k * _CONV_ROWS
            acc = jnp.zeros((_CONV_ROWS, CONV_WIDTH), F32) + vec(dwb_ref)
            for j in range(CONV_K):
                a, r = divmod(base + j, 8)
                src = upad if r == 0 else ushift.at[r - 1]
                acc = acc + src[8 * a + rb:8 * a + rb + _CONV_ROWS, :] * dw_ref[j:j + 1, :]
            mu = jnp.mean(acc, axis=-1, keepdims=True)
            ac = acc - mu
            var = jnp.mean(ac * ac, axis=-1, keepdims=True)
            yc = ac * lax.rsqrt(var + EPS) * vec(clg_ref) + vec(clb_ref)
            conv_blocks.append((yc * jax.nn.sigmoid(yc)).astype(BF16))

    n_conv = nb * tm // _CONV_ROWS
    per_stage = -(-n_conv // 4)

    pq, pkv, pks = seg("q"), seg("kv"), _dot_nt(u, wks_ref[...])
    prq, prk = seg("rq"), seg("rk")
    conv_rows(min(per_stage, n_conv - len(conv_blocks)))

    g_mla = silu(seg("g_mla"))
    for bi, rows in enumerate(seqs):
        for p in range(MLA_HEADS // 2):
            gm_ref[bi, p] = g_mla[rows, p * LANES:(p + 1) * LANES].astype(gm_ref.dtype)
    g_ret = silu(seg("g_ret")).astype(gr_ref.dtype)
    for bi, rows in enumerate(seqs):
        gr_ref[bi] = g_ret[rows]
    g_conv = silu(seg("g_conv"))
    conv_rows(min(per_stage, n_conv - len(conv_blocks)))

    nq = _rms(pq, vec(gq_ref)).astype(BF16)
    qc = _dot(nq, wq2_ref[...])
    keys(pkv, pks)
    conv_rows(min(per_stage, n_conv - len(conv_blocks)))

    rq = ret_rope(prq)
    for bi, rows in enumerate(seqs):
        rq_ref[bi] = rq[rows]
    ret_keys(prk)

    nr = MLA_HEADS * MLA_ROPE
    q_rope = qc[:, :nr] * tab(_T_CM, _T_SM) + qc[:, nr:] * tab(_T_SM, _T_KS)
    q = _dot(jnp.concatenate([nq, q_rope.astype(BF16)], axis=1), wqc_ref[...])
    store_pairs(q_ref, (q * (MLA_QK ** -0.5 * LOG2E)).astype(BF16))
    conv_rows(n_conv - len(conv_blocks))

    o_conv = _dot(jnp.concatenate(conv_blocks, axis=0), pw_ref[...].astype(BF16)) + vec(pwb_ref)
    o_conv = (o_conv * g_conv).astype(oc_ref.dtype)
    for bi, rows in enumerate(seqs):
        oc_ref[bi] = o_conv[rows]


def _layer_spec(a, layer):
    return pl.BlockSpec((None,) + a.shape[1:], lambda *_: (layer,) + (0,) * (a.ndim - 1),
                        pipeline_mode=pl.Buffered(1))


def _whole_spec(a):
    return pl.BlockSpec(a.shape, lambda *_: (0,) * a.ndim, pipeline_mode=pl.Buffered(1))


def _sequences_per_step(b, t, tm):
    return math.gcd(b, max(_TOKEN_TILE // tm, 1)) if tm == t else 1


def _mod_row(mod_ref, k, shared_row, batch_index):
    if shared_row is None:
        return mod_ref[k, pl.ds(batch_index, 1), :]
    return mod_ref[k, shared_row:shared_row + 1, :]


def _proj(x, mod_all, mod_row, tab, weights, gq, gkv, conv, layer, tm, keys_only=False):
    b, t, d = x.shape
    w_t, w_ks, w_q2, w_qcomb, w_kcomb, w_uvt = weights
    const = lambda a: _layer_spec(a, layer)
    nb = _sequences_per_step(b, t, tm) if mod_row is not None else 1
    row = lambda n: pl.BlockSpec((nb, tm, n), lambda i, j: (j, i, 0))
    col = lambda n: pl.BlockSpec((nb, n, tm), lambda i, j: (j, 0, i))
    n_pair, pair = MLA_HEADS // 2, 2 * HEAD_SLOT
    pairs = pl.BlockSpec((nb, n_pair, tm, pair), lambda i, j: (j, 0, i, 0))
    per_tile = tm // _HALO
    last = t // _HALO - 1
    halo_prev = pl.BlockSpec((1, _HALO, d), lambda i, j: (j, jnp.maximum(i * per_tile - 1, 0), 0))
    halo_next = pl.BlockSpec((1, _HALO, d), lambda i, j: (j, jnp.minimum((i + 1) * per_tile, last), 0))
    rot = jnp.asarray(_rot_cols(np.eye(RET_WIDTH, dtype=np.float32), RET_DK), BF16)
    sds = jax.ShapeDtypeStruct
    outs = dict(q=(sds((b, n_pair, t, pair), BF16), pairs),
                k=(sds((b, n_pair, t, pair), BF16), pairs),
                vt=(sds((b, MLA_WIDTH, t), BF16), col(MLA_WIDTH)),
                rq=(sds((b, t, RET_WIDTH), F32), row(RET_WIDTH)),
                rk=(sds((b, t, RET_WIDTH), F32), row(RET_WIDTH)),
                rvt=(sds((b, RET_WIDTH, t), BF16), col(RET_WIDTH)),
                gate_mla=(sds((b, n_pair, t, 2 * MLA_V), BF16),
                          pl.BlockSpec((nb, n_pair, tm, 2 * MLA_V), lambda i, j: (j, 0, i, 0))),
                gate_ret=(sds((b, t, RET_WIDTH), BF16), row(RET_WIDTH)),
                conv=(sds((b, t, CONV_WIDTH), BF16), row(CONV_WIDTH)))
    names = ("k", "vt", "rk", "rvt") if keys_only else tuple(outs)
    conv_rows = nb * (tm + 2 * _HALO)
    scratch = [] if keys_only else [pltpu.VMEM((conv_rows, CONV_WIDTH), F32),
                                    pltpu.VMEM((7, conv_rows - 8, CONV_WIDTH), F32)]
    res = pl.pallas_call(
        functools.partial(_proj_kernel, keys_only=keys_only, layer=layer, mod_row=mod_row),
        grid=(t // tm, b // nb),
        in_specs=[row(d), halo_prev, halo_next,
                  const(mod_all),
                  pl.BlockSpec((tm, _T_END), lambda i, j: (i, 0)),
                  pl.BlockSpec(rot.shape, lambda i, j: (0, 0)),
                  const(w_t), const(w_ks), _whole_spec(gq), _whole_spec(gkv), const(w_q2), const(w_qcomb),
                  const(w_kcomb), const(w_uvt)] + [const(a) if a.ndim == 3 else _whole_spec(a) for a in conv],
        out_specs=tuple(outs[n][1] for n in names),
        out_shape=tuple(outs[n][0] for n in names),
        scratch_shapes=scratch,
        compiler_params=_params(("arbitrary", "arbitrary")),
        name="proj",
    )(x, x, x, mod_all, tab, rot, w_t, w_ks, gq, gkv, w_q2, w_qcomb, w_kcomb, w_uvt, *conv)
    res = dict(zip(names, res))
    res["vt"] = res["vt"].reshape(b, n_pair, 2 * MLA_V, t)
    return res


_ATTN_KEY_BLOCK = 512


def _attn_kernel(*refs, n_seg, tq):
    q_ref, g_ref = refs[:2]
    k_refs = refs[2:2 + n_seg]
    vt_refs = refs[2 + n_seg:2 + 2 * n_seg]
    o_ref = refs[2 + 2 * n_seg]
    s_bufs = refs[3 + 2 * n_seg:5 + 2 * n_seg]
    m_bufs = refs[5 + 2 * n_seg:7 + 2 * n_seg]
    n_pair = q_ref.shape[1]
    n_q = q_ref.shape[2] // tq
    n_items = n_pair * n_q
    lanes = [slice(j * HEAD_SLOT, (j + 1) * HEAD_SLOT) for j in range(2)]
    rows = [slice(j * MLA_V, (j + 1) * MLA_V) for j in range(2)]

    blocks, off = [], 0
    for si in range(n_seg):
        ts = k_refs[si].shape[2]
        kb = min(_ATTN_KEY_BLOCK, ts)
        blocks += [(si, b0, kb, off + b0) for b0 in range(0, ts, kb)]
        off += ts

    def locate(item):
        if isinstance(item, int):
            return item // n_q, (item % n_q) * tq
        pair = lax.div(item, jnp.int32(n_q))
        return pair, pl.multiple_of((item - pair * n_q) * tq, tq)

    def phase(nxt, cur):
        if nxt is not None:
            item_n, s_n, m_n = nxt
            pn, rn = locate(item_n)
            qn = [q_ref[0, pn, pl.ds(rn, tq), lanes[j]] for j in range(2)]
            mx = [None, None]
        if cur is not None:
            item_c, s_c, m_c = cur
            pc, rc = locate(item_c)
            mc = [m_c[j] for j in range(2)]
            acc, l = [None, None], [None, None]
        for si, b0, kb, o0 in blocks:
            for j in range(2):
                if nxt is not None:
                    s = _dot_nt(k_refs[si][0, pn, b0:b0 + kb, lanes[j]], qn[j])
                    s_n[j, o0:o0 + kb, :] = s
                    mb = jnp.max(s, axis=0, keepdims=True)
                    mx[j] = mb if mx[j] is None else jnp.maximum(mx[j], mb)
                if cur is not None:
                    p = jnp.exp2(s_c[j, o0:o0 + kb, :] - mc[j])
                    ls = jnp.sum(p, axis=0, keepdims=True)
                    pv = _dot(vt_refs[si][0, pc, rows[j], b0:b0 + kb], p.astype(BF16))
                    l[j] = ls if l[j] is None else l[j] + ls
                    acc[j] = pv if acc[j] is None else acc[j] + pv
        if nxt is not None:
            for j in range(2):
                m_n[j] = mx[j]
        if cur is not None:
            outs = [acc[j] * (1.0 / l[j]) for j in range(2)]
            gate = g_ref[0, pc, pl.ds(rc, tq), :].astype(F32)
            o_ref[0, pc, pl.ds(rc, tq), :] = (jnp.concatenate(outs, axis=0).T * gate).astype(o_ref.dtype)

    buf = [(s_bufs[0], m_bufs[0]), (s_bufs[1], m_bufs[1])]
    phase((0, *buf[0]), None)
    assert n_items % 2 == 0

    def body(i, carry):
        n = 2 * i
        phase((n + 1, *buf[1]), (n, *buf[0]))
        phase((jnp.minimum(n + 2, n_items - 1), *buf[0]), (n + 1, *buf[1]))
        return carry

    lax.fori_loop(0, n_items // 2, body, 0)


def _attention(q, gate, ks, vts, tq):
    b, n_pair, t, pair = q.shape
    n_seg = len(ks)
    n_keys = sum(k.shape[2] for k in ks)
    whole = lambda a: pl.BlockSpec((1,) + a.shape[1:], lambda i: (i, 0, 0, 0))
    score_buf = pltpu.VMEM((2, n_keys, tq), F32)
    max_buf = pltpu.VMEM((2, 1, tq), F32)
    return pl.pallas_call(
        functools.partial(_attn_kernel, n_seg=n_seg, tq=tq),
        grid=(b,),
        in_specs=[whole(q), whole(gate)] + [whole(k) for k in ks] + [whole(v) for v in vts],
        out_specs=pl.BlockSpec((1, n_pair, t, 2 * MLA_V), lambda i: (i, 0, 0, 0)),
        out_shape=jax.ShapeDtypeStruct((b, n_pair, t, 2 * MLA_V), BF16),
        scratch_shapes=[score_buf, score_buf, max_buf, max_buf],
        compiler_params=_params(("arbitrary",)),
        name="attention",
    )(q, gate, *ks, *vts)


def _log_sigmoid(x):
    return jnp.minimum(x, 0.0) - jnp.log1p(jnp.exp(-jnp.abs(x)))


def _ret_kernel(dec_ref, *refs, chunk, layer, ctx_out):
    if ctx_out:
        (cq_ref, ck_ref, cvt_ref, cgate_ref, q_ref, k_ref, vt_ref, gate_ref, gng_ref, gnb_ref, oc_ref, o_ref,
         sf_scr, sb_scr, kd_scr, qd_scr, dt_scr, cc_scr) = refs
    else:
        (ck_ref, cvt_ref, q_ref, k_ref, vt_ref, gate_ref, gng_ref, gnb_ref, o_ref,
         sf_scr, sb_scr, kd_scr, qd_scr, dt_scr, cc_scr) = refs
        cq_ref = cgate_ref = oc_ref = None
    c_len = chunk
    hp = pl.program_id(0)
    chunks = []
    for q_r, k_r, vt_r, g_r, o_r in ((cq_ref, ck_ref, cvt_ref, cgate_ref, oc_ref),
                                     (q_ref, k_ref, vt_ref, gate_ref, o_ref)):
        for c in range(k_r.shape[1] // c_len):
            chunks.append(dict(q=q_r, k=k_r, vt=vt_r, gate=g_r, out=o_r, c0=c * c_len, idx=len(chunks)))
    n_ctx = ck_ref.shape[1] // c_len

    @pl.when(pl.program_id(1) == 0)
    def _():
        r = lax.broadcasted_iota(jnp.int32, (c_len, LANES), 0).astype(F32)
        lane = lax.broadcasted_iota(jnp.int32, (c_len, LANES), 1)
        km = lax.broadcasted_iota(jnp.int32, (c_len, c_len), 0)
        qn = lax.broadcasted_iota(jnp.int32, (c_len, c_len), 1)
        diff = (qn - km).astype(F32)
        for j in range(2):
            h = hp * 2 + j
            lg_f = _log_sigmoid(jnp.full((c_len, LANES), dec_ref[layer, 0, h], F32))
            lg_b = _log_sigmoid(jnp.full((c_len, LANES), dec_ref[layer, 1, h], F32))
            own = ((lane >= j * RET_DK) & (lane < (j + 1) * RET_DK)).astype(F32)
            lg_f2 = _log_sigmoid(jnp.full((c_len, c_len), dec_ref[layer, 0, h], F32))
            lg_b2 = _log_sigmoid(jnp.full((c_len, c_len), dec_ref[layer, 1, h], F32))
            kd_scr[j] = jnp.concatenate([jnp.exp(lg_f * (c_len - 1.0 - r)) * own, jnp.exp(lg_b * r) * own, own], axis=1)
            qd_scr[j] = jnp.concatenate([jnp.exp(lg_f * (r + 1.0)), jnp.exp(lg_b * (c_len - r))], axis=1)
            dt_scr[j] = jnp.exp(jnp.where(diff >= 0, lg_f2 * diff, -lg_b2 * diff))
            cc_scr[j] = jnp.concatenate(
                [jnp.exp(_log_sigmoid(jnp.full((RET_DV, LANES), dec_ref[layer, 0, h], F32)) * c_len),
                 jnp.exp(_log_sigmoid(jnp.full((RET_DV, LANES), dec_ref[layer, 1, h], F32)) * c_len)], axis=1)

    heads = []
    for j in range(2):
        heads.append(dict(
            kdf=kd_scr[j, :, 0:LANES],
            kdb=kd_scr[j, :, LANES:2 * LANES],
            own=kd_scr[j, :, 2 * LANES:3 * LANES],
            qd=qd_scr[j],
            dt=dt_scr[j],
            cf=cc_scr[j, :, 0:LANES],
            cb=cc_scr[j, :, LANES:2 * LANES],
            rows=slice(j * RET_DV, (j + 1) * RET_DV),
        ))

    def rows(ref, ch):
        return ref[0, ch["c0"]:ch["c0"] + c_len, :]

    def vt_rows(ch, hd):
        return ch["vt"][0, hd["rows"], ch["c0"]:ch["c0"] + c_len]

    for ch in chunks:
        kc = rows(ch["k"], ch)
        for j, hd in enumerate(heads):
            sf_scr[j, ch["idx"]] = _dot(vt_rows(ch, hd), (kc * hd["kdf"]).astype(BF16))
            sb_scr[j, ch["idx"]] = _dot(vt_rows(ch, hd), (kc * hd["kdb"]).astype(BF16))

    backward_order = list(reversed(chunks[:n_ctx])) + list(reversed(chunks[n_ctx:]))
    for j, hd in enumerate(heads):
        for scr, decay, order in ((sf_scr, hd["cf"], chunks), (sb_scr, hd["cb"], backward_order)):
            state = jnp.zeros((RET_DV, LANES), F32)
            for ch in order:
                own = scr[j, ch["idx"]]
                scr[j, ch["idx"]] = state
                state = decay * state + own

    items = []
    for ch in chunks:
        if ch["q"] is not None:
            qc = rows(ch["q"], ch)
            items.append(dict(ch=ch, kc=rows(ch["k"], ch), qb=qc.astype(BF16), qq=jnp.concatenate([qc, qc], axis=1)))
    work = [(it, j, hd) for it in items for j, hd in enumerate(heads)]
    st = [_dot_nt((it["kc"] * hd["own"]).astype(BF16), it["qb"]) for it, j, hd in work]
    cross = [_dot_nt(jnp.concatenate([sf_scr[j, it["ch"]["idx"]], sb_scr[j, it["ch"]["idx"]]], axis=1).astype(BF16),
                     (it["qq"] * hd["qd"]).astype(BF16)) for it, j, hd in work]
    o = [_dot(vt_rows(it["ch"], hd), (s * hd["dt"]).astype(BF16)) + x for (it, j, hd), s, x in zip(work, st, cross)]
    ys = []
    for v in o:
        mu = jnp.mean(v, axis=0, keepdims=True)
        vc = v - mu
        var = jnp.mean(vc * vc, axis=0, keepdims=True)
        ys.append(vc * lax.rsqrt(var + EPS))
    for g, it in enumerate(items):
        ch = it["ch"]
        y = jnp.concatenate(ys[2 * g:2 * g + 2], axis=0).T
        y = y * gng_ref[layer:layer + 1, :] + gnb_ref[layer:layer + 1, :]
        ch["out"][0, ch["c0"]:ch["c0"] + c_len, :] = (y * rows(ch["gate"], ch).astype(F32)).astype(ch["out"].dtype)


def _retention(dec, ctx, lat, gn_g, gn_b, layer, chunk):
    b, t, _ = lat["rk"].shape
    t_ctx = ctx["rk"].shape[1]
    ctx_out = "rq" in ctx
    n_pair = RET_HEADS // 2
    pair = 2 * RET_DK
    state_buf = pltpu.VMEM((2, (t + t_ctx) // chunk, RET_DV, pair), F32)
    tables = [pltpu.VMEM((2, chunk, 3 * LANES), F32), pltpu.VMEM((2, chunk, 2 * LANES), F32),
              pltpu.VMEM((2, chunk, chunk), F32), pltpu.VMEM((2, RET_DV, 2 * LANES), F32)]
    seq = lambda n: pl.BlockSpec((1, n, pair), lambda h, i: (i, 0, h))
    vts = lambda n: pl.BlockSpec((1, 2 * RET_DV, n), lambda h, i: (i, h, 0))
    affine = pl.BlockSpec((gn_g.shape[0], pair), lambda h, i: (0, h))
    smem = pl.BlockSpec(memory_space=pltpu.SMEM)
    out = lambda n: jax.ShapeDtypeStruct((b, n, RET_WIDTH), BF16)
    if ctx_out:
        in_specs = [smem, seq(t_ctx), seq(t_ctx), vts(t_ctx), seq(t_ctx)]
        args = (dec, ctx["rq"], ctx["rk"], ctx["rvt"], ctx["gate_ret"])
        out_specs, out_shape = (seq(t_ctx), seq(t)), (out(t_ctx), out(t))
    else:
        in_specs, args = [smem, seq(t_ctx), vts(t_ctx)], (dec, ctx["rk"], ctx["rvt"])
        out_specs, out_shape = seq(t), out(t)
    in_specs += [seq(t), seq(t), vts(t), seq(t), affine, affine]
    args += (lat["rq"], lat["rk"], lat["rvt"], lat["gate_ret"], gn_g, gn_b)
    res = pl.pallas_call(
        functools.partial(_ret_kernel, chunk=chunk, layer=layer, ctx_out=ctx_out),
        grid=(n_pair, b),
        in_specs=in_specs,
        out_specs=out_specs,
        out_shape=out_shape,
        scratch_shapes=[state_buf, state_buf] + tables,
        compiler_params=_params(("arbitrary", "arbitrary")),
        name="retention",
    )(*args)
    return res if ctx_out else (None, res)


_FINAL_SUB_ROWS = 256


def _final_kernel(x_ref, mod_ref, om_ref, or_ref, oc_ref, wo_ref, lng_ref, lnb_ref, o_ref, *, alpha, layer, mod_row):
    nb, tm = x_ref.shape[:2]
    sub = _FINAL_SUB_ROWS if tm % _FINAL_SUB_ROWS == 0 else tm
    w = wo_ref[...].astype(BF16)
    for bi in range(nb):
        gate = _mod_row(mod_ref, 2, mod_row, pl.program_id(0) * nb + bi)
        for r0 in range(0, tm, sub):
            rs = slice(r0, r0 + sub)
            o_cat = jnp.concatenate([om_ref[bi, p, rs, :] for p in range(om_ref.shape[1])]
                                    + [or_ref[bi, rs, :], oc_ref[bi, rs, :]], axis=1)
            y = _dot(o_cat, w)
            v = alpha * x_ref[bi, rs, :] + gate * y
            mu = jnp.mean(v, axis=-1, keepdims=True)
            vc = v - mu
            var = jnp.mean(vc * vc, axis=-1, keepdims=True)
            o_ref[bi, rs, :] = vc * lax.rsqrt(var + EPS) * lng_ref[layer:layer + 1, :] + lnb_ref[layer:layer + 1, :]


def _final(x, mod_all, mod_row, o_mla, o_ret, o_conv, w_out, ln_g, ln_b, layer, tm, alpha):
    b, t, d = x.shape
    nb = _sequences_per_step(b, t, tm)
    const = lambda a: _layer_spec(a, layer)
    row = lambda n: pl.BlockSpec((nb, tm, n), lambda i, j: (i, j, 0))
    return pl.pallas_call(
        functools.partial(_final_kernel, alpha=alpha, layer=layer, mod_row=mod_row),
        grid=(b // nb, t // tm),
        in_specs=[row(d),
                  const(mod_all),
                  pl.BlockSpec((nb, o_mla.shape[1], tm, o_mla.shape[3]), lambda i, j: (i, 0, j, 0)),
                  row(RET_WIDTH), row(CONV_WIDTH),
                  const(w_out), _whole_spec(ln_g), _whole_spec(ln_b)],
        out_specs=row(d),
        out_shape=jax.ShapeDtypeStruct((b, t, d), F32),
        compiler_params=_params(("arbitrary", "arbitrary")),
        name="final",
    )(x, mod_all, o_mla, o_ret, o_conv, w_out, ln_g, ln_b)


_TOKEN_TILE = 1024
_CTX_TILE = 256
_QUERY_CHUNK = 256
_RET_CHUNK = 256


def _tile(n, pref):
    return pref if n % pref == 0 else n


def kernel(x, c, ctx, c_ctx, w_mod, b_mod, w_in, mla_q_norm, w_uq, mla_kv_norm, w_ukv, ret_decay_fwd, ret_decay_bwd,
           ret_gn_g, ret_gn_b, conv_dw, conv_dw_b, conv_ln_g, conv_ln_b, conv_pw, conv_pw_b, w_out, ln_g, ln_b):
    depth = w_mod.shape[0]
    b, t, d = x.shape
    t_ctx = ctx.shape[1]
    alpha = (2 * depth) ** 0.25
    n_mod_rows = 16
    cc = jnp.concatenate([c, c_ctx[None, :], jnp.zeros((n_mod_rows - b - 1, d), c.dtype)], axis=0)
    mod_all = _modulation(cc, w_mod, b_mod)
    tab_x = _rope_table_block(t, True)
    tab_c = _rope_table_block(t_ctx, False)

    weights = _prep_weights(w_in, w_uq, w_ukv)
    gq, gkv = mla_q_norm, mla_kv_norm
    dec = jnp.stack([ret_decay_fwd, ret_decay_bwd], axis=1).astype(F32)
    gn_g, gn_b = ret_gn_g, ret_gn_b
    conv = (conv_dw, conv_dw_b, conv_ln_g, conv_ln_b, conv_pw, conv_pw_b)
    tail = (w_out, ln_g, ln_b)
    tm_x, tm_c = _tile(t, _TOKEN_TILE), _tile(t_ctx, _CTX_TILE)
    tf_x, tf_c = _tile(t, _TOKEN_TILE), _tile(t_ctx, _CTX_TILE)
    tq_x, tq_c = _tile(t, _QUERY_CHUNK), _tile(t_ctx, _QUERY_CHUNK)
    chunk = math.gcd(math.gcd(t, t_ctx), _RET_CHUNK)

    hc = ctx
    for l in range(depth):
        need_ctx = l < depth - 1
        pc = _proj(hc, mod_all, b, tab_c, weights, gq, gkv, conv, l, tm_c, keys_only=not need_ctx)
        px = _proj(x, mod_all, None, tab_x, weights, gq, gkv, conv, l, tm_x)

        oc_ret, o_ret = _retention(dec, pc, px, gn_g, gn_b, l, chunk)
        o_mla = _attention(px["q"], px["gate_mla"], [pc["k"], px["k"]], [pc["vt"], px["vt"]], tq_x)
        x_new = _final(x, mod_all, None, o_mla, o_ret, px["conv"], *tail, layer=l, tm=tf_x, alpha=alpha)
        if need_ctx:
            oc_mla = _attention(pc["q"], pc["gate_mla"], [pc["k"]], [pc["vt"]], tq_c)
            hc = _final(hc, mod_all, b, oc_mla, oc_ret, pc["conv"], *tail, layer=l, tm=tf_c, alpha=alpha)
        x = x_new
    return x
```

```python
import functools
import math

import jax
import jax.numpy as jnp
import numpy as np
from jax import lax
from jax.experimental import pallas as pl
from jax.experimental.pallas import tpu as pltpu

GRID_W = 64
MLA_HEADS = 8
MLA_NOPE = 64
MLA_ROPE = 32
MLA_V = 64
MLA_QK = MLA_NOPE + MLA_ROPE
MLA_WIDTH = MLA_HEADS * MLA_V
Q_LORA = 256
KV_LORA = 128
RET_HEADS = 4
RET_DK = 64
RET_DV = 64
RET_WIDTH = RET_HEADS * RET_DV
CONV_WIDTH = 256
CONV_K = 31
ROPE_BASE = 10000.0
EPS = 1e-5

LANES = 128
HEAD_SLOT = LANES
VMEM_LIMIT_BYTES = 56 * 1024 * 1024
LOG2E = 1.4426950408889634

F32 = jnp.float32
BF16 = jnp.bfloat16

_NT = (((1,), (1,)), ((), ()))


def _dot(a, b):
    return jnp.dot(a, b, preferred_element_type=F32)


def _dot_nt(a, b):
    return lax.dot_general(a, b, _NT, preferred_element_type=F32)


def _params(semantics):
    return pltpu.CompilerParams(dimension_semantics=semantics, vmem_limit_bytes=VMEM_LIMIT_BYTES)


def _rot_cols(w, unit):
    xp = np if isinstance(w, np.ndarray) else jnp
    q = unit // 4
    w5 = w.reshape(w.shape[:-1] + (w.shape[-1] // unit, 2, 2, q))
    return xp.stack([-w5[..., 1, :], w5[..., 0, :]], axis=-2).reshape(w.shape)


def _rope_tables(length, unit, reps):
    d2 = unit // 2
    t = np.arange(length, dtype=np.int32)
    inv = np.float32(ROPE_BASE) ** (-np.arange(0, d2, 2, dtype=np.float32) / np.float32(d2))

    def half(pos):
        ang = pos.astype(np.float32)[:, None] * inv[None, :]
        return np.concatenate([np.cos(ang)] * 2, axis=-1), np.concatenate([np.sin(ang)] * 2, axis=-1)

    cr, sr = half(t // GRID_W)
    cc, sc = half(t % GRID_W)
    cos = np.concatenate([cr, cc], axis=-1).astype(np.float32)
    sin = np.concatenate([sr, sc], axis=-1).astype(np.float32)
    return np.tile(cos, (1, reps)), np.tile(sin, (1, reps))


def _rope_table_block(length, rotate):
    if rotate:
        cos_r, sin_r = _rope_tables(length, RET_DK, RET_HEADS)
        cos_m, sin_m = _rope_tables(length, MLA_ROPE, MLA_HEADS)
    else:
        cos_r = np.ones((length, RET_WIDTH), np.float32)
        sin_r = np.zeros((length, RET_WIDTH), np.float32)
        cos_m = np.ones((length, MLA_HEADS * MLA_ROPE), np.float32)
        sin_m = np.zeros((length, MLA_HEADS * MLA_ROPE), np.float32)
    kslot = np.concatenate([cos_m[:, :MLA_ROPE], sin_m[:, :MLA_ROPE],
                            np.zeros((length, LANES - 2 * MLA_ROPE), np.float32)], axis=-1)
    return jnp.asarray(np.concatenate([cos_r, sin_r, cos_m, sin_m, kslot], axis=-1))


def _placement_matrices():
    q_place = np.zeros((MLA_HEADS, MLA_ROPE, MLA_HEADS, HEAD_SLOT), np.float32)
    k_place = np.zeros((LANES, MLA_HEADS, HEAD_SLOT), np.float32)
    for d in range(MLA_ROPE):
        for h in range(MLA_HEADS):
            q_place[h, d, h, MLA_NOPE + d] = 1.0
            k_place[d, h, MLA_NOPE + d] = 1.0
            k_place[MLA_ROPE + d, h, MLA_NOPE + d] = 1.0
    return (q_place.reshape(MLA_HEADS * MLA_ROPE, MLA_HEADS * HEAD_SLOT),
            k_place.reshape(LANES, MLA_HEADS * HEAD_SLOT))


_IN_SIZES = dict(q=Q_LORA, kv=KV_LORA, kr=MLA_ROPE, g_mla=MLA_WIDTH, rq=RET_HEADS * RET_DK, rk=RET_HEADS * RET_DK,
                 rv=RET_WIDTH, g_ret=RET_WIDTH, glu=2 * CONV_WIDTH, g_conv=CONV_WIDTH)


def _offsets(sizes):
    out, at = {}, 0
    for name, n in sizes.items():
        out[name] = (at, at + n)
        at += n
    return out


_IN_COLS = _offsets(_IN_SIZES)


def _prep_weights(w_in, w_uq, w_ukv):
    depth, d, _ = w_in.shape
    w_t = jnp.swapaxes(w_in, 1, 2).astype(BF16)
    wkr = w_in[..., _IN_COLS["kr"][0]:_IN_COLS["kr"][1]]
    w_ks = jnp.concatenate([wkr, _rot_cols(wkr, MLA_ROPE), jnp.zeros((depth, d, LANES - 2 * MLA_ROPE), w_in.dtype)],
                           axis=-1)
    w_ks = jnp.swapaxes(w_ks, 1, 2).astype(BF16)

    uq = w_uq.reshape(depth, Q_LORA, MLA_HEADS, MLA_QK)
    pad = ((0, 0), (0, 0), (0, 0), (0, HEAD_SLOT - MLA_NOPE))
    uq_nope = jnp.pad(uq[..., :MLA_NOPE], pad).reshape(depth, Q_LORA, -1)
    uq_rope = uq[..., MLA_NOPE:].reshape(depth, Q_LORA, MLA_HEADS * MLA_ROPE)
    w_q2 = jnp.concatenate([uq_rope, _rot_cols(uq_rope, MLA_ROPE)], axis=-1).astype(BF16)
    q_place, k_place = (jnp.broadcast_to(jnp.asarray(m, BF16), (depth,) + m.shape) for m in _placement_matrices())
    w_qcomb = jnp.concatenate([uq_nope.astype(BF16), q_place], axis=1)

    ukv = w_ukv.reshape(depth, KV_LORA, MLA_HEADS, MLA_NOPE + MLA_V)
    uk = jnp.pad(ukv[..., :MLA_NOPE], pad).reshape(depth, KV_LORA, -1)
    w_kcomb = jnp.concatenate([uk.astype(BF16), k_place], axis=1)
    w_uvt = jnp.swapaxes(ukv[..., MLA_NOPE:].reshape(depth, KV_LORA, MLA_WIDTH), 1, 2).astype(BF16)
    return w_t, w_ks, w_q2, w_qcomb, w_kcomb, w_uvt


def _mod_kernel(c_ref, w_ref, b_ref, o_ref):
    c = c_ref[...]
    a = (c * jax.nn.sigmoid(c)).astype(BF16)
    o_ref[0, 0] = _dot(a, w_ref[0].astype(BF16)) + b_ref[pl.ds(pl.program_id(0), 1), :]


def _modulation(cc, w_mod, b_mod):
    depth, d, d3 = w_mod.shape
    nblk = d3 // d
    return pl.pallas_call(
        _mod_kernel,
        grid=(depth, nblk),
        in_specs=[pl.BlockSpec(cc.shape, lambda l, j: (0, 0)),
                  pl.BlockSpec((1, d, d), lambda l, j: (l, 0, j)),
                  pl.BlockSpec((depth, d), lambda l, j: (0, j))],
        out_specs=pl.BlockSpec((1, 1, cc.shape[0], d), lambda l, j: (l, j, 0, 0)),
        out_shape=jax.ShapeDtypeStruct((depth, nblk, cc.shape[0], d), F32),
        compiler_params=_params(("arbitrary", "arbitrary")),
        name="mod",
    )(cc, w_mod, b_mod)


_T_CR = 0
_T_SR = _T_CR + RET_WIDTH
_T_CM = _T_SR + RET_WIDTH
_T_SM = _T_CM + MLA_HEADS * MLA_ROPE
_T_KS = _T_SM + MLA_HEADS * MLA_ROPE
_T_END = _T_KS + LANES


_HALO = 16
_CONV_ROWS = 64


def _rms(x, g):
    return x * lax.rsqrt(jnp.mean(x * x, axis=-1, keepdims=True) + EPS) * g


def _proj_kernel(x_ref, xp_ref, xn_ref, mod_ref, tab_ref, rot_ref, wt_ref, wks_ref, gq_ref, gkv_ref, wq2_ref,
                 wqc_ref, wkc_ref, wuvt_ref, dw_ref, dwb_ref, clg_ref, clb_ref, pw_ref, pwb_ref, *out_refs, keys_only,
                 layer, mod_row):
    if keys_only:
        k_ref, vt_ref, rk_ref, rvt_ref = out_refs
    else:
        q_ref, k_ref, vt_ref, rq_ref, rk_ref, rvt_ref, gm_ref, gr_ref, oc_ref, upad, ushift = out_refs
    nb, tm = x_ref.shape[:2]
    seqs = [slice(bi * tm, (bi + 1) * tm) for bi in range(nb)]
    shift = _mod_row(mod_ref, 0, mod_row, pl.program_id(1))
    scale = _mod_row(mod_ref, 1, mod_row, pl.program_id(1))
    pair = 2 * HEAD_SLOT
    vec = lambda ref: ref[layer:layer + 1, :]

    def modulate(x):
        mu = jnp.mean(x, axis=-1, keepdims=True)
        xc = x - mu
        var = jnp.mean(xc * xc, axis=-1, keepdims=True)
        return (xc * lax.rsqrt(var + EPS) * (1.0 + scale) + shift).astype(BF16)

    u = modulate(x_ref[...].reshape(nb * tm, x_ref.shape[2]))

    def tab(lo, hi):
        return jnp.concatenate([tab_ref[:, lo:hi]] * nb, axis=0)

    def w_rows(name):
        lo, hi = _IN_COLS[name]
        return wt_ref[lo:hi, :]

    def seg(name):
        return _dot_nt(u, w_rows(name))

    def silu(v):
        return v * jax.nn.sigmoid(v)

    def store_pairs(ref, val):
        for bi, rows in enumerate(seqs):
            for p in range(MLA_HEADS // 2):
                ref[bi, p] = val[rows, p * pair:(p + 1) * pair]

    def keys(pkv, pks):
        nkv = _rms(pkv, vec(gkv_ref)).astype(BF16)
        ks = pks * tab(_T_KS, _T_END)
        store_pairs(k_ref, _dot(jnp.concatenate([nkv, ks.astype(BF16)], axis=1), wkc_ref[...]).astype(BF16))
        vt = _dot_nt(wuvt_ref[...], nkv).astype(BF16)
        for bi, rows in enumerate(seqs):
            vt_ref[bi] = vt[:, rows]

    def ret_rope(raw):
        rot = _dot(raw.astype(BF16), rot_ref[...])
        return raw * tab(_T_CR, _T_SR) + rot * tab(_T_SR, _T_CM)

    def ret_keys(prk):
        rk = ret_rope(prk) * (RET_DK ** -0.5)
        rvt = _dot_nt(w_rows("rv"), u).astype(BF16)
        for bi, rows in enumerate(seqs):
            rk_ref[bi] = rk[rows]
            rvt_ref[bi] = rvt[:, rows]

    if keys_only:
        keys(seg("kv"), _dot_nt(u, wks_ref[...]))
        ret_keys(seg("rk"))
        return

    padded = tm + 2 * _HALO
    if nb == 1:
        i = pl.program_id(0)
        u_ext = jnp.concatenate([modulate(xp_ref[0]), u, modulate(xn_ref[0])], axis=0)
        glu = _dot_nt(u_ext, w_rows("glu"))
        uc = glu[:, :CONV_WIDTH] * jax.nn.sigmoid(glu[:, CONV_WIDTH:])
        row = lax.broadcasted_iota(jnp.int32, uc.shape, 0)
        inside = ((row >= _HALO) | (i > 0)) & ((row < _HALO + tm) | (i < pl.num_programs(0) - 1))
        upad[...] = jnp.where(inside, uc, 0.0)
    else:
        glu = _dot_nt(u, w_rows("glu"))
        uc = glu[:, :CONV_WIDTH] * jax.nn.sigmoid(glu[:, CONV_WIDTH:])
        zeros = jnp.zeros((_HALO, CONV_WIDTH), F32)
        upad[...] = jnp.concatenate([part for rows in seqs for part in (zeros, uc[rows], zeros)], axis=0)
    span = nb * padded - 8
    for r in range(1, 8):
        ushift[r - 1] = upad[r:r + span, :]
    base = _HALO - CONV_K // 2
    conv_blocks = []

    def conv_rows(n_blocks):
        for _ in range(n_blocks):
            bi, k = divmod(len(conv_blocks), tm // _CONV_ROWS)
            rb = bi * padded + k * _CONV_ROWS
            acc = jnp.zeros((_CONV_ROWS, CONV_WIDTH), F32) + vec(dwb_ref)
            for j in range(CONV_K):
                a, r = divmod(base + j, 8)
                src = upad if r == 0 else ushift.at[r - 1]
                acc = acc + src[8 * a + rb:8 * a + rb + _CONV_ROWS, :] * dw_ref[j:j + 1, :]
            mu = jnp.mean(acc, axis=-1, keepdims=True)
            ac = acc - mu
            var = jnp.mean(ac * ac, axis=-1, keepdims=True)
            yc = ac * lax.rsqrt(var + EPS) * vec(clg_ref) + vec(clb_ref)
            conv_blocks.append((yc * jax.nn.sigmoid(yc)).astype(BF16))

    n_conv = nb * tm // _CONV_ROWS
    per_stage = -(-n_conv // 4)

    pq, pkv, pks = seg("q"), seg("kv"), _dot_nt(u, wks_ref[...])
    prq, prk = seg("rq"), seg("rk")
    conv_rows(min(per_stage, n_conv - len(conv_blocks)))

    g_mla = silu(seg("g_mla"))
    for bi, rows in enumerate(seqs):
        for p in range(MLA_HEADS // 2):
            gm_ref[bi, p] = g_mla[rows, p * LANES:(p + 1) * LANES].astype(gm_ref.dtype)
    g_ret = silu(seg("g_ret")).astype(gr_ref.dtype)
    for bi, rows in enumerate(seqs):
        gr_ref[bi] = g_ret[rows]
    g_conv = silu(seg("g_conv"))
    conv_rows(min(per_stage, n_conv - len(conv_blocks)))

    nq = _rms(pq, vec(gq_ref)).astype(BF16)
    qc = _dot(nq, wq2_ref[...])
    keys(pkv, pks)
    conv_rows(min(per_stage, n_conv - len(conv_blocks)))

    rq = ret_rope(prq)
    for bi, rows in enumerate(seqs):
        rq_ref[bi] = rq[rows]
    ret_keys(prk)

    nr = MLA_HEADS * MLA_ROPE
    q_rope = qc[:, :nr] * tab(_T_CM, _T_SM) + qc[:, nr:] * tab(_T_SM, _T_KS)
    q = _dot(jnp.concatenate([nq, q_rope.astype(BF16)], axis=1), wqc_ref[...])
    store_pairs(q_ref, (q * (MLA_QK ** -0.5 * LOG2E)).astype(BF16))
    conv_rows(n_conv - len(conv_blocks))

    o_conv = _dot(jnp.concatenate(conv_blocks, axis=0), pw_ref[...].astype(BF16)) + vec(pwb_ref)
    o_conv = (o_conv * g_conv).astype(oc_ref.dtype)
    for bi, rows in enumerate(seqs):
        oc_ref[bi] = o_conv[rows]


def _layer_spec(a, layer):
    return pl.BlockSpec((None,) + a.shape[1:], lambda *_: (layer,) + (0,) * (a.ndim - 1),
                        pipeline_mode=pl.Buffered(1))


def _whole_spec(a):
    return pl.BlockSpec(a.shape, lambda *_: (0,) * a.ndim, pipeline_mode=pl.Buffered(1))


def _sequences_per_step(b, t, tm):
    return math.gcd(b, max(_TOKEN_TILE // tm, 1)) if tm == t else 1


def _mod_row(mod_ref, k, shared_row, batch_index):
    if shared_row is None:
        return mod_ref[k, pl.ds(batch_index, 1), :]
    return mod_ref[k, shared_row:shared_row + 1, :]


def _proj(x, mod_all, mod_row, tab, weights, gq, gkv, conv, layer, tm, keys_only=False):
    b, t, d = x.shape
    w_t, w_ks, w_q2, w_qcomb, w_kcomb, w_uvt = weights
    const = lambda a: _layer_spec(a, layer)
    nb = _sequences_per_step(b, t, tm) if mod_row is not None else 1
    row = lambda n: pl.BlockSpec((nb, tm, n), lambda i, j: (j, i, 0))
    col = lambda n: pl.BlockSpec((nb, n, tm), lambda i, j: (j, 0, i))
    n_pair, pair = MLA_HEADS // 2, 2 * HEAD_SLOT
    pairs = pl.BlockSpec((nb, n_pair, tm, pair), lambda i, j: (j, 0, i, 0))
    per_tile = tm // _HALO
    last = t // _HALO - 1
    halo_prev = pl.BlockSpec((1, _HALO, d), lambda i, j: (j, jnp.maximum(i * per_tile - 1, 0), 0))
    halo_next = pl.BlockSpec((1, _HALO, d), lambda i, j: (j, jnp.minimum((i + 1) * per_tile, last), 0))
    rot = jnp.asarray(_rot_cols(np.eye(RET_WIDTH, dtype=np.float32), RET_DK), BF16)
    sds = jax.ShapeDtypeStruct
    outs = dict(q=(sds((b, n_pair, t, pair), BF16), pairs),
                k=(sds((b, n_pair, t, pair), BF16), pairs),
                vt=(sds((b, MLA_WIDTH, t), BF16), col(MLA_WIDTH)),
                rq=(sds((b, t, RET_WIDTH), F32), row(RET_WIDTH)),
                rk=(sds((b, t, RET_WIDTH), F32), row(RET_WIDTH)),
                rvt=(sds((b, RET_WIDTH, t), BF16), col(RET_WIDTH)),
                gate_mla=(sds((b, n_pair, t, 2 * MLA_V), BF16),
                          pl.BlockSpec((nb, n_pair, tm, 2 * MLA_V), lambda i, j: (j, 0, i, 0))),
                gate_ret=(sds((b, t, RET_WIDTH), BF16), row(RET_WIDTH)),
                conv=(sds((b, t, CONV_WIDTH), BF16), row(CONV_WIDTH)))
    names = ("k", "vt", "rk", "rvt") if keys_only else tuple(outs)
    conv_rows = nb * (tm + 2 * _HALO)
    scratch = [] if keys_only else [pltpu.VMEM((conv_rows, CONV_WIDTH), F32),
                                    pltpu.VMEM((7, conv_rows - 8, CONV_WIDTH), F32)]
    res = pl.pallas_call(
        functools.partial(_proj_kernel, keys_only=keys_only, layer=layer, mod_row=mod_row),
        grid=(t // tm, b // nb),
        in_specs=[row(d), halo_prev, halo_next,
                  const(mod_all),
                  pl.BlockSpec((tm, _T_END), lambda i, j: (i, 0)),
                  pl.BlockSpec(rot.shape, lambda i, j: (0, 0)),
                  const(w_t), const(w_ks), _whole_spec(gq), _whole_spec(gkv), const(w_q2), const(w_qcomb),
                  const(w_kcomb), const(w_uvt)] + [const(a) if a.ndim == 3 else _whole_spec(a) for a in conv],
        out_specs=tuple(outs[n][1] for n in names),
        out_shape=tuple(outs[n][0] for n in names),
        scratch_shapes=scratch,
        compiler_params=_params(("arbitrary", "arbitrary")),
        name="proj",
    )(x, x, x, mod_all, tab, rot, w_t, w_ks, gq, gkv, w_q2, w_qcomb, w_kcomb, w_uvt, *conv)
    res = dict(zip(names, res))
    res["vt"] = res["vt"].reshape(b, n_pair, 2 * MLA_V, t)
    return res


_ATTN_KEY_BLOCK = 512


def _attn_kernel(*refs, n_seg, tq):
    q_ref, g_ref = refs[:2]
    k_refs = refs[2:2 + n_seg]
    vt_refs = refs[2 + n_seg:2 + 2 * n_seg]
    o_ref = refs[2 + 2 * n_seg]
    s_bufs = refs[3 + 2 * n_seg:5 + 2 * n_seg]
    m_bufs = refs[5 + 2 * n_seg:7 + 2 * n_seg]
    n_pair = q_ref.shape[1]
    n_q = q_ref.shape[2] // tq
    n_items = n_pair * n_q
    lanes = [slice(j * HEAD_SLOT, (j + 1) * HEAD_SLOT) for j in range(2)]
    rows = [slice(j * MLA_V, (j + 1) * MLA_V) for j in range(2)]

    blocks, off = [], 0
    for si in range(n_seg):
        ts = k_refs[si].shape[2]
        kb = min(_ATTN_KEY_BLOCK, ts)
        blocks += [(si, b0, kb, off + b0) for b0 in range(0, ts, kb)]
        off += ts

    def locate(item):
        if isinstance(item, int):
            return item // n_q, (item % n_q) * tq
        pair = lax.div(item, jnp.int32(n_q))
        return pair, pl.multiple_of((item - pair * n_q) * tq, tq)

    def phase(nxt, cur):
        if nxt is not None:
            item_n, s_n, m_n = nxt
            pn, rn = locate(item_n)
            qn = [q_ref[0, pn, pl.ds(rn, tq), lanes[j]] for j in range(2)]
            mx = [None, None]
        if cur is not None:
            item_c, s_c, m_c = cur
            pc, rc = locate(item_c)
            mc = [m_c[j] for j in range(2)]
            acc, l = [None, None], [None, None]
        for si, b0, kb, o0 in blocks:
            for j in range(2):
                if nxt is not None:
                    s = _dot_nt(k_refs[si][0, pn, b0:b0 + kb, lanes[j]], qn[j])
                    s_n[j, o0:o0 + kb, :] = s
                    mb = jnp.max(s, axis=0, keepdims=True)
                    mx[j] = mb if mx[j] is None else jnp.maximum(mx[j], mb)
                if cur is not None:
                    p = jnp.exp2(s_c[j, o0:o0 + kb, :] - mc[j])
                    ls = jnp.sum(p, axis=0, keepdims=True)
                    pv = _dot(vt_refs[si][0, pc, rows[j], b0:b0 + kb], p.astype(BF16))
                    l[j] = ls if l[j] is None else l[j] + ls
                    acc[j] = pv if acc[j] is None else acc[j] + pv
        if nxt is not None:
            for j in range(2):
                m_n[j] = mx[j]
        if cur is not None:
            outs = [acc[j] * (1.0 / l[j]) for j in range(2)]
            gate = g_ref[0, pc, pl.ds(rc, tq), :].astype(F32)
            o_ref[0, pc, pl.ds(rc, tq), :] = (jnp.concatenate(outs, axis=0).T * gate).astype(o_ref.dtype)

    buf = [(s_bufs[0], m_bufs[0]), (s_bufs[1], m_bufs[1])]
    phase((0, *buf[0]), None)
    assert n_items % 2 == 0

    def body(i, carry):
        n = 2 * i
        phase((n + 1, *buf[1]), (n, *buf[0]))
        phase((jnp.minimum(n + 2, n_items - 1), *buf[0]), (n + 1, *buf[1]))
        return carry

    lax.fori_loop(0, n_items // 2, body, 0)


def _attention(q, gate, ks, vts, tq):
    b, n_pair, t, pair = q.shape
    n_seg = len(ks)
    n_keys = sum(k.shape[2] for k in ks)
    whole = lambda a: pl.BlockSpec((1,) + a.shape[1:], lambda i: (i, 0, 0, 0))
    score_buf = pltpu.VMEM((2, n_keys, tq), F32)
    max_buf = pltpu.VMEM((2, 1, tq), F32)
    return pl.pallas_call(
        functools.partial(_attn_kernel, n_seg=n_seg, tq=tq),
        grid=(b,),
        in_specs=[whole(q), whole(gate)] + [whole(k) for k in ks] + [whole(v) for v in vts],
        out_specs=pl.BlockSpec((1, n_pair, t, 2 * MLA_V), lambda i: (i, 0, 0, 0)),
        out_shape=jax.ShapeDtypeStruct((b, n_pair, t, 2 * MLA_V), BF16),
        scratch_shapes=[score_buf, score_buf, max_buf, max_buf],
        compiler_params=_params(("arbitrary",)),
        name="attention",
    )(q, gate, *ks, *vts)


def _log_sigmoid(x):
    return jnp.minimum(x, 0.0) - jnp.log1p(jnp.exp(-jnp.abs(x)))


def _ret_kernel(dec_ref, *refs, chunk, layer, ctx_out):
    if ctx_out:
        (cq_ref, ck_ref, cvt_ref, cgate_ref, q_ref, k_ref, vt_ref, gate_ref, gng_ref, gnb_ref, oc_ref, o_ref,
         sf_scr, sb_scr, kd_scr, qd_scr, dt_scr, cc_scr) = refs
    else:
        (ck_ref, cvt_ref, q_ref, k_ref, vt_ref, gate_ref, gng_ref, gnb_ref, o_ref,
         sf_scr, sb_scr, kd_scr, qd_scr, dt_scr, cc_scr) = refs
        cq_ref = cgate_ref = oc_ref = None
    c_len = chunk
    hp = pl.program_id(0)
    chunks = []
    for q_r, k_r, vt_r, g_r, o_r in ((cq_ref, ck_ref, cvt_ref, cgate_ref, oc_ref),
                                     (q_ref, k_ref, vt_ref, gate_ref, o_ref)):
        for c in range(k_r.shape[1] // c_len):
            chunks.append(dict(q=q_r, k=k_r, vt=vt_r, gate=g_r, out=o_r, c0=c * c_len, idx=len(chunks)))
    n_ctx = ck_ref.shape[1] // c_len

    @pl.when(pl.program_id(1) == 0)
    def _():
        r = lax.broadcasted_iota(jnp.int32, (c_len, LANES), 0).astype(F32)
        lane = lax.broadcasted_iota(jnp.int32, (c_len, LANES), 1)
        km = lax.broadcasted_iota(jnp.int32, (c_len, c_len), 0)
        qn = lax.broadcasted_iota(jnp.int32, (c_len, c_len), 1)
        diff = (qn - km).astype(F32)
        for j in range(2):
            h = hp * 2 + j
            lg_f = _log_sigmoid(jnp.full((c_len, LANES), dec_ref[layer, 0, h], F32))
            lg_b = _log_sigmoid(jnp.full((c_len, LANES), dec_ref[layer, 1, h], F32))
            own = ((lane >= j * RET_DK) & (lane < (j + 1) * RET_DK)).astype(F32)
            lg_f2 = _log_sigmoid(jnp.full((c_len, c_len), dec_ref[layer, 0, h], F32))
            lg_b2 = _log_sigmoid(jnp.full((c_len, c_len), dec_ref[layer, 1, h], F32))
            kd_scr[j] = jnp.concatenate([jnp.exp(lg_f * (c_len - 1.0 - r)) * own, jnp.exp(lg_b * r) * own, own], axis=1)
            qd_scr[j] = jnp.concatenate([jnp.exp(lg_f * (r + 1.0)), jnp.exp(lg_b * (c_len - r))], axis=1)
            dt_scr[j] = jnp.exp(jnp.where(diff >= 0, lg_f2 * diff, -lg_b2 * diff))
            cc_scr[j] = jnp.concatenate(
                [jnp.exp(_log_sigmoid(jnp.full((RET_DV, LANES), dec_ref[layer, 0, h], F32)) * c_len),
                 jnp.exp(_log_sigmoid(jnp.full((RET_DV, LANES), dec_ref[layer, 1, h], F32)) * c_len)], axis=1)

    heads = []
    for j in range(2):
        heads.append(dict(
            kdf=kd_scr[j, :, 0:LANES],
            kdb=kd_scr[j, :, LANES:2 * LANES],
            own=kd_scr[j, :, 2 * LANES:3 * LANES],
            qd=qd_scr[j],
            dt=dt_scr[j],
            cf=cc_scr[j, :, 0:LANES],
            cb=cc_scr[j, :, LANES:2 * LANES],
            rows=slice(j * RET_DV, (j + 1) * RET_DV),
        ))

    def rows(ref, ch):
        return ref[0, ch["c0"]:ch["c0"] + c_len, :]

    def vt_rows(ch, hd):
        return ch["vt"][0, hd["rows"], ch["c0"]:ch["c0"] + c_len]

    for ch in chunks:
        kc = rows(ch["k"], ch)
        for j, hd in enumerate(heads):
            sf_scr[j, ch["idx"]] = _dot(vt_rows(ch, hd), (kc * hd["kdf"]).astype(BF16))
            sb_scr[j, ch["idx"]] = _dot(vt_rows(ch, hd), (kc * hd["kdb"]).astype(BF16))

    backward_order = list(reversed(chunks[:n_ctx])) + list(reversed(chunks[n_ctx:]))
    for j, hd in enumerate(heads):
        for scr, decay, order in ((sf_scr, hd["cf"], chunks), (sb_scr, hd["cb"], backward_order)):
            state = jnp.zeros((RET_DV, LANES), F32)
            for ch in order:
                own = scr[j, ch["idx"]]
                scr[j, ch["idx"]] = state
                state = decay * state + own

    items = []
    for ch in chunks:
        if ch["q"] is not None:
            qc = rows(ch["q"], ch)
            items.append(dict(ch=ch, kc=rows(ch["k"], ch), qb=qc.astype(BF16), qq=jnp.concatenate([qc, qc], axis=1)))
    work = [(it, j, hd) for it in items for j, hd in enumerate(heads)]
    st = [_dot_nt((it["kc"] * hd["own"]).astype(BF16), it["qb"]) for it, j, hd in work]
    cross = [_dot_nt(jnp.concatenate([sf_scr[j, it["ch"]["idx"]], sb_scr[j, it["ch"]["idx"]]], axis=1).astype(BF16),
                     (it["qq"] * hd["qd"]).astype(BF16)) for it, j, hd in work]
    o = [_dot(vt_rows(it["ch"], hd), (s * hd["dt"]).astype(BF16)) + x for (it, j, hd), s, x in zip(work, st, cross)]
    ys = []
    for v in o:
        mu = jnp.mean(v, axis=0, keepdims=True)
        vc = v - mu
        var = jnp.mean(vc * vc, axis=0, keepdims=True)
        ys.append(vc * lax.rsqrt(var + EPS))
    for g, it in enumerate(items):
        ch = it["ch"]
        y = jnp.concatenate(ys[2 * g:2 * g + 2], axis=0).T
        y = y * gng_ref[layer:layer + 1, :] + gnb_ref[layer:layer + 1, :]
        ch["out"][0, ch["c0"]:ch["c0"] + c_len, :] = (y * rows(ch["gate"], ch).astype(F32)).astype(ch["out"].dtype)


def _retention(dec, ctx, lat, gn_g, gn_b, layer, chunk):
    b, t, _ = lat["rk"].shape
    t_ctx = ctx["rk"].shape[1]
    ctx_out = "rq" in ctx
    n_pair = RET_HEADS // 2
    pair = 2 * RET_DK
    state_buf = pltpu.VMEM((2, (t + t_ctx) // chunk, RET_DV, pair), F32)
    tables = [pltpu.VMEM((2, chunk, 3 * LANES), F32), pltpu.VMEM((2, chunk, 2 * LANES), F32),
              pltpu.VMEM((2, chunk, chunk), F32), pltpu.VMEM((2, RET_DV, 2 * LANES), F32)]
    seq = lambda n: pl.BlockSpec((1, n, pair), lambda h, i: (i, 0, h))
    vts = lambda n: pl.BlockSpec((1, 2 * RET_DV, n), lambda h, i: (i, h, 0))
    affine = pl.BlockSpec((gn_g.shape[0], pair), lambda h, i: (0, h))
    smem = pl.BlockSpec(memory_space=pltpu.SMEM)
    out = lambda n: jax.ShapeDtypeStruct((b, n, RET_WIDTH), BF16)
    if ctx_out:
        in_specs = [smem, seq(t_ctx), seq(t_ctx), vts(t_ctx), seq(t_ctx)]
        args = (dec, ctx["rq"], ctx["rk"], ctx["rvt"], ctx["gate_ret"])
        out_specs, out_shape = (seq(t_ctx), seq(t)), (out(t_ctx), out(t))
    else:
        in_specs, args = [smem, seq(t_ctx), vts(t_ctx)], (dec, ctx["rk"], ctx["rvt"])
        out_specs, out_shape = seq(t), out(t)
    in_specs += [seq(t), seq(t), vts(t), seq(t), affine, affine]
    args += (lat["rq"], lat["rk"], lat["rvt"], lat["gate_ret"], gn_g, gn_b)
    res = pl.pallas_call(
        functools.partial(_ret_kernel, chunk=chunk, layer=layer, ctx_out=ctx_out),
        grid=(n_pair, b),
        in_specs=in_specs,
        out_specs=out_specs,
        out_shape=out_shape,
        scratch_shapes=[state_buf, state_buf] + tables,
        compiler_params=_params(("arbitrary", "arbitrary")),
        name="retention",
    )(*args)
    return res if ctx_out else (None, res)


_FINAL_SUB_ROWS = 256


def _final_kernel(x_ref, mod_ref, om_ref, or_ref, oc_ref, wo_ref, lng_ref, lnb_ref, o_ref, *, alpha, layer, mod_row):
    nb, tm = x_ref.shape[:2]
    sub = _FINAL_SUB_ROWS if tm % _FINAL_SUB_ROWS == 0 else tm
    w = wo_ref[...].astype(BF16)
    for bi in range(nb):
        gate = _mod_row(mod_ref, 2, mod_row, pl.program_id(0) * nb + bi)
        for r0 in range(0, tm, sub):
            rs = slice(r0, r0 + sub)
            o_cat = jnp.concatenate([om_ref[bi, p, rs, :] for p in range(om_ref.shape[1])]
                                    + [or_ref[bi, rs, :], oc_ref[bi, rs, :]], axis=1)
            y = _dot(o_cat, w)
            v = alpha * x_ref[bi, rs, :] + gate * y
            mu = jnp.mean(v, axis=-1, keepdims=True)
            vc = v - mu
            var = jnp.mean(vc * vc, axis=-1, keepdims=True)
            o_ref[bi, rs, :] = vc * lax.rsqrt(var + EPS) * lng_ref[layer:layer + 1, :] + lnb_ref[layer:layer + 1, :]


def _final(x, mod_all, mod_row, o_mla, o_ret, o_conv, w_out, ln_g, ln_b, layer, tm, alpha):
    b, t, d = x.shape
    nb = _sequences_per_step(b, t, tm)
    const = lambda a: _layer_spec(a, layer)
    row = lambda n: pl.BlockSpec((nb, tm, n), lambda i, j: (i, j, 0))
    return pl.pallas_call(
        functools.partial(_final_kernel, alpha=alpha, layer=layer, mod_row=mod_row),
        grid=(b // nb, t // tm),
        in_specs=[row(d),
                  const(mod_all),
                  pl.BlockSpec((nb, o_mla.shape[1], tm, o_mla.shape[3]), lambda i, j: (i, 0, j, 0)),
                  row(RET_WIDTH), row(CONV_WIDTH),
                  const(w_out), _whole_spec(ln_g), _whole_spec(ln_b)],
        out_specs=row(d),
        out_shape=jax.ShapeDtypeStruct((b, t, d), F32),
        compiler_params=_params(("arbitrary", "arbitrary")),
        name="final",
    )(x, mod_all, o_mla, o_ret, o_conv, w_out, ln_g, ln_b)


_TOKEN_TILE = 1024
_FINAL_TILE = 512
_CTX_TILE = 256
_QUERY_CHUNK = 256
_RET_CHUNK = 256


def _tile(n, pref):
    return pref if n % pref == 0 else n


def kernel(x, c, ctx, c_ctx, w_mod, b_mod, w_in, mla_q_norm, w_uq, mla_kv_norm, w_ukv, ret_decay_fwd, ret_decay_bwd,
           ret_gn_g, ret_gn_b, conv_dw, conv_dw_b, conv_ln_g, conv_ln_b, conv_pw, conv_pw_b, w_out, ln_g, ln_b):
    depth = w_mod.shape[0]
    b, t, d = x.shape
    t_ctx = ctx.shape[1]
    alpha = (2 * depth) ** 0.25
    n_mod_rows = 16
    cc = jnp.concatenate([c, c_ctx[None, :], jnp.zeros((n_mod_rows - b - 1, d), c.dtype)], axis=0)
    mod_all = _modulation(cc, w_mod, b_mod)
    tab_x = _rope_table_block(t, True)
    tab_c = _rope_table_block(t_ctx, False)

    weights = _prep_weights(w_in, w_uq, w_ukv)
    gq, gkv = mla_q_norm, mla_kv_norm
    dec = jnp.stack([ret_decay_fwd, ret_decay_bwd], axis=1).astype(F32)
    gn_g, gn_b = ret_gn_g, ret_gn_b
    conv = (conv_dw, conv_dw_b, conv_ln_g, conv_ln_b, conv_pw, conv_pw_b)
    tail = (w_out, ln_g, ln_b)
    tm_x, tm_c = _tile(t, _TOKEN_TILE), _tile(t_ctx, _CTX_TILE)
    tf_x, tf_c = _tile(t, _FINAL_TILE), _tile(t_ctx, _CTX_TILE)
    tq_x, tq_c = _tile(t, _QUERY_CHUNK), _tile(t_ctx, _QUERY_CHUNK)
    chunk = math.gcd(math.gcd(t, t_ctx), _RET_CHUNK)

    hc = ctx
    for l in range(depth):
        need_ctx = l < depth - 1
        pc = _proj(hc, mod_all, b, tab_c, weights, gq, gkv, conv, l, tm_c, keys_only=not need_ctx)
        px = _proj(x, mod_all, None, tab_x, weights, gq, gkv, conv, l, tm_x)

        oc_ret, o_ret = _retention(dec, pc, px, gn_g, gn_b, l, chunk)
        o_mla = _attention(px["q"], px["gate_mla"], [pc["k"], px["k"]], [pc["vt"], px["vt"]], tq_x)
        x_new = _final(x, mod_all, None, o_mla, o_ret, px["conv"], *tail, layer=l, tm=tf_x, alpha=alpha)
        if need_ctx:
            oc_mla = _attention(pc["q"], pc["gate_mla"], [pc["k"]], [pc["vt"]], tq_c)
            hc = _final(hc, mod_all, b, oc_mla, oc_ret, pc["conv"], *tail, layer=l, tm=tf_c, alpha=alpha)
        x = x_new
    return x
```

```python
import functools
import math

import jax
import jax.numpy as jnp
import numpy as np
from jax import lax
from jax.experimental import pallas as pl
from jax.experimental.pallas import tpu as pltpu

GRID_W = 64
MLA_HEADS = 8
MLA_NOPE = 64
MLA_ROPE = 32
MLA_V = 64
MLA_QK = MLA_NOPE + MLA_ROPE
MLA_WIDTH = MLA_HEADS * MLA_V
Q_LORA = 256
KV_LORA = 128
RET_HEADS = 4
RET_DK = 64
RET_DV = 64
RET_WIDTH = RET_HEADS * RET_DV
CONV_WIDTH = 256
CONV_K = 31
ROPE_BASE = 10000.0
EPS = 1e-5

LANES = 128
HEAD_SLOT = LANES
VMEM_LIMIT_BYTES = 56 * 1024 * 1024
LOG2E = 1.4426950408889634

F32 = jnp.float32
BF16 = jnp.bfloat16

_NT = (((1,), (1,)), ((), ()))


def _dot(a, b):
    return jnp.dot(a, b, preferred_element_type=F32)


def _dot_nt(a, b):
    return lax.dot_general(a, b, _NT, preferred_element_type=F32)


def _params(semantics):
    return pltpu.CompilerParams(dimension_semantics=semantics, vmem_limit_bytes=VMEM_LIMIT_BYTES)


def _rot_cols(w, unit):
    xp = np if isinstance(w, np.ndarray) else jnp
    q = unit // 4
    w5 = w.reshape(w.shape[:-1] + (w.shape[-1] // unit, 2, 2, q))
    return xp.stack([-w5[..., 1, :], w5[..., 0, :]], axis=-2).reshape(w.shape)


def _rope_tables(length, unit, reps):
    d2 = unit // 2
    t = np.arange(length, dtype=np.int32)
    inv = np.float32(ROPE_BASE) ** (-np.arange(0, d2, 2, dtype=np.float32) / np.float32(d2))

    def half(pos):
        ang = pos.astype(np.float32)[:, None] * inv[None, :]
        return np.concatenate([np.cos(ang)] * 2, axis=-1), np.concatenate([np.sin(ang)] * 2, axis=-1)

    cr, sr = half(t // GRID_W)
    cc, sc = half(t % GRID_W)
    cos = np.concatenate([cr, cc], axis=-1).astype(np.float32)
    sin = np.concatenate([sr, sc], axis=-1).astype(np.float32)
    return np.tile(cos, (1, reps)), np.tile(sin, (1, reps))


def _rope_table_block(length, rotate):
    if rotate:
        cos_r, sin_r = _rope_tables(length, RET_DK, RET_HEADS)
        cos_m, sin_m = _rope_tables(length, MLA_ROPE, MLA_HEADS)
    else:
        cos_r = np.ones((length, RET_WIDTH), np.float32)
        sin_r = np.zeros((length, RET_WIDTH), np.float32)
        cos_m = np.ones((length, MLA_HEADS * MLA_ROPE), np.float32)
        sin_m = np.zeros((length, MLA_HEADS * MLA_ROPE), np.float32)
    kslot = np.concatenate([cos_m[:, :MLA_ROPE], sin_m[:, :MLA_ROPE],
                            np.zeros((length, LANES - 2 * MLA_ROPE), np.float32)], axis=-1)
    return jnp.asarray(np.concatenate([cos_r, sin_r, cos_m, sin_m, kslot], axis=-1))


def _placement_matrices():
    q_place = np.zeros((MLA_HEADS, MLA_ROPE, MLA_HEADS, HEAD_SLOT), np.float32)
    k_place = np.zeros((LANES, MLA_HEADS, HEAD_SLOT), np.float32)
    for d in range(MLA_ROPE):
        for h in range(MLA_HEADS):
            q_place[h, d, h, MLA_NOPE + d] = 1.0
            k_place[d, h, MLA_NOPE + d] = 1.0
            k_place[MLA_ROPE + d, h, MLA_NOPE + d] = 1.0
    return (q_place.reshape(MLA_HEADS * MLA_ROPE, MLA_HEADS * HEAD_SLOT),
            k_place.reshape(LANES, MLA_HEADS * HEAD_SLOT))


_IN_SIZES = dict(q=Q_LORA, kv=KV_LORA, kr=MLA_ROPE, g_mla=MLA_WIDTH, rq=RET_HEADS * RET_DK, rk=RET_HEADS * RET_DK,
                 rv=RET_WIDTH, g_ret=RET_WIDTH, glu=2 * CONV_WIDTH, g_conv=CONV_WIDTH)


def _offsets(sizes):
    out, at = {}, 0
    for name, n in sizes.items():
        out[name] = (at, at + n)
        at += n
    return out


_IN_COLS = _offsets(_IN_SIZES)


def _prep_weights(w_in, w_uq, w_ukv):
    depth, d, _ = w_in.shape
    w_t = jnp.swapaxes(w_in, 1, 2).astype(BF16)
    wkr = w_in[..., _IN_COLS["kr"][0]:_IN_COLS["kr"][1]]
    w_ks = jnp.concatenate([wkr, _rot_cols(wkr, MLA_ROPE), jnp.zeros((depth, d, LANES - 2 * MLA_ROPE), w_in.dtype)],
                           axis=-1)
    w_ks = jnp.swapaxes(w_ks, 1, 2).astype(BF16)

    uq = w_uq.reshape(depth, Q_LORA, MLA_HEADS, MLA_QK)
    pad = ((0, 0), (0, 0), (0, 0), (0, HEAD_SLOT - MLA_NOPE))
    uq_nope = jnp.pad(uq[..., :MLA_NOPE], pad).reshape(depth, Q_LORA, -1)
    uq_rope = uq[..., MLA_NOPE:].reshape(depth, Q_LORA, MLA_HEADS * MLA_ROPE)
    w_q2 = jnp.concatenate([uq_rope, _rot_cols(uq_rope, MLA_ROPE)], axis=-1).astype(BF16)
    q_place, k_place = (jnp.broadcast_to(jnp.asarray(m, BF16), (depth,) + m.shape) for m in _placement_matrices())
    w_qcomb = jnp.concatenate([uq_nope.astype(BF16), q_place], axis=1)

    ukv = w_ukv.reshape(depth, KV_LORA, MLA_HEADS, MLA_NOPE + MLA_V)
    uk = jnp.pad(ukv[..., :MLA_NOPE], pad).reshape(depth, KV_LORA, -1)
    w_kcomb = jnp.concatenate([uk.astype(BF16), k_place], axis=1)
    w_uvt = jnp.swapaxes(ukv[..., MLA_NOPE:].reshape(depth, KV_LORA, MLA_WIDTH), 1, 2).astype(BF16)
    return w_t, w_ks, w_q2, w_qcomb, w_kcomb, w_uvt


def _mod_kernel(c_ref, w_ref, b_ref, o_ref):
    c = c_ref[...]
    a = (c * jax.nn.sigmoid(c)).astype(BF16)
    o_ref[0, 0] = _dot(a, w_ref[0].astype(BF16)) + b_ref[pl.ds(pl.program_id(0), 1), :]


def _modulation(cc, w_mod, b_mod):
    depth, d, d3 = w_mod.shape
    nblk = d3 // d
    return pl.pallas_call(
        _mod_kernel,
        grid=(depth, nblk),
        in_specs=[pl.BlockSpec(cc.shape, lambda l, j: (0, 0)),
                  pl.BlockSpec((1, d, d), lambda l, j: (l, 0, j)),
                  pl.BlockSpec((depth, d), lambda l, j: (0, j))],
        out_specs=pl.BlockSpec((1, 1, cc.shape[0], d), lambda l, j: (l, j, 0, 0)),
        out_shape=jax.ShapeDtypeStruct((depth, nblk, cc.shape[0], d), F32),
        compiler_params=_params(("arbitrary", "arbitrary")),
        name="mod",
    )(cc, w_mod, b_mod)


_T_CR = 0
_T_SR = _T_CR + RET_WIDTH
_T_CM = _T_SR + RET_WIDTH
_T_SM = _T_CM + MLA_HEADS * MLA_ROPE
_T_KS = _T_SM + MLA_HEADS * MLA_ROPE
_T_END = _T_KS + LANES


_HALO = 16
_CONV_ROWS = 64


def _rms(x, g):
    return x * lax.rsqrt(jnp.mean(x * x, axis=-1, keepdims=True) + EPS) * g


def _proj_kernel(x_ref, xp_ref, xn_ref, mod_ref, tab_ref, rot_ref, wt_ref, wks_ref, gq_ref, gkv_ref, wq2_ref,
                 wqc_ref, wkc_ref, wuvt_ref, dw_ref, dwb_ref, clg_ref, clb_ref, pw_ref, pwb_ref, *out_refs, keys_only,
                 layer, mod_row):
    if keys_only:
        k_ref, vt_ref, rk_ref, rvt_ref = out_refs
    else:
        q_ref, k_ref, vt_ref, rq_ref, rk_ref, rvt_ref, gm_ref, gr_ref, oc_ref, upad, ushift = out_refs
    nb, tm = x_ref.shape[:2]
    seqs = [slice(bi * tm, (bi + 1) * tm) for bi in range(nb)]
    shift = _mod_row(mod_ref, 0, mod_row, pl.program_id(1))
    scale = _mod_row(mod_ref, 1, mod_row, pl.program_id(1))
    pair = 2 * HEAD_SLOT
    vec = lambda ref: ref[layer:layer + 1, :]

    def modulate(x):
        mu = jnp.mean(x, axis=-1, keepdims=True)
        xc = x - mu
        var = jnp.mean(xc * xc, axis=-1, keepdims=True)
        return (xc * lax.rsqrt(var + EPS) * (1.0 + scale) + shift).astype(BF16)

    u = modulate(x_ref[...].reshape(nb * tm, x_ref.shape[2]))

    def tab(lo, hi):
        return jnp.concatenate([tab_ref[:, lo:hi]] * nb, axis=0)

    def w_rows(name):
        lo, hi = _IN_COLS[name]
        return wt_ref[lo:hi, :]

    def seg(name):
        return _dot_nt(u, w_rows(name))

    def silu(v):
        return v * jax.nn.sigmoid(v)

    def store_pairs(ref, val):
        for bi, rows in enumerate(seqs):
            for p in range(MLA_HEADS // 2):
                ref[bi, p] = val[rows, p * pair:(p + 1) * pair]

    def keys(pkv, pks):
        nkv = _rms(pkv, vec(gkv_ref)).astype(BF16)
        ks = pks * tab(_T_KS, _T_END)
        store_pairs(k_ref, _dot(jnp.concatenate([nkv, ks.astype(BF16)], axis=1), wkc_ref[...]).astype(BF16))
        vt = _dot_nt(wuvt_ref[...], nkv).astype(BF16)
        for bi, rows in enumerate(seqs):
            vt_ref[bi] = vt[:, rows]

    def ret_rope(raw):
        rot = _dot(raw.astype(BF16), rot_ref[...])
        return raw * tab(_T_CR, _T_SR) + rot * tab(_T_SR, _T_CM)

    def ret_keys(prk):
        rk = ret_rope(prk) * (RET_DK ** -0.5)
        rvt = _dot_nt(w_rows("rv"), u).astype(BF16)
        for bi, rows in enumerate(seqs):
            rk_ref[bi] = rk[rows]
            rvt_ref[bi] = rvt[:, rows]

    if keys_only:
        keys(seg("kv"), _dot_nt(u, wks_ref[...]))
        ret_keys(seg("rk"))
        return

    padded = tm + 2 * _HALO
    if nb == 1:
        i = pl.program_id(0)
        u_ext = jnp.concatenate([modulate(xp_ref[0]), u, modulate(xn_ref[0])], axis=0)
        glu = _dot_nt(u_ext, w_rows("glu"))
        uc = glu[:, :CONV_WIDTH] * jax.nn.sigmoid(glu[:, CONV_WIDTH:])
        row = lax.broadcasted_iota(jnp.int32, uc.shape, 0)
        inside = ((row >= _HALO) | (i > 0)) & ((row < _HALO + tm) | (i < pl.num_programs(0) - 1))
        upad[...] = jnp.where(inside, uc, 0.0)
    else:
        glu = _dot_nt(u, w_rows("glu"))
        uc = glu[:, :CONV_WIDTH] * jax.nn.sigmoid(glu[:, CONV_WIDTH:])
        zeros = jnp.zeros((_HALO, CONV_WIDTH), F32)
        upad[...] = jnp.concatenate([part for rows in seqs for part in (zeros, uc[rows], zeros)], axis=0)
    span = nb * padded - 8
    for r in range(1, 8):
        ushift[r - 1] = upad[r:r + span, :]
    base = _HALO - CONV_K // 2
    conv_blocks = []

    def conv_rows(n_blocks):
        for _ in range(n_blocks):
            bi, k = divmod(len(conv_blocks), tm // _CONV_ROWS)
            rb = bi * padded + k * _CONV_ROWS
            acc = jnp.zeros((_CONV_ROWS, CONV_WIDTH), F32) + vec(dwb_ref)
            for j in range(CONV_K):
                a, r = divmod(base + j, 8)
                src = upad if r == 0 else ushift.at[r - 1]
                acc = acc + src[8 * a + rb:8 * a + rb + _CONV_ROWS, :] * dw_ref[j:j + 1, :]
            mu = jnp.mean(acc, axis=-1, keepdims=True)
            ac = acc - mu
            var = jnp.mean(ac * ac, axis=-1, keepdims=True)
            yc = ac * lax.rsqrt(var + EPS) * vec(clg_ref) + vec(clb_ref)
            conv_blocks.append((yc * jax.nn.sigmoid(yc)).astype(BF16))

    n_conv = nb * tm // _CONV_ROWS
    per_stage = -(-n_conv // 4)

    pq, pkv, pks = seg("q"), seg("kv"), _dot_nt(u, wks_ref[...])
    prq, prk = seg("rq"), seg("rk")
    conv_rows(min(per_stage, n_conv - len(conv_blocks)))

    g_mla = silu(seg("g_mla"))
    for bi, rows in enumerate(seqs):
        for p in range(MLA_HEADS // 2):
            gm_ref[bi, p] = g_mla[rows, p * LANES:(p + 1) * LANES].astype(gm_ref.dtype)
    g_ret = silu(seg("g_ret")).astype(gr_ref.dtype)
    for bi, rows in enumerate(seqs):
        gr_ref[bi] = g_ret[rows]
    g_conv = silu(seg("g_conv"))
    conv_rows(min(per_stage, n_conv - len(conv_blocks)))

    nq = _rms(pq, vec(gq_ref)).astype(BF16)
    qc = _dot(nq, wq2_ref[...])
    keys(pkv, pks)
    conv_rows(min(per_stage, n_conv - len(conv_blocks)))

    rq = ret_rope(prq)
    for bi, rows in enumerate(seqs):
        rq_ref[bi] = rq[rows]
    ret_keys(prk)

    nr = MLA_HEADS * MLA_ROPE
    q_rope = qc[:, :nr] * tab(_T_CM, _T_SM) + qc[:, nr:] * tab(_T_SM, _T_KS)
    q = _dot(jnp.concatenate([nq, q_rope.astype(BF16)], axis=1), wqc_ref[...])
    store_pairs(q_ref, (q * (MLA_QK ** -0.5 * LOG2E)).astype(BF16))
    conv_rows(n_conv - len(conv_blocks))

    o_conv = _dot(jnp.concatenate(conv_blocks, axis=0), pw_ref[...].astype(BF16)) + vec(pwb_ref)
    o_conv = (o_conv * g_conv).astype(oc_ref.dtype)
    for bi, rows in enumerate(seqs):
        oc_ref[bi] = o_conv[rows]


def _layer_spec(a, layer):
    return pl.BlockSpec((None,) + a.shape[1:], lambda *_: (layer,) + (0,) * (a.ndim - 1),
                        pipeline_mode=pl.Buffered(1))


def _whole_spec(a):
    return pl.BlockSpec(a.shape, lambda *_: (0,) * a.ndim, pipeline_mode=pl.Buffered(1))


def _sequences_per_step(b, t, tm):
    return math.gcd(b, max(_TOKEN_TILE // tm, 1)) if tm == t else 1


def _mod_row(mod_ref, k, shared_row, batch_index):
    if shared_row is None:
        return mod_ref[k, pl.ds(batch_index, 1), :]
    return mod_ref[k, shared_row:shared_row + 1, :]


def _proj(x, mod_all, mod_row, tab, weights, gq, gkv, conv, layer, tm, keys_only=False):
    b, t, d = x.shape
    w_t, w_ks, w_q2, w_qcomb, w_kcomb, w_uvt = weights
    const = lambda a: _layer_spec(a, layer)
    nb = _sequences_per_step(b, t, tm) if mod_row is not None else 1
    row = lambda n: pl.BlockSpec((nb, tm, n), lambda i, j: (j, i, 0))
    col = lambda n: pl.BlockSpec((nb, n, tm), lambda i, j: (j, 0, i))
    n_pair, pair = MLA_HEADS // 2, 2 * HEAD_SLOT
    pairs = pl.BlockSpec((nb, n_pair, tm, pair), lambda i, j: (j, 0, i, 0))
    per_tile = tm // _HALO
    last = t // _HALO - 1
    halo_prev = pl.BlockSpec((1, _HALO, d), lambda i, j: (j, jnp.maximum(i * per_tile - 1, 0), 0))
    halo_next = pl.BlockSpec((1, _HALO, d), lambda i, j: (j, jnp.minimum((i + 1) * per_tile, last), 0))
    rot = jnp.asarray(_rot_cols(np.eye(RET_WIDTH, dtype=np.float32), RET_DK), BF16)
    sds = jax.ShapeDtypeStruct
    outs = dict(q=(sds((b, n_pair, t, pair), BF16), pairs),
                k=(sds((b, n_pair, t, pair), BF16), pairs),
                vt=(sds((b, MLA_WIDTH, t), BF16), col(MLA_WIDTH)),
                rq=(sds((b, t, RET_WIDTH), F32), row(RET_WIDTH)),
                rk=(sds((b, t, RET_WIDTH), F32), row(RET_WIDTH)),
                rvt=(sds((b, RET_WIDTH, t), BF16), col(RET_WIDTH)),
                gate_mla=(sds((b, n_pair, t, 2 * MLA_V), BF16),
                          pl.BlockSpec((nb, n_pair, tm, 2 * MLA_V), lambda i, j: (j, 0, i, 0))),
                gate_ret=(sds((b, t, RET_WIDTH), BF16), row(RET_WIDTH)),
                conv=(sds((b, t, CONV_WIDTH), BF16), row(CONV_WIDTH)))
    names = ("k", "vt", "rk", "rvt") if keys_only else tuple(outs)
    conv_rows = nb * (tm + 2 * _HALO)
    scratch = [] if keys_only else [pltpu.VMEM((conv_rows, CONV_WIDTH), F32),
                                    pltpu.VMEM((7, conv_rows - 8, CONV_WIDTH), F32)]
    res = pl.pallas_call(
        functools.partial(_proj_kernel, keys_only=keys_only, layer=layer, mod_row=mod_row),
        grid=(t // tm, b // nb),
        in_specs=[row(d), halo_prev, halo_next,
                  const(mod_all),
                  pl.BlockSpec((tm, _T_END), lambda i, j: (i, 0)),
                  pl.BlockSpec(rot.shape, lambda i, j: (0, 0)),
                  const(w_t), const(w_ks), _whole_spec(gq), _whole_spec(gkv), const(w_q2), const(w_qcomb),
                  const(w_kcomb), const(w_uvt)] + [const(a) if a.ndim == 3 else _whole_spec(a) for a in conv],
        out_specs=tuple(outs[n][1] for n in names),
        out_shape=tuple(outs[n][0] for n in names),
        scratch_shapes=scratch,
        compiler_params=_params(("arbitrary", "arbitrary")),
        name="proj",
    )(x, x, x, mod_all, tab, rot, w_t, w_ks, gq, gkv, w_q2, w_qcomb, w_kcomb, w_uvt, *conv)
    res = dict(zip(names, res))
    res["vt"] = res["vt"].reshape(b, n_pair, 2 * MLA_V, t)
    return res


_ATTN_KEY_BLOCK = 512


def _attn_kernel(*refs, n_seg, tq):
    q_ref, g_ref = refs[:2]
    k_refs = refs[2:2 + n_seg]
    vt_refs = refs[2 + n_seg:2 + 2 * n_seg]
    o_ref = refs[2 + 2 * n_seg]
    s_bufs = refs[3 + 2 * n_seg:5 + 2 * n_seg]
    m_bufs = refs[5 + 2 * n_seg:7 + 2 * n_seg]
    n_pair = q_ref.shape[1]
    n_q = q_ref.shape[2] // tq
    n_items = n_pair * n_q
    lanes = [slice(j * HEAD_SLOT, (j + 1) * HEAD_SLOT) for j in range(2)]
    rows = [slice(j * MLA_V, (j + 1) * MLA_V) for j in range(2)]

    blocks, off = [], 0
    for si in range(n_seg):
        ts = k_refs[si].shape[2]
        kb = min(_ATTN_KEY_BLOCK, ts)
        blocks += [(si, b0, kb, off + b0) for b0 in range(0, ts, kb)]
        off += ts

    def locate(item):
        if isinstance(item, int):
            return item // n_q, (item % n_q) * tq
        pair = lax.div(item, jnp.int32(n_q))
        return pair, pl.multiple_of((item - pair * n_q) * tq, tq)

    def phase(nxt, cur):
        if nxt is not None:
            item_n, s_n, m_n = nxt
            pn, rn = locate(item_n)
            qn = [q_ref[0, pn, pl.ds(rn, tq), lanes[j]] for j in range(2)]
            mx = [None, None]
        if cur is not None:
            item_c, s_c, m_c = cur
            pc, rc = locate(item_c)
            mc = [m_c[j] for j in range(2)]
            acc, l = [None, None], [None, None]
        for si, b0, kb, o0 in blocks:
            for j in range(2):
                if nxt is not None:
                    s = _dot_nt(k_refs[si][0, pn, b0:b0 + kb, lanes[j]], qn[j])
                    s_n[j, o0:o0 + kb, :] = s
                    mb = jnp.max(s, axis=0, keepdims=True)
                    mx[j] = mb if mx[j] is None else jnp.maximum(mx[j], mb)
                if cur is not None:
                    p = jnp.exp2(s_c[j, o0:o0 + kb, :] - mc[j])
                    ls = jnp.sum(p, axis=0, keepdims=True)
                    pv = _dot(vt_refs[si][0, pc, rows[j], b0:b0 + kb], p.astype(BF16))
                    l[j] = ls if l[j] is None else l[j] + ls
                    acc[j] = pv if acc[j] is None else acc[j] + pv
        if nxt is not None:
            for j in range(2):
                m_n[j] = mx[j]
        if cur is not None:
            outs = [acc[j] * (1.0 / l[j]) for j in range(2)]
            gate = g_ref[0, pc, pl.ds(rc, tq), :].astype(F32)
            o_ref[0, pc, pl.ds(rc, tq), :] = (jnp.concatenate(outs, axis=0).T * gate).astype(o_ref.dtype)

    buf = [(s_bufs[0], m_bufs[0]), (s_bufs[1], m_bufs[1])]
    phase((0, *buf[0]), None)
    assert n_items % 2 == 0

    def body(i, carry):
        n = 2 * i
        phase((n + 1, *buf[1]), (n, *buf[0]))
        phase((jnp.minimum(n + 2, n_items - 1), *buf[0]), (n + 1, *buf[1]))
        return carry

    lax.fori_loop(0, n_items // 2, body, 0)


def _attention(q, gate, ks, vts, tq):
    b, n_pair, t, pair = q.shape
    n_seg = len(ks)
    n_keys = sum(k.shape[2] for k in ks)
    whole = lambda a: pl.BlockSpec((1,) + a.shape[1:], lambda i: (i, 0, 0, 0))
    score_buf = pltpu.VMEM((2, n_keys, tq), F32)
    max_buf = pltpu.VMEM((2, 1, tq), F32)
    return pl.pallas_call(
        functools.partial(_attn_kernel, n_seg=n_seg, tq=tq),
        grid=(b,),
        in_specs=[whole(q), whole(gate)] + [whole(k) for k in ks] + [whole(v) for v in vts],
        out_specs=pl.BlockSpec((1, n_pair, t, 2 * MLA_V), lambda i: (i, 0, 0, 0)),
        out_shape=jax.ShapeDtypeStruct((b, n_pair, t, 2 * MLA_V), BF16),
        scratch_shapes=[score_buf, score_buf, max_buf, max_buf],
        compiler_params=_params(("arbitrary",)),
        name="attention",
    )(q, gate, *ks, *vts)


def _log_sigmoid(x):
    return jnp.minimum(x, 0.0) - jnp.log1p(jnp.exp(-jnp.abs(x)))


def _ret_kernel(dec_ref, *refs, chunk, layer, ctx_out):
    if ctx_out:
        (cq_ref, ck_ref, cvt_ref, cgate_ref, q_ref, k_ref, vt_ref, gate_ref, gng_ref, gnb_ref, oc_ref, o_ref,
         sf_scr, sb_scr, kd_scr, qd_scr, dt_scr, cc_scr) = refs
    else:
        (ck_ref, cvt_ref, q_ref, k_ref, vt_ref, gate_ref, gng_ref, gnb_ref, o_ref,
         sf_scr, sb_scr, kd_scr, qd_scr, dt_scr, cc_scr) = refs
        cq_ref = cgate_ref = oc_ref = None
    c_len = chunk
    hp = pl.program_id(0)
    chunks = []
    for q_r, k_r, vt_r, g_r, o_r in ((cq_ref, ck_ref, cvt_ref, cgate_ref, oc_ref),
                                     (q_ref, k_ref, vt_ref, gate_ref, o_ref)):
        for c in range(k_r.shape[1] // c_len):
            chunks.append(dict(q=q_r, k=k_r, vt=vt_r, gate=g_r, out=o_r, c0=c * c_len, idx=len(chunks)))
    n_ctx = ck_ref.shape[1] // c_len

    @pl.when(pl.program_id(1) == 0)
    def _():
        r = lax.broadcasted_iota(jnp.int32, (c_len, LANES), 0).astype(F32)
        lane = lax.broadcasted_iota(jnp.int32, (c_len, LANES), 1)
        km = lax.broadcasted_iota(jnp.int32, (c_len, c_len), 0)
        qn = lax.broadcasted_iota(jnp.int32, (c_len, c_len), 1)
        diff = (qn - km).astype(F32)
        for j in range(2):
            h = hp * 2 + j
            lg_f = _log_sigmoid(jnp.full((c_len, LANES), dec_ref[layer, 0, h], F32))
            lg_b = _log_sigmoid(jnp.full((c_len, LANES), dec_ref[layer, 1, h], F32))
            own = ((lane >= j * RET_DK) & (lane < (j + 1) * RET_DK)).astype(F32)
            lg_f2 = _log_sigmoid(jnp.full((c_len, c_len), dec_ref[layer, 0, h], F32))
            lg_b2 = _log_sigmoid(jnp.full((c_len, c_len), dec_ref[layer, 1, h], F32))
            kd_scr[j] = jnp.concatenate([jnp.exp(lg_f * (c_len - 1.0 - r)) * own, jnp.exp(lg_b * r) * own, own], axis=1)
            qd_scr[j] = jnp.concatenate([jnp.exp(lg_f * (r + 1.0)), jnp.exp(lg_b * (c_len - r))], axis=1)
            dt_scr[j] = jnp.exp(jnp.where(diff >= 0, lg_f2 * diff, -lg_b2 * diff))
            cc_scr[j] = jnp.concatenate(
                [jnp.exp(_log_sigmoid(jnp.full((RET_DV, LANES), dec_ref[layer, 0, h], F32)) * c_len),
                 jnp.exp(_log_sigmoid(jnp.full((RET_DV, LANES), dec_ref[layer, 1, h], F32)) * c_len)], axis=1)

    heads = []
    for j in range(2):
        heads.append(dict(
            kdf=kd_scr[j, :, 0:LANES],
            kdb=kd_scr[j, :, LANES:2 * LANES],
            own=kd_scr[j, :, 2 * LANES:3 * LANES],
            qd=qd_scr[j],
            dt=dt_scr[j],
            cf=cc_scr[j, :, 0:LANES],
            cb=cc_scr[j, :, LANES:2 * LANES],
            rows=slice(j * RET_DV, (j + 1) * RET_DV),
        ))

    def rows(ref, ch):
        return ref[0, ch["c0"]:ch["c0"] + c_len, :]

    def vt_rows(ch, hd):
        return ch["vt"][0, hd["rows"], ch["c0"]:ch["c0"] + c_len]

    for ch in chunks:
        kc = rows(ch["k"], ch)
        for j, hd in enumerate(heads):
            sf_scr[j, ch["idx"]] = _dot(vt_rows(ch, hd), (kc * hd["kdf"]).astype(BF16))
            sb_scr[j, ch["idx"]] = _dot(vt_rows(ch, hd), (kc * hd["kdb"]).astype(BF16))

    backward_order = list(reversed(chunks[:n_ctx])) + list(reversed(chunks[n_ctx:]))
    for j, hd in enumerate(heads):
        for scr, decay, order in ((sf_scr, hd["cf"], chunks), (sb_scr, hd["cb"], backward_order)):
            state = jnp.zeros((RET_DV, LANES), F32)
            for ch in order:
                own = scr[j, ch["idx"]]
                scr[j, ch["idx"]] = state
                state = decay * state + own

    items = []
    for ch in chunks:
        if ch["q"] is not None:
            qc = rows(ch["q"], ch)
            items.append(dict(ch=ch, kc=rows(ch["k"], ch), qb=qc.astype(BF16), qq=jnp.concatenate([qc, qc], axis=1)))
    work = [(it, j, hd) for it in items for j, hd in enumerate(heads)]
    st = [_dot_nt((it["kc"] * hd["own"]).astype(BF16), it["qb"]) for it, j, hd in work]
    cross = [_dot_nt(jnp.concatenate([sf_scr[j, it["ch"]["idx"]], sb_scr[j, it["ch"]["idx"]]], axis=1).astype(BF16),
                     (it["qq"] * hd["qd"]).astype(BF16)) for it, j, hd in work]
    o = [_dot(vt_rows(it["ch"], hd), (s * hd["dt"]).astype(BF16)) + x for (it, j, hd), s, x in zip(work, st, cross)]
    ys = []
    for v in o:
        mu = jnp.mean(v, axis=0, keepdims=True)
        vc = v - mu
        var = jnp.mean(vc * vc, axis=0, keepdims=True)
        ys.append(vc * lax.rsqrt(var + EPS))
    for g, it in enumerate(items):
        ch = it["ch"]
        y = jnp.concatenate(ys[2 * g:2 * g + 2], axis=0).T
        y = y * gng_ref[layer:layer + 1, :] + gnb_ref[layer:layer + 1, :]
        ch["out"][0, ch["c0"]:ch["c0"] + c_len, :] = (y * rows(ch["gate"], ch).astype(F32)).astype(ch["out"].dtype)


def _retention(dec, ctx, lat, gn_g, gn_b, layer, chunk):
    b, t, _ = lat["rk"].shape
    t_ctx = ctx["rk"].shape[1]
    ctx_out = "rq" in ctx
    n_pair = RET_HEADS // 2
    pair = 2 * RET_DK
    state_buf = pltpu.VMEM((2, (t + t_ctx) // chunk, RET_DV, pair), F32)
    tables = [pltpu.VMEM((2, chunk, 3 * LANES), F32), pltpu.VMEM((2, chunk, 2 * LANES), F32),
              pltpu.VMEM((2, chunk, chunk), F32), pltpu.VMEM((2, RET_DV, 2 * LANES), F32)]
    seq = lambda n: pl.BlockSpec((1, n, pair), lambda h, i: (i, 0, h))
    vts = lambda n: pl.BlockSpec((1, 2 * RET_DV, n), lambda h, i: (i, h, 0))
    affine = pl.BlockSpec((gn_g.shape[0], pair), lambda h, i: (0, h))
    smem = pl.BlockSpec(memory_space=pltpu.SMEM)
    out = lambda n: jax.ShapeDtypeStruct((b, n, RET_WIDTH), BF16)
    if ctx_out:
        in_specs = [smem, seq(t_ctx), seq(t_ctx), vts(t_ctx), seq(t_ctx)]
        args = (dec, ctx["rq"], ctx["rk"], ctx["rvt"], ctx["gate_ret"])
        out_specs, out_shape = (seq(t_ctx), seq(t)), (out(t_ctx), out(t))
    else:
        in_specs, args = [smem, seq(t_ctx), vts(t_ctx)], (dec, ctx["rk"], ctx["rvt"])
        out_specs, out_shape = seq(t), out(t)
    in_specs += [seq(t), seq(t), vts(t), seq(t), affine, affine]
    args += (lat["rq"], lat["rk"], lat["rvt"], lat["gate_ret"], gn_g, gn_b)
    res = pl.pallas_call(
        functools.partial(_ret_kernel, chunk=chunk, layer=layer, ctx_out=ctx_out),
        grid=(n_pair, b),
        in_specs=in_specs,
        out_specs=out_specs,
        out_shape=out_shape,
        scratch_shapes=[state_buf, state_buf] + tables,
        compiler_params=_params(("arbitrary", "arbitrary")),
        name="retention",
    )(*args)
    return res if ctx_out else (None, res)


_FINAL_SUB_ROWS = 256


def _final_kernel(x_ref, mod_ref, om_ref, or_ref, oc_ref, wo_ref, lng_ref, lnb_ref, o_ref, w_buf, *, alpha, layer,
                  mod_row):
    nb, tm = x_ref.shape[:2]
    sub = _FINAL_SUB_ROWS if tm % _FINAL_SUB_ROWS == 0 else tm

    @pl.when((pl.program_id(0) == 0) & (pl.program_id(1) == 0))
    def _():
        w_buf[...] = wo_ref[...].astype(BF16)

    w = w_buf[...]
    for bi in range(nb):
        gate = _mod_row(mod_ref, 2, mod_row, pl.program_id(0) * nb + bi)
        for r0 in range(0, tm, sub):
            rs = slice(r0, r0 + sub)
            o_cat = jnp.concatenate([om_ref[bi, p, rs, :] for p in range(om_ref.shape[1])]
                                    + [or_ref[bi, rs, :], oc_ref[bi, rs, :]], axis=1)
            y = _dot(o_cat, w)
            v = alpha * x_ref[bi, rs, :] + gate * y
            mu = jnp.mean(v, axis=-1, keepdims=True)
            vc = v - mu
            var = jnp.mean(vc * vc, axis=-1, keepdims=True)
            o_ref[bi, rs, :] = vc * lax.rsqrt(var + EPS) * lng_ref[layer:layer + 1, :] + lnb_ref[layer:layer + 1, :]


def _final(x, mod_all, mod_row, o_mla, o_ret, o_conv, w_out, ln_g, ln_b, layer, tm, alpha):
    b, t, d = x.shape
    nb = _sequences_per_step(b, t, tm)
    const = lambda a: _layer_spec(a, layer)
    row = lambda n: pl.BlockSpec((nb, tm, n), lambda i, j: (i, j, 0))
    return pl.pallas_call(
        functools.partial(_final_kernel, alpha=alpha, layer=layer, mod_row=mod_row),
        grid=(b // nb, t // tm),
        in_specs=[row(d),
                  const(mod_all),
                  pl.BlockSpec((nb, o_mla.shape[1], tm, o_mla.shape[3]), lambda i, j: (i, 0, j, 0)),
                  row(RET_WIDTH), row(CONV_WIDTH),
                  const(w_out), _whole_spec(ln_g), _whole_spec(ln_b)],
        out_specs=row(d),
        out_shape=jax.ShapeDtypeStruct((b, t, d), F32),
        scratch_shapes=[pltpu.VMEM(w_out.shape[1:], BF16)],
        compiler_params=_params(("arbitrary", "arbitrary")),
        name="final",
    )(x, mod_all, o_mla, o_ret, o_conv, w_out, ln_g, ln_b)


_TOKEN_TILE = 1024
_CTX_TILE = 256
_QUERY_CHUNK = 256
_RET_CHUNK = 256


def _tile(n, pref):
    return pref if n % pref == 0 else n


def kernel(x, c, ctx, c_ctx, w_mod, b_mod, w_in, mla_q_norm, w_uq, mla_kv_norm, w_ukv, ret_decay_fwd, ret_decay_bwd,
           ret_gn_g, ret_gn_b, conv_dw, conv_dw_b, conv_ln_g, conv_ln_b, conv_pw, conv_pw_b, w_out, ln_g, ln_b):
    depth = w_mod.shape[0]
    b, t, d = x.shape
    t_ctx = ctx.shape[1]
    alpha = (2 * depth) ** 0.25
    n_mod_rows = 16
    cc = jnp.concatenate([c, c_ctx[None, :], jnp.zeros((n_mod_rows - b - 1, d), c.dtype)], axis=0)
    mod_all = _modulation(cc, w_mod, b_mod)
    tab_x = _rope_table_block(t, True)
    tab_c = _rope_table_block(t_ctx, False)

    weights = _prep_weights(w_in, w_uq, w_ukv)
    gq, gkv = mla_q_norm, mla_kv_norm
    dec = jnp.stack([ret_decay_fwd, ret_decay_bwd], axis=1).astype(F32)
    gn_g, gn_b = ret_gn_g, ret_gn_b
    conv = (conv_dw, conv_dw_b, conv_ln_g, conv_ln_b, conv_pw, conv_pw_b)
    tail = (w_out, ln_g, ln_b)
    tm_x, tm_c = _tile(t, _TOKEN_TILE), _tile(t_ctx, _CTX_TILE)
    tf_x, tf_c = _tile(t, _TOKEN_TILE), _tile(t_ctx, _CTX_TILE)
    tq_x, tq_c = _tile(t, _QUERY_CHUNK), _tile(t_ctx, _QUERY_CHUNK)
    chunk = math.gcd(math.gcd(t, t_ctx), _RET_CHUNK)

    hc = ctx
    for l in range(depth):
        need_ctx = l < depth - 1
        pc = _proj(hc, mod_all, b, tab_c, weights, gq, gkv, conv, l, tm_c, keys_only=not need_ctx)
        px = _proj(x, mod_all, None, tab_x, weights, gq, gkv, conv, l, tm_x)

        oc_ret, o_ret = _retention(dec, pc, px, gn_g, gn_b, l, chunk)
        o_mla = _attention(px["q"], px["gate_mla"], [pc["k"], px["k"]], [pc["vt"], px["vt"]], tq_x)
        x_new = _final(x, mod_all, None, o_mla, o_ret, px["conv"], *tail, layer=l, tm=tf_x, alpha=alpha)
        if need_ctx:
            oc_mla = _attention(pc["q"], pc["gate_mla"], [pc["k"]], [pc["vt"]], tq_c)
            hc = _final(hc, mod_all, b, oc_mla, oc_ret, pc["conv"], *tail, layer=l, tm=tf_c, alpha=alpha)
        x = x_new
    return x
```
